```python
import math
import jax, jax.numpy as jnp
from jax import lax
import numpy as np

D_MODEL = 2048
BATCH = 4
SEQ = 4096
DEPTH = 1

N_Q_HEADS = 16
N_KV_HEADS = 4
HEAD_DIM = 64
WINDOW = 128
ATTN_BLOCK = 128
N_BUCKETS = 32
MAX_DISTANCE = 128
GLA_HEADS = 4
GLA_DK = D_MODEL // 2 // GLA_HEADS
GLA_DV = D_MODEL // GLA_HEADS
GLA_LOWRANK = 16
GLA_NORMALIZER = 16.0
GLA_CHUNK = 64
D_FF = 5632
CONV_WIDTH = 3
EPS = 1e-6
NEG_INF = -1e30

ATTN_Q_W = N_Q_HEADS * HEAD_DIM
ATTN_KV_W = N_KV_HEADS * HEAD_DIM
GLA_K_W = GLA_HEADS * GLA_DK
GLA_V_W = GLA_HEADS * GLA_DV
IN_SPLITS = (ATTN_Q_W, ATTN_KV_W, ATTN_KV_W, GLA_K_W, GLA_K_W, GLA_V_W, GLA_V_W, GLA_LOWRANK, D_MODEL, D_MODEL)
D_IN = sum(IN_SPLITS)

kernel_name = "hybrid_swa_sink_gla_convffn_adaln"


def rms_norm(x, gain):
    xf = x.astype(jnp.float32)
    y = xf * lax.rsqrt(jnp.mean(xf * xf, axis=-1, keepdims=True) + EPS)
    return (y * gain.astype(jnp.float32)).astype(x.dtype)


def modulate(h, shift, scale):
    return h * (1 + scale[:, None, :]) + shift[:, None, :]


def t5_bucket(dist):
    max_exact = N_BUCKETS // 2
    d = jnp.maximum(dist, 0)
    large = max_exact + (jnp.log(jnp.maximum(d, 1).astype(jnp.float32) / max_exact)
                         / math.log(MAX_DISTANCE / max_exact) * (N_BUCKETS - max_exact)).astype(jnp.int32)
    large = jnp.minimum(large, N_BUCKETS - 1)
    return jnp.where(d < max_exact, d, large)


def sliding_window_attention(q, k, v, q_gain, k_gain, sinks, rel_bias_table):
    B, S = q.shape[0], q.shape[1]
    nb = S // ATTN_BLOCK
    G = N_Q_HEADS // N_KV_HEADS
    f32 = jnp.float32
    q = rms_norm(q, q_gain)
    k = rms_norm(k, k_gain)
    qb = q.reshape(B, nb, ATTN_BLOCK, N_KV_HEADS, G, HEAD_DIM)

    def band(t):
        tb = t.reshape(B, nb, ATTN_BLOCK, N_KV_HEADS, HEAD_DIM)
        prev = jnp.pad(tb, ((0, 0), (1, 0), (0, 0), (0, 0), (0, 0)))[:, :-1]
        return jnp.concatenate([prev, tb], axis=2)

    kb, vb = band(k), band(v)
    scores = jnp.einsum('bnqhgd,bnkhd->bhgnqk', qb, kb, preferred_element_type=f32) * (HEAD_DIM ** -0.5)

    i = jnp.arange(ATTN_BLOCK)[:, None]
    j = jnp.arange(2 * ATTN_BLOCK)[None, :]
    dist = i + ATTN_BLOCK - j
    in_window = (dist >= 0) & (dist < WINDOW)
    first_block = (jnp.arange(nb) == 0)[:, None, None]
    valid = in_window[None] & ~(first_block & (j < ATTN_BLOCK)[None])

    bias = rel_bias_table[t5_bucket(dist)].astype(f32)
    bias = bias.transpose(2, 0, 1).reshape(N_KV_HEADS, G, 1, ATTN_BLOCK, 2 * ATTN_BLOCK)
    scores = jnp.where(valid, scores + bias, NEG_INF)

    sink = sinks.astype(f32).reshape(N_KV_HEADS, G, 1, 1, 1)
    m = jnp.maximum(jnp.max(scores, axis=-1, keepdims=True), sink)
    p = jnp.exp(scores - m)
    probs = p / (jnp.sum(p, axis=-1, keepdims=True) + jnp.exp(sink - m))
    out = jnp.einsum('bhgnqk,bnkhd->bnqhgd', probs.astype(vb.dtype), vb)
    return out.reshape(B, S, ATTN_Q_W).astype(q.dtype)


def gated_linear_attention(q, k, v, gk_log, out_gate, o_gain):
    B, S = q.shape[0], q.shape[1]
    nc = S // GLA_CHUNK
    C = GLA_CHUNK
    f32 = jnp.float32

    def chunk(t):
        return t.astype(f32).reshape(B, nc, C, GLA_HEADS, t.shape[-1]).transpose(0, 3, 1, 2, 4)

    qc = chunk(q) * (GLA_DK ** -0.5)
    kc = chunk(k)
    vc = chunk(v)
    g = jnp.cumsum(chunk(gk_log), axis=3)
    g_last = g[:, :, :, -1:, :]
    q_dec = qc * jnp.exp(g)
    k_inv = kc * jnp.exp(-g)
    k_to_end = kc * jnp.exp(g_last - g)

    causal = jnp.tril(jnp.ones((C, C), dtype=bool))
    A = jnp.where(causal, jnp.einsum('bhnid,bhnjd->bhnij', q_dec, k_inv), 0.0)
    o_intra = jnp.einsum('bhnij,bhnjv->bhniv', A, vc)

    def step(state, inp):
        qd, kte, vv, decay = inp
        o = jnp.einsum('bhcd,bhdv->bhcv', qd, state)
        state = decay[..., None] * state + jnp.einsum('bhcd,bhcv->bhdv', kte, vv)
        return state, o

    xs = (jnp.moveaxis(q_dec, 2, 0), jnp.moveaxis(k_to_end, 2, 0), jnp.moveaxis(vc, 2, 0),
          jnp.moveaxis(jnp.exp(g_last[:, :, :, 0, :]), 2, 0))
    state0 = jnp.zeros((B, GLA_HEADS, GLA_DK, GLA_DV), f32)
    _, o_inter = lax.scan(step, state0, xs)
    o = o_intra + jnp.moveaxis(o_inter, 0, 2)
    o = o.transpose(0, 2, 3, 1, 4).reshape(B, S, GLA_HEADS, GLA_DV)
    o = rms_norm(o, o_gain) * jax.nn.silu(out_gate.astype(f32))
    return o.reshape(B, S, GLA_V_W).astype(q.dtype)


def conv_ffn(h, w_up, conv_w, conv_b, w_down):
    S = h.shape[1]
    u = h @ w_up
    up = jnp.pad(u, ((0, 0), (CONV_WIDTH - 1, 0), (0, 0)))
    y = conv_b
    for tap in range(CONV_WIDTH):
        y = y + conv_w[tap] * up[:, tap:tap + S]
    a, b = jnp.split(y, 2, axis=-1)
    return (jax.nn.silu(a) * b) @ w_down


def setup_inputs(seed: int = 0) -> dict:
    key = jax.random.key(seed)
    ks = jax.random.split(key, 24)
    f32 = jnp.float32
    L, D = DEPTH, D_MODEL

    def nrm(k, shape, scale):
        return jax.random.normal(k, shape, f32) * scale

    return {
        "x": nrm(ks[0], (BATCH, SEQ, D), 1.0),
        "c": nrm(ks[1], (BATCH, D), 1.0),
        "rel_bias_table": nrm(ks[2], (N_BUCKETS, N_Q_HEADS), 0.5),
        "w_ada": nrm(ks[3], (L, D, 6 * D), D ** -0.5),
        "b_ada": nrm(ks[4], (L, 6 * D), 0.02),
        "norm1_gain": 1.0 + nrm(ks[5], (L, D), 0.02),
        "w_in": nrm(ks[6], (L, D, D_IN), D ** -0.5),
        "q_norm_gain": 1.0 + nrm(ks[7], (L, HEAD_DIM), 0.02),
        "k_norm_gain": 1.0 + nrm(ks[8], (L, HEAD_DIM), 0.02),
        "attn_sinks": nrm(ks[9], (L, N_Q_HEADS), 0.5),
        "w_gk_up": nrm(ks[10], (L, GLA_LOWRANK, GLA_K_W), GLA_LOWRANK ** -0.5),
        "b_gk": nrm(ks[11], (L, GLA_K_W), 0.02),
        "gla_norm_gain": 1.0 + nrm(ks[12], (L, GLA_DV), 0.02),
        "w_branch_attn": nrm(ks[13], (L, ATTN_Q_W, D), ATTN_Q_W ** -0.5),
        "w_branch_gla": nrm(ks[14], (L, GLA_V_W, D), GLA_V_W ** -0.5),
        "w_out": nrm(ks[15], (L, D, D), D ** -0.5),
        "norm2_gain": 1.0 + nrm(ks[16], (L, D), 0.02),
        "w_ffn_up": nrm(ks[17], (L, D, 2 * D_FF), D ** -0.5),
        "ffn_conv_w": nrm(ks[18], (L, CONV_WIDTH, 2 * D_FF), CONV_WIDTH ** -0.5),
        "ffn_conv_b": nrm(ks[19], (L, 2 * D_FF), 0.02),
        "w_ffn_down": nrm(ks[20], (L, D_FF, D), D_FF ** -0.5),
    }


def reference(x, c, rel_bias_table, w_ada, b_ada, norm1_gain, w_in, q_norm_gain, k_norm_gain,
              attn_sinks, w_gk_up, b_gk, gla_norm_gain, w_branch_attn, w_branch_gla, w_out,
              norm2_gain, w_ffn_up, ffn_conv_w, ffn_conv_b, w_ffn_down):
    B, S = x.shape[0], x.shape[1]
    offsets = []
    acc = 0
    for width in IN_SPLITS[:-1]:
        acc += width
        offsets.append(acc)
    c_act = jax.nn.silu(c)

    for l in range(DEPTH):
        mod = c_act @ w_ada[l] + b_ada[l]
        shift1, scale1, gate1, shift2, scale2, gate2 = jnp.split(mod, 6, axis=-1)

        h = modulate(rms_norm(x, norm1_gain[l]), shift1, scale1)
        proj = h @ w_in[l]
        aq, ak, av, gq, gk, gv, gr, glr, ga, gb = jnp.split(proj, offsets, axis=-1)

        y_attn = sliding_window_attention(
            aq.reshape(B, S, N_Q_HEADS, HEAD_DIM),
            ak.reshape(B, S, N_KV_HEADS, HEAD_DIM),
            av.reshape(B, S, N_KV_HEADS, HEAD_DIM),
            q_norm_gain[l], k_norm_gain[l], attn_sinks[l], rel_bias_table)

        gk_log = jax.nn.log_sigmoid((glr @ w_gk_up[l] + b_gk[l]).astype(jnp.float32)) / GLA_NORMALIZER
        y_gla = gated_linear_attention(
            gq.reshape(B, S, GLA_HEADS, GLA_DK),
            gk.reshape(B, S, GLA_HEADS, GLA_DK),
            gv.reshape(B, S, GLA_HEADS, GLA_DV),
            gk_log.reshape(B, S, GLA_HEADS, GLA_DK),
            gr.reshape(B, S, GLA_HEADS, GLA_DV),
            gla_norm_gain[l])

        merged = (jax.nn.sigmoid(ga) * (y_attn @ w_branch_attn[l])
                  + jax.nn.sigmoid(gb) * (y_gla @ w_branch_gla[l]))
        x = x + gate1[:, None, :] * (merged @ w_out[l])

        h2 = modulate(rms_norm(x, norm2_gain[l]), shift2, scale2)
        x = x + gate2[:, None, :] * conv_ffn(h2, w_ffn_up[l], ffn_conv_w[l], ffn_conv_b[l], w_ffn_down[l])
    return x
```

```python
import functools
import math

import numpy as np
import jax
import jax.numpy as jnp
from jax import lax
from jax.experimental import pallas as pl
from jax.experimental.pallas import tpu as pltpu

F32 = jnp.float32
BF16 = jnp.bfloat16

D_MODEL = 2048
N_Q_HEADS = 16
N_KV_HEADS = 4
GQA_GROUP = N_Q_HEADS // N_KV_HEADS
HEAD_DIM = 64
WINDOW = 128
ATTN_BLOCK = 128
N_BUCKETS = 32
MAX_DISTANCE = 128
GLA_HEADS = 4
GLA_DK = 256
GLA_DV = 512
GLA_LOWRANK = 16
GLA_NORMALIZER = 16.0
GLA_CHUNK = 64
D_FF = 5632
EPS = 1e-6
NEG_INF = -1e30

ATTN_Q_W = N_Q_HEADS * HEAD_DIM
ATTN_KV_W = N_KV_HEADS * HEAD_DIM
GLA_K_W = GLA_HEADS * GLA_DK
GLA_V_W = GLA_HEADS * GLA_DV

COL_GA = 0
COL_GB = COL_GA + D_MODEL
COL_GV = COL_GB + D_MODEL
COL_GR = COL_GV + GLA_V_W
COL_GQ = COL_GR + GLA_V_W
COL_GK = COL_GQ + GLA_K_W
COL_AQ = COL_GK + GLA_K_W
COL_AK = COL_AQ + ATTN_Q_W
COL_AV = COL_AK + ATTN_KV_W
COL_LR = COL_AV + ATTN_KV_W
LR_PAD = 128
PROJ_W = 12288

VMEM_LIMIT = 60 * 1024 * 1024

ADALN_TN = 1024
INPROJ_TM, INPROJ_TN = 1024, 1024
ATTN_TQ = 256
GLA_ROWS = 256
MERGE_TM = 256
FFN_TM, FFN_TN = 1024, 512
FFN_HALO = 16
DOWN_TM, DOWN_TN = 1024, 512


def _params(*sem):
    return pltpu.CompilerParams(dimension_semantics=sem, vmem_limit_bytes=VMEM_LIMIT)


def _dot(a, b):
    return jnp.dot(a, b, preferred_element_type=F32)


def _dot_nt(a, b):
    return lax.dot_general(a, b, (((1,), (1,)), ((), ())), preferred_element_type=F32)


def _dot_tn(a, b):
    return lax.dot_general(a, b, (((0,), (0,)), ((), ())), preferred_element_type=F32)


def _rms(x, gain):
    return x * lax.rsqrt(jnp.mean(x * x, axis=-1, keepdims=True) + EPS) * gain


def _adaln_kernel(c_ref, w_ref, b_ref, o_ref):
    c = c_ref[...]
    ca = c * jax.nn.sigmoid(c)
    o_ref[...] = _dot(ca.astype(BF16), w_ref[...].astype(BF16)) + b_ref[...]


def _adaln(c, w_ada, b_ada):
    B, D = c.shape
    N = w_ada.shape[1]
    return pl.pallas_call(
        _adaln_kernel,
        grid=(N // ADALN_TN,),
        in_specs=[pl.BlockSpec((B, D), lambda j: (0, 0)),
                  pl.BlockSpec((D, ADALN_TN), lambda j: (0, j)),
                  pl.BlockSpec((1, ADALN_TN), lambda j: (0, j))],
        out_specs=pl.BlockSpec((B, ADALN_TN), lambda j: (0, j)),
        out_shape=jax.ShapeDtypeStruct((B, N), F32),
        compiler_params=_params("parallel"),
        name="adaln",
    )(c, w_ada, b_ada.reshape(1, N))


def _bucket_table():
    i = np.arange(ATTN_BLOCK)[:, None]
    j = np.arange(2 * ATTN_BLOCK)[None, :]
    dist = i + ATTN_BLOCK - j
    max_exact = N_BUCKETS // 2
    d = np.maximum(dist, 0)
    ratio = np.log(np.maximum(d, 1).astype(np.float32) / np.float32(max_exact)) / np.float32(
        math.log(MAX_DISTANCE / max_exact))
    large = max_exact + (ratio.astype(np.float32) * np.float32(N_BUCKETS - max_exact)).astype(np.int32)
    large = np.minimum(large, N_BUCKETS - 1)
    bucket = np.where(d < max_exact, d, large)
    in_window = (dist >= 0) & (dist < WINDOW)
    return np.where(in_window, bucket, -1).astype(np.int32)


def _relbias_kernel(tab_ref, bkt_ref, o_ref):
    h = pl.program_id(0)
    bkt = bkt_ref[...]
    acc = jnp.zeros(bkt.shape, F32)
    for b in range(N_BUCKETS):
        acc = jnp.where(bkt == b, tab_ref[b, h], acc)
    regular = jnp.where(bkt >= 0, acc, NEG_INF)
    col = lax.broadcasted_iota(jnp.int32, bkt.shape, 1)
    o_ref[0, 0] = jnp.where(col >= ATTN_BLOCK, regular, NEG_INF)
    o_ref[1, 0] = regular


def _relbias(rel_bias_table):
    bkt = jnp.asarray(_bucket_table())
    return pl.pallas_call(
        _relbias_kernel,
        grid=(N_Q_HEADS,),
        in_specs=[pl.BlockSpec(memory_space=pltpu.SMEM),
                  pl.BlockSpec(bkt.shape, lambda h: (0, 0))],
        out_specs=pl.BlockSpec((2, 1) + bkt.shape, lambda h: (0, h, 0, 0)),
        out_shape=jax.ShapeDtypeStruct((2, N_Q_HEADS) + bkt.shape, F32),
        compiler_params=_params("parallel"),
        name="relbias",
    )(rel_bias_table, bkt)


def _inproj_kernel(x_ref, mod_ref, g_ref, w_ref, o_ref, h_ref):
    @pl.when(pl.program_id(1) == 0)
    def _():
        scale = 1.0 + mod_ref[1:2, :]
        shift = mod_ref[0:1, :]
        gain = g_ref[...]

        def body(r, carry):
            rows = pl.ds(pl.multiple_of(r * 128, 128), 128)
            h_ref[rows, :] = (_rms(x_ref[rows, :], gain) * scale + shift).astype(BF16)
            return carry

        lax.fori_loop(0, INPROJ_TM // 128, body, 0)

    o_ref[...] = _dot(h_ref[...], w_ref[...]).astype(BF16)


def _inproj(x2, mod3, gain, w):
    T, D = x2.shape
    S = T // mod3.shape[0]
    tm, tn = INPROJ_TM, INPROJ_TN
    return pl.pallas_call(
        _inproj_kernel,
        grid=(T // tm, PROJ_W // tn),
        in_specs=[pl.BlockSpec((tm, D), lambda i, j: (i, 0)),
                  pl.BlockSpec((None, 6, D), lambda i, j: (i // (S // tm), 0, 0)),
                  pl.BlockSpec((1, D), lambda i, j: (0, 0)),
                  pl.BlockSpec((D, tn), lambda i, j: (0, j))],
        out_specs=pl.BlockSpec((tm, tn), lambda i, j: (i, j)),
        out_shape=jax.ShapeDtypeStruct((T, PROJ_W), BF16),
        scratch_shapes=[pltpu.VMEM((tm, D), BF16)],
        compiler_params=_params("parallel", "arbitrary"),
        name="inproj",
    )(x2, mod3, gain, w)


def _attn_kernel(q_ref, kc_ref, kp_ref, vc_ref, vp_ref, b0_ref, b1_ref, qg_ref, kg_ref, sink_ref, o_ref):
    nsub = ATTN_TQ // ATTN_BLOCK
    blk, G, dh = ATTN_BLOCK, GQA_GROUP, HEAD_DIM
    kall = jnp.concatenate([kp_ref[...], kc_ref[...]], axis=0).astype(F32)
    vall = jnp.concatenate([vp_ref[...], vc_ref[...]], axis=0)
    qgain = qg_ref[...] * (HEAD_DIM ** -0.5)
    kgain = kg_ref[...]
    for g in range(N_KV_HEADS):
        kn = _rms(kall[:, g * dh:(g + 1) * dh], kgain).astype(BF16)
        vh = vall[:, g * dh:(g + 1) * dh]
        sink = jnp.concatenate(
            [jnp.full((blk, 1), sink_ref[g * G + e], F32) for e in range(G)], axis=0)
        for sb in range(nsub):
            bref = b0_ref if sb == 0 else b1_ref
            bias = jnp.concatenate([bref[g * G + e] for e in range(G)], axis=0)
            qs = q_ref[sb * blk:(sb + 1) * blk, g * G * dh:(g + 1) * G * dh].astype(F32)
            qst = jnp.concatenate([qs[:, e * dh:(e + 1) * dh] for e in range(G)], axis=0)
            qn = _rms(qst, qgain).astype(BF16)
            s = _dot_nt(qn, kn[sb * blk:sb * blk + 2 * blk]) + bias
            m = jnp.maximum(jnp.max(s, axis=-1, keepdims=True), sink)
            p = jnp.exp(s - m)
            denom = jnp.sum(p, axis=-1, keepdims=True) + jnp.exp(sink - m)
            o = _dot(p.astype(BF16), vh[sb * blk:sb * blk + 2 * blk]) / denom
            og = jnp.concatenate([o[e * blk:(e + 1) * blk] for e in range(G)], axis=1)
            o_ref[sb * blk:(sb + 1) * blk, g * G * dh:(g + 1) * G * dh] = og.astype(BF16)


def _attn(proj3, biasm, q_gain, k_gain, sinks):
    B, S, _ = proj3.shape
    tq, blk = ATTN_TQ, ATTN_BLOCK
    nsub = tq // blk
    kvw = ATTN_KV_W

    def prev(i):
        return jnp.maximum(i * nsub - 1, 0)

    bias_block = (None, N_Q_HEADS, blk, 2 * blk)
    return pl.pallas_call(
        _attn_kernel,
        grid=(B, S // tq),
        in_specs=[pl.BlockSpec((None, tq, ATTN_Q_W), lambda b, i: (b, i, COL_AQ // ATTN_Q_W)),
                  pl.BlockSpec((None, tq, kvw), lambda b, i: (b, i, COL_AK // kvw)),
                  pl.BlockSpec((None, blk, kvw), lambda b, i: (b, prev(i), COL_AK // kvw)),
                  pl.BlockSpec((None, tq, kvw), lambda b, i: (b, i, COL_AV // kvw)),
                  pl.BlockSpec((None, blk, kvw), lambda b, i: (b, prev(i), COL_AV // kvw)),
                  pl.BlockSpec(bias_block, lambda b, i: (jnp.minimum(i, 1), 0, 0, 0)),
                  pl.BlockSpec(bias_block, lambda b, i: (1, 0, 0, 0)),
                  pl.BlockSpec((1, HEAD_DIM), lambda b, i: (0, 0)),
                  pl.BlockSpec((1, HEAD_DIM), lambda b, i: (0, 0)),
                  pl.BlockSpec(memory_space=pltpu.SMEM)],
        out_specs=pl.BlockSpec((None, tq, ATTN_Q_W), lambda b, i: (b, i, 0)),
        out_shape=jax.ShapeDtypeStruct((B, S, ATTN_Q_W), BF16),
        compiler_params=_params("parallel", "parallel"),
        name="attn",
    )(proj3, proj3, proj3, proj3, proj3, biasm, biasm, q_gain, k_gain, sinks)


def _split3(x):
    hi = x.astype(BF16)
    r1 = x - hi.astype(F32)
    mid = r1.astype(BF16)
    lo = (r1 - mid.astype(F32)).astype(BF16)
    return hi, mid, lo


def _gla_kernel(q_ref, k_ref, v_ref, r_ref, lr_ref, wgk_ref, bgk_ref, gain_ref, o_ref, state_ref):
    R, C = GLA_ROWS, GLA_CHUNK

    @pl.when(pl.program_id(2) == 0)
    def _():
        state_ref[...] = jnp.zeros_like(state_ref)

    z = _dot(lr_ref[...], wgk_ref[...]) + bgk_ref[...]
    gl = (jnp.minimum(z, 0.0) - jnp.log1p(jnp.exp(-jnp.abs(z)))) * (1.0 / GLA_NORMALIZER)

    row = lax.broadcasted_iota(jnp.int32, (R, R), 0)
    col = lax.broadcasted_iota(jnp.int32, (R, R), 1)
    causal = (row >= col) & ((row // C) == (col // C))
    tril = causal.astype(BF16)
    hi, mid, lo = _split3(gl)
    g = _dot(tril, hi) + _dot(tril, mid) + _dot(tril, lo)

    q = q_ref[...].astype(F32) * (GLA_DK ** -0.5)
    k = k_ref[...].astype(F32)
    v = v_ref[...]
    q_dec = (q * jnp.exp(g)).astype(BF16)
    k_inv = (k * jnp.exp(-g)).astype(BF16)
    a = jnp.where(causal, _dot_nt(q_dec, k_inv), 0.0)
    o = _dot(a.astype(BF16), v)

    eye = lax.broadcasted_iota(jnp.int32, (GLA_DK, GLA_DK), 0) == lax.broadcasted_iota(
        jnp.int32, (GLA_DK, GLA_DK), 1)
    o_inter = []
    for c in range(R // C):
        rows = slice(c * C, (c + 1) * C)
        g_last = g[(c + 1) * C - 1:(c + 1) * C]
        k_to_end = (k[rows] * jnp.exp(g_last - g[rows])).astype(BF16)
        state = state_ref[...]
        o_inter.append(_dot(q_dec[rows], state.astype(BF16)))
        decay = jnp.exp(g_last)
        decay_col = jnp.sum(jnp.where(eye, jnp.broadcast_to(decay, eye.shape), 0.0), axis=1, keepdims=True)
        state_ref[...] = decay_col * state + _dot_tn(k_to_end, v[rows])
    o = o + jnp.concatenate(o_inter, axis=0)

    r = r_ref[...].astype(F32)
    o_ref[...] = (_rms(o, gain_ref[...]) * (r * jax.nn.sigmoid(r))).astype(BF16)


def _gla(proj3, wgk, bgk, gain):
    B, S, _ = proj3.shape
    R = GLA_ROWS
    dk, dv = GLA_DK, GLA_DV
    return pl.pallas_call(
        _gla_kernel,
        grid=(B, GLA_HEADS, S // R),
        in_specs=[pl.BlockSpec((None, R, dk), lambda b, h, t: (b, t, COL_GQ // dk + h)),
                  pl.BlockSpec((None, R, dk), lambda b, h, t: (b, t, COL_GK // dk + h)),
                  pl.BlockSpec((None, R, dv), lambda b, h, t: (b, t, COL_GV // dv + h)),
                  pl.BlockSpec((None, R, dv), lambda b, h, t: (b, t, COL_GR // dv + h)),
                  pl.BlockSpec((None, R, LR_PAD), lambda b, h, t: (b, t, COL_LR // LR_PAD)),
                  pl.BlockSpec((LR_PAD, dk), lambda b, h, t: (0, h)),
                  pl.BlockSpec((1, dk), lambda b, h, t: (0, h)),
                  pl.BlockSpec((1, dv), lambda b, h, t: (0, 0))],
        out_specs=pl.BlockSpec((None, R, dv), lambda b, h, t: (b, t, h)),
        out_shape=jax.ShapeDtypeStruct((B, S, GLA_V_W), BF16),
        scratch_shapes=[pltpu.VMEM((dk, dv), F32)],
        compiler_params=_params("parallel", "parallel", "arbitrary"),
        name="gla",
    )(proj3, proj3, proj3, proj3, proj3, wgk, bgk, gain)


def _merge_kernel(ya_ref, yg_ref, ga_ref, gb_ref, x_ref, mod_ref, g2_ref, wa_ref, wg_ref, wo_ref,
                  x1_ref, h2_ref):
    ga = jax.nn.sigmoid(ga_ref[...].astype(F32))
    gb = jax.nn.sigmoid(gb_ref[...].astype(F32))
    merged = ga * _dot(ya_ref[...], wa_ref[...]) + gb * _dot(yg_ref[...], wg_ref[...])
    x1 = x_ref[...] + mod_ref[2:3, :] * _dot(merged.astype(BF16), wo_ref[...])
    x1_ref[...] = x1
    h2_ref[...] = (_rms(x1, g2_ref[...]) * (1.0 + mod_ref[4:5, :]) + mod_ref[3:4, :]).astype(BF16)


def _merge(ya, yg, proj, x2, mod3, gain2, wa, wg, wo):
    T, D = x2.shape
    S = T // mod3.shape[0]
    tm = MERGE_TM
    once = pl.Buffered(1)
    return pl.pallas_call(
        _merge_kernel,
        grid=(T // tm,),
        in_specs=[pl.BlockSpec((tm, ATTN_Q_W), lambda i: (i, 0)),
                  pl.BlockSpec((tm, GLA_V_W), lambda i: (i, 0)),
                  pl.BlockSpec((tm, D), lambda i: (i, COL_GA // D)),
                  pl.BlockSpec((tm, D), lambda i: (i, COL_GB // D)),
                  pl.BlockSpec((tm, D), lambda i: (i, 0)),
                  pl.BlockSpec((None, 6, D), lambda i: (i // (S // tm), 0, 0)),
                  pl.BlockSpec((1, D), lambda i: (0, 0)),
                  pl.BlockSpec((ATTN_Q_W, D), lambda i: (0, 0), pipeline_mode=once),
                  pl.BlockSpec((GLA_V_W, D), lambda i: (0, 0), pipeline_mode=once),
                  pl.BlockSpec((D, D), lambda i: (0, 0), pipeline_mode=once)],
        out_specs=[pl.BlockSpec((tm, D), lambda i: (i, 0)),
                   pl.BlockSpec((tm, D), lambda i: (i, 0))],
        out_shape=[jax.ShapeDtypeStruct((T, D), F32), jax.ShapeDtypeStruct((T, D), BF16)],
        compiler_params=_params("parallel"),
        name="merge",
    )(ya, yg, proj, proj, x2, mod3, gain2, wa, wg, wo)


def _ffn_up_kernel(h_ref, halo_ref, wa_ref, wb_ref, cwa_ref, cwb_ref, cba_ref, cbb_ref, o_ref,
                   hs_ref, ua_ref, ub_ref, *, tiles_per_seq):
    tm, halo = FFN_TM, FFN_HALO

    @pl.when(pl.program_id(1) == 0)
    def _():
        first = (pl.program_id(0) % tiles_per_seq) == 0
        hs_ref[0:halo, :] = jnp.where(first, jnp.zeros_like(halo_ref), halo_ref[...])
        hs_ref[halo:, :] = h_ref[...]

    ua_ref[...] = _dot(hs_ref[...], wa_ref[...])
    ub_ref[...] = _dot(hs_ref[...], wb_ref[...])

    def conv(u_ref, cw_ref, cb_ref, r0, n):
        y = cb_ref[...] + cw_ref[0:1, :] * u_ref[halo - 2 + r0:halo - 2 + r0 + n, :]
        y = y + cw_ref[1:2, :] * u_ref[halo - 1 + r0:halo - 1 + r0 + n, :]
        return y + cw_ref[2:3, :] * u_ref[halo + r0:halo + r0 + n, :]

    rc = 256
    for r0 in range(0, tm, rc):
        ya = conv(ua_ref, cwa_ref, cba_ref, r0, rc)
        yb = conv(ub_ref, cwb_ref, cbb_ref, r0, rc)
        o_ref[r0:r0 + rc, :] = (ya * jax.nn.sigmoid(ya) * yb).astype(BF16)


def _ffn_up(h2, S, wa, wb, cwa, cwb, cba, cbb):
    T, D = h2.shape
    tm, tn, halo = FFN_TM, FFN_TN, FFN_HALO
    return pl.pallas_call(
        functools.partial(_ffn_up_kernel, tiles_per_seq=S // tm),
        grid=(T // tm, D_FF // tn),
        in_specs=[pl.BlockSpec((tm, D), lambda i, j: (i, 0)),
                  pl.BlockSpec((halo, D), lambda i, j: (jnp.maximum(i * (tm // halo) - 1, 0), 0)),
                  pl.BlockSpec((D, tn), lambda i, j: (0, j)),
                  pl.BlockSpec((D, tn), lambda i, j: (0, j)),
                  pl.BlockSpec((3, tn), lambda i, j: (0, j)),
                  pl.BlockSpec((3, tn), lambda i, j: (0, j)),
                  pl.BlockSpec((1, tn), lambda i, j: (0, j)),
                  pl.BlockSpec((1, tn), lambda i, j: (0, j))],
        out_specs=pl.BlockSpec((tm, tn), lambda i, j: (i, j)),
        out_shape=jax.ShapeDtypeStruct((T, D_FF), BF16),
        scratch_shapes=[pltpu.VMEM((tm + halo, D), BF16),
                        pltpu.VMEM((tm + halo, tn), F32),
                        pltpu.VMEM((tm + halo, tn), F32)],
        compiler_params=_params("parallel", "arbitrary"),
        name="ffn_up",
    )(h2, h2, wa, wb, cwa, cwb, cba, cbb)


def _ffn_down_kernel(a_ref, w_ref, x_ref, mod_ref, o_ref):
    o_ref[...] = x_ref[...] + mod_ref[5:6, :] * _dot(a_ref[...], w_ref[...])


def _ffn_down(act, wd, x1, mod3):
    T, D = x1.shape
    S = T // mod3.shape[0]
    tm, tn = DOWN_TM, DOWN_TN
    return pl.pallas_call(
        _ffn_down_kernel,
        grid=(T // tm, D // tn),
        in_specs=[pl.BlockSpec((tm, D_FF), lambda i, j: (i, 0)),
                  pl.BlockSpec((D_FF, tn), lambda i, j: (0, j)),
                  pl.BlockSpec((tm, tn), lambda i, j: (i, j)),
                  pl.BlockSpec((None, 6, tn), lambda i, j: (i // (S // tm), 0, j))],
        out_specs=pl.BlockSpec((tm, tn), lambda i, j: (i, j)),
        out_shape=jax.ShapeDtypeStruct((T, D), F32),
        compiler_params=_params("parallel", "arbitrary"),
        name="ffn_down",
    )(act, wd, x1, mod3)


def _layer(x2, B, mod3, biasm, norm1_gain, w_in, q_norm_gain, k_norm_gain, attn_sinks, w_gk_up, b_gk,
           gla_norm_gain, w_branch_attn, w_branch_gla, w_out, norm2_gain, w_ffn_up, ffn_conv_w,
           ffn_conv_b, w_ffn_down):
    T, D = x2.shape
    S = T // B
    o_lr = ATTN_Q_W + 2 * ATTN_KV_W + 2 * GLA_K_W + 2 * GLA_V_W
    o_ga = o_lr + GLA_LOWRANK
    o_gb = o_ga + D
    o_gq = ATTN_Q_W + 2 * ATTN_KV_W
    o_gv = o_gq + 2 * GLA_K_W
    w_p = jnp.concatenate(
        [w_in[:, o_ga:o_gb], w_in[:, o_gb:o_gb + D],
         w_in[:, o_gv:o_gv + 2 * GLA_V_W],
         w_in[:, o_gq:o_gq + 2 * GLA_K_W],
         w_in[:, 0:o_gq],
         w_in[:, o_lr:o_ga],
         jnp.zeros((D, PROJ_W - COL_LR - GLA_LOWRANK), w_in.dtype)], axis=1).astype(BF16)
    wgk = jnp.concatenate([w_gk_up, jnp.zeros((LR_PAD - GLA_LOWRANK, GLA_K_W), w_gk_up.dtype)],
                          axis=0).astype(BF16)

    proj = _inproj(x2, mod3, norm1_gain.reshape(1, D), w_p)
    proj3 = proj.reshape(B, S, PROJ_W)
    ya = _attn(proj3, biasm, q_norm_gain.reshape(1, HEAD_DIM), k_norm_gain.reshape(1, HEAD_DIM), attn_sinks)
    yg = _gla(proj3, wgk, b_gk.reshape(1, GLA_K_W), gla_norm_gain.reshape(1, GLA_DV))
    x1, h2 = _merge(ya.reshape(T, ATTN_Q_W), yg.reshape(T, GLA_V_W), proj, x2, mod3,
                    norm2_gain.reshape(1, D), w_branch_attn.astype(BF16), w_branch_gla.astype(BF16),
                    w_out.astype(BF16))
    act = _ffn_up(h2, S, w_ffn_up[:, :D_FF].astype(BF16), w_ffn_up[:, D_FF:].astype(BF16),
                  ffn_conv_w[:, :D_FF], ffn_conv_w[:, D_FF:],
                  ffn_conv_b[:D_FF].reshape(1, D_FF), ffn_conv_b[D_FF:].reshape(1, D_FF))
    return _ffn_down(act, w_ffn_down.astype(BF16), x1, mod3)


def kernel(x, c, rel_bias_table, w_ada, b_ada, norm1_gain, w_in, q_norm_gain, k_norm_gain, attn_sinks,
           w_gk_up, b_gk, gla_norm_gain, w_branch_attn, w_branch_gla, w_out, norm2_gain, w_ffn_up,
           ffn_conv_w, ffn_conv_b, w_ffn_down):
    B, S, D = x.shape
    depth = w_in.shape[0]
    biasm = _relbias(rel_bias_table)
    x2 = x.reshape(B * S, D)
    for l in range(depth):
        mod3 = _adaln(c, w_ada[l], b_ada[l]).reshape(B, 6, D)
        x2 = _layer(x2, B, mod3, biasm, norm1_gain[l], w_in[l], q_norm_gain[l], k_norm_gain[l],
                    attn_sinks[l], w_gk_up[l], b_gk[l], gla_norm_gain[l], w_branch_attn[l],
                    w_branch_gla[l], w_out[l], norm2_gain[l], w_ffn_up[l], ffn_conv_w[l], ffn_conv_b[l],
                    w_ffn_down[l])
    return x2.reshape(B, S, D)
```

```python
import functools
import math

import numpy as np
import jax
import jax.numpy as jnp
from jax import lax
from jax.experimental import pallas as pl
from jax.experimental.pallas import tpu as pltpu

F32 = jnp.float32
BF16 = jnp.bfloat16

D_MODEL = 2048
N_Q_HEADS = 16
N_KV_HEADS = 4
GQA_GROUP = N_Q_HEADS // N_KV_HEADS
HEAD_DIM = 64
WINDOW = 128
ATTN_BLOCK = 128
N_BUCKETS = 32
MAX_DISTANCE = 128
GLA_HEADS = 4
GLA_DK = 256
GLA_DV = 512
GLA_LOWRANK = 16
GLA_NORMALIZER = 16.0
GLA_CHUNK = 64
D_FF = 5632
EPS = 1e-6
NEG_INF = -1e30

ATTN_Q_W = N_Q_HEADS * HEAD_DIM
ATTN_KV_W = N_KV_HEADS * HEAD_DIM
GLA_K_W = GLA_HEADS * GLA_DK
GLA_V_W = GLA_HEADS * GLA_DV

COL_GA = 0
COL_GB = COL_GA + D_MODEL
COL_GV = COL_GB + D_MODEL
COL_GR = COL_GV + GLA_V_W
COL_GQ = COL_GR + GLA_V_W
COL_GK = COL_GQ + GLA_K_W
COL_AQ = COL_GK + GLA_K_W
COL_AK = COL_AQ + ATTN_Q_W
COL_AV = COL_AK + ATTN_KV_W
COL_LR = COL_AV + ATTN_KV_W
LR_PAD = 128
PROJ_W = 12288

VMEM_LIMIT = 60 * 1024 * 1024

ADALN_TN = 1024
INPROJ_TM, INPROJ_TN = 1024, 1024
ATTN_TQ = 256
GLA_ROWS = 256
MERGE_TM = 256
FFN_TM, FFN_TN = 1024, 512
FFN_HALO = 16
DOWN_TM, DOWN_TN = 1024, 512


def _params(*sem):
    return pltpu.CompilerParams(dimension_semantics=sem, vmem_limit_bytes=VMEM_LIMIT)


def _dot(a, b):
    return jnp.dot(a, b, preferred_element_type=F32)


def _dot_nt(a, b):
    return lax.dot_general(a, b, (((1,), (1,)), ((), ())), preferred_element_type=F32)


def _dot_tn(a, b):
    return lax.dot_general(a, b, (((0,), (0,)), ((), ())), preferred_element_type=F32)


def _rms(x, gain):
    return x * lax.rsqrt(jnp.mean(x * x, axis=-1, keepdims=True) + EPS) * gain


def _adaln_kernel(c_ref, w_ref, b_ref, o_ref):
    c = c_ref[...]
    ca = c * jax.nn.sigmoid(c)
    o_ref[...] = _dot(ca.astype(BF16), w_ref[...].astype(BF16)) + b_ref[...]


def _adaln(c, w_ada, b_ada):
    B, D = c.shape
    N = w_ada.shape[1]
    return pl.pallas_call(
        _adaln_kernel,
        grid=(N // ADALN_TN,),
        in_specs=[pl.BlockSpec((B, D), lambda j: (0, 0)),
                  pl.BlockSpec((D, ADALN_TN), lambda j: (0, j)),
                  pl.BlockSpec((1, ADALN_TN), lambda j: (0, j))],
        out_specs=pl.BlockSpec((B, ADALN_TN), lambda j: (0, j)),
        out_shape=jax.ShapeDtypeStruct((B, N), F32),
        compiler_params=_params("parallel"),
        name="adaln",
    )(c, w_ada, b_ada.reshape(1, N))


def _bucket_table():
    j = np.arange(2 * ATTN_BLOCK)[:, None]
    i = np.arange(ATTN_BLOCK)[None, :]
    dist = i + ATTN_BLOCK - j
    max_exact = N_BUCKETS // 2
    d = np.maximum(dist, 0)
    ratio = np.log(np.maximum(d, 1).astype(np.float32) / np.float32(max_exact)) / np.float32(
        math.log(MAX_DISTANCE / max_exact))
    large = max_exact + (ratio.astype(np.float32) * np.float32(N_BUCKETS - max_exact)).astype(np.int32)
    large = np.minimum(large, N_BUCKETS - 1)
    bucket = np.where(d < max_exact, d, large)
    in_window = (dist >= 0) & (dist < WINDOW)
    return np.where(in_window, bucket, -1).astype(np.int32)


def _relbias_kernel(tab_ref, bkt_ref, o_ref):
    h = pl.program_id(0)
    bkt = bkt_ref[...]
    acc = jnp.zeros(bkt.shape, F32)
    for b in range(N_BUCKETS):
        acc = jnp.where(bkt == b, tab_ref[b, h], acc)
    regular = jnp.where(bkt >= 0, acc, NEG_INF)
    key = lax.broadcasted_iota(jnp.int32, bkt.shape, 0)
    o_ref[0] = jnp.where(key >= ATTN_BLOCK, regular, NEG_INF)
    o_ref[1] = regular


def _relbias(rel_bias_table):
    bkt = jnp.asarray(_bucket_table())
    nk, nq = bkt.shape
    return pl.pallas_call(
        _relbias_kernel,
        grid=(N_Q_HEADS,),
        in_specs=[pl.BlockSpec(memory_space=pltpu.SMEM),
                  pl.BlockSpec(bkt.shape, lambda h: (0, 0))],
        out_specs=pl.BlockSpec((2, None, nk, nq), lambda h: (0, h // GQA_GROUP, h % 2, (h // 2) % 2)),
        out_shape=jax.ShapeDtypeStruct((2, N_KV_HEADS, 2 * nk, 2 * nq), F32),
        compiler_params=_params("parallel"),
        name="relbias",
    )(rel_bias_table, bkt)


def _inproj_kernel(x_ref, mod_ref, g_ref, w_ref, o_ref, h_ref):
    @pl.when(pl.program_id(1) == 0)
    def _():
        scale = 1.0 + mod_ref[1:2, :]
        shift = mod_ref[0:1, :]
        gain = g_ref[...]

        def body(r, carry):
            rows = pl.ds(pl.multiple_of(r * 128, 128), 128)
            h_ref[rows, :] = (_rms(x_ref[rows, :], gain) * scale + shift).astype(BF16)
            return carry

        lax.fori_loop(0, INPROJ_TM // 128, body, 0)

    o_ref[...] = _dot(h_ref[...], w_ref[...]).astype(BF16)


def _inproj(x2, mod3, gain, w):
    T, D = x2.shape
    S = T // mod3.shape[0]
    tm, tn = INPROJ_TM, INPROJ_TN
    return pl.pallas_call(
        _inproj_kernel,
        grid=(T // tm, PROJ_W // tn),
        in_specs=[pl.BlockSpec((tm, D), lambda i, j: (i, 0)),
                  pl.BlockSpec((None, 6, D), lambda i, j: (i // (S // tm), 0, 0)),
                  pl.BlockSpec((1, D), lambda i, j: (0, 0)),
                  pl.BlockSpec((D, tn), lambda i, j: (0, j))],
        out_specs=pl.BlockSpec((tm, tn), lambda i, j: (i, j)),
        out_shape=jax.ShapeDtypeStruct((T, PROJ_W), BF16),
        scratch_shapes=[pltpu.VMEM((tm, D), BF16)],
        compiler_params=_params("parallel", "arbitrary"),
        name="inproj",
    )(x2, mod3, gain, w)


def _attn_kernel(q_ref, kc_ref, kp_ref, vc_ref, vp_ref, b0_ref, b1_ref, qg_ref, kg_ref, sink_ref, seg_ref,
                 o_ref):
    nsub = ATTN_TQ // ATTN_BLOCK
    blk, dh = ATTN_BLOCK, HEAD_DIM
    seg = seg_ref[...]

    def normed(x, gain):
        ssq = _dot((x * x).astype(BF16), seg)
        return x * lax.rsqrt(ssq * (1.0 / dh) + EPS) * gain

    kn = normed(jnp.concatenate([kp_ref[...], kc_ref[...]], axis=0).astype(F32), kg_ref[...])
    vt = jnp.concatenate([vp_ref[...], vc_ref[...]], axis=0).astype(F32).T.astype(BF16)
    lane = lax.broadcasted_iota(jnp.int32, (kn.shape[0], 2 * dh), 1)
    qgain = qg_ref[...] * (dh ** -0.5)

    for g in range(N_KV_HEADS):
        pair = kn[:, (g // 2) * 2 * dh:(g // 2 + 1) * 2 * dh]
        own = jnp.where(lane >= dh if g % 2 else lane < dh, pair, 0.0)
        swapped = pltpu.roll(own, dh, axis=1)
        kz = [own, swapped] if g % 2 == 0 else [swapped, own]
        kz = [z.astype(BF16) for z in kz]
        qn = normed(q_ref[:, g * 4 * dh:(g + 1) * 4 * dh].astype(F32), qgain).astype(BF16)
        for sb in range(nsub):
            bref = b0_ref if sb == 0 else b1_ref
            band = slice(sb * blk, sb * blk + 2 * blk)
            rows = slice(sb * blk, (sb + 1) * blk)
            kband = jnp.concatenate([kz[0][band], kz[1][band]], axis=0)
            qr = jnp.concatenate([qn[rows, 0:2 * dh], qn[rows, 2 * dh:4 * dh]], axis=0)
            s = _dot_nt(kband, qr) + bref[g]
            ps, inv = [], []
            for e in range(2):
                se = s[e * 2 * blk:(e + 1) * 2 * blk]
                sink = sink_ref[g, e]
                m = jnp.maximum(jnp.max(se, axis=0, keepdims=True), sink)
                p = jnp.exp(se - m)
                inv.append(1.0 / (jnp.sum(p, axis=0, keepdims=True) + jnp.exp(sink - m)))
                ps.append(p.astype(BF16))
            ot = _dot(vt[g * dh:(g + 1) * dh, band], jnp.concatenate(ps, axis=1))
            ot = ot * jnp.concatenate(inv, axis=1)
            o = jnp.concatenate([ot[:, 0:2 * blk], ot[:, 2 * blk:4 * blk]], axis=0).T
            o_ref[rows, (2 * g) * 2 * dh:(2 * g + 1) * 2 * dh] = o[0:blk].astype(BF16)
            o_ref[rows, (2 * g + 1) * 2 * dh:(2 * g + 2) * 2 * dh] = o[blk:2 * blk].astype(BF16)


def _attn(proj3, biasm, q_gain, k_gain, sinks):
    B, S, _ = proj3.shape
    tq, blk = ATTN_TQ, ATTN_BLOCK
    nsub = tq // blk
    kvw = ATTN_KV_W
    seg_id = np.arange(kvw) // HEAD_DIM
    seg = jnp.asarray(seg_id[:, None] == seg_id[None, :], dtype=BF16)
    sink_rows = jnp.repeat(sinks.reshape(N_KV_HEADS, 2, 2).transpose(0, 2, 1), blk, axis=-1)
    sink_rows = sink_rows.reshape(N_KV_HEADS, 2, 1, 2 * blk)

    def prev(i):
        return jnp.maximum(i * nsub - 1, 0)

    bias_block = (None,) + biasm.shape[1:]
    return pl.pallas_call(
        _attn_kernel,
        grid=(B, S // tq),
        in_specs=[pl.BlockSpec((None, tq, ATTN_Q_W), lambda b, i: (b, i, COL_AQ // ATTN_Q_W)),
                  pl.BlockSpec((None, tq, kvw), lambda b, i: (b, i, COL_AK // kvw)),
                  pl.BlockSpec((None, blk, kvw), lambda b, i: (b, prev(i), COL_AK // kvw)),
                  pl.BlockSpec((None, tq, kvw), lambda b, i: (b, i, COL_AV // kvw)),
                  pl.BlockSpec((None, blk, kvw), lambda b, i: (b, prev(i), COL_AV // kvw)),
                  pl.BlockSpec(bias_block, lambda b, i: (jnp.minimum(i, 1), 0, 0, 0)),
                  pl.BlockSpec(bias_block, lambda b, i: (1, 0, 0, 0)),
                  pl.BlockSpec((1, kvw), lambda b, i: (0, 0)),
                  pl.BlockSpec((1, kvw), lambda b, i: (0, 0)),
                  pl.BlockSpec(sink_rows.shape, lambda b, i: (0, 0, 0, 0)),
                  pl.BlockSpec(seg.shape, lambda b, i: (0, 0))],
        out_specs=pl.BlockSpec((None, tq, ATTN_Q_W), lambda b, i: (b, i, 0)),
        out_shape=jax.ShapeDtypeStruct((B, S, ATTN_Q_W), BF16),
        compiler_params=_params("parallel", "parallel"),
        name="attn",
    )(proj3, proj3, proj3, proj3, proj3, biasm, biasm, jnp.tile(q_gain, (1, GQA_GROUP)),
      jnp.tile(k_gain, (1, N_KV_HEADS)), sink_rows, seg)


def _split3(x):
    hi = x.astype(BF16)
    r1 = x - hi.astype(F32)
    mid = r1.astype(BF16)
    lo = (r1 - mid.astype(F32)).astype(BF16)
    return hi, mid, lo


def _gla_kernel(q_ref, k_ref, v_ref, r_ref, lr_ref, wgk_ref, bgk_ref, gain_ref, o_ref, state_ref):
    R, C = GLA_ROWS, GLA_CHUNK

    @pl.when(pl.program_id(2) == 0)
    def _():
        state_ref[...] = jnp.zeros_like(state_ref)

    z = _dot(lr_ref[...], wgk_ref[...]) + bgk_ref[...]
    gl = (jnp.minimum(z, 0.0) - jnp.log1p(jnp.exp(-jnp.abs(z)))) * (1.0 / GLA_NORMALIZER)

    row = lax.broadcasted_iota(jnp.int32, (R, R), 0)
    col = lax.broadcasted_iota(jnp.int32, (R, R), 1)
    causal = (row >= col) & ((row // C) == (col // C))
    tril = causal.astype(BF16)
    hi, mid, lo = _split3(gl)
    g = _dot(tril, hi) + _dot(tril, mid) + _dot(tril, lo)

    q = q_ref[...].astype(F32) * (GLA_DK ** -0.5)
    k = k_ref[...].astype(F32)
    v = v_ref[...]
    q_dec = (q * jnp.exp(g)).astype(BF16)
    k_inv = (k * jnp.exp(-g)).astype(BF16)
    a = jnp.where(causal, _dot_nt(q_dec, k_inv), 0.0)
    o = _dot(a.astype(BF16), v)

    eye = lax.broadcasted_iota(jnp.int32, (GLA_DK, GLA_DK), 0) == lax.broadcasted_iota(
        jnp.int32, (GLA_DK, GLA_DK), 1)
    o_inter = []
    for c in range(R // C):
        rows = slice(c * C, (c + 1) * C)
        g_last = g[(c + 1) * C - 1:(c + 1) * C]
        k_to_end = (k[rows] * jnp.exp(g_last - g[rows])).astype(BF16)
        state = state_ref[...]
        o_inter.append(_dot(q_dec[rows], state.astype(BF16)))
        decay = jnp.exp(g_last)
        decay_col = jnp.sum(jnp.where(eye, jnp.broadcast_to(decay, eye.shape), 0.0), axis=1, keepdims=True)
        state_ref[...] = decay_col * state + _dot_tn(k_to_end, v[rows])
    o = o + jnp.concatenate(o_inter, axis=0)

    r = r_ref[...].astype(F32)
    o_ref[...] = (_rms(o, gain_ref[...]) * (r * jax.nn.sigmoid(r))).astype(BF16)


def _gla(proj3, wgk, bgk, gain):
    B, S, _ = proj3.shape
    R = GLA_ROWS
    dk, dv = GLA_DK, GLA_DV
    return pl.pallas_call(
        _gla_kernel,
        grid=(B, GLA_HEADS, S // R),
        in_specs=[pl.BlockSpec((None, R, dk), lambda b, h, t: (b, t, COL_GQ // dk + h)),
                  pl.BlockSpec((None, R, dk), lambda b, h, t: (b, t, COL_GK // dk + h)),
                  pl.BlockSpec((None, R, dv), lambda b, h, t: (b, t, COL_GV // dv + h)),
                  pl.BlockSpec((None, R, dv), lambda b, h, t: (b, t, COL_GR // dv + h)),
                  pl.BlockSpec((None, R, LR_PAD), lambda b, h, t: (b, t, COL_LR // LR_PAD)),
                  pl.BlockSpec((LR_PAD, dk), lambda b, h, t: (0, h)),
                  pl.BlockSpec((1, dk), lambda b, h, t: (0, h)),
                  pl.BlockSpec((1, dv), lambda b, h, t: (0, 0))],
        out_specs=pl.BlockSpec((None, R, dv), lambda b, h, t: (b, t, h)),
        out_shape=jax.ShapeDtypeStruct((B, S, GLA_V_W), BF16),
        scratch_shapes=[pltpu.VMEM((dk, dv), F32)],
        compiler_params=_params("parallel", "parallel", "arbitrary"),
        name="gla",
    )(proj3, proj3, proj3, proj3, proj3, wgk, bgk, gain)


def _merge_kernel(ya_ref, yg_ref, ga_ref, gb_ref, x_ref, mod_ref, g2_ref, wa_ref, wg_ref, wo_ref,
                  x1_ref, h2_ref):
    ga = jax.nn.sigmoid(ga_ref[...].astype(F32))
    gb = jax.nn.sigmoid(gb_ref[...].astype(F32))
    merged = ga * _dot(ya_ref[...], wa_ref[...]) + gb * _dot(yg_ref[...], wg_ref[...])
    x1 = x_ref[...] + mod_ref[2:3, :] * _dot(merged.astype(BF16), wo_ref[...])
    x1_ref[...] = x1
    h2_ref[...] = (_rms(x1, g2_ref[...]) * (1.0 + mod_ref[4:5, :]) + mod_ref[3:4, :]).astype(BF16)


def _merge(ya, yg, proj, x2, mod3, gain2, wa, wg, wo):
    T, D = x2.shape
    S = T // mod3.shape[0]
    tm = MERGE_TM
    once = pl.Buffered(1)
    return pl.pallas_call(
        _merge_kernel,
        grid=(T // tm,),
        in_specs=[pl.BlockSpec((tm, ATTN_Q_W), lambda i: (i, 0)),
                  pl.BlockSpec((tm, GLA_V_W), lambda i: (i, 0)),
                  pl.BlockSpec((tm, D), lambda i: (i, COL_GA // D)),
                  pl.BlockSpec((tm, D), lambda i: (i, COL_GB // D)),
                  pl.BlockSpec((tm, D), lambda i: (i, 0)),
                  pl.BlockSpec((None, 6, D), lambda i: (i // (S // tm), 0, 0)),
                  pl.BlockSpec((1, D), lambda i: (0, 0)),
                  pl.BlockSpec((ATTN_Q_W, D), lambda i: (0, 0), pipeline_mode=once),
                  pl.BlockSpec((GLA_V_W, D), lambda i: (0, 0), pipeline_mode=once),
                  pl.BlockSpec((D, D), lambda i: (0, 0), pipeline_mode=once)],
        out_specs=[pl.BlockSpec((tm, D), lambda i: (i, 0)),
                   pl.BlockSpec((tm, D), lambda i: (i, 0))],
        out_shape=[jax.ShapeDtypeStruct((T, D), F32), jax.ShapeDtypeStruct((T, D), BF16)],
        compiler_params=_params("parallel"),
        name="merge",
    )(ya, yg, proj, proj, x2, mod3, gain2, wa, wg, wo)


def _ffn_up_kernel(h_ref, halo_ref, wa_ref, wb_ref, cwa_ref, cwb_ref, cba_ref, cbb_ref, o_ref,
                   hs_ref, ua_ref, ub_ref, *, tiles_per_seq):
    tm, halo = FFN_TM, FFN_HALO

    @pl.when(pl.program_id(1) == 0)
    def _():
        first = (pl.program_id(0) % tiles_per_seq) == 0
        hs_ref[0:halo, :] = jnp.where(first, jnp.zeros_like(halo_ref), halo_ref[...])
        hs_ref[halo:, :] = h_ref[...]

    ua_ref[...] = _dot(hs_ref[...], wa_ref[...])
    ub_ref[...] = _dot(hs_ref[...], wb_ref[...])

    def conv(u_ref, cw_ref, cb_ref, r0, n):
        y = cb_ref[...] + cw_ref[0:1, :] * u_ref[halo - 2 + r0:halo - 2 + r0 + n, :]
        y = y + cw_ref[1:2, :] * u_ref[halo - 1 + r0:halo - 1 + r0 + n, :]
        return y + cw_ref[2:3, :] * u_ref[halo + r0:halo + r0 + n, :]

    rc = 256
    for r0 in range(0, tm, rc):
        ya = conv(ua_ref, cwa_ref, cba_ref, r0, rc)
        yb = conv(ub_ref, cwb_ref, cbb_ref, r0, rc)
        o_ref[r0:r0 + rc, :] = (ya * jax.nn.sigmoid(ya) * yb).astype(BF16)


def _ffn_up(h2, S, w_up, conv_w, conv_b):
    T, D = h2.shape
    tm, tn, halo = FFN_TM, FFN_TN, FFN_HALO
    nj = D_FF // tn
    return pl.pallas_call(
        functools.partial(_ffn_up_kernel, tiles_per_seq=S // tm),
        grid=(T // tm, D_FF // tn),
        in_specs=[pl.BlockSpec((tm, D), lambda i, j: (i, 0)),
                  pl.BlockSpec((halo, D), lambda i, j: (jnp.maximum(i * (tm // halo) - 1, 0), 0)),
                  pl.BlockSpec((D, tn), lambda i, j: (0, j)),
                  pl.BlockSpec((D, tn), lambda i, j: (0, j + nj)),
                  pl.BlockSpec((3, tn), lambda i, j: (0, j)),
                  pl.BlockSpec((3, tn), lambda i, j: (0, j + nj)),
                  pl.BlockSpec((1, tn), lambda i, j: (0, j)),
                  pl.BlockSpec((1, tn), lambda i, j: (0, j + nj))],
        out_specs=pl.BlockSpec((tm, tn), lambda i, j: (i, j)),
        out_shape=jax.ShapeDtypeStruct((T, D_FF), BF16),
        scratch_shapes=[pltpu.VMEM((tm + halo, D), BF16),
                        pltpu.VMEM((tm + halo, tn), F32),
                        pltpu.VMEM((tm + halo, tn), F32)],
        compiler_params=_params("parallel", "arbitrary"),
        name="ffn_up",
    )(h2, h2, w_up, w_up, conv_w, conv_w, conv_b, conv_b)


def _ffn_down_kernel(a_ref, w_ref, x_ref, mod_ref, o_ref):
    o_ref[...] = x_ref[...] + mod_ref[5:6, :] * _dot(a_ref[...], w_ref[...])


def _ffn_down(act, wd, x1, mod3):
    T, D = x1.shape
    S = T // mod3.shape[0]
    tm, tn = DOWN_TM, DOWN_TN
    return pl.pallas_call(
        _ffn_down_kernel,
        grid=(T // tm, D // tn),
        in_specs=[pl.BlockSpec((tm, D_FF), lambda i, j: (i, 0)),
                  pl.BlockSpec((D_FF, tn), lambda i, j: (0, j)),
                  pl.BlockSpec((tm, tn), lambda i, j: (i, j)),
                  pl.BlockSpec((None, 6, tn), lambda i, j: (i // (S // tm), 0, j))],
        out_specs=pl.BlockSpec((tm, tn), lambda i, j: (i, j)),
        out_shape=jax.ShapeDtypeStruct((T, D), F32),
        compiler_params=_params("parallel", "arbitrary"),
        name="ffn_down",
    )(act, wd, x1, mod3)


def _layer(x2, B, mod3, biasm, norm1_gain, w_in, q_norm_gain, k_norm_gain, attn_sinks, w_gk_up, b_gk,
           gla_norm_gain, w_branch_attn, w_branch_gla, w_out, norm2_gain, w_ffn_up, ffn_conv_w,
           ffn_conv_b, w_ffn_down):
    T, D = x2.shape
    S = T // B
    o_lr = ATTN_Q_W + 2 * ATTN_KV_W + 2 * GLA_K_W + 2 * GLA_V_W
    o_ga = o_lr + GLA_LOWRANK
    o_gb = o_ga + D
    o_gq = ATTN_Q_W + 2 * ATTN_KV_W
    o_gv = o_gq + 2 * GLA_K_W
    w_p = jnp.concatenate(
        [w_in[:, o_ga:o_gb], w_in[:, o_gb:o_gb + D],
         w_in[:, o_gv:o_gv + 2 * GLA_V_W],
         w_in[:, o_gq:o_gq + 2 * GLA_K_W],
         w_in[:, 0:o_gq],
         w_in[:, o_lr:o_ga],
         jnp.zeros((D, PROJ_W - COL_LR - GLA_LOWRANK), w_in.dtype)], axis=1).astype(BF16)
    wgk = jnp.concatenate([w_gk_up, jnp.zeros((LR_PAD - GLA_LOWRANK, GLA_K_W), w_gk_up.dtype)],
                          axis=0).astype(BF16)

    proj = _inproj(x2, mod3, norm1_gain.reshape(1, D), w_p)
    proj3 = proj.reshape(B, S, PROJ_W)
    ya = _attn(proj3, biasm, q_norm_gain.reshape(1, HEAD_DIM), k_norm_gain.reshape(1, HEAD_DIM), attn_sinks)
    yg = _gla(proj3, wgk, b_gk.reshape(1, GLA_K_W), gla_norm_gain.reshape(1, GLA_DV))
    x1, h2 = _merge(ya.reshape(T, ATTN_Q_W), yg.reshape(T, GLA_V_W), proj, x2, mod3,
                    norm2_gain.reshape(1, D), w_branch_attn.astype(BF16), w_branch_gla.astype(BF16),
                    w_out.astype(BF16))
    act = _ffn_up(h2, S, w_ffn_up.astype(BF16), ffn_conv_w, ffn_conv_b.reshape(1, 2 * D_FF))
    return _ffn_down(act, w_ffn_down.astype(BF16), x1, mod3)


def kernel(x, c, rel_bias_table, w_ada, b_ada, norm1_gain, w_in, q_norm_gain, k_norm_gain, attn_sinks,
           w_gk_up, b_gk, gla_norm_gain, w_branch_attn, w_branch_gla, w_out, norm2_gain, w_ffn_up,
           ffn_conv_w, ffn_conv_b, w_ffn_down):
    B, S, D = x.shape
    depth = w_in.shape[0]
    biasm = _relbias(rel_bias_table)
    x2 = x.reshape(B * S, D)
    for l in range(depth):
        mod3 = _adaln(c, w_ada[l], b_ada[l]).reshape(B, 6, D)
        x2 = _layer(x2, B, mod3, biasm, norm1_gain[l], w_in[l], q_norm_gain[l], k_norm_gain[l],
                    attn_sinks[l], w_gk_up[l], b_gk[l], gla_norm_gain[l], w_branch_attn[l],
                    w_branch_gla[l], w_out[l], norm2_gain[l], w_ffn_up[l], ffn_conv_w[l], ffn_conv_b[l],
                    w_ffn_down[l])
    return x2.reshape(B, S, D)
```

```python
import functools
import math

import numpy as np
import jax
import jax.numpy as jnp
from jax import lax
from jax.experimental import pallas as pl
from jax.experimental.pallas import tpu as pltpu

F32 = jnp.float32
BF16 = jnp.bfloat16

D_MODEL = 2048
N_Q_HEADS = 16
N_KV_HEADS = 4
GQA_GROUP = N_Q_HEADS // N_KV_HEADS
HEAD_DIM = 64
WINDOW = 128
ATTN_BLOCK = 128
N_BUCKETS = 32
MAX_DISTANCE = 128
GLA_HEADS = 4
GLA_DK = 256
GLA_DV = 512
GLA_LOWRANK = 16
GLA_NORMALIZER = 16.0
GLA_CHUNK = 64
D_FF = 5632
EPS = 1e-6
NEG_INF = -1e30

ATTN_Q_W = N_Q_HEADS * HEAD_DIM
ATTN_KV_W = N_KV_HEADS * HEAD_DIM
GLA_K_W = GLA_HEADS * GLA_DK
GLA_V_W = GLA_HEADS * GLA_DV

COL_GA = 0
COL_GB = COL_GA + D_MODEL
COL_GV = COL_GB + D_MODEL
COL_GR = COL_GV + GLA_V_W
COL_GQ = COL_GR + GLA_V_W
COL_GK = COL_GQ + GLA_K_W
COL_AQ = COL_GK + GLA_K_W
COL_AK = COL_AQ + ATTN_Q_W
COL_AV = COL_AK + ATTN_KV_W
COL_LR = COL_AV + ATTN_KV_W
LR_PAD = 128
PROJ_W = 12288

VMEM_LIMIT = 60 * 1024 * 1024

ADALN_TN = 1024
INPROJ_TM, INPROJ_TN = 1024, 1024
ATTN_TQ = 256
GLA_ROWS = 256
GLA_HEADS_PER_STEP = 2
MERGE_TM = 256
FFN_TM, FFN_TN = 1024, 512
FFN_HALO = 16
DOWN_TM, DOWN_TN = 1024, 512


def _params(*sem):
    return pltpu.CompilerParams(dimension_semantics=sem, vmem_limit_bytes=VMEM_LIMIT)


def _dot(a, b):
    return jnp.dot(a, b, preferred_element_type=F32)


def _dot_nt(a, b):
    return lax.dot_general(a, b, (((1,), (1,)), ((), ())), preferred_element_type=F32)


def _dot_tn(a, b):
    return lax.dot_general(a, b, (((0,), (0,)), ((), ())), preferred_element_type=F32)


def _rms(x, gain):
    return x * lax.rsqrt(jnp.mean(x * x, axis=-1, keepdims=True) + EPS) * gain


def _adaln_kernel(c_ref, w_ref, b_ref, o_ref):
    c = c_ref[...]
    ca = c * jax.nn.sigmoid(c)
    o_ref[...] = _dot(ca.astype(BF16), w_ref[...].astype(BF16)) + b_ref[...]


def _adaln(c, w_ada, b_ada):
    B, D = c.shape
    N = w_ada.shape[1]
    return pl.pallas_call(
        _adaln_kernel,
        grid=(N // ADALN_TN,),
        in_specs=[pl.BlockSpec((B, D), lambda j: (0, 0)),
                  pl.BlockSpec((D, ADALN_TN), lambda j: (0, j)),
                  pl.BlockSpec((1, ADALN_TN), lambda j: (0, j))],
        out_specs=pl.BlockSpec((B, ADALN_TN), lambda j: (0, j)),
        out_shape=jax.ShapeDtypeStruct((B, N), F32),
        compiler_params=_params("parallel"),
        name="adaln",
    )(c, w_ada, b_ada.reshape(1, N))


def _bucket_table():
    j = np.arange(2 * ATTN_BLOCK)[:, None]
    i = np.arange(ATTN_BLOCK)[None, :]
    dist = i + ATTN_BLOCK - j
    max_exact = N_BUCKETS // 2
    d = np.maximum(dist, 0)
    ratio = np.log(np.maximum(d, 1).astype(np.float32) / np.float32(max_exact)) / np.float32(
        math.log(MAX_DISTANCE / max_exact))
    large = max_exact + (ratio.astype(np.float32) * np.float32(N_BUCKETS - max_exact)).astype(np.int32)
    large = np.minimum(large, N_BUCKETS - 1)
    bucket = np.where(d < max_exact, d, large)
    in_window = (dist >= 0) & (dist < WINDOW)
    return np.where(in_window, bucket, -1).astype(np.int32)


def _relbias_kernel(tab_ref, bkt_ref, o_ref):
    h = pl.program_id(0)
    bkt = bkt_ref[...]
    acc = jnp.zeros(bkt.shape, F32)
    for b in range(N_BUCKETS):
        acc = jnp.where(bkt == b, tab_ref[b, h], acc)
    regular = jnp.where(bkt >= 0, acc, NEG_INF)
    key = lax.broadcasted_iota(jnp.int32, bkt.shape, 0)
    o_ref[0] = jnp.where(key >= ATTN_BLOCK, regular, NEG_INF)
    o_ref[1] = regular


def _relbias(rel_bias_table):
    bkt = jnp.asarray(_bucket_table())
    nk, nq = bkt.shape
    return pl.pallas_call(
        _relbias_kernel,
        grid=(N_Q_HEADS,),
        in_specs=[pl.BlockSpec(memory_space=pltpu.SMEM),
                  pl.BlockSpec(bkt.shape, lambda h: (0, 0))],
        out_specs=pl.BlockSpec((2, None, nk, nq), lambda h: (0, h // GQA_GROUP, h % 2, (h // 2) % 2)),
        out_shape=jax.ShapeDtypeStruct((2, N_KV_HEADS, 2 * nk, 2 * nq), F32),
        compiler_params=_params("parallel"),
        name="relbias",
    )(rel_bias_table, bkt)


SRC_GQ = ATTN_Q_W + 2 * ATTN_KV_W
SRC_GV = SRC_GQ + 2 * GLA_K_W
SRC_LR = SRC_GV + 2 * GLA_V_W
SRC_GA = SRC_LR + GLA_LOWRANK
SRC_END = SRC_GA + 2 * D_MODEL
WPACK_ROWS = 256


def _wpack_kernel(w_ref, o_ref):
    gates = w_ref[:, SRC_LR:SRC_END][:, GLA_LOWRANK:]
    o_ref[:, COL_GA:COL_GV] = gates.astype(BF16)
    o_ref[:, COL_GV:COL_GQ] = w_ref[:, SRC_GV:SRC_LR].astype(BF16)
    o_ref[:, COL_GQ:COL_AQ] = w_ref[:, SRC_GQ:SRC_GV].astype(BF16)
    o_ref[:, COL_AQ:COL_LR] = w_ref[:, 0:SRC_GQ].astype(BF16)
    lr = w_ref[:, SRC_LR:SRC_LR + LR_PAD]
    lane = lax.broadcasted_iota(jnp.int32, lr.shape, 1)
    o_ref[:, COL_LR:COL_LR + LR_PAD] = jnp.where(lane < GLA_LOWRANK, lr, 0.0).astype(BF16)
    o_ref[:, COL_LR + LR_PAD:] = jnp.zeros((WPACK_ROWS, PROJ_W - COL_LR - LR_PAD), BF16)


def _wpack(w_in):
    D, n_in = w_in.shape
    return pl.pallas_call(
        _wpack_kernel,
        grid=(D // WPACK_ROWS,),
        in_specs=[pl.BlockSpec((WPACK_ROWS, n_in), lambda i: (i, 0))],
        out_specs=pl.BlockSpec((WPACK_ROWS, PROJ_W), lambda i: (i, 0)),
        out_shape=jax.ShapeDtypeStruct((D, PROJ_W), BF16),
        compiler_params=_params("parallel"),
        name="wpack",
    )(w_in)


def _inproj_kernel(x_ref, mod_ref, g_ref, w_ref, o_ref, h_ref):
    @pl.when(pl.program_id(1) == 0)
    def _():
        scale = 1.0 + mod_ref[1:2, :]
        shift = mod_ref[0:1, :]
        gain = g_ref[...]

        def body(r, carry):
            rows = pl.ds(pl.multiple_of(r * 128, 128), 128)
            h_ref[rows, :] = (_rms(x_ref[rows, :], gain) * scale + shift).astype(BF16)
            return carry

        lax.fori_loop(0, INPROJ_TM // 128, body, 0)

    o_ref[...] = _dot(h_ref[...], w_ref[...]).astype(BF16)


def _inproj(x2, mod3, gain, w):
    T, D = x2.shape
    S = T // mod3.shape[0]
    tm, tn = INPROJ_TM, INPROJ_TN
    return pl.pallas_call(
        _inproj_kernel,
        grid=(T // tm, PROJ_W // tn),
        in_specs=[pl.BlockSpec((tm, D), lambda i, j: (i, 0)),
                  pl.BlockSpec((None, 6, D), lambda i, j: (i // (S // tm), 0, 0)),
                  pl.BlockSpec((1, D), lambda i, j: (0, 0)),
                  pl.BlockSpec((D, tn), lambda i, j: (0, j))],
        out_specs=pl.BlockSpec((tm, tn), lambda i, j: (i, j)),
        out_shape=jax.ShapeDtypeStruct((T, PROJ_W), BF16),
        scratch_shapes=[pltpu.VMEM((tm, D), BF16)],
        compiler_params=_params("parallel", "arbitrary"),
        name="inproj",
    )(x2, mod3, gain, w)


def _attn_kernel(q_ref, kc_ref, kp_ref, vc_ref, vp_ref, b0_ref, b1_ref, qg_ref, kg_ref, sink_ref, seg_ref,
                 o_ref):
    nsub = ATTN_TQ // ATTN_BLOCK
    blk, dh = ATTN_BLOCK, HEAD_DIM
    seg = seg_ref[...]

    def normed(x, gain):
        ssq = _dot((x * x).astype(BF16), seg)
        return x * lax.rsqrt(ssq * (1.0 / dh) + EPS) * gain

    kn = normed(jnp.concatenate([kp_ref[...], kc_ref[...]], axis=0).astype(F32), kg_ref[...])
    vt = jnp.concatenate([vp_ref[...], vc_ref[...]], axis=0).astype(F32).T.astype(BF16)
    lane = lax.broadcasted_iota(jnp.int32, (kn.shape[0], 2 * dh), 1)
    qgain = qg_ref[...] * (dh ** -0.5)

    for g in range(N_KV_HEADS):
        pair = kn[:, (g // 2) * 2 * dh:(g // 2 + 1) * 2 * dh]
        own = jnp.where(lane >= dh if g % 2 else lane < dh, pair, 0.0)
        swapped = pltpu.roll(own, dh, axis=1)
        kz = [own, swapped] if g % 2 == 0 else [swapped, own]
        kz = [z.astype(BF16) for z in kz]
        qn = normed(q_ref[:, g * 4 * dh:(g + 1) * 4 * dh].astype(F32), qgain).astype(BF16)
        for sb in range(nsub):
            bref = b0_ref if sb == 0 else b1_ref
            band = slice(sb * blk, sb * blk + 2 * blk)
            rows = slice(sb * blk, (sb + 1) * blk)
            kband = jnp.concatenate([kz[0][band], kz[1][band]], axis=0)
            qr = jnp.concatenate([qn[rows, 0:2 * dh], qn[rows, 2 * dh:4 * dh]], axis=0)
            s = _dot_nt(kband, qr) + bref[g]
            ps, inv = [], []
            for e in range(2):
                se = s[e * 2 * blk:(e + 1) * 2 * blk]
                sink = sink_ref[g, e]
                m = jnp.maximum(jnp.max(se, axis=0, keepdims=True), sink)
                p = jnp.exp(se - m)
                inv.append(1.0 / (jnp.sum(p, axis=0, keepdims=True) + jnp.exp(sink - m)))
                ps.append(p.astype(BF16))
            ot = _dot(vt[g * dh:(g + 1) * dh, band], jnp.concatenate(ps, axis=1))
            ot = ot * jnp.concatenate(inv, axis=1)
            o = jnp.concatenate([ot[:, 0:2 * blk], ot[:, 2 * blk:4 * blk]], axis=0).T
            o_ref[rows, (2 * g) * 2 * dh:(2 * g + 1) * 2 * dh] = o[0:blk].astype(BF16)
            o_ref[rows, (2 * g + 1) * 2 * dh:(2 * g + 2) * 2 * dh] = o[blk:2 * blk].astype(BF16)


def _attn(proj3, biasm, q_gain, k_gain, sinks):
    B, S, _ = proj3.shape
    tq, blk = ATTN_TQ, ATTN_BLOCK
    nsub = tq // blk
    kvw = ATTN_KV_W
    seg_id = np.arange(kvw) // HEAD_DIM
    seg = jnp.asarray(seg_id[:, None] == seg_id[None, :], dtype=BF16)
    sink_rows = jnp.repeat(sinks.reshape(N_KV_HEADS, 2, 2).transpose(0, 2, 1), blk, axis=-1)
    sink_rows = sink_rows.reshape(N_KV_HEADS, 2, 1, 2 * blk)

    def prev(i):
        return jnp.maximum(i * nsub - 1, 0)

    bias_block = (None,) + biasm.shape[1:]
    return pl.pallas_call(
        _attn_kernel,
        grid=(B, S // tq),
        in_specs=[pl.BlockSpec((None, tq, ATTN_Q_W), lambda b, i: (b, i, COL_AQ // ATTN_Q_W)),
                  pl.BlockSpec((None, tq, kvw), lambda b, i: (b, i, COL_AK // kvw)),
                  pl.BlockSpec((None, blk, kvw), lambda b, i: (b, prev(i), COL_AK // kvw)),
                  pl.BlockSpec((None, tq, kvw), lambda b, i: (b, i, COL_AV // kvw)),
                  pl.BlockSpec((None, blk, kvw), lambda b, i: (b, prev(i), COL_AV // kvw)),
                  pl.BlockSpec(bias_block, lambda b, i: (jnp.minimum(i, 1), 0, 0, 0)),
                  pl.BlockSpec(bias_block, lambda b, i: (1, 0, 0, 0)),
                  pl.BlockSpec((1, kvw), lambda b, i: (0, 0)),
                  pl.BlockSpec((1, kvw), lambda b, i: (0, 0)),
                  pl.BlockSpec(sink_rows.shape, lambda b, i: (0, 0, 0, 0)),
                  pl.BlockSpec(seg.shape, lambda b, i: (0, 0))],
        out_specs=pl.BlockSpec((None, tq, ATTN_Q_W), lambda b, i: (b, i, 0)),
        out_shape=jax.ShapeDtypeStruct((B, S, ATTN_Q_W), BF16),
        compiler_params=_params("parallel", "parallel"),
        name="attn",
    )(proj3, proj3, proj3, proj3, proj3, biasm, biasm, jnp.tile(q_gain, (1, GQA_GROUP)),
      jnp.tile(k_gain, (1, N_KV_HEADS)), sink_rows, seg)


def _split2(x):
    hi = x.astype(BF16)
    return hi, (x - hi.astype(F32)).astype(BF16)


def _gla_head(q, k, v, r, lr, wgk, bgk, gain, state_ref, masks):
    C = GLA_CHUNK
    tril, m_same, m_next, m_far, eye = masks
    z = _dot(lr, wgk) + bgk
    gl = (jnp.minimum(z, 0.0) - jnp.log(1.0 + jnp.exp(-jnp.abs(z)))) * (1.0 / GLA_NORMALIZER)
    hi, lo = _split2(gl)
    g = _dot(tril, hi) + _dot(tril, lo)
    t = [g[(c + 1) * C - 1:(c + 1) * C] for c in range(4)]
    t_rows = jnp.concatenate([jnp.broadcast_to(tc, (C, tc.shape[1])) for tc in t], axis=0)

    q_dec = q.astype(F32) * (GLA_DK ** -0.5) * jnp.exp(g)
    kf = k.astype(F32)
    k_inv = (kf * jnp.exp(-g)).astype(BF16)
    k_end = kf * jnp.exp(t_rows - g)
    qd = [q_dec[c * C:(c + 1) * C] for c in range(4)]
    ke = [k_end[c * C:(c + 1) * C] for c in range(4)]

    def rows(parts):
        return jnp.concatenate(parts, axis=0).astype(BF16)

    q_b, k_b = q_dec.astype(BF16), k_end.astype(BF16)
    q_far = rows([qd[0], qd[1], qd[2], qd[3] * jnp.exp(t[2])])
    k_far = rows([ke[0] * jnp.exp(t[1]), ke[1], ke[2], ke[3]])
    q_abs = rows([qd[0], qd[1] * jnp.exp(t[0]), qd[2] * jnp.exp(t[0] + t[1]), qd[3] * jnp.exp(t[0] + t[1] + t[2])])
    k_abs = rows([ke[0] * jnp.exp(t[1] + t[2] + t[3]), ke[1] * jnp.exp(t[2] + t[3]), ke[2] * jnp.exp(t[3]), ke[3]])

    a = jnp.where(m_same, _dot_nt(q_b, k_inv),
                  jnp.where(m_next, _dot_nt(q_b, k_b), jnp.where(m_far, _dot_nt(q_far, k_far), 0.0)))
    state = state_ref[...]
    o = _dot(a.astype(BF16), v) + _dot(q_abs, state.astype(BF16))

    decay = jnp.exp(t[0] + t[1] + t[2] + t[3])
    decay_col = jnp.sum(jnp.where(eye, jnp.broadcast_to(decay, eye.shape), 0.0), axis=1, keepdims=True)
    state_ref[...] = decay_col * state + _dot_tn(k_abs, v)

    rf = r.astype(F32)
    return (_rms(o, gain) * (rf * jax.nn.sigmoid(rf))).astype(BF16)


def _gla_kernel(q_ref, k_ref, v_ref, r_ref, lr_ref, wgk_ref, bgk_ref, gain_ref, o_ref, state_ref):
    R, C, dk, dv = GLA_ROWS, GLA_CHUNK, GLA_DK, GLA_DV

    @pl.when(pl.program_id(2) == 0)
    def _():
        state_ref[...] = jnp.zeros_like(state_ref)

    row = lax.broadcasted_iota(jnp.int32, (R, R), 0)
    col = lax.broadcasted_iota(jnp.int32, (R, R), 1)
    ci, cj = row // C, col // C
    m_same = (row >= col) & (ci == cj)
    m_next = (ci == cj + 1) & (ci != 2)
    m_far = (ci >= 2) & (cj <= 1)
    eye = lax.broadcasted_iota(jnp.int32, (dk, dk), 0) == lax.broadcasted_iota(jnp.int32, (dk, dk), 1)
    masks = (m_same.astype(BF16), m_same, m_next, m_far, eye)

    lr = lr_ref[...]
    for h in range(GLA_HEADS_PER_STEP):
        ks, vs = slice(h * dk, (h + 1) * dk), slice(h * dv, (h + 1) * dv)
        o_ref[:, vs] = _gla_head(q_ref[:, ks], k_ref[:, ks], v_ref[:, vs], r_ref[:, vs], lr,
                                 wgk_ref[:, ks], bgk_ref[:, ks], gain_ref[...], state_ref.at[h], masks)


def _gla(proj3, wgk, bgk, gain):
    B, S, _ = proj3.shape
    R, hps = GLA_ROWS, GLA_HEADS_PER_STEP
    dk, dv = hps * GLA_DK, hps * GLA_DV
    return pl.pallas_call(
        _gla_kernel,
        grid=(B, GLA_HEADS // hps, S // R),
        in_specs=[pl.BlockSpec((None, R, dk), lambda b, h, t: (b, t, COL_GQ // dk + h)),
                  pl.BlockSpec((None, R, dk), lambda b, h, t: (b, t, COL_GK // dk + h)),
                  pl.BlockSpec((None, R, dv), lambda b, h, t: (b, t, COL_GV // dv + h)),
                  pl.BlockSpec((None, R, dv), lambda b, h, t: (b, t, COL_GR // dv + h)),
                  pl.BlockSpec((None, R, LR_PAD), lambda b, h, t: (b, t, COL_LR // LR_PAD)),
                  pl.BlockSpec((LR_PAD, dk), lambda b, h, t: (0, h)),
                  pl.BlockSpec((1, dk), lambda b, h, t: (0, h)),
                  pl.BlockSpec((1, GLA_DV), lambda b, h, t: (0, 0))],
        out_specs=pl.BlockSpec((None, R, dv), lambda b, h, t: (b, t, h)),
        out_shape=jax.ShapeDtypeStruct((B, S, GLA_V_W), BF16),
        scratch_shapes=[pltpu.VMEM((hps, GLA_DK, GLA_DV), F32)],
        compiler_params=_params("parallel", "parallel", "arbitrary"),
        name="gla",
    )(proj3, proj3, proj3, proj3, proj3, wgk, bgk, gain)


def _merge_kernel(ya_ref, yg_ref, ga_ref, gb_ref, x_ref, mod_ref, g2_ref, wa_ref, wg_ref, wo_ref,
                  x1_ref, h2_ref):
    ga = jax.nn.sigmoid(ga_ref[...].astype(F32))
    gb = jax.nn.sigmoid(gb_ref[...].astype(F32))
    merged = ga * _dot(ya_ref[...], wa_ref[...]) + gb * _dot(yg_ref[...], wg_ref[...])
    x1 = x_ref[...] + mod_ref[2:3, :] * _dot(merged.astype(BF16), wo_ref[...])
    x1_ref[...] = x1
    h2_ref[...] = (_rms(x1, g2_ref[...]) * (1.0 + mod_ref[4:5, :]) + mod_ref[3:4, :]).astype(BF16)


def _merge(ya, yg, proj, x2, mod3, gain2, wa, wg, wo):
    T, D = x2.shape
    S = T // mod3.shape[0]
    tm = MERGE_TM
    once = pl.Buffered(1)
    return pl.pallas_call(
        _merge_kernel,
        grid=(T // tm,),
        in_specs=[pl.BlockSpec((tm, ATTN_Q_W), lambda i: (i, 0)),
                  pl.BlockSpec((tm, GLA_V_W), lambda i: (i, 0)),
                  pl.BlockSpec((tm, D), lambda i: (i, COL_GA // D)),
                  pl.BlockSpec((tm, D), lambda i: (i, COL_GB // D)),
                  pl.BlockSpec((tm, D), lambda i: (i, 0)),
                  pl.BlockSpec((None, 6, D), lambda i: (i // (S // tm), 0, 0)),
                  pl.BlockSpec((1, D), lambda i: (0, 0)),
                  pl.BlockSpec((ATTN_Q_W, D), lambda i: (0, 0), pipeline_mode=once),
                  pl.BlockSpec((GLA_V_W, D), lambda i: (0, 0), pipeline_mode=once),
                  pl.BlockSpec((D, D), lambda i: (0, 0), pipeline_mode=once)],
        out_specs=[pl.BlockSpec((tm, D), lambda i: (i, 0)),
                   pl.BlockSpec((tm, D), lambda i: (i, 0))],
        out_shape=[jax.ShapeDtypeStruct((T, D), F32), jax.ShapeDtypeStruct((T, D), BF16)],
        compiler_params=_params("parallel"),
        name="merge",
    )(ya, yg, proj, proj, x2, mod3, gain2, wa, wg, wo)


def _ffn_up_kernel(h_ref, halo_ref, wa_ref, wb_ref, cwa_ref, cwb_ref, cba_ref, cbb_ref, o_ref,
                   hs_ref, ua_ref, ub_ref, *, tiles_per_seq):
    tm, halo = FFN_TM, FFN_HALO

    @pl.when(pl.program_id(1) == 0)
    def _():
        first = (pl.program_id(0) % tiles_per_seq) == 0
        hs_ref[0:halo, :] = jnp.where(first, jnp.zeros_like(halo_ref), halo_ref[...])
        hs_ref[halo:, :] = h_ref[...]

    ua_ref[...] = _dot(hs_ref[...], wa_ref[...])
    ub_ref[...] = _dot(hs_ref[...], wb_ref[...])

    def conv(u_ref, cw_ref, cb_ref, r0, n):
        y = cb_ref[...] + cw_ref[0:1, :] * u_ref[halo - 2 + r0:halo - 2 + r0 + n, :]
        y = y + cw_ref[1:2, :] * u_ref[halo - 1 + r0:halo - 1 + r0 + n, :]
        return y + cw_ref[2:3, :] * u_ref[halo + r0:halo + r0 + n, :]

    rc = 256
    for r0 in range(0, tm, rc):
        ya = conv(ua_ref, cwa_ref, cba_ref, r0, rc)
        yb = conv(ub_ref, cwb_ref, cbb_ref, r0, rc)
        o_ref[r0:r0 + rc, :] = (ya * jax.nn.sigmoid(ya) * yb).astype(BF16)


def _ffn_up(h2, S, w_up, conv_w, conv_b):
    T, D = h2.shape
    tm, tn, halo = FFN_TM, FFN_TN, FFN_HALO
    nj = D_FF // tn
    return pl.pallas_call(
        functools.partial(_ffn_up_kernel, tiles_per_seq=S // tm),
        grid=(T // tm, D_FF // tn),
        in_specs=[pl.BlockSpec((tm, D), lambda i, j: (i, 0)),
                  pl.BlockSpec((halo, D), lambda i, j: (jnp.maximum(i * (tm // halo) - 1, 0), 0)),
                  pl.BlockSpec((D, tn), lambda i, j: (0, j)),
                  pl.BlockSpec((D, tn), lambda i, j: (0, j + nj)),
                  pl.BlockSpec((3, tn), lambda i, j: (0, j)),
                  pl.BlockSpec((3, tn), lambda i, j: (0, j + nj)),
                  pl.BlockSpec((1, tn), lambda i, j: (0, j)),
                  pl.BlockSpec((1, tn), lambda i, j: (0, j + nj))],
        out_specs=pl.BlockSpec((tm, tn), lambda i, j: (i, j)),
        out_shape=jax.ShapeDtypeStruct((T, D_FF), BF16),
        scratch_shapes=[pltpu.VMEM((tm + halo, D), BF16),
                        pltpu.VMEM((tm + halo, tn), F32),
                        pltpu.VMEM((tm + halo, tn), F32)],
        compiler_params=_params("parallel", "arbitrary"),
        name="ffn_up",
    )(h2, h2, w_up, w_up, conv_w, conv_w, conv_b, conv_b)


def _ffn_down_kernel(a_ref, w_ref, x_ref, mod_ref, o_ref):
    o_ref[...] = x_ref[...] + mod_ref[5:6, :] * _dot(a_ref[...], w_ref[...])


def _ffn_down(act, wd, x1, mod3):
    T, D = x1.shape
    S = T // mod3.shape[0]
    tm, tn = DOWN_TM, DOWN_TN
    return pl.pallas_call(
        _ffn_down_kernel,
        grid=(T // tm, D // tn),
        in_specs=[pl.BlockSpec((tm, D_FF), lambda i, j: (i, 0)),
                  pl.BlockSpec((D_FF, tn), lambda i, j: (0, j)),
                  pl.BlockSpec((tm, tn), lambda i, j: (i, j)),
                  pl.BlockSpec((None, 6, tn), lambda i, j: (i // (S // tm), 0, j))],
        out_specs=pl.BlockSpec((tm, tn), lambda i, j: (i, j)),
        out_shape=jax.ShapeDtypeStruct((T, D), F32),
        compiler_params=_params("parallel", "arbitrary"),
        name="ffn_down",
    )(act, wd, x1, mod3)


def _layer(x2, B, mod3, biasm, norm1_gain, w_in, q_norm_gain, k_norm_gain, attn_sinks, w_gk_up, b_gk,
           gla_norm_gain, w_branch_attn, w_branch_gla, w_out, norm2_gain, w_ffn_up, ffn_conv_w,
           ffn_conv_b, w_ffn_down):
    T, D = x2.shape
    S = T // B
    w_p = _wpack(w_in)
    wgk = jnp.concatenate([w_gk_up, jnp.zeros((LR_PAD - GLA_LOWRANK, GLA_K_W), w_gk_up.dtype)],
                          axis=0).astype(BF16)

    proj = _inproj(x2, mod3, norm1_gain.reshape(1, D), w_p)
    proj3 = proj.reshape(B, S, PROJ_W)
    ya = _attn(proj3, biasm, q_norm_gain.reshape(1, HEAD_DIM), k_norm_gain.reshape(1, HEAD_DIM), attn_sinks)
    yg = _gla(proj3, wgk, b_gk.reshape(1, GLA_K_W), gla_norm_gain.reshape(1, GLA_DV))
    x1, h2 = _merge(ya.reshape(T, ATTN_Q_W), yg.reshape(T, GLA_V_W), proj, x2, mod3,
                    norm2_gain.reshape(1, D), w_branch_attn.astype(BF16), w_branch_gla.astype(BF16),
                    w_out.astype(BF16))
    act = _ffn_up(h2, S, w_ffn_up.astype(BF16), ffn_conv_w, ffn_conv_b.reshape(1, 2 * D_FF))
    return _ffn_down(act, w_ffn_down.astype(BF16), x1, mod3)


def kernel(x, c, rel_bias_table, w_ada, b_ada, norm1_gain, w_in, q_norm_gain, k_norm_gain, attn_sinks,
           w_gk_up, b_gk, gla_norm_gain, w_branch_attn, w_branch_gla, w_out, norm2_gain, w_ffn_up,
           ffn_conv_w, ffn_conv_b, w_ffn_down):
    B, S, D = x.shape
    depth = w_in.shape[0]
    biasm = _relbias(rel_bias_table)
    x2 = x.reshape(B * S, D)
    for l in range(depth):
        mod3 = _adaln(c, w_ada[l], b_ada[l]).reshape(B, 6, D)
        x2 = _layer(x2, B, mod3, biasm, norm1_gain[l], w_in[l], q_norm_gain[l], k_norm_gain[l],
                    attn_sinks[l], w_gk_up[l], b_gk[l], gla_norm_gain[l], w_branch_attn[l],
                    w_branch_gla[l], w_out[l], norm2_gain[l], w_ffn_up[l], ffn_conv_w[l], ffn_conv_b[l],
                    w_ffn_down[l])
    return x2.reshape(B, S, D)
```

```python
import functools
import math

import numpy as np
import jax
import jax.numpy as jnp
from jax import lax
from jax.experimental import pallas as pl
from jax.experimental.pallas import tpu as pltpu

F32 = jnp.float32
BF16 = jnp.bfloat16

D_MODEL = 2048
N_Q_HEADS = 16
N_KV_HEADS = 4
GQA_GROUP = N_Q_HEADS // N_KV_HEADS
HEAD_DIM = 64
WINDOW = 128
ATTN_BLOCK = 128
N_BUCKETS = 32
MAX_DISTANCE = 128
GLA_HEADS = 4
GLA_DK = 256
GLA_DV = 512
GLA_LOWRANK = 16
GLA_NORMALIZER = 16.0
GLA_CHUNK = 64
D_FF = 5632
EPS = 1e-6
NEG_INF = -1e30

ATTN_Q_W = N_Q_HEADS * HEAD_DIM
ATTN_KV_W = N_KV_HEADS * HEAD_DIM
GLA_K_W = GLA_HEADS * GLA_DK
GLA_V_W = GLA_HEADS * GLA_DV

COL_GA = 0
COL_GB = COL_GA + D_MODEL
COL_GV = COL_GB + D_MODEL
COL_GR = COL_GV + GLA_V_W
COL_GQ = COL_GR + GLA_V_W
COL_GK = COL_GQ + GLA_K_W
COL_AQ = COL_GK + GLA_K_W
COL_AK = COL_AQ + ATTN_Q_W
COL_AV = COL_AK + ATTN_KV_W
COL_LR = COL_AV + ATTN_KV_W
LR_PAD = 128
PROJ_W = 12288

VMEM_LIMIT = 60 * 1024 * 1024

ADALN_TN = 1024
INPROJ_TM, INPROJ_TN = 1024, 1024
ATTN_TQ = 256
GLA_ROWS = 256
GLA_HEADS_PER_STEP = 2
MERGE_TM, MERGE_SUB = 512, 256
FFN_TM, FFN_TN = 2048, 512
FFN_HALO = 16
FFN_ROW_CHUNKS = (128,) * 16
DOWN_TM = 512


def _params(*sem):
    return pltpu.CompilerParams(dimension_semantics=sem, vmem_limit_bytes=VMEM_LIMIT)


def _dot(a, b):
    return jnp.dot(a, b, preferred_element_type=F32)


def _dot_nt(a, b):
    return lax.dot_general(a, b, (((1,), (1,)), ((), ())), preferred_element_type=F32)


def _dot_tn(a, b):
    return lax.dot_general(a, b, (((0,), (0,)), ((), ())), preferred_element_type=F32)


def _rms(x, gain):
    return x * lax.rsqrt(jnp.mean(x * x, axis=-1, keepdims=True) + EPS) * gain


def _adaln_kernel(c_ref, w_ref, b_ref, o_ref):
    c = c_ref[...]
    ca = c * jax.nn.sigmoid(c)
    o_ref[...] = _dot(ca.astype(BF16), w_ref[...].astype(BF16)) + b_ref[...]


def _adaln(c, w_ada, b_ada):
    B, D = c.shape
    N = w_ada.shape[1]
    return pl.pallas_call(
        _adaln_kernel,
        grid=(N // ADALN_TN,),
        in_specs=[pl.BlockSpec((B, D), lambda j: (0, 0)),
                  pl.BlockSpec((D, ADALN_TN), lambda j: (0, j)),
                  pl.BlockSpec((1, ADALN_TN), lambda j: (0, j))],
        out_specs=pl.BlockSpec((B, ADALN_TN), lambda j: (0, j)),
        out_shape=jax.ShapeDtypeStruct((B, N), F32),
        compiler_params=_params("parallel"),
        name="adaln",
    )(c, w_ada, b_ada.reshape(1, N))


def _bucket_table():
    j = np.arange(2 * ATTN_BLOCK)[:, None]
    i = np.arange(ATTN_BLOCK)[None, :]
    dist = i + ATTN_BLOCK - j
    max_exact = N_BUCKETS // 2
    d = np.maximum(dist, 0)
    ratio = np.log(np.maximum(d, 1).astype(np.float32) / np.float32(max_exact)) / np.float32(
        math.log(MAX_DISTANCE / max_exact))
    large = max_exact + (ratio.astype(np.float32) * np.float32(N_BUCKETS - max_exact)).astype(np.int32)
    large = np.minimum(large, N_BUCKETS - 1)
    bucket = np.where(d < max_exact, d, large)
    in_window = (dist >= 0) & (dist < WINDOW)
    return np.where(in_window, bucket, -1).astype(np.int32)


def _relbias_kernel(tab_ref, bkt_ref, o_ref):
    h = pl.program_id(0)
    bkt = bkt_ref[...]
    acc = jnp.zeros(bkt.shape, F32)
    for b in range(N_BUCKETS):
        acc = jnp.where(bkt == b, tab_ref[b, h], acc)
    regular = jnp.where(bkt >= 0, acc, NEG_INF)
    key = lax.broadcasted_iota(jnp.int32, bkt.shape, 0)
    o_ref[0] = jnp.where(key >= ATTN_BLOCK, regular, NEG_INF)
    o_ref[1] = regular


def _relbias(rel_bias_table):
    bkt = jnp.asarray(_bucket_table())
    nk, nq = bkt.shape
    return pl.pallas_call(
        _relbias_kernel,
        grid=(N_Q_HEADS,),
        in_specs=[pl.BlockSpec(memory_space=pltpu.SMEM),
                  pl.BlockSpec(bkt.shape, lambda h: (0, 0))],
        out_specs=pl.BlockSpec((2, None, nk, nq), lambda h: (0, h // GQA_GROUP, h % 2, (h // 2) % 2)),
        out_shape=jax.ShapeDtypeStruct((2, N_KV_HEADS, 2 * nk, 2 * nq), F32),
        compiler_params=_params("parallel"),
        name="relbias",
    )(rel_bias_table, bkt)


SRC_GQ = ATTN_Q_W + 2 * ATTN_KV_W
SRC_GV = SRC_GQ + 2 * GLA_K_W
SRC_LR = SRC_GV + 2 * GLA_V_W
SRC_GA = SRC_LR + GLA_LOWRANK
SRC_END = SRC_GA + 2 * D_MODEL
WPACK_ROWS = 256


def _wpack_kernel(w_ref, o_ref):
    gates = w_ref[:, SRC_LR:SRC_END][:, GLA_LOWRANK:]
    o_ref[:, COL_GA:COL_GV] = gates.astype(BF16)
    o_ref[:, COL_GV:COL_GQ] = w_ref[:, SRC_GV:SRC_LR].astype(BF16)
    o_ref[:, COL_GQ:COL_AQ] = w_ref[:, SRC_GQ:SRC_GV].astype(BF16)
    o_ref[:, COL_AQ:COL_LR] = w_ref[:, 0:SRC_GQ].astype(BF16)
    lr = w_ref[:, SRC_LR:SRC_LR + LR_PAD]
    lane = lax.broadcasted_iota(jnp.int32, lr.shape, 1)
    o_ref[:, COL_LR:COL_LR + LR_PAD] = jnp.where(lane < GLA_LOWRANK, lr, 0.0).astype(BF16)
    o_ref[:, COL_LR + LR_PAD:] = jnp.zeros((WPACK_ROWS, PROJ_W - COL_LR - LR_PAD), BF16)


def _wpack(w_in):
    _, D, n_in = w_in.shape
    return pl.pallas_call(
        _wpack_kernel,
        grid=(D // WPACK_ROWS,),
        in_specs=[pl.BlockSpec((None, WPACK_ROWS, n_in), lambda i: (0, i, 0))],
        out_specs=pl.BlockSpec((WPACK_ROWS, PROJ_W), lambda i: (i, 0)),
        out_shape=jax.ShapeDtypeStruct((D, PROJ_W), BF16),
        compiler_params=_params("parallel"),
        name="wpack",
    )(w_in)


def _inproj_kernel(x_ref, mod_ref, g_ref, w_ref, o_ref, h_ref):
    @pl.when(pl.program_id(1) == 0)
    def _():
        scale = 1.0 + mod_ref[1:2, :]
        shift = mod_ref[0:1, :]
        gain = g_ref[...]

        def body(r, carry):
            rows = pl.ds(pl.multiple_of(r * 128, 128), 128)
            h_ref[rows, :] = (_rms(x_ref[rows, :], gain) * scale + shift).astype(BF16)
            return carry

        lax.fori_loop(0, INPROJ_TM // 128, body, 0)

    o_ref[...] = _dot(h_ref[...], w_ref[...]).astype(BF16)


def _inproj(x2, mod3, gain, w):
    T, D = x2.shape
    S = T // mod3.shape[0]
    tm, tn = INPROJ_TM, INPROJ_TN
    return pl.pallas_call(
        _inproj_kernel,
        grid=(T // tm, PROJ_W // tn),
        in_specs=[pl.BlockSpec((tm, D), lambda i, j: (i, 0)),
                  pl.BlockSpec((None, 6, D), lambda i, j: (i // (S // tm), 0, 0)),
                  pl.BlockSpec((1, D), lambda i, j: (0, 0)),
                  pl.BlockSpec((D, tn), lambda i, j: (0, j))],
        out_specs=pl.BlockSpec((tm, tn), lambda i, j: (i, j)),
        out_shape=jax.ShapeDtypeStruct((T, PROJ_W), BF16),
        scratch_shapes=[pltpu.VMEM((tm, D), BF16)],
        compiler_params=_params("parallel", "arbitrary"),
        name="inproj",
    )(x2, mod3, gain, w)


def _attn_kernel(q_ref, kc_ref, kp_ref, vc_ref, vp_ref, b0_ref, b1_ref, qg_ref, kg_ref, sink_ref, seg_ref,
                 o_ref):
    nsub = ATTN_TQ // ATTN_BLOCK
    blk, dh = ATTN_BLOCK, HEAD_DIM
    seg = seg_ref[...]

    def normed(x, gain):
        ssq = _dot((x * x).astype(BF16), seg)
        return x * lax.rsqrt(ssq * (1.0 / dh) + EPS) * gain

    kn = normed(jnp.concatenate([kp_ref[...], kc_ref[...]], axis=0).astype(F32), kg_ref[...])
    vt = jnp.concatenate([vp_ref[...], vc_ref[...]], axis=0).astype(F32).T.astype(BF16)
    lane = lax.broadcasted_iota(jnp.int32, (kn.shape[0], 2 * dh), 1)
    qgain = qg_ref[...] * (dh ** -0.5)

    for g in range(N_KV_HEADS):
        pair = kn[:, (g // 2) * 2 * dh:(g // 2 + 1) * 2 * dh]
        own = jnp.where(lane >= dh if g % 2 else lane < dh, pair, 0.0)
        swapped = pltpu.roll(own, dh, axis=1)
        kz = [own, swapped] if g % 2 == 0 else [swapped, own]
        kz = [z.astype(BF16) for z in kz]
        qn = normed(q_ref[:, g * 4 * dh:(g + 1) * 4 * dh].astype(F32), qgain).astype(BF16)
        for sb in range(nsub):
            bref = b0_ref if sb == 0 else b1_ref
            band = slice(sb * blk, sb * blk + 2 * blk)
            rows = slice(sb * blk, (sb + 1) * blk)
            kband = jnp.concatenate([kz[0][band], kz[1][band]], axis=0)
            qr = jnp.concatenate([qn[rows, 0:2 * dh], qn[rows, 2 * dh:4 * dh]], axis=0)
            s = _dot_nt(kband, qr) + bref[g]
            ps, inv = [], []
            for e in range(2):
                se = s[e * 2 * blk:(e + 1) * 2 * blk]
                sink = sink_ref[g, e]
                m = jnp.maximum(jnp.max(se, axis=0, keepdims=True), sink)
                p = jnp.exp(se - m)
                inv.append(1.0 / (jnp.sum(p, axis=0, keepdims=True) + jnp.exp(sink - m)))
                ps.append(p.astype(BF16))
            ot = _dot(vt[g * dh:(g + 1) * dh, band], jnp.concatenate(ps, axis=1))
            ot = ot * jnp.concatenate(inv, axis=1)
            o = jnp.concatenate([ot[:, 0:2 * blk], ot[:, 2 * blk:4 * blk]], axis=0).T
            o_ref[rows, (2 * g) * 2 * dh:(2 * g + 1) * 2 * dh] = o[0:blk].astype(BF16)
            o_ref[rows, (2 * g + 1) * 2 * dh:(2 * g + 2) * 2 * dh] = o[blk:2 * blk].astype(BF16)


def _attn(proj3, biasm, q_gain, k_gain, sinks):
    B, S, _ = proj3.shape
    tq, blk = ATTN_TQ, ATTN_BLOCK
    nsub = tq // blk
    kvw = ATTN_KV_W
    seg_id = np.arange(kvw) // HEAD_DIM
    seg = jnp.asarray(seg_id[:, None] == seg_id[None, :], dtype=BF16)
    sink_rows = jnp.repeat(sinks.reshape(N_KV_HEADS, 2, 2).transpose(0, 2, 1), blk, axis=-1)
    sink_rows = sink_rows.reshape(N_KV_HEADS, 2, 1, 2 * blk)

    def prev(i):
        return jnp.maximum(i * nsub - 1, 0)

    bias_block = (None,) + biasm.shape[1:]
    return pl.pallas_call(
        _attn_kernel,
        grid=(B, S // tq),
        in_specs=[pl.BlockSpec((None, tq, ATTN_Q_W), lambda b, i: (b, i, COL_AQ // ATTN_Q_W)),
                  pl.BlockSpec((None, tq, kvw), lambda b, i: (b, i, COL_AK // kvw)),
                  pl.BlockSpec((None, blk, kvw), lambda b, i: (b, prev(i), COL_AK // kvw)),
                  pl.BlockSpec((None, tq, kvw), lambda b, i: (b, i, COL_AV // kvw)),
                  pl.BlockSpec((None, blk, kvw), lambda b, i: (b, prev(i), COL_AV // kvw)),
                  pl.BlockSpec(bias_block, lambda b, i: (jnp.minimum(i, 1), 0, 0, 0)),
                  pl.BlockSpec(bias_block, lambda b, i: (1, 0, 0, 0)),
                  pl.BlockSpec((1, kvw), lambda b, i: (0, 0)),
                  pl.BlockSpec((1, kvw), lambda b, i: (0, 0)),
                  pl.BlockSpec(sink_rows.shape, lambda b, i: (0, 0, 0, 0)),
                  pl.BlockSpec(seg.shape, lambda b, i: (0, 0))],
        out_specs=pl.BlockSpec((None, tq, ATTN_Q_W), lambda b, i: (b, i, 0)),
        out_shape=jax.ShapeDtypeStruct((B, S, ATTN_Q_W), BF16),
        compiler_params=_params("parallel", "parallel"),
        name="attn",
    )(proj3, proj3, proj3, proj3, proj3, biasm, biasm, jnp.tile(q_gain, (1, GQA_GROUP)),
      jnp.tile(k_gain, (1, N_KV_HEADS)), sink_rows, seg)


def _split2(x):
    hi = x.astype(BF16)
    return hi, (x - hi.astype(F32)).astype(BF16)


def _gla_head(q, k, v, r, lr, wgk, bgk, gain, state_ref, masks):
    C = GLA_CHUNK
    tril, m_same, m_next, m_far, eye = masks
    z = _dot(lr, wgk) + bgk
    gl = (jnp.minimum(z, 0.0) - jnp.log(1.0 + jnp.exp(-jnp.abs(z)))) * (1.0 / GLA_NORMALIZER)
    hi, lo = _split2(gl)
    g = _dot(tril, hi) + _dot(tril, lo)
    t = [g[(c + 1) * C - 1:(c + 1) * C] for c in range(4)]
    t_rows = jnp.concatenate([jnp.broadcast_to(tc, (C, tc.shape[1])) for tc in t], axis=0)

    q_dec = q.astype(F32) * (GLA_DK ** -0.5) * jnp.exp(g)
    kf = k.astype(F32)
    k_inv = (kf * jnp.exp(-g)).astype(BF16)
    k_end = kf * jnp.exp(t_rows - g)
    qd = [q_dec[c * C:(c + 1) * C] for c in range(4)]
    ke = [k_end[c * C:(c + 1) * C] for c in range(4)]

    def rows(parts):
        return jnp.concatenate(parts, axis=0).astype(BF16)

    q_b, k_b = q_dec.astype(BF16), k_end.astype(BF16)
    q_far = rows([qd[0], qd[1], qd[2], qd[3] * jnp.exp(t[2])])
    k_far = rows([ke[0] * jnp.exp(t[1]), ke[1], ke[2], ke[3]])
    q_abs = rows([qd[0], qd[1] * jnp.exp(t[0]), qd[2] * jnp.exp(t[0] + t[1]), qd[3] * jnp.exp(t[0] + t[1] + t[2])])
    k_abs = rows([ke[0] * jnp.exp(t[1] + t[2] + t[3]), ke[1] * jnp.exp(t[2] + t[3]), ke[2] * jnp.exp(t[3]), ke[3]])

    a = jnp.where(m_same, _dot_nt(q_b, k_inv),
                  jnp.where(m_next, _dot_nt(q_b, k_b), jnp.where(m_far, _dot_nt(q_far, k_far), 0.0)))
    state = state_ref[...]
    o = _dot(a.astype(BF16), v) + _dot(q_abs, state.astype(BF16))

    decay = jnp.exp(t[0] + t[1] + t[2] + t[3])
    decay_col = jnp.sum(jnp.where(eye, jnp.broadcast_to(decay, eye.shape), 0.0), axis=1, keepdims=True)
    state_ref[...] = decay_col * state + _dot_tn(k_abs, v)

    rf = r.astype(F32)
    return (_rms(o, gain) * (rf * jax.nn.sigmoid(rf))).astype(BF16)


def _gla_kernel(q_ref, k_ref, v_ref, r_ref, lr_ref, wgk_ref, bgk_ref, gain_ref, o_ref, state_ref):
    R, C, dk, dv = GLA_ROWS, GLA_CHUNK, GLA_DK, GLA_DV

    @pl.when(pl.program_id(2) == 0)
    def _():
        state_ref[...] = jnp.zeros_like(state_ref)

    row = lax.broadcasted_iota(jnp.int32, (R, R), 0)
    col = lax.broadcasted_iota(jnp.int32, (R, R), 1)
    ci, cj = row // C, col // C
    m_same = (row >= col) & (ci == cj)
    m_next = (ci == cj + 1) & (ci != 2)
    m_far = (ci >= 2) & (cj <= 1)
    eye = lax.broadcasted_iota(jnp.int32, (dk, dk), 0) == lax.broadcasted_iota(jnp.int32, (dk, dk), 1)
    masks = (m_same.astype(BF16), m_same, m_next, m_far, eye)

    lr = lr_ref[...]
    for h in range(GLA_HEADS_PER_STEP):
        ks, vs = slice(h * dk, (h + 1) * dk), slice(h * dv, (h + 1) * dv)
        o_ref[:, vs] = _gla_head(q_ref[:, ks], k_ref[:, ks], v_ref[:, vs], r_ref[:, vs], lr,
                                 wgk_ref[:, ks], bgk_ref[:, ks], gain_ref[...], state_ref.at[h], masks)


def _gla(proj3, wgk, bgk, gain):
    B, S, _ = proj3.shape
    R, hps = GLA_ROWS, GLA_HEADS_PER_STEP
    dk, dv = hps * GLA_DK, hps * GLA_DV
    return pl.pallas_call(
        _gla_kernel,
        grid=(B, GLA_HEADS // hps, S // R),
        in_specs=[pl.BlockSpec((None, R, dk), lambda b, h, t: (b, t, COL_GQ // dk + h)),
                  pl.BlockSpec((None, R, dk), lambda b, h, t: (b, t, COL_GK // dk + h)),
                  pl.BlockSpec((None, R, dv), lambda b, h, t: (b, t, COL_GV // dv + h)),
                  pl.BlockSpec((None, R, dv), lambda b, h, t: (b, t, COL_GR // dv + h)),
                  pl.BlockSpec((None, R, LR_PAD), lambda b, h, t: (b, t, COL_LR // LR_PAD)),
                  pl.BlockSpec((LR_PAD, dk), lambda b, h, t: (0, h)),
                  pl.BlockSpec((1, dk), lambda b, h, t: (0, h)),
                  pl.BlockSpec((1, GLA_DV), lambda b, h, t: (0, 0))],
        out_specs=pl.BlockSpec((None, R, dv), lambda b, h, t: (b, t, h)),
        out_shape=jax.ShapeDtypeStruct((B, S, GLA_V_W), BF16),
        scratch_shapes=[pltpu.VMEM((hps, GLA_DK, GLA_DV), F32)],
        compiler_params=_params("parallel", "parallel", "arbitrary"),
        name="gla",
    )(proj3, proj3, proj3, proj3, proj3, wgk, bgk, gain)


def _merge_kernel(ya_ref, yg_ref, ga_ref, gb_ref, x_ref, mod_ref, g2_ref, wa_ref, wg_ref, wo_ref,
                  x1_ref, h2_ref):
    for r0 in range(0, MERGE_TM, MERGE_SUB):
        rows = slice(r0, r0 + MERGE_SUB)
        ga = jax.nn.sigmoid(ga_ref[rows, :].astype(F32))
        gb = jax.nn.sigmoid(gb_ref[rows, :].astype(F32))
        merged = ga * _dot(ya_ref[rows, :], wa_ref[...]) + gb * _dot(yg_ref[rows, :], wg_ref[...])
        x1 = x_ref[rows, :] + mod_ref[2:3, :] * _dot(merged.astype(BF16), wo_ref[...])
        x1_ref[rows, :] = x1
        h2_ref[rows, :] = (_rms(x1, g2_ref[...]) * (1.0 + mod_ref[4:5, :]) + mod_ref[3:4, :]).astype(BF16)


def _merge(ya, yg, proj, x2, mod3, gain2, wa, wg, wo):
    T, D = x2.shape
    S = T // mod3.shape[0]
    tm = MERGE_TM
    once = pl.Buffered(1)
    return pl.pallas_call(
        _merge_kernel,
        grid=(T // tm,),
        in_specs=[pl.BlockSpec((tm, ATTN_Q_W), lambda i: (i, 0)),
                  pl.BlockSpec((tm, GLA_V_W), lambda i: (i, 0)),
                  pl.BlockSpec((tm, D), lambda i: (i, COL_GA // D)),
                  pl.BlockSpec((tm, D), lambda i: (i, COL_GB // D)),
                  pl.BlockSpec((tm, D), lambda i: (i, 0)),
                  pl.BlockSpec((None, 6, D), lambda i: (i // (S // tm), 0, 0)),
                  pl.BlockSpec((1, D), lambda i: (0, 0)),
                  pl.BlockSpec((ATTN_Q_W, D), lambda i: (0, 0), pipeline_mode=once),
                  pl.BlockSpec((GLA_V_W, D), lambda i: (0, 0), pipeline_mode=once),
                  pl.BlockSpec((D, D), lambda i: (0, 0), pipeline_mode=once)],
        out_specs=[pl.BlockSpec((tm, D), lambda i: (i, 0)),
                   pl.BlockSpec((tm, D), lambda i: (i, 0))],
        out_shape=[jax.ShapeDtypeStruct((T, D), F32), jax.ShapeDtypeStruct((T, D), BF16)],
        compiler_params=_params("parallel"),
        name="merge",
    )(ya, yg, proj, proj, x2, mod3, gain2, wa, wg, wo)


def _ffn_up_kernel(h_ref, halo_ref, wa_ref, wb_ref, cwa_ref, cwb_ref, cba_ref, cbb_ref, o_ref,
                   hs_ref, ua_ref, ub_ref, *, tiles_per_seq):
    tm, halo = FFN_TM, FFN_HALO

    @pl.when(pl.program_id(1) == 0)
    def _():
        first = (pl.program_id(0) % tiles_per_seq) == 0
        hs_ref[0:halo, :] = jnp.where(first, jnp.zeros_like(halo_ref), halo_ref[...])
        hs_ref[halo:, :] = h_ref[...]

    def conv(u_ref, cw_ref, cb_ref, r0, n):
        y = cb_ref[...] + cw_ref[0:1, :] * u_ref[halo - 2 + r0:halo - 2 + r0 + n, :]
        y = y + cw_ref[1:2, :] * u_ref[halo - 1 + r0:halo - 1 + r0 + n, :]
        return y + cw_ref[2:3, :] * u_ref[halo + r0:halo + r0 + n, :]

    assert sum(FFN_ROW_CHUNKS) == tm

    def matmul(r0, rc):
        lo = 0 if r0 == 0 else halo + r0
        hi = halo + r0 + rc
        ua_ref[lo:hi, :] = _dot(hs_ref[lo:hi, :], wa_ref[...])
        ub_ref[lo:hi, :] = _dot(hs_ref[lo:hi, :], wb_ref[...])

    def epilogue(r0, rc):
        ya = conv(ua_ref, cwa_ref, cba_ref, r0, rc)
        yb = conv(ub_ref, cwb_ref, cbb_ref, r0, rc)
        o_ref[r0:r0 + rc, :] = (ya * jax.nn.sigmoid(ya) * yb).astype(BF16)

    starts = [sum(FFN_ROW_CHUNKS[:n]) for n in range(len(FFN_ROW_CHUNKS))]
    matmul(starts[0], FFN_ROW_CHUNKS[0])
    for n in range(1, len(FFN_ROW_CHUNKS)):
        matmul(starts[n], FFN_ROW_CHUNKS[n])
        epilogue(starts[n - 1], FFN_ROW_CHUNKS[n - 1])
    epilogue(starts[-1], FFN_ROW_CHUNKS[-1])


def _ffn_up(h2, S, w_up, conv_w, conv_b):
    T, D = h2.shape
    tm, tn, halo = FFN_TM, FFN_TN, FFN_HALO
    nj = D_FF // tn
    return pl.pallas_call(
        functools.partial(_ffn_up_kernel, tiles_per_seq=S // tm),
        grid=(T // tm, D_FF // tn),
        in_specs=[pl.BlockSpec((tm, D), lambda i, j: (i, 0)),
                  pl.BlockSpec((halo, D), lambda i, j: (jnp.maximum(i * (tm // halo) - 1, 0), 0)),
                  pl.BlockSpec((D, tn), lambda i, j: (0, j)),
                  pl.BlockSpec((D, tn), lambda i, j: (0, j + nj)),
                  pl.BlockSpec((3, tn), lambda i, j: (0, j)),
                  pl.BlockSpec((3, tn), lambda i, j: (0, j + nj)),
                  pl.BlockSpec((1, tn), lambda i, j: (0, j)),
                  pl.BlockSpec((1, tn), lambda i, j: (0, j + nj))],
        out_specs=pl.BlockSpec((tm, tn), lambda i, j: (i, j)),
        out_shape=jax.ShapeDtypeStruct((T, D_FF), BF16),
        scratch_shapes=[pltpu.VMEM((tm + halo, D), BF16),
                        pltpu.VMEM((tm + halo, tn), F32),
                        pltpu.VMEM((tm + halo, tn), F32)],
        compiler_params=_params("parallel", "arbitrary"),
        name="ffn_up",
    )(h2, h2, w_up, w_up, conv_w, conv_w, conv_b, conv_b)


def _ffn_down_kernel(a_ref, w_ref, x_ref, mod_ref, o_ref):
    o_ref[...] = x_ref[...] + mod_ref[5:6, :] * _dot(a_ref[...], w_ref[...])


def _ffn_down(act, wd, x1, mod3):
    T, D = x1.shape
    S = T // mod3.shape[0]
    tm = DOWN_TM
    return pl.pallas_call(
        _ffn_down_kernel,
        grid=(T // tm,),
        in_specs=[pl.BlockSpec((tm, D_FF), lambda i: (i, 0)),
                  pl.BlockSpec((D_FF, D), lambda i: (0, 0), pipeline_mode=pl.Buffered(1)),
                  pl.BlockSpec((tm, D), lambda i: (i, 0)),
                  pl.BlockSpec((None, 6, D), lambda i: (i // (S // tm), 0, 0))],
        out_specs=pl.BlockSpec((tm, D), lambda i: (i, 0)),
        out_shape=jax.ShapeDtypeStruct((T, D), F32),
        compiler_params=_params("parallel"),
        name="ffn_down",
    )(act, wd, x1, mod3)


def _layer(x2, B, mod3, biasm, norm1_gain, w_in, q_norm_gain, k_norm_gain, attn_sinks, w_gk_up, b_gk,
           gla_norm_gain, w_branch_attn, w_branch_gla, w_out, norm2_gain, w_ffn_up, ffn_conv_w,
           ffn_conv_b, w_ffn_down):
    T, D = x2.shape
    S = T // B
    w_p = _wpack(w_in)
    wgk = jnp.concatenate([w_gk_up, jnp.zeros((LR_PAD - GLA_LOWRANK, GLA_K_W), w_gk_up.dtype)],
                          axis=0).astype(BF16)

    proj = _inproj(x2, mod3, norm1_gain.reshape(1, D), w_p)
    proj3 = proj.reshape(B, S, PROJ_W)
    ya = _attn(proj3, biasm, q_norm_gain.reshape(1, HEAD_DIM), k_norm_gain.reshape(1, HEAD_DIM), attn_sinks)
    yg = _gla(proj3, wgk, b_gk.reshape(1, GLA_K_W), gla_norm_gain.reshape(1, GLA_DV))
    x1, h2 = _merge(ya.reshape(T, ATTN_Q_W), yg.reshape(T, GLA_V_W), proj, x2, mod3,
                    norm2_gain.reshape(1, D), w_branch_attn.astype(BF16), w_branch_gla.astype(BF16),
                    w_out.astype(BF16))
    act = _ffn_up(h2, S, w_ffn_up.astype(BF16), ffn_conv_w, ffn_conv_b.reshape(1, 2 * D_FF))
    return _ffn_down(act, w_ffn_down.astype(BF16), x1, mod3)


def kernel(x, c, rel_bias_table, w_ada, b_ada, norm1_gain, w_in, q_norm_gain, k_norm_gain, attn_sinks,
           w_gk_up, b_gk, gla_norm_gain, w_branch_attn, w_branch_gla, w_out, norm2_gain, w_ffn_up,
           ffn_conv_w, ffn_conv_b, w_ffn_down):
    B, S, D = x.shape
    depth = w_in.shape[0]
    biasm = _relbias(rel_bias_table)
    x2 = x.reshape(B * S, D)
    for l in range(depth):
        mod3 = _adaln(c, w_ada[l], b_ada[l]).reshape(B, 6, D)
        x2 = _layer(x2, B, mod3, biasm, norm1_gain[l], w_in[l:l + 1], q_norm_gain[l], k_norm_gain[l],
                    attn_sinks[l], w_gk_up[l], b_gk[l], gla_norm_gain[l], w_branch_attn[l],
                    w_branch_gla[l], w_out[l], norm2_gain[l], w_ffn_up[l], ffn_conv_w[l], ffn_conv_b[l],
                    w_ffn_down[l])
    return x2.reshape(B, S, D)
```

```python
import functools
import math

import numpy as np
import jax
import jax.numpy as jnp
from jax import lax
from jax.experimental import pallas as pl
from jax.experimental.pallas import tpu as pltpu

F32 = jnp.float32
BF16 = jnp.bfloat16

D_MODEL = 2048
N_Q_HEADS = 16
N_KV_HEADS = 4
GQA_GROUP = N_Q_HEADS // N_KV_HEADS
HEAD_DIM = 64
WINDOW = 128
ATTN_BLOCK = 128
N_BUCKETS = 32
MAX_DISTANCE = 128
GLA_HEADS = 4
GLA_DK = 256
GLA_DV = 512
GLA_LOWRANK = 16
GLA_NORMALIZER = 16.0
GLA_CHUNK = 64
D_FF = 5632
EPS = 1e-6
NEG_INF = -1e30

ATTN_Q_W = N_Q_HEADS * HEAD_DIM
ATTN_KV_W = N_KV_HEADS * HEAD_DIM
GLA_K_W = GLA_HEADS * GLA_DK
GLA_V_W = GLA_HEADS * GLA_DV

COL_GA = 0
COL_GB = COL_GA + D_MODEL
COL_GV = COL_GB + D_MODEL
COL_GR = COL_GV + GLA_V_W
COL_GQ = COL_GR + GLA_V_W
COL_GK = COL_GQ + GLA_K_W
COL_AQ = COL_GK + GLA_K_W
COL_AK = COL_AQ + ATTN_Q_W
COL_AV = COL_AK + ATTN_KV_W
COL_LR = COL_AV + ATTN_KV_W
LR_PAD = 128
PROJ_W = 12288

VMEM_LIMIT = 60 * 1024 * 1024

ADALN_TN = 1024
INPROJ_TM, INPROJ_TN = 1024, 1024
ATTN_TQ = 256
GLA_ROWS = 256
GLA_HEADS_PER_STEP = 2
MERGE_TM, MERGE_SUB = 512, 256
FFN_TM, FFN_TN = 1024, 512
FFN_HALO = 16
FFN_EPI_ROWS = 256
DOWN_TM = 512


def _params(*sem):
    return pltpu.CompilerParams(dimension_semantics=sem, vmem_limit_bytes=VMEM_LIMIT)


def _dot(a, b):
    return jnp.dot(a, b, preferred_element_type=F32)


def _dot_nt(a, b):
    return lax.dot_general(a, b, (((1,), (1,)), ((), ())), preferred_element_type=F32)


def _dot_tn(a, b):
    return lax.dot_general(a, b, (((0,), (0,)), ((), ())), preferred_element_type=F32)


def _rms(x, gain):
    return x * lax.rsqrt(jnp.mean(x * x, axis=-1, keepdims=True) + EPS) * gain


def _adaln_kernel(c_ref, w_ref, b_ref, o_ref):
    c = c_ref[...]
    ca = c * jax.nn.sigmoid(c)
    o_ref[...] = _dot(ca.astype(BF16), w_ref[...].astype(BF16)) + b_ref[...]


def _adaln(c, w_ada, b_ada):
    B, D = c.shape
    N = w_ada.shape[1]
    return pl.pallas_call(
        _adaln_kernel,
        grid=(N // ADALN_TN,),
        in_specs=[pl.BlockSpec((B, D), lambda j: (0, 0)),
                  pl.BlockSpec((D, ADALN_TN), lambda j: (0, j)),
                  pl.BlockSpec((1, ADALN_TN), lambda j: (0, j))],
        out_specs=pl.BlockSpec((B, ADALN_TN), lambda j: (0, j)),
        out_shape=jax.ShapeDtypeStruct((B, N), F32),
        compiler_params=_params("parallel"),
        name="adaln",
    )(c, w_ada, b_ada.reshape(1, N))


def _bucket_table():
    j = np.arange(2 * ATTN_BLOCK)[:, None]
    i = np.arange(ATTN_BLOCK)[None, :]
    dist = i + ATTN_BLOCK - j
    max_exact = N_BUCKETS // 2
    d = np.maximum(dist, 0)
    ratio = np.log(np.maximum(d, 1).astype(np.float32) / np.float32(max_exact)) / np.float32(
        math.log(MAX_DISTANCE / max_exact))
    large = max_exact + (ratio.astype(np.float32) * np.float32(N_BUCKETS - max_exact)).astype(np.int32)
    large = np.minimum(large, N_BUCKETS - 1)
    bucket = np.where(d < max_exact, d, large)
    in_window = (dist >= 0) & (dist < WINDOW)
    return np.where(in_window, bucket, -1).astype(np.int32)


def _relbias_kernel(tab_ref, bkt_ref, o_ref):
    h = pl.program_id(0)
    bkt = bkt_ref[...]
    acc = jnp.zeros(bkt.shape, F32)
    for b in range(N_BUCKETS):
        acc = jnp.where(bkt == b, tab_ref[b, h], acc)
    regular = jnp.where(bkt >= 0, acc, NEG_INF)
    key = lax.broadcasted_iota(jnp.int32, bkt.shape, 0)
    o_ref[0] = jnp.where(key >= ATTN_BLOCK, regular, NEG_INF)
    o_ref[1] = regular


def _relbias(rel_bias_table):
    bkt = jnp.asarray(_bucket_table())
    nk, nq = bkt.shape
    return pl.pallas_call(
        _relbias_kernel,
        grid=(N_Q_HEADS,),
        in_specs=[pl.BlockSpec(memory_space=pltpu.SMEM),
                  pl.BlockSpec(bkt.shape, lambda h: (0, 0))],
        out_specs=pl.BlockSpec((2, None, nk, nq), lambda h: (0, h // GQA_GROUP, h % 2, (h // 2) % 2)),
        out_shape=jax.ShapeDtypeStruct((2, N_KV_HEADS, 2 * nk, 2 * nq), F32),
        compiler_params=_params("parallel"),
        name="relbias",
    )(rel_bias_table, bkt)


SRC_GQ = ATTN_Q_W + 2 * ATTN_KV_W
SRC_GV = SRC_GQ + 2 * GLA_K_W
SRC_LR = SRC_GV + 2 * GLA_V_W
SRC_GA = SRC_LR + GLA_LOWRANK
WPACK_ROWS = 256


def _wpack_src_row(b):
    r = b * WPACK_ROWS
    src = jnp.where(r < COL_GV, SRC_GA + r,
                    jnp.where(r < COL_GQ, SRC_GV + (r - COL_GV),
                              jnp.where(r < COL_AQ, SRC_GQ + (r - COL_GQ),
                                        jnp.where(r < COL_LR, r - COL_AQ, SRC_LR))))
    return pl.multiple_of(src, 8)


def _wpack_kernel(w_ref, o_ref):
    r = pl.program_id(0) * WPACK_ROWS
    nvalid = jnp.where(r < COL_LR, WPACK_ROWS, jnp.where(r == COL_LR, GLA_LOWRANK, 0))
    row = lax.broadcasted_iota(jnp.int32, w_ref.shape, 0)
    o_ref[...] = jnp.where(row < nvalid, w_ref[...], 0.0).astype(BF16)


def _wpack(w_in_t):
    _, D = w_in_t.shape
    return pl.pallas_call(
        _wpack_kernel,
        grid=(PROJ_W // WPACK_ROWS,),
        in_specs=[pl.BlockSpec((pl.Element(WPACK_ROWS), pl.Element(D)), lambda b: (_wpack_src_row(b), 0))],
        out_specs=pl.BlockSpec((WPACK_ROWS, D), lambda b: (b, 0)),
        out_shape=jax.ShapeDtypeStruct((PROJ_W, D), BF16),
        compiler_params=_params("parallel"),
        name="wpack",
    )(w_in_t)


def _inproj_kernel(x_ref, mod_ref, g_ref, w_ref, o_ref, h_ref):
    @pl.when(pl.program_id(1) == 0)
    def _():
        scale = 1.0 + mod_ref[1:2, :]
        shift = mod_ref[0:1, :]
        gain = g_ref[...]

        def body(r, carry):
            rows = pl.ds(pl.multiple_of(r * 128, 128), 128)
            h_ref[rows, :] = (_rms(x_ref[rows, :], gain) * scale + shift).astype(BF16)
            return carry

        lax.fori_loop(0, INPROJ_TM // 128, body, 0)

    o_ref[...] = _dot_nt(h_ref[...], w_ref[...]).astype(BF16)


def _inproj(x2, mod3, gain, w):
    T, D = x2.shape
    S = T // mod3.shape[0]
    tm, tn = INPROJ_TM, INPROJ_TN
    return pl.pallas_call(
        _inproj_kernel,
        grid=(T // tm, PROJ_W // tn),
        in_specs=[pl.BlockSpec((tm, D), lambda i, j: (i, 0)),
                  pl.BlockSpec((None, 6, D), lambda i, j: (i // (S // tm), 0, 0)),
                  pl.BlockSpec((1, D), lambda i, j: (0, 0)),
                  pl.BlockSpec((tn, D), lambda i, j: (j, 0))],
        out_specs=pl.BlockSpec((tm, tn), lambda i, j: (i, j)),
        out_shape=jax.ShapeDtypeStruct((T, PROJ_W), BF16),
        scratch_shapes=[pltpu.VMEM((tm, D), BF16)],
        compiler_params=_params("parallel", "arbitrary"),
        name="inproj",
    )(x2, mod3, gain, w)


def _attn_kernel(q_ref, kc_ref, kp_ref, vc_ref, vp_ref, b0_ref, b1_ref, qg_ref, kg_ref, sink_ref, seg_ref,
                 o_ref):
    nsub = ATTN_TQ // ATTN_BLOCK
    blk, dh = ATTN_BLOCK, HEAD_DIM
    seg = seg_ref[...]

    def normed(x, gain):
        ssq = _dot((x * x).astype(BF16), seg)
        return x * lax.rsqrt(ssq * (1.0 / dh) + EPS) * gain

    kn = normed(jnp.concatenate([kp_ref[...], kc_ref[...]], axis=0).astype(F32), kg_ref[...])
    vt = jnp.concatenate([vp_ref[...], vc_ref[...]], axis=0).astype(F32).T.astype(BF16)
    lane = lax.broadcasted_iota(jnp.int32, (kn.shape[0], 2 * dh), 1)
    qgain = qg_ref[...] * (dh ** -0.5)

    for g in range(N_KV_HEADS):
        pair = kn[:, (g // 2) * 2 * dh:(g // 2 + 1) * 2 * dh]
        own = jnp.where(lane >= dh if g % 2 else lane < dh, pair, 0.0)
        swapped = pltpu.roll(own, dh, axis=1)
        kz = [own, swapped] if g % 2 == 0 else [swapped, own]
        kz = [z.astype(BF16) for z in kz]
        qn = normed(q_ref[:, g * 4 * dh:(g + 1) * 4 * dh].astype(F32), qgain).astype(BF16)
        for sb in range(nsub):
            bref = b0_ref if sb == 0 else b1_ref
            band = slice(sb * blk, sb * blk + 2 * blk)
            rows = slice(sb * blk, (sb + 1) * blk)
            kband = jnp.concatenate([kz[0][band], kz[1][band]], axis=0)
            qr = jnp.concatenate([qn[rows, 0:2 * dh], qn[rows, 2 * dh:4 * dh]], axis=0)
            s = _dot_nt(kband, qr) + bref[g]
            ps, inv = [], []
            for e in range(2):
                se = s[e * 2 * blk:(e + 1) * 2 * blk]
                sink = sink_ref[g, e]
                m = jnp.maximum(jnp.max(se, axis=0, keepdims=True), sink)
                p = jnp.exp(se - m)
                inv.append(1.0 / (jnp.sum(p, axis=0, keepdims=True) + jnp.exp(sink - m)))
                ps.append(p.astype(BF16))
            ot = _dot(vt[g * dh:(g + 1) * dh, band], jnp.concatenate(ps, axis=1))
            ot = ot * jnp.concatenate(inv, axis=1)
            o = jnp.concatenate([ot[:, 0:2 * blk], ot[:, 2 * blk:4 * blk]], axis=0).T
            o_ref[rows, (2 * g) * 2 * dh:(2 * g + 1) * 2 * dh] = o[0:blk].astype(BF16)
            o_ref[rows, (2 * g + 1) * 2 * dh:(2 * g + 2) * 2 * dh] = o[blk:2 * blk].astype(BF16)


def _attn(proj3, biasm, q_gain, k_gain, sinks):
    B, S, _ = proj3.shape
    tq, blk = ATTN_TQ, ATTN_BLOCK
    nsub = tq // blk
    kvw = ATTN_KV_W
    seg_id = np.arange(kvw) // HEAD_DIM
    seg = jnp.asarray(seg_id[:, None] == seg_id[None, :], dtype=BF16)
    sink_rows = jnp.repeat(sinks.reshape(N_KV_HEADS, 2, 2).transpose(0, 2, 1), blk, axis=-1)
    sink_rows = sink_rows.reshape(N_KV_HEADS, 2, 1, 2 * blk)

    def prev(i):
        return jnp.maximum(i * nsub - 1, 0)

    bias_block = (None,) + biasm.shape[1:]
    return pl.pallas_call(
        _attn_kernel,
        grid=(B, S // tq),
        in_specs=[pl.BlockSpec((None, tq, ATTN_Q_W), lambda b, i: (b, i, COL_AQ // ATTN_Q_W)),
                  pl.BlockSpec((None, tq, kvw), lambda b, i: (b, i, COL_AK // kvw)),
                  pl.BlockSpec((None, blk, kvw), lambda b, i: (b, prev(i), COL_AK // kvw)),
                  pl.BlockSpec((None, tq, kvw), lambda b, i: (b, i, COL_AV // kvw)),
                  pl.BlockSpec((None, blk, kvw), lambda b, i: (b, prev(i), COL_AV // kvw)),
                  pl.BlockSpec(bias_block, lambda b, i: (jnp.minimum(i, 1), 0, 0, 0)),
                  pl.BlockSpec(bias_block, lambda b, i: (1, 0, 0, 0)),
                  pl.BlockSpec((1, kvw), lambda b, i: (0, 0)),
                  pl.BlockSpec((1, kvw), lambda b, i: (0, 0)),
                  pl.BlockSpec(sink_rows.shape, lambda b, i: (0, 0, 0, 0)),
                  pl.BlockSpec(seg.shape, lambda b, i: (0, 0))],
        out_specs=pl.BlockSpec((None, tq, ATTN_Q_W), lambda b, i: (b, i, 0)),
        out_shape=jax.ShapeDtypeStruct((B, S, ATTN_Q_W), BF16),
        compiler_params=_params("parallel", "parallel"),
        name="attn",
    )(proj3, proj3, proj3, proj3, proj3, biasm, biasm, jnp.tile(q_gain, (1, GQA_GROUP)),
      jnp.tile(k_gain, (1, N_KV_HEADS)), sink_rows, seg)


def _split2(x):
    hi = x.astype(BF16)
    return hi, (x - hi.astype(F32)).astype(BF16)


def _gla_head(q, k, v, r, lr, wgk, bgk, gain, state_ref, masks):
    C = GLA_CHUNK
    tril, m_same, m_next, m_far, eye = masks
    z = _dot(lr, wgk) + bgk
    gl = (jnp.minimum(z, 0.0) - jnp.log(1.0 + jnp.exp(-jnp.abs(z)))) * (1.0 / GLA_NORMALIZER)
    hi, lo = _split2(gl)
    g = _dot(tril, hi) + _dot(tril, lo)
    t = [g[(c + 1) * C - 1:(c + 1) * C] for c in range(4)]
    t_rows = jnp.concatenate([jnp.broadcast_to(tc, (C, tc.shape[1])) for tc in t], axis=0)

    q_dec = q.astype(F32) * (GLA_DK ** -0.5) * jnp.exp(g)
    kf = k.astype(F32)
    k_inv = (kf * jnp.exp(-g)).astype(BF16)
    k_end = kf * jnp.exp(t_rows - g)
    qd = [q_dec[c * C:(c + 1) * C] for c in range(4)]
    ke = [k_end[c * C:(c + 1) * C] for c in range(4)]

    def rows(parts):
        return jnp.concatenate(parts, axis=0).astype(BF16)

    q_b, k_b = q_dec.astype(BF16), k_end.astype(BF16)
    q_far = rows([qd[0], qd[1], qd[2], qd[3] * jnp.exp(t[2])])
    k_far = rows([ke[0] * jnp.exp(t[1]), ke[1], ke[2], ke[3]])
    q_abs = rows([qd[0], qd[1] * jnp.exp(t[0]), qd[2] * jnp.exp(t[0] + t[1]), qd[3] * jnp.exp(t[0] + t[1] + t[2])])
    k_abs = rows([ke[0] * jnp.exp(t[1] + t[2] + t[3]), ke[1] * jnp.exp(t[2] + t[3]), ke[2] * jnp.exp(t[3]), ke[3]])

    a = jnp.where(m_same, _dot_nt(q_b, k_inv),
                  jnp.where(m_next, _dot_nt(q_b, k_b), jnp.where(m_far, _dot_nt(q_far, k_far), 0.0)))
    state = state_ref[...]
    o = _dot(a.astype(BF16), v) + _dot(q_abs, state.astype(BF16))

    decay = jnp.exp(t[0] + t[1] + t[2] + t[3])
    decay_col = jnp.sum(jnp.where(eye, jnp.broadcast_to(decay, eye.shape), 0.0), axis=1, keepdims=True)
    state_ref[...] = decay_col * state + _dot_tn(k_abs, v)

    rf = r.astype(F32)
    return (_rms(o, gain) * (rf * jax.nn.sigmoid(rf))).astype(BF16)


def _gla_kernel(q_ref, k_ref, v_ref, r_ref, lr_ref, wgk_ref, bgk_ref, gain_ref, o_ref, state_ref):
    R, C, dk, dv = GLA_ROWS, GLA_CHUNK, GLA_DK, GLA_DV

    @pl.when(pl.program_id(2) == 0)
    def _():
        state_ref[...] = jnp.zeros_like(state_ref)

    row = lax.broadcasted_iota(jnp.int32, (R, R), 0)
    col = lax.broadcasted_iota(jnp.int32, (R, R), 1)
    ci, cj = row // C, col // C
    m_same = (row >= col) & (ci == cj)
    m_next = (ci == cj + 1) & (ci != 2)
    m_far = (ci >= 2) & (cj <= 1)
    eye = lax.broadcasted_iota(jnp.int32, (dk, dk), 0) == lax.broadcasted_iota(jnp.int32, (dk, dk), 1)
    masks = (m_same.astype(BF16), m_same, m_next, m_far, eye)

    lr = lr_ref[...]
    for h in range(GLA_HEADS_PER_STEP):
        ks, vs = slice(h * dk, (h + 1) * dk), slice(h * dv, (h + 1) * dv)
        o_ref[:, vs] = _gla_head(q_ref[:, ks], k_ref[:, ks], v_ref[:, vs], r_ref[:, vs], lr,
                                 wgk_ref[:, ks], bgk_ref[:, ks], gain_ref[...], state_ref.at[h], masks)


def _gla(proj3, wgk, bgk, gain):
    B, S, _ = proj3.shape
    R, hps = GLA_ROWS, GLA_HEADS_PER_STEP
    dk, dv = hps * GLA_DK, hps * GLA_DV
    return pl.pallas_call(
        _gla_kernel,
        grid=(B, GLA_HEADS // hps, S // R),
        in_specs=[pl.BlockSpec((None, R, dk), lambda b, h, t: (b, t, COL_GQ // dk + h)),
                  pl.BlockSpec((None, R, dk), lambda b, h, t: (b, t, COL_GK // dk + h)),
                  pl.BlockSpec((None, R, dv), lambda b, h, t: (b, t, COL_GV // dv + h)),
                  pl.BlockSpec((None, R, dv), lambda b, h, t: (b, t, COL_GR // dv + h)),
                  pl.BlockSpec((None, R, LR_PAD), lambda b, h, t: (b, t, COL_LR // LR_PAD)),
                  pl.BlockSpec((LR_PAD, dk), lambda b, h, t: (0, h)),
                  pl.BlockSpec((1, dk), lambda b, h, t: (0, h)),
                  pl.BlockSpec((1, GLA_DV), lambda b, h, t: (0, 0))],
        out_specs=pl.BlockSpec((None, R, dv), lambda b, h, t: (b, t, h)),
        out_shape=jax.ShapeDtypeStruct((B, S, GLA_V_W), BF16),
        scratch_shapes=[pltpu.VMEM((hps, GLA_DK, GLA_DV), F32)],
        compiler_params=_params("parallel", "parallel", "arbitrary"),
        name="gla",
    )(proj3, proj3, proj3, proj3, proj3, wgk, bgk, gain)


def _merge_kernel(ya_ref, yg_ref, ga_ref, gb_ref, x_ref, mod_ref, g2_ref, wa_ref, wg_ref, wo_ref,
                  x1_ref, h2_ref):
    for r0 in range(0, MERGE_TM, MERGE_SUB):
        rows = slice(r0, r0 + MERGE_SUB)
        ga = jax.nn.sigmoid(ga_ref[rows, :].astype(F32))
        gb = jax.nn.sigmoid(gb_ref[rows, :].astype(F32))
        merged = ga * _dot(ya_ref[rows, :], wa_ref[...]) + gb * _dot(yg_ref[rows, :], wg_ref[...])
        x1 = x_ref[rows, :] + mod_ref[2:3, :] * _dot(merged.astype(BF16), wo_ref[...])
        x1_ref[rows, :] = x1
        h2_ref[rows, :] = (_rms(x1, g2_ref[...]) * (1.0 + mod_ref[4:5, :]) + mod_ref[3:4, :]).astype(BF16)


def _merge(ya, yg, proj, x2, mod3, gain2, wa, wg, wo):
    T, D = x2.shape
    S = T // mod3.shape[0]
    tm = MERGE_TM
    once = pl.Buffered(1)
    return pl.pallas_call(
        _merge_kernel,
        grid=(T // tm,),
        in_specs=[pl.BlockSpec((tm, ATTN_Q_W), lambda i: (i, 0)),
                  pl.BlockSpec((tm, GLA_V_W), lambda i: (i, 0)),
                  pl.BlockSpec((tm, D), lambda i: (i, COL_GA // D)),
                  pl.BlockSpec((tm, D), lambda i: (i, COL_GB // D)),
                  pl.BlockSpec((tm, D), lambda i: (i, 0)),
                  pl.BlockSpec((None, 6, D), lambda i: (i // (S // tm), 0, 0)),
                  pl.BlockSpec((1, D), lambda i: (0, 0)),
                  pl.BlockSpec((ATTN_Q_W, D), lambda i: (0, 0), pipeline_mode=once),
                  pl.BlockSpec((GLA_V_W, D), lambda i: (0, 0), pipeline_mode=once),
                  pl.BlockSpec((D, D), lambda i: (0, 0), pipeline_mode=once)],
        out_specs=[pl.BlockSpec((tm, D), lambda i: (i, 0)),
                   pl.BlockSpec((tm, D), lambda i: (i, 0))],
        out_shape=[jax.ShapeDtypeStruct((T, D), F32), jax.ShapeDtypeStruct((T, D), BF16)],
        compiler_params=_params("parallel"),
        name="merge",
    )(ya, yg, proj, proj, x2, mod3, gain2, wa, wg, wo)


def _ffn_up_kernel(h_ref, halo_ref, wa_ref, wb_ref, cwa_ref, cwb_ref, cba_ref, cbb_ref, o_ref,
                   hs_ref, ua_ref, ub_ref, *, tiles_per_seq):
    tm, halo = FFN_TM, FFN_HALO

    @pl.when(pl.program_id(1) == 0)
    def _():
        first = (pl.program_id(0) % tiles_per_seq) == 0
        hs_ref[0:halo, :] = jnp.where(first, jnp.zeros_like(halo_ref), halo_ref[...])
        hs_ref[halo:, :] = h_ref[...]

    def conv(u_ref, cw_ref, cb_ref, r0, n):
        y = cb_ref[...] + cw_ref[0:1, :] * u_ref[halo - 2 + r0:halo - 2 + r0 + n, :]
        y = y + cw_ref[1:2, :] * u_ref[halo - 1 + r0:halo - 1 + r0 + n, :]
        return y + cw_ref[2:3, :] * u_ref[halo + r0:halo + r0 + n, :]

    ua_ref[...] = _dot(hs_ref[...], wa_ref[...])
    ub_ref[...] = _dot(hs_ref[...], wb_ref[...])
    for r0 in range(0, tm, FFN_EPI_ROWS):
        ya = conv(ua_ref, cwa_ref, cba_ref, r0, FFN_EPI_ROWS)
        yb = conv(ub_ref, cwb_ref, cbb_ref, r0, FFN_EPI_ROWS)
        o_ref[r0:r0 + FFN_EPI_ROWS, :] = (ya * jax.nn.sigmoid(ya) * yb).astype(BF16)


def _ffn_up(h2, S, w_up, conv_w, conv_b):
    T, D = h2.shape
    tm, tn, halo = FFN_TM, FFN_TN, FFN_HALO
    nj = D_FF // tn
    return pl.pallas_call(
        functools.partial(_ffn_up_kernel, tiles_per_seq=S // tm),
        grid=(T // tm, D_FF // tn),
        in_specs=[pl.BlockSpec((tm, D), lambda i, j: (i, 0)),
                  pl.BlockSpec((halo, D), lambda i, j: (jnp.maximum(i * (tm // halo) - 1, 0), 0)),
                  pl.BlockSpec((D, tn), lambda i, j: (0, j)),
                  pl.BlockSpec((D, tn), lambda i, j: (0, j + nj)),
                  pl.BlockSpec((3, tn), lambda i, j: (0, j)),
                  pl.BlockSpec((3, tn), lambda i, j: (0, j + nj)),
                  pl.BlockSpec((1, tn), lambda i, j: (0, j)),
                  pl.BlockSpec((1, tn), lambda i, j: (0, j + nj))],
        out_specs=pl.BlockSpec((tm, tn), lambda i, j: (i, j)),
        out_shape=jax.ShapeDtypeStruct((T, D_FF), BF16),
        scratch_shapes=[pltpu.VMEM((tm + halo, D), BF16),
                        pltpu.VMEM((tm + halo, tn), F32),
                        pltpu.VMEM((tm + halo, tn), F32)],
        compiler_params=_params("parallel", "arbitrary"),
        name="ffn_up",
    )(h2, h2, w_up, w_up, conv_w, conv_w, conv_b, conv_b)


def _ffn_down_kernel(a_ref, w_ref, x_ref, mod_ref, o_ref):
    o_ref[...] = x_ref[...] + mod_ref[5:6, :] * _dot(a_ref[...], w_ref[...])


def _ffn_down(act, wd, x1, mod3):
    T, D = x1.shape
    S = T // mod3.shape[0]
    tm = DOWN_TM
    return pl.pallas_call(
        _ffn_down_kernel,
        grid=(T // tm,),
        in_specs=[pl.BlockSpec((tm, D_FF), lambda i: (i, 0)),
                  pl.BlockSpec((D_FF, D), lambda i: (0, 0), pipeline_mode=pl.Buffered(1)),
                  pl.BlockSpec((tm, D), lambda i: (i, 0)),
                  pl.BlockSpec((None, 6, D), lambda i: (i // (S // tm), 0, 0))],
        out_specs=pl.BlockSpec((tm, D), lambda i: (i, 0)),
        out_shape=jax.ShapeDtypeStruct((T, D), F32),
        compiler_params=_params("parallel"),
        name="ffn_down",
    )(act, wd, x1, mod3)


def _layer(x2, B, mod3, biasm, norm1_gain, w_in, q_norm_gain, k_norm_gain, attn_sinks, w_gk_up, b_gk,
           gla_norm_gain, w_branch_attn, w_branch_gla, w_out, norm2_gain, w_ffn_up, ffn_conv_w,
           ffn_conv_b, w_ffn_down):
    T, D = x2.shape
    S = T // B
    w_p = _wpack(w_in.T)
    wgk = jnp.concatenate([w_gk_up, jnp.zeros((LR_PAD - GLA_LOWRANK, GLA_K_W), w_gk_up.dtype)],
                          axis=0).astype(BF16)

    proj = _inproj(x2, mod3, norm1_gain.reshape(1, D), w_p)
    proj3 = proj.reshape(B, S, PROJ_W)
    ya = _attn(proj3, biasm, q_norm_gain.reshape(1, HEAD_DIM), k_norm_gain.reshape(1, HEAD_DIM), attn_sinks)
    yg = _gla(proj3, wgk, b_gk.reshape(1, GLA_K_W), gla_norm_gain.reshape(1, GLA_DV))
    x1, h2 = _merge(ya.reshape(T, ATTN_Q_W), yg.reshape(T, GLA_V_W), proj, x2, mod3,
                    norm2_gain.reshape(1, D), w_branch_attn.astype(BF16), w_branch_gla.astype(BF16),
                    w_out.astype(BF16))
    act = _ffn_up(h2, S, w_ffn_up.astype(BF16), ffn_conv_w, ffn_conv_b.reshape(1, 2 * D_FF))
    return _ffn_down(act, w_ffn_down.astype(BF16), x1, mod3)


def kernel(x, c, rel_bias_table, w_ada, b_ada, norm1_gain, w_in, q_norm_gain, k_norm_gain, attn_sinks,
           w_gk_up, b_gk, gla_norm_gain, w_branch_attn, w_branch_gla, w_out, norm2_gain, w_ffn_up,
           ffn_conv_w, ffn_conv_b, w_ffn_down):
    B, S, D = x.shape
    depth = w_in.shape[0]
    biasm = _relbias(rel_bias_table)
    x2 = x.reshape(B * S, D)
    for l in range(depth):
        mod3 = _adaln(c, w_ada[l], b_ada[l]).reshape(B, 6, D)
        x2 = _layer(x2, B, mod3, biasm, norm1_gain[l], w_in[l], q_norm_gain[l], k_norm_gain[l],
                    attn_sinks[l], w_gk_up[l], b_gk[l], gla_norm_gain[l], w_branch_attn[l],
                    w_branch_gla[l], w_out[l], norm2_gain[l], w_ffn_up[l], ffn_conv_w[l], ffn_conv_b[l],
                    w_ffn_down[l])
    return x2.reshape(B, S, D)
```

```python
import functools
import itertools
import math

import numpy as np
import jax
import jax.numpy as jnp
from jax import lax
from jax.experimental import pallas as pl
from jax.experimental.pallas import tpu as pltpu

F32 = jnp.float32
BF16 = jnp.bfloat16

D_MODEL = 2048
N_Q_HEADS = 16
N_KV_HEADS = 4
GQA_GROUP = N_Q_HEADS // N_KV_HEADS
HEAD_DIM = 64
WINDOW = 128
ATTN_BLOCK = 128
N_BUCKETS = 32
MAX_DISTANCE = 128
GLA_HEADS = 4
GLA_DK = 256
GLA_DV = 512
GLA_LOWRANK = 16
GLA_NORMALIZER = 16.0
GLA_CHUNK = 64
D_FF = 5632
EPS = 1e-6
NEG_INF = -1e30
LOG2E = math.log2(math.e)

ATTN_Q_W = N_Q_HEADS * HEAD_DIM
ATTN_KV_W = N_KV_HEADS * HEAD_DIM
GLA_K_W = GLA_HEADS * GLA_DK
GLA_V_W = GLA_HEADS * GLA_DV

COL_GA = 0
COL_GB = COL_GA + D_MODEL
COL_GV = COL_GB + D_MODEL
COL_GR = COL_GV + GLA_V_W
COL_GQ = COL_GR + GLA_V_W
COL_GK = COL_GQ + GLA_K_W
COL_AQ = COL_GK + GLA_K_W
COL_AK = COL_AQ + ATTN_Q_W
COL_AV = COL_AK + ATTN_KV_W
COL_LR = COL_AV + ATTN_KV_W
LR_PAD = 128
PROJ_W = 12288

VMEM_LIMIT = 60 * 1024 * 1024

ADALN_TN = 1024
INPROJ_TM, INPROJ_TN = 1024, 2048
ATTN_TQ = 256
GLA_ROWS = 256
GLA_HEADS_PER_STEP = 4
MERGE_TM, MERGE_SUB = 512, 256
FFN_TM, FFN_TN = 1024, 512
FFN_HALO = 16
FFN_EPI_ROWS = 256
DOWN_TM = 512


def _params(*sem):
    return pltpu.CompilerParams(dimension_semantics=sem, vmem_limit_bytes=VMEM_LIMIT)


def _dot(a, b):
    return jnp.dot(a, b, preferred_element_type=F32)


def _dot_nt(a, b):
    return lax.dot_general(a, b, (((1,), (1,)), ((), ())), preferred_element_type=F32)


def _dot_tn(a, b):
    return lax.dot_general(a, b, (((0,), (0,)), ((), ())), preferred_element_type=F32)


def _rms(x, gain):
    return x * lax.rsqrt(jnp.mean(x * x, axis=-1, keepdims=True) + EPS) * gain


def _adaln_kernel(c_ref, w_ref, b_ref, o_ref):
    c = c_ref[...]
    ca = c * jax.nn.sigmoid(c)
    o_ref[...] = _dot(ca.astype(BF16), w_ref[...].astype(BF16)) + b_ref[...]


def _adaln(c, w_ada, b_ada):
    B, D = c.shape
    N = w_ada.shape[1]
    return pl.pallas_call(
        _adaln_kernel,
        grid=(N // ADALN_TN,),
        in_specs=[pl.BlockSpec((B, D), lambda j: (0, 0)),
                  pl.BlockSpec((D, ADALN_TN), lambda j: (0, j)),
                  pl.BlockSpec((1, ADALN_TN), lambda j: (0, j))],
        out_specs=pl.BlockSpec((B, ADALN_TN), lambda j: (0, j)),
        out_shape=jax.ShapeDtypeStruct((B, N), F32),
        compiler_params=_params("parallel"),
        name="adaln",
    )(c, w_ada, b_ada.reshape(1, N))


def _bucket_table():
    j = np.arange(2 * ATTN_BLOCK)[:, None]
    i = np.arange(ATTN_BLOCK)[None, :]
    dist = i + ATTN_BLOCK - j
    max_exact = N_BUCKETS // 2
    d = np.maximum(dist, 0)
    ratio = np.log(np.maximum(d, 1).astype(np.float32) / np.float32(max_exact)) / np.float32(
        math.log(MAX_DISTANCE / max_exact))
    large = max_exact + (ratio.astype(np.float32) * np.float32(N_BUCKETS - max_exact)).astype(np.int32)
    large = np.minimum(large, N_BUCKETS - 1)
    bucket = np.where(d < max_exact, d, large)
    in_window = (dist >= 0) & (dist < WINDOW)
    return np.where(in_window, bucket, -1).astype(np.int32)


def _relbias_kernel(tab_ref, bkt_ref, o_ref):
    h = pl.program_id(0)
    bkt = bkt_ref[...]
    acc = jnp.zeros(bkt.shape, F32)
    for b in range(N_BUCKETS):
        acc = jnp.where(bkt == b, tab_ref[b, h], acc)
    regular = jnp.where(bkt >= 0, acc * LOG2E, NEG_INF)
    key = lax.broadcasted_iota(jnp.int32, bkt.shape, 0)
    o_ref[0] = jnp.where(key >= ATTN_BLOCK, regular, NEG_INF)
    o_ref[1] = regular


def _relbias(rel_bias_table):
    bkt = jnp.asarray(_bucket_table())
    nk, nq = bkt.shape
    return pl.pallas_call(
        _relbias_kernel,
        grid=(N_Q_HEADS,),
        in_specs=[pl.BlockSpec(memory_space=pltpu.SMEM),
                  pl.BlockSpec(bkt.shape, lambda h: (0, 0))],
        out_specs=pl.BlockSpec((2, None, nk, nq), lambda h: (0, h // GQA_GROUP, h % 2, (h // 2) % 2)),
        out_shape=jax.ShapeDtypeStruct((2, N_KV_HEADS, 2 * nk, 2 * nq), F32),
        compiler_params=_params("parallel"),
        name="relbias",
    )(rel_bias_table, bkt)


SRC_GQ = ATTN_Q_W + 2 * ATTN_KV_W
SRC_GV = SRC_GQ + 2 * GLA_K_W
SRC_LR = SRC_GV + 2 * GLA_V_W
SRC_GA = SRC_LR + GLA_LOWRANK
WPACK_ROWS = 256


def _wpack_src_row(b):
    r = b * WPACK_ROWS
    src = jnp.where(r < COL_GV, SRC_GA + r,
                    jnp.where(r < COL_GQ, SRC_GV + (r - COL_GV),
                              jnp.where(r < COL_AQ, SRC_GQ + (r - COL_GQ),
                                        jnp.where(r < COL_LR, r - COL_AQ, SRC_LR))))
    return pl.multiple_of(src, 8)


def _wpack_kernel(w_ref, o_ref):
    r = pl.program_id(0) * WPACK_ROWS
    nvalid = jnp.where(r < COL_LR, WPACK_ROWS, jnp.where(r == COL_LR, GLA_LOWRANK, 0))
    row = lax.broadcasted_iota(jnp.int32, w_ref.shape, 0)
    o_ref[...] = jnp.where(row < nvalid, w_ref[...], 0.0).astype(BF16)


def _wpack(w_in_t):
    _, D = w_in_t.shape
    return pl.pallas_call(
        _wpack_kernel,
        grid=(PROJ_W // WPACK_ROWS,),
        in_specs=[pl.BlockSpec((pl.Element(WPACK_ROWS), pl.Element(D)), lambda b: (_wpack_src_row(b), 0))],
        out_specs=pl.BlockSpec((WPACK_ROWS, D), lambda b: (b, 0)),
        out_shape=jax.ShapeDtypeStruct((PROJ_W, D), BF16),
        compiler_params=_params("parallel"),
        name="wpack",
    )(w_in_t)


def _inproj_kernel(x_ref, mod_ref, g_ref, w_ref, o_ref, h_ref):
    @pl.when(pl.program_id(1) == 0)
    def _():
        scale = 1.0 + mod_ref[1:2, :]
        shift = mod_ref[0:1, :]
        gain = g_ref[...]

        def body(r, carry):
            rows = pl.ds(pl.multiple_of(r * 128, 128), 128)
            h_ref[rows, :] = (_rms(x_ref[rows, :], gain) * scale + shift).astype(BF16)
            return carry

        lax.fori_loop(0, INPROJ_TM // 128, body, 0)

    o_ref[...] = _dot_nt(h_ref[...], w_ref[...]).astype(BF16)


def _inproj(x2, mod3, gain, w):
    T, D = x2.shape
    S = T // mod3.shape[0]
    tm, tn = INPROJ_TM, INPROJ_TN
    return pl.pallas_call(
        _inproj_kernel,
        grid=(T // tm, PROJ_W // tn),
        in_specs=[pl.BlockSpec((tm, D), lambda i, j: (i, 0)),
                  pl.BlockSpec((None, 6, D), lambda i, j: (i // (S // tm), 0, 0)),
                  pl.BlockSpec((1, D), lambda i, j: (0, 0)),
                  pl.BlockSpec((tn, D), lambda i, j: (j, 0))],
        out_specs=pl.BlockSpec((tm, tn), lambda i, j: (i, j)),
        out_shape=jax.ShapeDtypeStruct((T, PROJ_W), BF16),
        scratch_shapes=[pltpu.VMEM((tm, D), BF16)],
        compiler_params=_params("parallel", "arbitrary"),
        name="inproj",
    )(x2, mod3, gain, w)


def _attn_kernel(q_ref, kc_ref, kp_ref, vc_ref, vp_ref, b0_ref, b1_ref, qg_ref, kg_ref, sink_ref, seg_ref,
                 o_ref):
    nsub = ATTN_TQ // ATTN_BLOCK
    blk, dh = ATTN_BLOCK, HEAD_DIM
    seg = seg_ref[...]

    def normed(x, gain):
        ssq = _dot((x * x).astype(BF16), seg)
        return x * lax.rsqrt(ssq * (1.0 / dh) + EPS) * gain

    kn = normed(jnp.concatenate([kp_ref[...], kc_ref[...]], axis=0).astype(F32), kg_ref[...])
    vt = jnp.concatenate([vp_ref[...], vc_ref[...]], axis=0).astype(F32).T.astype(BF16)
    lane = lax.broadcasted_iota(jnp.int32, (kn.shape[0], 2 * dh), 1)
    qgain = qg_ref[...] * (dh ** -0.5 * LOG2E)

    kzs, qns = [], []
    for g in range(N_KV_HEADS):
        pair = kn[:, (g // 2) * 2 * dh:(g // 2 + 1) * 2 * dh]
        own = jnp.where(lane >= dh if g % 2 else lane < dh, pair, 0.0)
        swapped = pltpu.roll(own, dh, axis=1)
        kz = [own, swapped] if g % 2 == 0 else [swapped, own]
        kzs.append([z.astype(BF16) for z in kz])
        qns.append(normed(q_ref[:, g * 4 * dh:(g + 1) * 4 * dh].astype(F32), qgain).astype(BF16))

    def scores(g, sb):
        bref = b0_ref if sb == 0 else b1_ref
        band = slice(sb * blk, sb * blk + 2 * blk)
        rows = slice(sb * blk, (sb + 1) * blk)
        kband = jnp.concatenate([kzs[g][0][band], kzs[g][1][band]], axis=0)
        qr = jnp.concatenate([qns[g][rows, 0:2 * dh], qns[g][rows, 2 * dh:4 * dh]], axis=0)
        return _dot_nt(kband, qr) + bref[g]

    def finish(g, sb, s):
        band = slice(sb * blk, sb * blk + 2 * blk)
        rows = slice(sb * blk, (sb + 1) * blk)
        ps, inv = [], []
        for e in range(2):
            se = s[e * 2 * blk:(e + 1) * 2 * blk]
            sink = sink_ref[g, e] * LOG2E
            m = jnp.maximum(jnp.max(se, axis=0, keepdims=True), sink)
            p = jnp.exp2(se - m)
            inv.append(1.0 / (jnp.sum(p, axis=0, keepdims=True) + jnp.exp2(sink - m)))
            ps.append(p.astype(BF16))
        ot = _dot(vt[g * dh:(g + 1) * dh, band], jnp.concatenate(ps, axis=1))
        ot = ot * jnp.concatenate(inv, axis=1)
        o = jnp.concatenate([ot[:, 0:2 * blk], ot[:, 2 * blk:4 * blk]], axis=0).T
        o_ref[rows, (2 * g) * 2 * dh:(2 * g + 1) * 2 * dh] = o[0:blk].astype(BF16)
        o_ref[rows, (2 * g + 1) * 2 * dh:(2 * g + 2) * 2 * dh] = o[blk:2 * blk].astype(BF16)

    units = [(g, sb) for g in range(N_KV_HEADS) for sb in range(nsub)]
    s_next = scores(*units[0])
    for n, unit in enumerate(units):
        s_cur = s_next
        if n + 1 < len(units):
            s_next = scores(*units[n + 1])
        finish(*unit, s_cur)


def _attn(proj3, biasm, q_gain, k_gain, sinks):
    B, S, _ = proj3.shape
    tq, blk = ATTN_TQ, ATTN_BLOCK
    nsub = tq // blk
    kvw = ATTN_KV_W
    seg_id = np.arange(kvw) // HEAD_DIM
    seg = jnp.asarray(seg_id[:, None] == seg_id[None, :], dtype=BF16)
    sink_rows = jnp.repeat(sinks.reshape(N_KV_HEADS, 2, 2).transpose(0, 2, 1), blk, axis=-1)
    sink_rows = sink_rows.reshape(N_KV_HEADS, 2, 1, 2 * blk)

    def prev(i):
        return jnp.maximum(i * nsub - 1, 0)

    bias_block = (None,) + biasm.shape[1:]
    return pl.pallas_call(
        _attn_kernel,
        grid=(B, S // tq),
        in_specs=[pl.BlockSpec((None, tq, ATTN_Q_W), lambda b, i: (b, i, COL_AQ // ATTN_Q_W)),
                  pl.BlockSpec((None, tq, kvw), lambda b, i: (b, i, COL_AK // kvw)),
                  pl.BlockSpec((None, blk, kvw), lambda b, i: (b, prev(i), COL_AK // kvw)),
                  pl.BlockSpec((None, tq, kvw), lambda b, i: (b, i, COL_AV // kvw)),
                  pl.BlockSpec((None, blk, kvw), lambda b, i: (b, prev(i), COL_AV // kvw)),
                  pl.BlockSpec(bias_block, lambda b, i: (jnp.minimum(i, 1), 0, 0, 0)),
                  pl.BlockSpec(bias_block, lambda b, i: (1, 0, 0, 0)),
                  pl.BlockSpec((1, kvw), lambda b, i: (0, 0)),
                  pl.BlockSpec((1, kvw), lambda b, i: (0, 0)),
                  pl.BlockSpec(sink_rows.shape, lambda b, i: (0, 0, 0, 0)),
                  pl.BlockSpec(seg.shape, lambda b, i: (0, 0))],
        out_specs=pl.BlockSpec((None, tq, ATTN_Q_W), lambda b, i: (b, i, 0)),
        out_shape=jax.ShapeDtypeStruct((B, S, ATTN_Q_W), BF16),
        compiler_params=_params("parallel", "parallel"),
        name="attn",
    )(proj3, proj3, proj3, proj3, proj3, biasm, biasm, jnp.tile(q_gain, (1, GQA_GROUP)),
      jnp.tile(k_gain, (1, N_KV_HEADS)), sink_rows, seg)


def _split2(x):
    hi = x.astype(BF16)
    return hi, (x - hi.astype(F32)).astype(BF16)


def _gla_head(q, k, v, r, lr, wgk, bgk, gain, state_ref, masks, store):
    C = GLA_CHUNK
    tril, m_same, m_next, m_far, eye = masks
    z = _dot(lr, wgk) + bgk
    yield
    gl = (jnp.minimum(z, 0.0) - jnp.log(1.0 + jnp.exp(-jnp.abs(z)))) * (1.0 / GLA_NORMALIZER)
    hi, lo = _split2(gl)
    g = _dot(tril, hi) + _dot(tril, lo)
    yield
    t = [g[(c + 1) * C - 1:(c + 1) * C] for c in range(4)]
    t_rows = jnp.concatenate([jnp.broadcast_to(tc, (C, tc.shape[1])) for tc in t], axis=0)

    q_dec = q.astype(F32) * (GLA_DK ** -0.5) * jnp.exp(g)
    kf = k.astype(F32)
    k_inv = (kf * jnp.exp(-g)).astype(BF16)
    k_end = kf * jnp.exp(t_rows - g)
    qd = [q_dec[c * C:(c + 1) * C] for c in range(4)]
    ke = [k_end[c * C:(c + 1) * C] for c in range(4)]

    def rows(parts):
        return jnp.concatenate(parts, axis=0).astype(BF16)

    q_b, k_b = q_dec.astype(BF16), k_end.astype(BF16)
    q_far = rows([qd[0], qd[1], qd[2], qd[3] * jnp.exp(t[2])])
    k_far = rows([ke[0] * jnp.exp(t[1]), ke[1], ke[2], ke[3]])
    q_abs = rows([qd[0], qd[1] * jnp.exp(t[0]), qd[2] * jnp.exp(t[0] + t[1]), qd[3] * jnp.exp(t[0] + t[1] + t[2])])
    k_abs = rows([ke[0] * jnp.exp(t[1] + t[2] + t[3]), ke[1] * jnp.exp(t[2] + t[3]), ke[2] * jnp.exp(t[3]), ke[3]])

    a_same, a_next, a_far = _dot_nt(q_b, k_inv), _dot_nt(q_b, k_b), _dot_nt(q_far, k_far)
    state = state_ref[...]
    o_state = _dot(q_abs, state.astype(BF16))
    update = _dot_tn(k_abs, v)
    yield
    a = jnp.where(m_same, a_same, jnp.where(m_next, a_next, jnp.where(m_far, a_far, 0.0)))
    o = _dot(a.astype(BF16), v) + o_state
    yield

    decay = jnp.exp(t[0] + t[1] + t[2] + t[3])
    decay_col = jnp.sum(jnp.where(eye, jnp.broadcast_to(decay, eye.shape), 0.0), axis=1, keepdims=True)
    state_ref[...] = decay_col * state + update

    rf = r.astype(F32)
    store((_rms(o, gain) * (rf * jax.nn.sigmoid(rf))).astype(BF16))


def _gla_kernel(q_ref, k_ref, v_ref, r_ref, lr_ref, wgk_ref, bgk_ref, gain_ref, o_ref, state_ref):
    R, C, dk, dv = GLA_ROWS, GLA_CHUNK, GLA_DK, GLA_DV

    @pl.when(pl.program_id(2) == 0)
    def _():
        state_ref[...] = jnp.zeros_like(state_ref)

    row = lax.broadcasted_iota(jnp.int32, (R, R), 0)
    col = lax.broadcasted_iota(jnp.int32, (R, R), 1)
    ci, cj = row // C, col // C
    m_same = (row >= col) & (ci == cj)
    m_next = (ci == cj + 1) & (ci != 2)
    m_far = (ci >= 2) & (cj <= 1)
    eye = lax.broadcasted_iota(jnp.int32, (dk, dk), 0) == lax.broadcasted_iota(jnp.int32, (dk, dk), 1)
    masks = (m_same.astype(BF16), m_same, m_next, m_far, eye)

    lr = lr_ref[...]
    heads = []
    for h in range(GLA_HEADS_PER_STEP):
        ks, vs = slice(h * dk, (h + 1) * dk), slice(h * dv, (h + 1) * dv)
        heads.append(_gla_head(q_ref[:, ks], k_ref[:, ks], v_ref[:, vs], r_ref[:, vs], lr, wgk_ref[:, ks],
                               bgk_ref[:, ks], gain_ref[...], state_ref.at[h], masks,
                               functools.partial(o_ref.__setitem__, (slice(None), vs))))
    for _ in itertools.zip_longest(*heads):
        pass


def _gla(proj3, wgk, bgk, gain):
    B, S, _ = proj3.shape
    R, hps = GLA_ROWS, GLA_HEADS_PER_STEP
    dk, dv = hps * GLA_DK, hps * GLA_DV
    return pl.pallas_call(
        _gla_kernel,
        grid=(B, GLA_HEADS // hps, S // R),
        in_specs=[pl.BlockSpec((None, R, dk), lambda b, h, t: (b, t, COL_GQ // dk + h)),
                  pl.BlockSpec((None, R, dk), lambda b, h, t: (b, t, COL_GK // dk + h)),
                  pl.BlockSpec((None, R, dv), lambda b, h, t: (b, t, COL_GV // dv + h)),
                  pl.BlockSpec((None, R, dv), lambda b, h, t: (b, t, COL_GR // dv + h)),
                  pl.BlockSpec((None, R, LR_PAD), lambda b, h, t: (b, t, COL_LR // LR_PAD)),
                  pl.BlockSpec((LR_PAD, dk), lambda b, h, t: (0, h)),
                  pl.BlockSpec((1, dk), lambda b, h, t: (0, h)),
                  pl.BlockSpec((1, GLA_DV), lambda b, h, t: (0, 0))],
        out_specs=pl.BlockSpec((None, R, dv), lambda b, h, t: (b, t, h)),
        out_shape=jax.ShapeDtypeStruct((B, S, GLA_V_W), BF16),
        scratch_shapes=[pltpu.VMEM((hps, GLA_DK, GLA_DV), F32)],
        compiler_params=_params("parallel", "parallel", "arbitrary"),
        name="gla",
    )(proj3, proj3, proj3, proj3, proj3, wgk, bgk, gain)


def _merge_kernel(ya_ref, yg_ref, ga_ref, gb_ref, x_ref, mod_ref, g2_ref, wa_ref, wg_ref, wo_ref,
                  x1_ref, h2_ref):
    def merged_branches(rows):
        ma, mg = _dot(ya_ref[rows, :], wa_ref[...]), _dot(yg_ref[rows, :], wg_ref[...])
        ga = jax.nn.sigmoid(ga_ref[rows, :].astype(F32))
        gb = jax.nn.sigmoid(gb_ref[rows, :].astype(F32))
        return (ga * ma + gb * mg).astype(BF16)

    def project(rows, merged):
        x1 = x_ref[rows, :] + mod_ref[2:3, :] * _dot(merged, wo_ref[...])
        x1_ref[rows, :] = x1
        h2_ref[rows, :] = (_rms(x1, g2_ref[...]) * (1.0 + mod_ref[4:5, :]) + mod_ref[3:4, :]).astype(BF16)

    for r0 in range(0, MERGE_TM, MERGE_SUB):
        rows = slice(r0, r0 + MERGE_SUB)
        project(rows, merged_branches(rows))


def _merge(ya, yg, proj, x2, mod3, gain2, wa, wg, wo):
    T, D = x2.shape
    S = T // mod3.shape[0]
    tm = MERGE_TM
    once = pl.Buffered(1)
    return pl.pallas_call(
        _merge_kernel,
        grid=(T // tm,),
        in_specs=[pl.BlockSpec((tm, ATTN_Q_W), lambda i: (i, 0)),
                  pl.BlockSpec((tm, GLA_V_W), lambda i: (i, 0)),
                  pl.BlockSpec((tm, D), lambda i: (i, COL_GA // D)),
                  pl.BlockSpec((tm, D), lambda i: (i, COL_GB // D)),
                  pl.BlockSpec((tm, D), lambda i: (i, 0)),
                  pl.BlockSpec((None, 6, D), lambda i: (i // (S // tm), 0, 0)),
                  pl.BlockSpec((1, D), lambda i: (0, 0)),
                  pl.BlockSpec((ATTN_Q_W, D), lambda i: (0, 0), pipeline_mode=once),
                  pl.BlockSpec((GLA_V_W, D), lambda i: (0, 0), pipeline_mode=once),
                  pl.BlockSpec((D, D), lambda i: (0, 0), pipeline_mode=once)],
        out_specs=[pl.BlockSpec((tm, D), lambda i: (i, 0)),
                   pl.BlockSpec((tm, D), lambda i: (i, 0))],
        out_shape=[jax.ShapeDtypeStruct((T, D), F32), jax.ShapeDtypeStruct((T, D), BF16)],
        compiler_params=_params("parallel"),
        name="merge",
    )(ya, yg, proj, proj, x2, mod3, gain2, wa, wg, wo)


def _ffn_up_kernel(h_ref, halo_ref, wa_ref, wb_ref, cwa_ref, cwb_ref, cba_ref, cbb_ref, o_ref,
                   hs_ref, ua_ref, ub_ref, *, tiles_per_seq):
    tm, halo = FFN_TM, FFN_HALO

    @pl.when(pl.program_id(1) == 0)
    def _():
        first = (pl.program_id(0) % tiles_per_seq) == 0
        hs_ref[0:halo, :] = jnp.where(first, jnp.zeros_like(halo_ref), halo_ref[...])
        hs_ref[halo:, :] = h_ref[...]

    def conv(u_ref, cw_ref, cb_ref, r0, n):
        u = u_ref[halo + r0 - 8:halo + r0 + n, :]
        y = cb_ref[...] + cw_ref[0:1, :] * pltpu.roll(u, 2, axis=0)[8:]
        y = y + cw_ref[1:2, :] * pltpu.roll(u, 1, axis=0)[8:]
        return y + cw_ref[2:3, :] * u[8:]

    ua_ref[...] = _dot(hs_ref[...], wa_ref[...])
    ub_ref[...] = _dot(hs_ref[...], wb_ref[...])
    for r0 in range(0, tm, FFN_EPI_ROWS):
        ya = conv(ua_ref, cwa_ref, cba_ref, r0, FFN_EPI_ROWS)
        yb = conv(ub_ref, cwb_ref, cbb_ref, r0, FFN_EPI_ROWS)
        o_ref[r0:r0 + FFN_EPI_ROWS, :] = (ya * jax.nn.sigmoid(ya) * yb).astype(BF16)


def _ffn_up(h2, S, w_up, conv_w, conv_b):
    T, D = h2.shape
    tm, tn, halo = FFN_TM, FFN_TN, FFN_HALO
    nj = D_FF // tn
    return pl.pallas_call(
        functools.partial(_ffn_up_kernel, tiles_per_seq=S // tm),
        grid=(T // tm, D_FF // tn),
        in_specs=[pl.BlockSpec((tm, D), lambda i, j: (i, 0)),
                  pl.BlockSpec((halo, D), lambda i, j: (jnp.maximum(i * (tm // halo) - 1, 0), 0)),
                  pl.BlockSpec((D, tn), lambda i, j: (0, j)),
                  pl.BlockSpec((D, tn), lambda i, j: (0, j + nj)),
                  pl.BlockSpec((3, tn), lambda i, j: (0, j)),
                  pl.BlockSpec((3, tn), lambda i, j: (0, j + nj)),
                  pl.BlockSpec((1, tn), lambda i, j: (0, j)),
                  pl.BlockSpec((1, tn), lambda i, j: (0, j + nj))],
        out_specs=pl.BlockSpec((tm, tn), lambda i, j: (i, j)),
        out_shape=jax.ShapeDtypeStruct((T, D_FF), BF16),
        scratch_shapes=[pltpu.VMEM((tm + halo, D), BF16),
                        pltpu.VMEM((tm + halo, tn), F32),
                        pltpu.VMEM((tm + halo, tn), F32)],
        compiler_params=_params("parallel", "arbitrary"),
        name="ffn_up",
    )(h2, h2, w_up, w_up, conv_w, conv_w, conv_b, conv_b)


def _ffn_down_kernel(a_ref, w_ref, x_ref, mod_ref, o_ref):
    o_ref[...] = x_ref[...] + mod_ref[5:6, :] * _dot(a_ref[...], w_ref[...])


def _ffn_down(act, wd, x1, mod3):
    T, D = x1.shape
    S = T // mod3.shape[0]
    tm = DOWN_TM
    return pl.pallas_call(
        _ffn_down_kernel,
        grid=(T // tm,),
        in_specs=[pl.BlockSpec((tm, D_FF), lambda i: (i, 0)),
                  pl.BlockSpec((D_FF, D), lambda i: (0, 0), pipeline_mode=pl.Buffered(1)),
                  pl.BlockSpec((tm, D), lambda i: (i, 0)),
                  pl.BlockSpec((None, 6, D), lambda i: (i // (S // tm), 0, 0))],
        out_specs=pl.BlockSpec((tm, D), lambda i: (i, 0)),
        out_shape=jax.ShapeDtypeStruct((T, D), F32),
        compiler_params=_params("parallel"),
        name="ffn_down",
    )(act, wd, x1, mod3)


def _layer(x2, B, mod3, biasm, norm1_gain, w_in, q_norm_gain, k_norm_gain, attn_sinks, w_gk_up, b_gk,
           gla_norm_gain, w_branch_attn, w_branch_gla, w_out, norm2_gain, w_ffn_up, ffn_conv_w,
           ffn_conv_b, w_ffn_down):
    T, D = x2.shape
    S = T // B
    w_p = _wpack(w_in.T)
    wgk = jnp.concatenate([w_gk_up, jnp.zeros((LR_PAD - GLA_LOWRANK, GLA_K_W), w_gk_up.dtype)],
                          axis=0).astype(BF16)

    proj = _inproj(x2, mod3, norm1_gain.reshape(1, D), w_p)
    proj3 = proj.reshape(B, S, PROJ_W)
    ya = _attn(proj3, biasm, q_norm_gain.reshape(1, HEAD_DIM), k_norm_gain.reshape(1, HEAD_DIM), attn_sinks)
    yg = _gla(proj3, wgk, b_gk.reshape(1, GLA_K_W), gla_norm_gain.reshape(1, GLA_DV))
    x1, h2 = _merge(ya.reshape(T, ATTN_Q_W), yg.reshape(T, GLA_V_W), proj, x2, mod3,
                    norm2_gain.reshape(1, D), w_branch_attn.astype(BF16), w_branch_gla.astype(BF16),
                    w_out.astype(BF16))
    act = _ffn_up(h2, S, w_ffn_up.astype(BF16), ffn_conv_w, ffn_conv_b.reshape(1, 2 * D_FF))
    return _ffn_down(act, w_ffn_down.astype(BF16), x1, mod3)


def kernel(x, c, rel_bias_table, w_ada, b_ada, norm1_gain, w_in, q_norm_gain, k_norm_gain, attn_sinks,
           w_gk_up, b_gk, gla_norm_gain, w_branch_attn, w_branch_gla, w_out, norm2_gain, w_ffn_up,
           ffn_conv_w, ffn_conv_b, w_ffn_down):
    B, S, D = x.shape
    depth = w_in.shape[0]
    biasm = _relbias(rel_bias_table)
    x2 = x.reshape(B * S, D)
    for l in range(depth):
        mod3 = _adaln(c, w_ada[l], b_ada[l]).reshape(B, 6, D)
        x2 = _layer(x2, B, mod3, biasm, norm1_gain[l], w_in[l], q_norm_gain[l], k_norm_gain[l],
                    attn_sinks[l], w_gk_up[l], b_gk[l], gla_norm_gain[l], w_branch_attn[l],
                    w_branch_gla[l], w_out[l], norm2_gain[l], w_ffn_up[l], ffn_conv_w[l], ffn_conv_b[l],
                    w_ffn_down[l])
    return x2.reshape(B, S, D)
```

```python
import functools
import itertools
import math

import numpy as np
import jax
import jax.numpy as jnp
from jax import lax
from jax.experimental import pallas as pl
from jax.experimental.pallas import tpu as pltpu

F32 = jnp.float32
BF16 = jnp.bfloat16

D_MODEL = 2048
N_Q_HEADS = 16
N_KV_HEADS = 4
GQA_GROUP = N_Q_HEADS // N_KV_HEADS
HEAD_DIM = 64
WINDOW = 128
ATTN_BLOCK = 128
N_BUCKETS = 32
MAX_DISTANCE = 128
GLA_HEADS = 4
GLA_DK = 256
GLA_DV = 512
GLA_LOWRANK = 16
GLA_NORMALIZER = 16.0
GLA_CHUNK = 64
D_FF = 5632
EPS = 1e-6
NEG_INF = -1e30
LOG2E = math.log2(math.e)

ATTN_Q_W = N_Q_HEADS * HEAD_DIM
ATTN_KV_W = N_KV_HEADS * HEAD_DIM
GLA_K_W = GLA_HEADS * GLA_DK
GLA_V_W = GLA_HEADS * GLA_DV

COL_GA = 0
COL_GB = COL_GA + D_MODEL
COL_GV = COL_GB + D_MODEL
COL_GR = COL_GV + GLA_V_W
COL_GQ = COL_GR + GLA_V_W
COL_GK = COL_GQ + GLA_K_W
COL_AQ = COL_GK + GLA_K_W
COL_AK = COL_AQ + ATTN_Q_W
COL_AV = COL_AK + ATTN_KV_W
COL_LR = COL_AV + ATTN_KV_W
LR_PAD = 128
PROJ_W = 12288

VMEM_LIMIT = 60 * 1024 * 1024

ADALN_TN = 1024
INPROJ_TM, INPROJ_TN = 1024, 2048
INPROJ_NORM_ROWS = 256
ATTN_TQ = 256
GLA_ROWS = 256
GLA_HEADS_PER_STEP = 4
MERGE_TM, MERGE_SUB = 512, 256
FFN_TM, FFN_TN = 1024, 512
FFN_HALO = 16
FFN_EPI_ROWS = 256
DOWN_TM = 512


def _params(*sem):
    return pltpu.CompilerParams(dimension_semantics=sem, vmem_limit_bytes=VMEM_LIMIT)


def _dot(a, b):
    return jnp.dot(a, b, preferred_element_type=F32)


def _dot_nt(a, b):
    return lax.dot_general(a, b, (((1,), (1,)), ((), ())), preferred_element_type=F32)


def _dot_tn(a, b):
    return lax.dot_general(a, b, (((0,), (0,)), ((), ())), preferred_element_type=F32)


def _rms(x, gain):
    return x * lax.rsqrt(jnp.mean(x * x, axis=-1, keepdims=True) + EPS) * gain


def _adaln_kernel(c_ref, w_ref, b_ref, o_ref):
    c = c_ref[...]
    ca = c * jax.nn.sigmoid(c)
    o_ref[...] = _dot(ca.astype(BF16), w_ref[...].astype(BF16)) + b_ref[...]


def _adaln(c, w_ada, b_ada):
    B, D = c.shape
    N = w_ada.shape[1]
    return pl.pallas_call(
        _adaln_kernel,
        grid=(N // ADALN_TN,),
        in_specs=[pl.BlockSpec((B, D), lambda j: (0, 0)),
                  pl.BlockSpec((D, ADALN_TN), lambda j: (0, j)),
                  pl.BlockSpec((1, ADALN_TN), lambda j: (0, j))],
        out_specs=pl.BlockSpec((B, ADALN_TN), lambda j: (0, j)),
        out_shape=jax.ShapeDtypeStruct((B, N), F32),
        compiler_params=_params("parallel"),
        name="adaln",
    )(c, w_ada, b_ada.reshape(1, N))


def _bucket_table():
    j = np.arange(2 * ATTN_BLOCK)[:, None]
    i = np.arange(ATTN_BLOCK)[None, :]
    dist = i + ATTN_BLOCK - j
    max_exact = N_BUCKETS // 2
    d = np.maximum(dist, 0)
    ratio = np.log(np.maximum(d, 1).astype(np.float32) / np.float32(max_exact)) / np.float32(
        math.log(MAX_DISTANCE / max_exact))
    large = max_exact + (ratio.astype(np.float32) * np.float32(N_BUCKETS - max_exact)).astype(np.int32)
    large = np.minimum(large, N_BUCKETS - 1)
    bucket = np.where(d < max_exact, d, large)
    in_window = (dist >= 0) & (dist < WINDOW)
    return np.where(in_window, bucket, -1).astype(np.int32)


def _relbias_kernel(tab_ref, bkt_ref, o_ref):
    h = pl.program_id(0)
    bkt = bkt_ref[...]
    acc = jnp.zeros(bkt.shape, F32)
    for b in range(N_BUCKETS):
        acc = jnp.where(bkt == b, tab_ref[b, h], acc)
    regular = jnp.where(bkt >= 0, acc * LOG2E, NEG_INF)
    key = lax.broadcasted_iota(jnp.int32, bkt.shape, 0)
    o_ref[0] = jnp.where(key >= ATTN_BLOCK, regular, NEG_INF)
    o_ref[1] = regular


def _relbias(rel_bias_table):
    bkt = jnp.asarray(_bucket_table())
    nk, nq = bkt.shape
    return pl.pallas_call(
        _relbias_kernel,
        grid=(N_Q_HEADS,),
        in_specs=[pl.BlockSpec(memory_space=pltpu.SMEM),
                  pl.BlockSpec(bkt.shape, lambda h: (0, 0))],
        out_specs=pl.BlockSpec((2, None, nk, nq), lambda h: (0, h // GQA_GROUP, h % 2, (h // 2) % 2)),
        out_shape=jax.ShapeDtypeStruct((2, N_KV_HEADS, 2 * nk, 2 * nq), F32),
        compiler_params=_params("parallel"),
        name="relbias",
    )(rel_bias_table, bkt)


SRC_GQ = ATTN_Q_W + 2 * ATTN_KV_W
SRC_GV = SRC_GQ + 2 * GLA_K_W
SRC_LR = SRC_GV + 2 * GLA_V_W
SRC_GA = SRC_LR + GLA_LOWRANK
WPACK_ROWS = 512


def _wpack_src_row(b):
    r = b * WPACK_ROWS
    src = jnp.where(r < COL_GV, SRC_GA + r,
                    jnp.where(r < COL_GQ, SRC_GV + (r - COL_GV),
                              jnp.where(r < COL_AQ, SRC_GQ + (r - COL_GQ),
                                        jnp.where(r < COL_LR, r - COL_AQ, SRC_LR))))
    return pl.multiple_of(src, 8)


def _wpack_kernel(w_ref, o_ref):
    r = pl.program_id(0) * WPACK_ROWS
    nvalid = jnp.where(r < COL_LR, WPACK_ROWS, jnp.where(r == COL_LR, GLA_LOWRANK, 0))
    row = lax.broadcasted_iota(jnp.int32, w_ref.shape, 0)
    o_ref[...] = jnp.where(row < nvalid, w_ref[...], 0.0).astype(BF16)


def _wpack(w_in_t):
    _, D = w_in_t.shape
    return pl.pallas_call(
        _wpack_kernel,
        grid=(PROJ_W // WPACK_ROWS,),
        in_specs=[pl.BlockSpec((pl.Element(WPACK_ROWS), pl.Element(D)), lambda b: (_wpack_src_row(b), 0))],
        out_specs=pl.BlockSpec((WPACK_ROWS, D), lambda b: (b, 0)),
        out_shape=jax.ShapeDtypeStruct((PROJ_W, D), BF16),
        compiler_params=_params("parallel"),
        name="wpack",
    )(w_in_t)


def _norm_modulate(x, mod_ref, gain):
    return (_rms(x, gain) * (1.0 + mod_ref[1:2, :]) + mod_ref[0:1, :]).astype(BF16)


def _hnorm_kernel(x_ref, mod_ref, g_ref, o_ref):
    o_ref[...] = _norm_modulate(x_ref[...], mod_ref, g_ref[...])


def _hnorm_first(x2, mod3, gain):
    D = x2.shape[1]
    nr = INPROJ_NORM_ROWS
    return pl.pallas_call(
        _hnorm_kernel,
        grid=(INPROJ_TM // nr,),
        in_specs=[pl.BlockSpec((nr, D), lambda r: (r, 0)),
                  pl.BlockSpec((None, 6, D), lambda r: (0, 0, 0)),
                  pl.BlockSpec((1, D), lambda r: (0, 0))],
        out_specs=pl.BlockSpec((nr, D), lambda r: (r, 0)),
        out_shape=jax.ShapeDtypeStruct((INPROJ_TM, D), BF16),
        compiler_params=_params("parallel"),
        name="hnorm_first",
    )(x2, mod3, gain)


def _inproj_kernel(xn_ref, modn_ref, g_ref, w_ref, h0_ref, o_ref, ha_ref, hb_ref):
    i, j = pl.program_id(0), pl.program_id(1)
    nr = INPROJ_NORM_ROWS

    @pl.when((i == 0) & (j == 0))
    def _():
        ha_ref[...] = h0_ref[...]

    def step(cur_ref, nxt_ref):
        o_ref[...] = _dot_nt(cur_ref[...], w_ref[...]).astype(BF16)
        chunk = jnp.minimum(j, INPROJ_TM // nr - 1)
        rows = pl.ds(pl.multiple_of(chunk * nr, nr), nr)
        nxt_ref[rows, :] = _norm_modulate(xn_ref[...], modn_ref, g_ref[...])

    @pl.when(i % 2 == 0)
    def _():
        step(ha_ref, hb_ref)

    @pl.when(i % 2 == 1)
    def _():
        step(hb_ref, ha_ref)


def _inproj(x2, mod3, gain, w):
    T, D = x2.shape
    S = T // mod3.shape[0]
    tm, tn, nr = INPROJ_TM, INPROJ_TN, INPROJ_NORM_ROWS
    n_i, chunks = T // tm, tm // nr
    assert PROJ_W // tn >= chunks

    def next_tile(i):
        return jnp.minimum(i + 1, n_i - 1)

    return pl.pallas_call(
        _inproj_kernel,
        grid=(n_i, PROJ_W // tn),
        in_specs=[pl.BlockSpec((nr, D), lambda i, j: (next_tile(i) * chunks + jnp.minimum(j, chunks - 1), 0)),
                  pl.BlockSpec((None, 6, D), lambda i, j: (next_tile(i) // (S // tm), 0, 0)),
                  pl.BlockSpec((1, D), lambda i, j: (0, 0)),
                  pl.BlockSpec((tn, D), lambda i, j: (j, 0)),
                  pl.BlockSpec((tm, D), lambda i, j: (0, 0), pipeline_mode=pl.Buffered(1))],
        out_specs=pl.BlockSpec((tm, tn), lambda i, j: (i, j)),
        out_shape=jax.ShapeDtypeStruct((T, PROJ_W), BF16),
        scratch_shapes=[pltpu.VMEM((tm, D), BF16), pltpu.VMEM((tm, D), BF16)],
        compiler_params=_params("arbitrary", "arbitrary"),
        name="inproj",
    )(x2, mod3, gain, w, _hnorm_first(x2, mod3, gain))


def _attn_kernel(q_ref, kc_ref, kp_ref, vc_ref, vp_ref, b0_ref, b1_ref, qg_ref, kg_ref, sink_ref, seg_ref,
                 o_ref):
    nsub = ATTN_TQ // ATTN_BLOCK
    blk, dh = ATTN_BLOCK, HEAD_DIM
    seg = seg_ref[...]

    def normed(x, gain):
        ssq = _dot((x * x).astype(BF16), seg)
        return x * lax.rsqrt(ssq * (1.0 / dh) + EPS) * gain

    kn = normed(jnp.concatenate([kp_ref[...], kc_ref[...]], axis=0).astype(F32), kg_ref[...])
    vt = jnp.concatenate([vp_ref[...], vc_ref[...]], axis=0).astype(F32).T.astype(BF16)
    lane = lax.broadcasted_iota(jnp.int32, (kn.shape[0], 2 * dh), 1)
    qgain = qg_ref[...] * (dh ** -0.5 * LOG2E)

    kzs, qns = [], []
    for g in range(N_KV_HEADS):
        pair = kn[:, (g // 2) * 2 * dh:(g // 2 + 1) * 2 * dh]
        own = jnp.where(lane >= dh if g % 2 else lane < dh, pair, 0.0)
        swapped = pltpu.roll(own, dh, axis=1)
        kz = [own, swapped] if g % 2 == 0 else [swapped, own]
        kzs.append([z.astype(BF16) for z in kz])
        qns.append(normed(q_ref[:, g * 4 * dh:(g + 1) * 4 * dh].astype(F32), qgain).astype(BF16))

    def scores(g, sb):
        bref = b0_ref if sb == 0 else b1_ref
        band = slice(sb * blk, sb * blk + 2 * blk)
        rows = slice(sb * blk, (sb + 1) * blk)
        kband = jnp.concatenate([kzs[g][0][band], kzs[g][1][band]], axis=0)
        qr = jnp.concatenate([qns[g][rows, 0:2 * dh], qns[g][rows, 2 * dh:4 * dh]], axis=0)
        return _dot_nt(kband, qr) + bref[g]

    def finish(g, sb, s):
        band = slice(sb * blk, sb * blk + 2 * blk)
        rows = slice(sb * blk, (sb + 1) * blk)
        ps, inv = [], []
        for e in range(2):
            se = s[e * 2 * blk:(e + 1) * 2 * blk]
            sink = sink_ref[g, e] * LOG2E
            m = jnp.maximum(jnp.max(se, axis=0, keepdims=True), sink)
            p = jnp.exp2(se - m)
            inv.append(1.0 / (jnp.sum(p, axis=0, keepdims=True) + jnp.exp2(sink - m)))
            ps.append(p.astype(BF16))
        ot = _dot(vt[g * dh:(g + 1) * dh, band], jnp.concatenate(ps, axis=1))
        ot = ot * jnp.concatenate(inv, axis=1)
        o = jnp.concatenate([ot[:, 0:2 * blk], ot[:, 2 * blk:4 * blk]], axis=0).T
        o_ref[rows, (2 * g) * 2 * dh:(2 * g + 1) * 2 * dh] = o[0:blk].astype(BF16)
        o_ref[rows, (2 * g + 1) * 2 * dh:(2 * g + 2) * 2 * dh] = o[blk:2 * blk].astype(BF16)

    units = [(g, sb) for g in range(N_KV_HEADS) for sb in range(nsub)]
    s_next = scores(*units[0])
    for n, unit in enumerate(units):
        s_cur = s_next
        if n + 1 < len(units):
            s_next = scores(*units[n + 1])
        finish(*unit, s_cur)


def _attn(proj3, biasm, q_gain, k_gain, sinks):
    B, S, _ = proj3.shape
    tq, blk = ATTN_TQ, ATTN_BLOCK
    nsub = tq // blk
    kvw = ATTN_KV_W
    seg_id = np.arange(kvw) // HEAD_DIM
    seg = jnp.asarray(seg_id[:, None] == seg_id[None, :], dtype=BF16)
    sink_rows = jnp.repeat(sinks.reshape(N_KV_HEADS, 2, 2).transpose(0, 2, 1), blk, axis=-1)
    sink_rows = sink_rows.reshape(N_KV_HEADS, 2, 1, 2 * blk)

    def prev(i):
        return jnp.maximum(i * nsub - 1, 0)

    bias_block = (None,) + biasm.shape[1:]
    return pl.pallas_call(
        _attn_kernel,
        grid=(B, S // tq),
        in_specs=[pl.BlockSpec((None, tq, ATTN_Q_W), lambda b, i: (b, i, COL_AQ // ATTN_Q_W)),
                  pl.BlockSpec((None, tq, kvw), lambda b, i: (b, i, COL_AK // kvw)),
                  pl.BlockSpec((None, blk, kvw), lambda b, i: (b, prev(i), COL_AK // kvw)),
                  pl.BlockSpec((None, tq, kvw), lambda b, i: (b, i, COL_AV // kvw)),
                  pl.BlockSpec((None, blk, kvw), lambda b, i: (b, prev(i), COL_AV // kvw)),
                  pl.BlockSpec(bias_block, lambda b, i: (jnp.minimum(i, 1), 0, 0, 0)),
                  pl.BlockSpec(bias_block, lambda b, i: (1, 0, 0, 0)),
                  pl.BlockSpec((1, kvw), lambda b, i: (0, 0)),
                  pl.BlockSpec((1, kvw), lambda b, i: (0, 0)),
                  pl.BlockSpec(sink_rows.shape, lambda b, i: (0, 0, 0, 0)),
                  pl.BlockSpec(seg.shape, lambda b, i: (0, 0))],
        out_specs=pl.BlockSpec((None, tq, ATTN_Q_W), lambda b, i: (b, i, 0)),
        out_shape=jax.ShapeDtypeStruct((B, S, ATTN_Q_W), BF16),
        compiler_params=_params("parallel", "parallel"),
        name="attn",
    )(proj3, proj3, proj3, proj3, proj3, biasm, biasm, jnp.tile(q_gain, (1, GQA_GROUP)),
      jnp.tile(k_gain, (1, N_KV_HEADS)), sink_rows, seg)


def _split2(x):
    hi = x.astype(BF16)
    return hi, (x - hi.astype(F32)).astype(BF16)


def _gla_head(q, k, v, r, lr, wgk, bgk, gain, state_ref, masks, store):
    C = GLA_CHUNK
    tril, m_same, m_next, m_far, eye = masks
    z = _dot(lr, wgk) + bgk
    yield
    gl = (jnp.minimum(z, 0.0) - jnp.log(1.0 + jnp.exp(-jnp.abs(z)))) * (1.0 / GLA_NORMALIZER)
    hi, lo = _split2(gl)
    g = _dot(tril, hi) + _dot(tril, lo)
    yield
    t = [g[(c + 1) * C - 1:(c + 1) * C] for c in range(4)]
    t_rows = jnp.concatenate([jnp.broadcast_to(tc, (C, tc.shape[1])) for tc in t], axis=0)

    q_dec = q.astype(F32) * (GLA_DK ** -0.5) * jnp.exp(g)
    kf = k.astype(F32)
    k_inv = (kf * jnp.exp(-g)).astype(BF16)
    k_end = kf * jnp.exp(t_rows - g)
    qd = [q_dec[c * C:(c + 1) * C] for c in range(4)]
    ke = [k_end[c * C:(c + 1) * C] for c in range(4)]

    def rows(parts):
        return jnp.concatenate(parts, axis=0).astype(BF16)

    q_b, k_b = q_dec.astype(BF16), k_end.astype(BF16)
    q_far = rows([qd[0], qd[1], qd[2], qd[3] * jnp.exp(t[2])])
    k_far = rows([ke[0] * jnp.exp(t[1]), ke[1], ke[2], ke[3]])
    q_abs = rows([qd[0], qd[1] * jnp.exp(t[0]), qd[2] * jnp.exp(t[0] + t[1]), qd[3] * jnp.exp(t[0] + t[1] + t[2])])
    k_abs = rows([ke[0] * jnp.exp(t[1] + t[2] + t[3]), ke[1] * jnp.exp(t[2] + t[3]), ke[2] * jnp.exp(t[3]), ke[3]])

    a_same, a_next, a_far = _dot_nt(q_b, k_inv), _dot_nt(q_b, k_b), _dot_nt(q_far, k_far)
    state = state_ref[...]
    o_state = _dot(q_abs, state.astype(BF16))
    update = _dot_tn(k_abs, v)
    yield
    a = jnp.where(m_same, a_same, jnp.where(m_next, a_next, jnp.where(m_far, a_far, 0.0)))
    o = _dot(a.astype(BF16), v) + o_state
    yield

    decay = jnp.exp(t[0] + t[1] + t[2] + t[3])
    decay_col = jnp.sum(jnp.where(eye, jnp.broadcast_to(decay, eye.shape), 0.0), axis=1, keepdims=True)
    state_ref[...] = decay_col * state + update

    rf = r.astype(F32)
    store((_rms(o, gain) * (rf * jax.nn.sigmoid(rf))).astype(BF16))


def _gla_kernel(q_ref, k_ref, v_ref, r_ref, lr_ref, wgk_ref, bgk_ref, gain_ref, o_ref, state_ref):
    R, C, dk, dv = GLA_ROWS, GLA_CHUNK, GLA_DK, GLA_DV

    @pl.when(pl.program_id(2) == 0)
    def _():
        state_ref[...] = jnp.zeros_like(state_ref)

    row = lax.broadcasted_iota(jnp.int32, (R, R), 0)
    col = lax.broadcasted_iota(jnp.int32, (R, R), 1)
    ci, cj = row // C, col // C
    m_same = (row >= col) & (ci == cj)
    m_next = (ci == cj + 1) & (ci != 2)
    m_far = (ci >= 2) & (cj <= 1)
    eye = lax.broadcasted_iota(jnp.int32, (dk, dk), 0) == lax.broadcasted_iota(jnp.int32, (dk, dk), 1)
    masks = (m_same.astype(BF16), m_same, m_next, m_far, eye)

    lr = lr_ref[...]
    heads = []
    for h in range(GLA_HEADS_PER_STEP):
        ks, vs = slice(h * dk, (h + 1) * dk), slice(h * dv, (h + 1) * dv)
        heads.append(_gla_head(q_ref[:, ks], k_ref[:, ks], v_ref[:, vs], r_ref[:, vs], lr, wgk_ref[:, ks],
                               bgk_ref[:, ks], gain_ref[...], state_ref.at[h], masks,
                               functools.partial(o_ref.__setitem__, (slice(None), vs))))
    for _ in itertools.zip_longest(*heads):
        pass


def _gla(proj3, wgk, bgk, gain):
    B, S, _ = proj3.shape
    R, hps = GLA_ROWS, GLA_HEADS_PER_STEP
    dk, dv = hps * GLA_DK, hps * GLA_DV
    return pl.pallas_call(
        _gla_kernel,
        grid=(B, GLA_HEADS // hps, S // R),
        in_specs=[pl.BlockSpec((None, R, dk), lambda b, h, t: (b, t, COL_GQ // dk + h)),
                  pl.BlockSpec((None, R, dk), lambda b, h, t: (b, t, COL_GK // dk + h)),
                  pl.BlockSpec((None, R, dv), lambda b, h, t: (b, t, COL_GV // dv + h)),
                  pl.BlockSpec((None, R, dv), lambda b, h, t: (b, t, COL_GR // dv + h)),
                  pl.BlockSpec((None, R, LR_PAD), lambda b, h, t: (b, t, COL_LR // LR_PAD)),
                  pl.BlockSpec((LR_PAD, dk), lambda b, h, t: (0, h)),
                  pl.BlockSpec((1, dk), lambda b, h, t: (0, h)),
                  pl.BlockSpec((1, GLA_DV), lambda b, h, t: (0, 0))],
        out_specs=pl.BlockSpec((None, R, dv), lambda b, h, t: (b, t, h)),
        out_shape=jax.ShapeDtypeStruct((B, S, GLA_V_W), BF16),
        scratch_shapes=[pltpu.VMEM((hps, GLA_DK, GLA_DV), F32)],
        compiler_params=_params("parallel", "parallel", "arbitrary"),
        name="gla",
    )(proj3, proj3, proj3, proj3, proj3, wgk, bgk, gain)


def _merge_kernel(ya_ref, yg_ref, ga_ref, gb_ref, x_ref, mod_ref, g2_ref, wa_ref, wg_ref, wo_ref,
                  x1_ref, h2_ref):
    def merged_branches(rows):
        ma, mg = _dot(ya_ref[rows, :], wa_ref[...]), _dot(yg_ref[rows, :], wg_ref[...])
        ga = jax.nn.sigmoid(ga_ref[rows, :].astype(F32))
        gb = jax.nn.sigmoid(gb_ref[rows, :].astype(F32))
        return (ga * ma + gb * mg).astype(BF16)

    def project(rows, merged):
        x1 = x_ref[rows, :] + mod_ref[2:3, :] * _dot(merged, wo_ref[...])
        x1_ref[rows, :] = x1
        h2_ref[rows, :] = (_rms(x1, g2_ref[...]) * (1.0 + mod_ref[4:5, :]) + mod_ref[3:4, :]).astype(BF16)

    for r0 in range(0, MERGE_TM, MERGE_SUB):
        rows = slice(r0, r0 + MERGE_SUB)
        project(rows, merged_branches(rows))


def _merge(ya, yg, proj, x2, mod3, gain2, wa, wg, wo):
    T, D = x2.shape
    S = T // mod3.shape[0]
    tm = MERGE_TM
    once = pl.Buffered(1)
    return pl.pallas_call(
        _merge_kernel,
        grid=(T // tm,),
        in_specs=[pl.BlockSpec((tm, ATTN_Q_W), lambda i: (i, 0)),
                  pl.BlockSpec((tm, GLA_V_W), lambda i: (i, 0)),
                  pl.BlockSpec((tm, D), lambda i: (i, COL_GA // D)),
                  pl.BlockSpec((tm, D), lambda i: (i, COL_GB // D)),
                  pl.BlockSpec((tm, D), lambda i: (i, 0)),
                  pl.BlockSpec((None, 6, D), lambda i: (i // (S // tm), 0, 0)),
                  pl.BlockSpec((1, D), lambda i: (0, 0)),
                  pl.BlockSpec((ATTN_Q_W, D), lambda i: (0, 0), pipeline_mode=once),
                  pl.BlockSpec((GLA_V_W, D), lambda i: (0, 0), pipeline_mode=once),
                  pl.BlockSpec((D, D), lambda i: (0, 0), pipeline_mode=once)],
        out_specs=[pl.BlockSpec((tm, D), lambda i: (i, 0)),
                   pl.BlockSpec((tm, D), lambda i: (i, 0))],
        out_shape=[jax.ShapeDtypeStruct((T, D), F32), jax.ShapeDtypeStruct((T, D), BF16)],
        compiler_params=_params("parallel"),
        name="merge",
    )(ya, yg, proj, proj, x2, mod3, gain2, wa, wg, wo)


def _ffn_up_kernel(h_ref, halo_ref, wa_ref, wb_ref, cwa_ref, cwb_ref, cba_ref, cbb_ref, o_ref,
                   hs_ref, ua_ref, ub_ref, *, tiles_per_seq):
    tm, halo = FFN_TM, FFN_HALO

    @pl.when(pl.program_id(1) == 0)
    def _():
        first = (pl.program_id(0) % tiles_per_seq) == 0
        hs_ref[0:halo, :] = jnp.where(first, jnp.zeros_like(halo_ref), halo_ref[...])
        hs_ref[halo:, :] = h_ref[...]

    def conv(u_ref, cw_ref, cb_ref, r0, n):
        u = u_ref[halo + r0 - 8:halo + r0 + n, :]
        y = cb_ref[...] + cw_ref[0:1, :] * pltpu.roll(u, 2, axis=0)[8:]
        y = y + cw_ref[1:2, :] * pltpu.roll(u, 1, axis=0)[8:]
        return y + cw_ref[2:3, :] * u[8:]

    ua_ref[...] = _dot(hs_ref[...], wa_ref[...])
    ub_ref[...] = _dot(hs_ref[...], wb_ref[...])
    for r0 in range(0, tm, FFN_EPI_ROWS):
        ya = conv(ua_ref, cwa_ref, cba_ref, r0, FFN_EPI_ROWS)
        yb = conv(ub_ref, cwb_ref, cbb_ref, r0, FFN_EPI_ROWS)
        o_ref[r0:r0 + FFN_EPI_ROWS, :] = (ya * jax.nn.sigmoid(ya) * yb).astype(BF16)


def _ffn_up(h2, S, w_up, conv_w, conv_b):
    T, D = h2.shape
    tm, tn, halo = FFN_TM, FFN_TN, FFN_HALO
    nj = D_FF // tn
    return pl.pallas_call(
        functools.partial(_ffn_up_kernel, tiles_per_seq=S // tm),
        grid=(T // tm, D_FF // tn),
        in_specs=[pl.BlockSpec((tm, D), lambda i, j: (i, 0)),
                  pl.BlockSpec((halo, D), lambda i, j: (jnp.maximum(i * (tm // halo) - 1, 0), 0)),
                  pl.BlockSpec((D, tn), lambda i, j: (0, j)),
                  pl.BlockSpec((D, tn), lambda i, j: (0, j + nj)),
                  pl.BlockSpec((3, tn), lambda i, j: (0, j)),
                  pl.BlockSpec((3, tn), lambda i, j: (0, j + nj)),
                  pl.BlockSpec((1, tn), lambda i, j: (0, j)),
                  pl.BlockSpec((1, tn), lambda i, j: (0, j + nj))],
        out_specs=pl.BlockSpec((tm, tn), lambda i, j: (i, j)),
        out_shape=jax.ShapeDtypeStruct((T, D_FF), BF16),
        scratch_shapes=[pltpu.VMEM((tm + halo, D), BF16),
                        pltpu.VMEM((tm + halo, tn), F32),
                        pltpu.VMEM((tm + halo, tn), F32)],
        compiler_params=_params("parallel", "arbitrary"),
        name="ffn_up",
    )(h2, h2, w_up, w_up, conv_w, conv_w, conv_b, conv_b)


def _ffn_down_kernel(a_ref, w_ref, x_ref, mod_ref, o_ref):
    o_ref[...] = x_ref[...] + mod_ref[5:6, :] * _dot(a_ref[...], w_ref[...])


def _ffn_down(act, wd, x1, mod3):
    T, D = x1.shape
    S = T // mod3.shape[0]
    tm = DOWN_TM
    return pl.pallas_call(
        _ffn_down_kernel,
        grid=(T // tm,),
        in_specs=[pl.BlockSpec((tm, D_FF), lambda i: (i, 0)),
                  pl.BlockSpec((D_FF, D), lambda i: (0, 0), pipeline_mode=pl.Buffered(1)),
                  pl.BlockSpec((tm, D), lambda i: (i, 0)),
                  pl.BlockSpec((None, 6, D), lambda i: (i // (S // tm), 0, 0))],
        out_specs=pl.BlockSpec((tm, D), lambda i: (i, 0)),
        out_shape=jax.ShapeDtypeStruct((T, D), F32),
        compiler_params=_params("parallel"),
        name="ffn_down",
    )(act, wd, x1, mod3)


def _layer(x2, B, mod3, biasm, norm1_gain, w_in, q_norm_gain, k_norm_gain, attn_sinks, w_gk_up, b_gk,
           gla_norm_gain, w_branch_attn, w_branch_gla, w_out, norm2_gain, w_ffn_up, ffn_conv_w,
           ffn_conv_b, w_ffn_down):
    T, D = x2.shape
    S = T // B
    w_p = _wpack(w_in.T)
    wgk = jnp.concatenate([w_gk_up, jnp.zeros((LR_PAD - GLA_LOWRANK, GLA_K_W), w_gk_up.dtype)],
                          axis=0).astype(BF16)

    proj = _inproj(x2, mod3, norm1_gain.reshape(1, D), w_p)
    proj3 = proj.reshape(B, S, PROJ_W)
    ya = _attn(proj3, biasm, q_norm_gain.reshape(1, HEAD_DIM), k_norm_gain.reshape(1, HEAD_DIM), attn_sinks)
    yg = _gla(proj3, wgk, b_gk.reshape(1, GLA_K_W), gla_norm_gain.reshape(1, GLA_DV))
    x1, h2 = _merge(ya.reshape(T, ATTN_Q_W), yg.reshape(T, GLA_V_W), proj, x2, mod3,
                    norm2_gain.reshape(1, D), w_branch_attn.astype(BF16), w_branch_gla.astype(BF16),
                    w_out.astype(BF16))
    act = _ffn_up(h2, S, w_ffn_up.astype(BF16), ffn_conv_w, ffn_conv_b.reshape(1, 2 * D_FF))
    return _ffn_down(act, w_ffn_down.astype(BF16), x1, mod3)


def kernel(x, c, rel_bias_table, w_ada, b_ada, norm1_gain, w_in, q_norm_gain, k_norm_gain, attn_sinks,
           w_gk_up, b_gk, gla_norm_gain, w_branch_attn, w_branch_gla, w_out, norm2_gain, w_ffn_up,
           ffn_conv_w, ffn_conv_b, w_ffn_down):
    B, S, D = x.shape
    depth = w_in.shape[0]
    biasm = _relbias(rel_bias_table)
    x2 = x.reshape(B * S, D)
    for l in range(depth):
        mod3 = _adaln(c, w_ada[l], b_ada[l]).reshape(B, 6, D)
        x2 = _layer(x2, B, mod3, biasm, norm1_gain[l], w_in[l], q_norm_gain[l], k_norm_gain[l],
                    attn_sinks[l], w_gk_up[l], b_gk[l], gla_norm_gain[l], w_branch_attn[l],
                    w_branch_gla[l], w_out[l], norm2_gain[l], w_ffn_up[l], ffn_conv_w[l], ffn_conv_b[l],
                    w_ffn_down[l])
    return x2.reshape(B, S, D)
```

```python
import functools
import itertools
import math

import numpy as np
import jax
import jax.numpy as jnp
from jax import lax
from jax.experimental import pallas as pl
from jax.experimental.pallas import tpu as pltpu

F32 = jnp.float32
BF16 = jnp.bfloat16

D_MODEL = 2048
N_Q_HEADS = 16
N_KV_HEADS = 4
GQA_GROUP = N_Q_HEADS // N_KV_HEADS
HEAD_DIM = 64
WINDOW = 128
ATTN_BLOCK = 128
N_BUCKETS = 32
MAX_DISTANCE = 128
GLA_HEADS = 4
GLA_DK = 256
GLA_DV = 512
GLA_LOWRANK = 16
GLA_NORMALIZER = 16.0
GLA_CHUNK = 64
D_FF = 5632
EPS = 1e-6
NEG_INF = -1e30
LOG2E = math.log2(math.e)

ATTN_Q_W = N_Q_HEADS * HEAD_DIM
ATTN_KV_W = N_KV_HEADS * HEAD_DIM
GLA_K_W = GLA_HEADS * GLA_DK
GLA_V_W = GLA_HEADS * GLA_DV

COL_GA = 0
COL_GB = COL_GA + D_MODEL
COL_GV = COL_GB + D_MODEL
COL_GR = COL_GV + GLA_V_W
COL_GQ = COL_GR + GLA_V_W
COL_GK = COL_GQ + GLA_K_W
COL_AQ = COL_GK + GLA_K_W
COL_AK = COL_AQ + ATTN_Q_W
COL_AV = COL_AK + ATTN_KV_W
COL_LR = COL_AV + ATTN_KV_W
LR_PAD = 128
PROJ_W = 12288

VMEM_LIMIT = 60 * 1024 * 1024

ADALN_TN = 1024
INPROJ_TM, INPROJ_TN = 1024, 2048
INPROJ_NORM_ROWS = 256
ATTN_TQ = 256
GLA_ROWS = 256
GLA_HEADS_PER_STEP = 4
MERGE_TM, MERGE_SUB = 512, 256
FFN_TM, FFN_TN = 1024, 512
FFN_HALO = 16
FFN_EPI_ROWS = 256
DOWN_TM = 512


def _params(*sem):
    return pltpu.CompilerParams(dimension_semantics=sem, vmem_limit_bytes=VMEM_LIMIT)


def _dot(a, b):
    return jnp.dot(a, b, preferred_element_type=F32)


def _dot_nt(a, b):
    return lax.dot_general(a, b, (((1,), (1,)), ((), ())), preferred_element_type=F32)


def _dot_tn(a, b):
    return lax.dot_general(a, b, (((0,), (0,)), ((), ())), preferred_element_type=F32)


def _rms(x, gain):
    return x * lax.rsqrt(jnp.mean(x * x, axis=-1, keepdims=True) + EPS) * gain


def _adaln_kernel(c_ref, w_ref, b_ref, o_ref):
    c = c_ref[...]
    ca = c * jax.nn.sigmoid(c)
    o_ref[...] = _dot(ca.astype(BF16), w_ref[...].astype(BF16)) + b_ref[...]


def _adaln(c, w_ada, b_ada):
    B, D = c.shape
    N = w_ada.shape[1]
    return pl.pallas_call(
        _adaln_kernel,
        grid=(N // ADALN_TN,),
        in_specs=[pl.BlockSpec((B, D), lambda j: (0, 0)),
                  pl.BlockSpec((D, ADALN_TN), lambda j: (0, j)),
                  pl.BlockSpec((1, ADALN_TN), lambda j: (0, j))],
        out_specs=pl.BlockSpec((B, ADALN_TN), lambda j: (0, j)),
        out_shape=jax.ShapeDtypeStruct((B, N), F32),
        compiler_params=_params("parallel"),
        name="adaln",
    )(c, w_ada, b_ada.reshape(1, N))


def _bucket_table():
    j = np.arange(2 * ATTN_BLOCK)[:, None]
    i = np.arange(ATTN_BLOCK)[None, :]
    dist = i + ATTN_BLOCK - j
    max_exact = N_BUCKETS // 2
    d = np.maximum(dist, 0)
    ratio = np.log(np.maximum(d, 1).astype(np.float32) / np.float32(max_exact)) / np.float32(
        math.log(MAX_DISTANCE / max_exact))
    large = max_exact + (ratio.astype(np.float32) * np.float32(N_BUCKETS - max_exact)).astype(np.int32)
    large = np.minimum(large, N_BUCKETS - 1)
    bucket = np.where(d < max_exact, d, large)
    in_window = (dist >= 0) & (dist < WINDOW)
    return np.where(in_window, bucket, -1).astype(np.int32)


def _relbias_kernel(tab_ref, bkt_ref, o_ref):
    h = pl.program_id(0)
    bkt = bkt_ref[...]
    acc = jnp.zeros(bkt.shape, F32)
    for b in range(N_BUCKETS):
        acc = jnp.where(bkt == b, tab_ref[b, h], acc)
    regular = jnp.where(bkt >= 0, acc * LOG2E, NEG_INF)
    key = lax.broadcasted_iota(jnp.int32, bkt.shape, 0)
    o_ref[0] = jnp.where(key >= ATTN_BLOCK, regular, NEG_INF)
    o_ref[1] = regular


def _relbias(rel_bias_table):
    bkt = jnp.asarray(_bucket_table())
    nk, nq = bkt.shape
    return pl.pallas_call(
        _relbias_kernel,
        grid=(N_Q_HEADS,),
        in_specs=[pl.BlockSpec(memory_space=pltpu.SMEM),
                  pl.BlockSpec(bkt.shape, lambda h: (0, 0))],
        out_specs=pl.BlockSpec((2, None, nk, nq), lambda h: (0, h // GQA_GROUP, h % 2, (h // 2) % 2)),
        out_shape=jax.ShapeDtypeStruct((2, N_KV_HEADS, 2 * nk, 2 * nq), F32),
        compiler_params=_params("parallel"),
        name="relbias",
    )(rel_bias_table, bkt)


SRC_GQ = ATTN_Q_W + 2 * ATTN_KV_W
SRC_GV = SRC_GQ + 2 * GLA_K_W
SRC_LR = SRC_GV + 2 * GLA_V_W
SRC_GA = SRC_LR + GLA_LOWRANK
WPACK_ROWS = 512


def _wpack_src_row(b):
    r = b * WPACK_ROWS
    src = jnp.where(r < COL_GV, SRC_GA + r,
                    jnp.where(r < COL_GQ, SRC_GV + (r - COL_GV),
                              jnp.where(r < COL_AQ, SRC_GQ + (r - COL_GQ),
                                        jnp.where(r < COL_LR, r - COL_AQ, SRC_LR))))
    return pl.multiple_of(src, 8)


def _wpack_kernel(w_ref, o_ref):
    r = pl.program_id(0) * WPACK_ROWS
    nvalid = jnp.where(r < COL_LR, WPACK_ROWS, jnp.where(r == COL_LR, GLA_LOWRANK, 0))
    row = lax.broadcasted_iota(jnp.int32, w_ref.shape, 0)
    o_ref[...] = jnp.where(row < nvalid, w_ref[...], 0.0).astype(BF16)


def _wpack(w_in_t):
    _, D = w_in_t.shape
    return pl.pallas_call(
        _wpack_kernel,
        grid=(PROJ_W // WPACK_ROWS,),
        in_specs=[pl.BlockSpec((pl.Element(WPACK_ROWS), pl.Element(D)), lambda b: (_wpack_src_row(b), 0))],
        out_specs=pl.BlockSpec((WPACK_ROWS, D), lambda b: (b, 0)),
        out_shape=jax.ShapeDtypeStruct((PROJ_W, D), BF16),
        compiler_params=_params("parallel"),
        name="wpack",
    )(w_in_t)


def _norm_modulate(x, mod_ref, gain):
    return (_rms(x, gain) * (1.0 + mod_ref[1:2, :]) + mod_ref[0:1, :]).astype(BF16)


def _hnorm_kernel(x_ref, mod_ref, g_ref, o_ref):
    o_ref[...] = _norm_modulate(x_ref[...], mod_ref, g_ref[...])


def _hnorm_first(x2, mod3, gain):
    D = x2.shape[1]
    nr = INPROJ_NORM_ROWS
    return pl.pallas_call(
        _hnorm_kernel,
        grid=(INPROJ_TM // nr,),
        in_specs=[pl.BlockSpec((nr, D), lambda r: (r, 0)),
                  pl.BlockSpec((None, 6, D), lambda r: (0, 0, 0)),
                  pl.BlockSpec((1, D), lambda r: (0, 0))],
        out_specs=pl.BlockSpec((nr, D), lambda r: (r, 0)),
        out_shape=jax.ShapeDtypeStruct((INPROJ_TM, D), BF16),
        compiler_params=_params("parallel"),
        name="hnorm_first",
    )(x2, mod3, gain)


def _cast_block(shape, steps):
    rows, cols = shape
    bc = min(cols, 1024)
    assert cols % bc == 0
    for br in range(16, rows + 1, 16):
        if rows % br == 0 and (rows // br) * (cols // bc) <= steps:
            return br, bc
    raise ValueError(f"no cast block for {shape} in {steps} steps")


def _inproj_kernel(xn_ref, modn_ref, g_ref, w_ref, h0_ref, *rest, n_cast):
    cast_in, o_ref, cast_out = rest[:n_cast], rest[n_cast], rest[n_cast + 1:2 * n_cast + 1]
    ha_ref, hb_ref = rest[2 * n_cast + 1:]
    i, j = pl.program_id(0), pl.program_id(1)
    nr = INPROJ_NORM_ROWS

    @pl.when((i == 0) & (j == 0))
    def _():
        ha_ref[...] = h0_ref[...]

    def step(cur_ref, nxt_ref):
        o_ref[...] = _dot_nt(cur_ref[...], w_ref[...]).astype(BF16)
        chunk = jnp.minimum(j, INPROJ_TM // nr - 1)
        rows = pl.ds(pl.multiple_of(chunk * nr, nr), nr)
        nxt_ref[rows, :] = _norm_modulate(xn_ref[...], modn_ref, g_ref[...])
        for src_ref, dst_ref in zip(cast_in, cast_out):
            dst_ref[...] = src_ref[...].astype(BF16)

    @pl.when(i % 2 == 0)
    def _():
        step(ha_ref, hb_ref)

    @pl.when(i % 2 == 1)
    def _():
        step(hb_ref, ha_ref)


def _inproj(x2, mod3, gain, w, cast_weights):
    T, D = x2.shape
    S = T // mod3.shape[0]
    tm, tn, nr = INPROJ_TM, INPROJ_TN, INPROJ_NORM_ROWS
    n_i, n_j, chunks = T // tm, PROJ_W // tn, tm // nr
    assert n_j >= chunks

    def next_tile(i):
        return jnp.minimum(i + 1, n_i - 1)

    def cast_spec(shape):
        br, bc = _cast_block(shape, n_i * n_j)
        ncb = shape[1] // bc
        last = (shape[0] // br) * ncb - 1

        def index(i, j):
            t = jnp.minimum(i * n_j + j, last)
            return t // ncb, t % ncb

        return pl.BlockSpec((br, bc), index)

    cast_specs = [cast_spec(cw.shape) for cw in cast_weights]
    outs = pl.pallas_call(
        functools.partial(_inproj_kernel, n_cast=len(cast_weights)),
        grid=(n_i, n_j),
        in_specs=[pl.BlockSpec((nr, D), lambda i, j: (next_tile(i) * chunks + jnp.minimum(j, chunks - 1), 0)),
                  pl.BlockSpec((None, 6, D), lambda i, j: (next_tile(i) // (S // tm), 0, 0)),
                  pl.BlockSpec((1, D), lambda i, j: (0, 0)),
                  pl.BlockSpec((tn, D), lambda i, j: (j, 0)),
                  pl.BlockSpec((tm, D), lambda i, j: (0, 0), pipeline_mode=pl.Buffered(1))] + cast_specs,
        out_specs=[pl.BlockSpec((tm, tn), lambda i, j: (i, j))] + cast_specs,
        out_shape=[jax.ShapeDtypeStruct((T, PROJ_W), BF16)]
        + [jax.ShapeDtypeStruct(cw.shape, BF16) for cw in cast_weights],
        scratch_shapes=[pltpu.VMEM((tm, D), BF16), pltpu.VMEM((tm, D), BF16)],
        compiler_params=_params("arbitrary", "arbitrary"),
        name="inproj",
    )(x2, mod3, gain, w, _hnorm_first(x2, mod3, gain), *cast_weights)
    return outs[0], outs[1:]


def _attn_kernel(q_ref, kc_ref, kp_ref, vc_ref, vp_ref, b0_ref, b1_ref, qg_ref, kg_ref, sink_ref, seg_ref,
                 o_ref):
    nsub = ATTN_TQ // ATTN_BLOCK
    blk, dh = ATTN_BLOCK, HEAD_DIM
    seg = seg_ref[...]

    def normed(x, gain):
        ssq = _dot((x * x).astype(BF16), seg)
        return x * lax.rsqrt(ssq * (1.0 / dh) + EPS) * gain

    kn = normed(jnp.concatenate([kp_ref[...], kc_ref[...]], axis=0).astype(F32), kg_ref[...])
    vt = jnp.concatenate([vp_ref[...], vc_ref[...]], axis=0).astype(F32).T.astype(BF16)
    lane = lax.broadcasted_iota(jnp.int32, (kn.shape[0], 2 * dh), 1)
    qgain = qg_ref[...] * (dh ** -0.5 * LOG2E)

    kzs, qns = [], []
    for g in range(N_KV_HEADS):
        pair = kn[:, (g // 2) * 2 * dh:(g // 2 + 1) * 2 * dh]
        own = jnp.where(lane >= dh if g % 2 else lane < dh, pair, 0.0)
        swapped = pltpu.roll(own, dh, axis=1)
        kz = [own, swapped] if g % 2 == 0 else [swapped, own]
        kzs.append([z.astype(BF16) for z in kz])
        qns.append(normed(q_ref[:, g * 4 * dh:(g + 1) * 4 * dh].astype(F32), qgain).astype(BF16))

    def scores(g, sb):
        bref = b0_ref if sb == 0 else b1_ref
        band = slice(sb * blk, sb * blk + 2 * blk)
        rows = slice(sb * blk, (sb + 1) * blk)
        kband = jnp.concatenate([kzs[g][0][band], kzs[g][1][band]], axis=0)
        qr = jnp.concatenate([qns[g][rows, 0:2 * dh], qns[g][rows, 2 * dh:4 * dh]], axis=0)
        return _dot_nt(kband, qr) + bref[g]

    def finish(g, sb, s):
        band = slice(sb * blk, sb * blk + 2 * blk)
        rows = slice(sb * blk, (sb + 1) * blk)
        ps, inv = [], []
        for e in range(2):
            se = s[e * 2 * blk:(e + 1) * 2 * blk]
            sink = sink_ref[g, e] * LOG2E
            m = jnp.maximum(jnp.max(se, axis=0, keepdims=True), sink)
            p = jnp.exp2(se - m)
            inv.append(1.0 / (jnp.sum(p, axis=0, keepdims=True) + jnp.exp2(sink - m)))
            ps.append(p.astype(BF16))
        ot = _dot(vt[g * dh:(g + 1) * dh, band], jnp.concatenate(ps, axis=1))
        ot = ot * jnp.concatenate(inv, axis=1)
        o = jnp.concatenate([ot[:, 0:2 * blk], ot[:, 2 * blk:4 * blk]], axis=0).T
        o_ref[rows, (2 * g) * 2 * dh:(2 * g + 1) * 2 * dh] = o[0:blk].astype(BF16)
        o_ref[rows, (2 * g + 1) * 2 * dh:(2 * g + 2) * 2 * dh] = o[blk:2 * blk].astype(BF16)

    units = [(g, sb) for g in range(N_KV_HEADS) for sb in range(nsub)]
    s_next = scores(*units[0])
    for n, unit in enumerate(units):
        s_cur = s_next
        if n + 1 < len(units):
            s_next = scores(*units[n + 1])
        finish(*unit, s_cur)


def _attn(proj3, biasm, q_gain, k_gain, sinks):
    B, S, _ = proj3.shape
    tq, blk = ATTN_TQ, ATTN_BLOCK
    nsub = tq // blk
    kvw = ATTN_KV_W
    seg_id = np.arange(kvw) // HEAD_DIM
    seg = jnp.asarray(seg_id[:, None] == seg_id[None, :], dtype=BF16)
    sink_rows = jnp.repeat(sinks.reshape(N_KV_HEADS, 2, 2).transpose(0, 2, 1), blk, axis=-1)
    sink_rows = sink_rows.reshape(N_KV_HEADS, 2, 1, 2 * blk)

    def prev(i):
        return jnp.maximum(i * nsub - 1, 0)

    bias_block = (None,) + biasm.shape[1:]
    return pl.pallas_call(
        _attn_kernel,
        grid=(B, S // tq),
        in_specs=[pl.BlockSpec((None, tq, ATTN_Q_W), lambda b, i: (b, i, COL_AQ // ATTN_Q_W)),
                  pl.BlockSpec((None, tq, kvw), lambda b, i: (b, i, COL_AK // kvw)),
                  pl.BlockSpec((None, blk, kvw), lambda b, i: (b, prev(i), COL_AK // kvw)),
                  pl.BlockSpec((None, tq, kvw), lambda b, i: (b, i, COL_AV // kvw)),
                  pl.BlockSpec((None, blk, kvw), lambda b, i: (b, prev(i), COL_AV // kvw)),
                  pl.BlockSpec(bias_block, lambda b, i: (jnp.minimum(i, 1), 0, 0, 0)),
                  pl.BlockSpec(bias_block, lambda b, i: (1, 0, 0, 0)),
                  pl.BlockSpec((1, kvw), lambda b, i: (0, 0)),
                  pl.BlockSpec((1, kvw), lambda b, i: (0, 0)),
                  pl.BlockSpec(sink_rows.shape, lambda b, i: (0, 0, 0, 0)),
                  pl.BlockSpec(seg.shape, lambda b, i: (0, 0))],
        out_specs=pl.BlockSpec((None, tq, ATTN_Q_W), lambda b, i: (b, i, 0)),
        out_shape=jax.ShapeDtypeStruct((B, S, ATTN_Q_W), BF16),
        compiler_params=_params("parallel", "parallel"),
        name="attn",
    )(proj3, proj3, proj3, proj3, proj3, biasm, biasm, jnp.tile(q_gain, (1, GQA_GROUP)),
      jnp.tile(k_gain, (1, N_KV_HEADS)), sink_rows, seg)


def _split2(x):
    hi = x.astype(BF16)
    return hi, (x - hi.astype(F32)).astype(BF16)


def _gla_head(q, k, v, r, lr, wgk, bgk, gain, state_ref, masks, store):
    C = GLA_CHUNK
    tril, m_same, m_next, m_far, eye = masks
    z = _dot(lr, wgk) + bgk
    yield
    gl = (jnp.minimum(z, 0.0) - jnp.log(1.0 + jnp.exp(-jnp.abs(z)))) * (1.0 / GLA_NORMALIZER)
    hi, lo = _split2(gl)
    g = _dot(tril, hi) + _dot(tril, lo)
    yield
    t = [g[(c + 1) * C - 1:(c + 1) * C] for c in range(4)]
    t_rows = jnp.concatenate([jnp.broadcast_to(tc, (C, tc.shape[1])) for tc in t], axis=0)

    q_dec = q.astype(F32) * (GLA_DK ** -0.5) * jnp.exp(g)
    kf = k.astype(F32)
    k_inv = (kf * jnp.exp(-g)).astype(BF16)
    k_end = kf * jnp.exp(t_rows - g)
    qd = [q_dec[c * C:(c + 1) * C] for c in range(4)]
    ke = [k_end[c * C:(c + 1) * C] for c in range(4)]

    def rows(parts):
        return jnp.concatenate(parts, axis=0).astype(BF16)

    q_b, k_b = q_dec.astype(BF16), k_end.astype(BF16)
    q_far = rows([qd[0], qd[1], qd[2], qd[3] * jnp.exp(t[2])])
    k_far = rows([ke[0] * jnp.exp(t[1]), ke[1], ke[2], ke[3]])
    q_abs = rows([qd[0], qd[1] * jnp.exp(t[0]), qd[2] * jnp.exp(t[0] + t[1]), qd[3] * jnp.exp(t[0] + t[1] + t[2])])
    k_abs = rows([ke[0] * jnp.exp(t[1] + t[2] + t[3]), ke[1] * jnp.exp(t[2] + t[3]), ke[2] * jnp.exp(t[3]), ke[3]])

    a_same, a_next, a_far = _dot_nt(q_b, k_inv), _dot_nt(q_b, k_b), _dot_nt(q_far, k_far)
    state = state_ref[...]
    o_state = _dot(q_abs, state.astype(BF16))
    update = _dot_tn(k_abs, v)
    yield
    a = jnp.where(m_same, a_same, jnp.where(m_next, a_next, jnp.where(m_far, a_far, 0.0)))
    o = _dot(a.astype(BF16), v) + o_state
    yield

    decay = jnp.exp(t[0] + t[1] + t[2] + t[3])
    decay_col = jnp.sum(jnp.where(eye, jnp.broadcast_to(decay, eye.shape), 0.0), axis=1, keepdims=True)
    state_ref[...] = decay_col * state + update

    rf = r.astype(F32)
    store((_rms(o, gain) * (rf * jax.nn.sigmoid(rf))).astype(BF16))


def _gla_kernel(q_ref, k_ref, v_ref, r_ref, lr_ref, wgk_ref, bgk_ref, gain_ref, o_ref, state_ref):
    R, C, dk, dv = GLA_ROWS, GLA_CHUNK, GLA_DK, GLA_DV

    @pl.when(pl.program_id(2) == 0)
    def _():
        state_ref[...] = jnp.zeros_like(state_ref)

    row = lax.broadcasted_iota(jnp.int32, (R, R), 0)
    col = lax.broadcasted_iota(jnp.int32, (R, R), 1)
    ci, cj = row // C, col // C
    m_same = (row >= col) & (ci == cj)
    m_next = (ci == cj + 1) & (ci != 2)
    m_far = (ci >= 2) & (cj <= 1)
    eye = lax.broadcasted_iota(jnp.int32, (dk, dk), 0) == lax.broadcasted_iota(jnp.int32, (dk, dk), 1)
    masks = (m_same.astype(BF16), m_same, m_next, m_far, eye)

    lr = lr_ref[...]
    heads = []
    for h in range(GLA_HEADS_PER_STEP):
        ks, vs = slice(h * dk, (h + 1) * dk), slice(h * dv, (h + 1) * dv)
        heads.append(_gla_head(q_ref[:, ks], k_ref[:, ks], v_ref[:, vs], r_ref[:, vs], lr, wgk_ref[:, ks],
                               bgk_ref[:, ks], gain_ref[...], state_ref.at[h], masks,
                               functools.partial(o_ref.__setitem__, (slice(None), vs))))
    for _ in itertools.zip_longest(*heads):
        pass


def _gla(proj3, wgk, bgk, gain):
    B, S, _ = proj3.shape
    R, hps = GLA_ROWS, GLA_HEADS_PER_STEP
    dk, dv = hps * GLA_DK, hps * GLA_DV
    return pl.pallas_call(
        _gla_kernel,
        grid=(B, GLA_HEADS // hps, S // R),
        in_specs=[pl.BlockSpec((None, R, dk), lambda b, h, t: (b, t, COL_GQ // dk + h)),
                  pl.BlockSpec((None, R, dk), lambda b, h, t: (b, t, COL_GK // dk + h)),
                  pl.BlockSpec((None, R, dv), lambda b, h, t: (b, t, COL_GV // dv + h)),
                  pl.BlockSpec((None, R, dv), lambda b, h, t: (b, t, COL_GR // dv + h)),
                  pl.BlockSpec((None, R, LR_PAD), lambda b, h, t: (b, t, COL_LR // LR_PAD)),
                  pl.BlockSpec((LR_PAD, dk), lambda b, h, t: (0, h)),
                  pl.BlockSpec((1, dk), lambda b, h, t: (0, h)),
                  pl.BlockSpec((1, GLA_DV), lambda b, h, t: (0, 0))],
        out_specs=pl.BlockSpec((None, R, dv), lambda b, h, t: (b, t, h)),
        out_shape=jax.ShapeDtypeStruct((B, S, GLA_V_W), BF16),
        scratch_shapes=[pltpu.VMEM((hps, GLA_DK, GLA_DV), F32)],
        compiler_params=_params("parallel", "parallel", "arbitrary"),
        name="gla",
    )(proj3, proj3, proj3, proj3, proj3, wgk, bgk, gain)


def _merge_kernel(ya_ref, yg_ref, ga_ref, gb_ref, x_ref, mod_ref, g2_ref, wa_ref, wg_ref, wo_ref,
                  x1_ref, h2_ref):
    def merged_branches(rows):
        ma, mg = _dot(ya_ref[rows, :], wa_ref[...]), _dot(yg_ref[rows, :], wg_ref[...])
        ga = jax.nn.sigmoid(ga_ref[rows, :].astype(F32))
        gb = jax.nn.sigmoid(gb_ref[rows, :].astype(F32))
        return (ga * ma + gb * mg).astype(BF16)

    def project(rows, merged):
        x1 = x_ref[rows, :] + mod_ref[2:3, :] * _dot(merged, wo_ref[...])
        x1_ref[rows, :] = x1
        h2_ref[rows, :] = (_rms(x1, g2_ref[...]) * (1.0 + mod_ref[4:5, :]) + mod_ref[3:4, :]).astype(BF16)

    for r0 in range(0, MERGE_TM, MERGE_SUB):
        rows = slice(r0, r0 + MERGE_SUB)
        project(rows, merged_branches(rows))


def _merge(ya, yg, proj, x2, mod3, gain2, wa, wg, wo):
    T, D = x2.shape
    S = T // mod3.shape[0]
    tm = MERGE_TM
    once = pl.Buffered(1)
    return pl.pallas_call(
        _merge_kernel,
        grid=(T // tm,),
        in_specs=[pl.BlockSpec((tm, ATTN_Q_W), lambda i: (i, 0)),
                  pl.BlockSpec((tm, GLA_V_W), lambda i: (i, 0)),
                  pl.BlockSpec((tm, D), lambda i: (i, COL_GA // D)),
                  pl.BlockSpec((tm, D), lambda i: (i, COL_GB // D)),
                  pl.BlockSpec((tm, D), lambda i: (i, 0)),
                  pl.BlockSpec((None, 6, D), lambda i: (i // (S // tm), 0, 0)),
                  pl.BlockSpec((1, D), lambda i: (0, 0)),
                  pl.BlockSpec((ATTN_Q_W, D), lambda i: (0, 0), pipeline_mode=once),
                  pl.BlockSpec((GLA_V_W, D), lambda i: (0, 0), pipeline_mode=once),
                  pl.BlockSpec((D, D), lambda i: (0, 0), pipeline_mode=once)],
        out_specs=[pl.BlockSpec((tm, D), lambda i: (i, 0)),
                   pl.BlockSpec((tm, D), lambda i: (i, 0))],
        out_shape=[jax.ShapeDtypeStruct((T, D), F32), jax.ShapeDtypeStruct((T, D), BF16)],
        compiler_params=_params("parallel"),
        name="merge",
    )(ya, yg, proj, proj, x2, mod3, gain2, wa, wg, wo)


def _ffn_up_kernel(h_ref, halo_ref, wa_ref, wb_ref, cwa_ref, cwb_ref, cba_ref, cbb_ref, o_ref,
                   hs_ref, ua_ref, ub_ref, *, tiles_per_seq):
    tm, halo = FFN_TM, FFN_HALO

    @pl.when(pl.program_id(1) == 0)
    def _():
        first = (pl.program_id(0) % tiles_per_seq) == 0
        hs_ref[0:halo, :] = jnp.where(first, jnp.zeros_like(halo_ref), halo_ref[...])
        hs_ref[halo:, :] = h_ref[...]

    def conv(u_ref, cw_ref, cb_ref, r0, n):
        u = u_ref[halo + r0 - 8:halo + r0 + n, :]
        y = cb_ref[...] + cw_ref[0:1, :] * pltpu.roll(u, 2, axis=0)[8:]
        y = y + cw_ref[1:2, :] * pltpu.roll(u, 1, axis=0)[8:]
        return y + cw_ref[2:3, :] * u[8:]

    ua_ref[...] = _dot(hs_ref[...], wa_ref[...])
    ub_ref[...] = _dot(hs_ref[...], wb_ref[...])
    for r0 in range(0, tm, FFN_EPI_ROWS):
        ya = conv(ua_ref, cwa_ref, cba_ref, r0, FFN_EPI_ROWS)
        yb = conv(ub_ref, cwb_ref, cbb_ref, r0, FFN_EPI_ROWS)
        o_ref[r0:r0 + FFN_EPI_ROWS, :] = (ya * jax.nn.sigmoid(ya) * yb).astype(BF16)


def _ffn_up(h2, S, w_up, conv_w, conv_b):
    T, D = h2.shape
    tm, tn, halo = FFN_TM, FFN_TN, FFN_HALO
    nj = D_FF // tn
    return pl.pallas_call(
        functools.partial(_ffn_up_kernel, tiles_per_seq=S // tm),
        grid=(T // tm, D_FF // tn),
        in_specs=[pl.BlockSpec((tm, D), lambda i, j: (i, 0)),
                  pl.BlockSpec((halo, D), lambda i, j: (jnp.maximum(i * (tm // halo) - 1, 0), 0)),
                  pl.BlockSpec((D, tn), lambda i, j: (0, j)),
                  pl.BlockSpec((D, tn), lambda i, j: (0, j + nj)),
                  pl.BlockSpec((3, tn), lambda i, j: (0, j)),
                  pl.BlockSpec((3, tn), lambda i, j: (0, j + nj)),
                  pl.BlockSpec((1, tn), lambda i, j: (0, j)),
                  pl.BlockSpec((1, tn), lambda i, j: (0, j + nj))],
        out_specs=pl.BlockSpec((tm, tn), lambda i, j: (i, j)),
        out_shape=jax.ShapeDtypeStruct((T, D_FF), BF16),
        scratch_shapes=[pltpu.VMEM((tm + halo, D), BF16),
                        pltpu.VMEM((tm + halo, tn), F32),
                        pltpu.VMEM((tm + halo, tn), F32)],
        compiler_params=_params("parallel", "arbitrary"),
        name="ffn_up",
    )(h2, h2, w_up, w_up, conv_w, conv_w, conv_b, conv_b)


def _ffn_down_kernel(a_ref, w_ref, x_ref, mod_ref, o_ref):
    o_ref[...] = x_ref[...] + mod_ref[5:6, :] * _dot(a_ref[...], w_ref[...])


def _ffn_down(act, wd, x1, mod3):
    T, D = x1.shape
    S = T // mod3.shape[0]
    tm = DOWN_TM
    return pl.pallas_call(
        _ffn_down_kernel,
        grid=(T // tm,),
        in_specs=[pl.BlockSpec((tm, D_FF), lambda i: (i, 0)),
                  pl.BlockSpec((D_FF, D), lambda i: (0, 0), pipeline_mode=pl.Buffered(1)),
                  pl.BlockSpec((tm, D), lambda i: (i, 0)),
                  pl.BlockSpec((None, 6, D), lambda i: (i // (S // tm), 0, 0))],
        out_specs=pl.BlockSpec((tm, D), lambda i: (i, 0)),
        out_shape=jax.ShapeDtypeStruct((T, D), F32),
        compiler_params=_params("parallel"),
        name="ffn_down",
    )(act, wd, x1, mod3)


def _layer(x2, B, mod3, biasm, norm1_gain, w_in, q_norm_gain, k_norm_gain, attn_sinks, w_gk_up, b_gk,
           gla_norm_gain, w_branch_attn, w_branch_gla, w_out, norm2_gain, w_ffn_up, ffn_conv_w,
           ffn_conv_b, w_ffn_down):
    T, D = x2.shape
    S = T // B
    w_p = _wpack(w_in.T)
    wgk = jnp.concatenate([w_gk_up, jnp.zeros((LR_PAD - GLA_LOWRANK, GLA_K_W), w_gk_up.dtype)],
                          axis=0).astype(BF16)

    proj, (w_ba, w_bg, w_o, w_up, w_down) = _inproj(
        x2, mod3, norm1_gain.reshape(1, D), w_p, (w_branch_attn, w_branch_gla, w_out, w_ffn_up, w_ffn_down))
    proj3 = proj.reshape(B, S, PROJ_W)
    ya = _attn(proj3, biasm, q_norm_gain.reshape(1, HEAD_DIM), k_norm_gain.reshape(1, HEAD_DIM), attn_sinks)
    yg = _gla(proj3, wgk, b_gk.reshape(1, GLA_K_W), gla_norm_gain.reshape(1, GLA_DV))
    x1, h2 = _merge(ya.reshape(T, ATTN_Q_W), yg.reshape(T, GLA_V_W), proj, x2, mod3,
                    norm2_gain.reshape(1, D), w_ba, w_bg, w_o)
    act = _ffn_up(h2, S, w_up, ffn_conv_w, ffn_conv_b.reshape(1, 2 * D_FF))
    return _ffn_down(act, w_down, x1, mod3)


def kernel(x, c, rel_bias_table, w_ada, b_ada, norm1_gain, w_in, q_norm_gain, k_norm_gain, attn_sinks,
           w_gk_up, b_gk, gla_norm_gain, w_branch_attn, w_branch_gla, w_out, norm2_gain, w_ffn_up,
           ffn_conv_w, ffn_conv_b, w_ffn_down):
    B, S, D = x.shape
    depth = w_in.shape[0]
    biasm = _relbias(rel_bias_table)
    x2 = x.reshape(B * S, D)
    for l in range(depth):
        mod3 = _adaln(c, w_ada[l], b_ada[l]).reshape(B, 6, D)
        x2 = _layer(x2, B, mod3, biasm, norm1_gain[l], w_in[l], q_norm_gain[l], k_norm_gain[l],
                    attn_sinks[l], w_gk_up[l], b_gk[l], gla_norm_gain[l], w_branch_attn[l],
                    w_branch_gla[l], w_out[l], norm2_gain[l], w_ffn_up[l], ffn_conv_w[l], ffn_conv_b[l],
                    w_ffn_down[l])
    return x2.reshape(B, S, D)
```

```python
import functools
import itertools
import math

import numpy as np
import jax
import jax.numpy as jnp
from jax import lax
from jax.experimental import pallas as pl
from jax.experimental.pallas import tpu as pltpu

F32 = jnp.float32
BF16 = jnp.bfloat16

D_MODEL = 2048
N_Q_HEADS = 16
N_KV_HEADS = 4
GQA_GROUP = N_Q_HEADS // N_KV_HEADS
HEAD_DIM = 64
WINDOW = 128
ATTN_BLOCK = 128
N_BUCKETS = 32
MAX_DISTANCE = 128
GLA_HEADS = 4
GLA_DK = 256
GLA_DV = 512
GLA_LOWRANK = 16
GLA_NORMALIZER = 16.0
GLA_CHUNK = 64
D_FF = 5632
EPS = 1e-6
NEG_INF = -1e30
LOG2E = math.log2(math.e)

ATTN_Q_W = N_Q_HEADS * HEAD_DIM
ATTN_KV_W = N_KV_HEADS * HEAD_DIM
GLA_K_W = GLA_HEADS * GLA_DK
GLA_V_W = GLA_HEADS * GLA_DV

COL_GA = 0
COL_GB = COL_GA + D_MODEL
COL_GV = COL_GB + D_MODEL
COL_GR = COL_GV + GLA_V_W
COL_GQ = COL_GR + GLA_V_W
COL_GK = COL_GQ + GLA_K_W
COL_AQ = COL_GK + GLA_K_W
COL_AK = COL_AQ + ATTN_Q_W
COL_AV = COL_AK + ATTN_KV_W
COL_LR = COL_AV + ATTN_KV_W
LR_PAD = 128
PROJ_W = 12288

VMEM_LIMIT = 60 * 1024 * 1024

ADALN_TN = 1024
INPROJ_TM, INPROJ_TN = 1024, 2048
INPROJ_NORM_ROWS = 256
ATTN_TQ = 256
GLA_ROWS = 256
GLA_HEADS_PER_STEP = 4
MERGE_TM, MERGE_SUB = 512, 256
FFN_TM, FFN_TN = 1024, 512
FFN_HALO = 16
FFN_EPI_ROWS = 256
FFN_B_SPLIT = 256
DOWN_TM = 512


def _params(*sem):
    return pltpu.CompilerParams(dimension_semantics=sem, vmem_limit_bytes=VMEM_LIMIT)


def _dot(a, b):
    return jnp.dot(a, b, preferred_element_type=F32)


def _dot_nt(a, b):
    return lax.dot_general(a, b, (((1,), (1,)), ((), ())), preferred_element_type=F32)


def _dot_tn(a, b):
    return lax.dot_general(a, b, (((0,), (0,)), ((), ())), preferred_element_type=F32)


def _rms(x, gain):
    return x * lax.rsqrt(jnp.mean(x * x, axis=-1, keepdims=True) + EPS) * gain


def _adaln_kernel(c_ref, w_ref, b_ref, o_ref):
    c = c_ref[...]
    ca = c * jax.nn.sigmoid(c)
    o_ref[...] = _dot(ca.astype(BF16), w_ref[...].astype(BF16)) + b_ref[...]


def _adaln(c, w_ada, b_ada):
    B, D = c.shape
    N = w_ada.shape[1]
    return pl.pallas_call(
        _adaln_kernel,
        grid=(N // ADALN_TN,),
        in_specs=[pl.BlockSpec((B, D), lambda j: (0, 0)),
                  pl.BlockSpec((D, ADALN_TN), lambda j: (0, j)),
                  pl.BlockSpec((1, ADALN_TN), lambda j: (0, j))],
        out_specs=pl.BlockSpec((B, ADALN_TN), lambda j: (0, j)),
        out_shape=jax.ShapeDtypeStruct((B, N), F32),
        compiler_params=_params("parallel"),
        name="adaln",
    )(c, w_ada, b_ada.reshape(1, N))


def _bucket_table():
    j = np.arange(2 * ATTN_BLOCK)[:, None]
    i = np.arange(ATTN_BLOCK)[None, :]
    dist = i + ATTN_BLOCK - j
    max_exact = N_BUCKETS // 2
    d = np.maximum(dist, 0)
    ratio = np.log(np.maximum(d, 1).astype(np.float32) / np.float32(max_exact)) / np.float32(
        math.log(MAX_DISTANCE / max_exact))
    large = max_exact + (ratio.astype(np.float32) * np.float32(N_BUCKETS - max_exact)).astype(np.int32)
    large = np.minimum(large, N_BUCKETS - 1)
    bucket = np.where(d < max_exact, d, large)
    in_window = (dist >= 0) & (dist < WINDOW)
    return np.where(in_window, bucket, -1).astype(np.int32)


def _relbias_kernel(tab_ref, bkt_ref, o_ref):
    h = pl.program_id(0)
    bkt = bkt_ref[...]
    acc = jnp.zeros(bkt.shape, F32)
    for b in range(N_BUCKETS):
        acc = jnp.where(bkt == b, tab_ref[b, h], acc)
    regular = jnp.where(bkt >= 0, acc * LOG2E, NEG_INF)
    key = lax.broadcasted_iota(jnp.int32, bkt.shape, 0)
    o_ref[0] = jnp.where(key >= ATTN_BLOCK, regular, NEG_INF)
    o_ref[1] = regular


def _relbias(rel_bias_table):
    bkt = jnp.asarray(_bucket_table())
    nk, nq = bkt.shape
    return pl.pallas_call(
        _relbias_kernel,
        grid=(N_Q_HEADS,),
        in_specs=[pl.BlockSpec(memory_space=pltpu.SMEM),
                  pl.BlockSpec(bkt.shape, lambda h: (0, 0))],
        out_specs=pl.BlockSpec((2, None, nk, nq), lambda h: (0, h // GQA_GROUP, h % 2, (h // 2) % 2)),
        out_shape=jax.ShapeDtypeStruct((2, N_KV_HEADS, 2 * nk, 2 * nq), F32),
        compiler_params=_params("parallel"),
        name="relbias",
    )(rel_bias_table, bkt)


SRC_GQ = ATTN_Q_W + 2 * ATTN_KV_W
SRC_GV = SRC_GQ + 2 * GLA_K_W
SRC_LR = SRC_GV + 2 * GLA_V_W
SRC_GA = SRC_LR + GLA_LOWRANK
WPACK_ROWS = 512


def _wpack_src_row(b):
    r = b * WPACK_ROWS
    src = jnp.where(r < COL_GV, SRC_GA + r,
                    jnp.where(r < COL_GQ, SRC_GV + (r - COL_GV),
                              jnp.where(r < COL_AQ, SRC_GQ + (r - COL_GQ),
                                        jnp.where(r < COL_LR, r - COL_AQ, SRC_LR))))
    return pl.multiple_of(src, 8)


def _wpack_kernel(w_ref, o_ref):
    r = pl.program_id(0) * WPACK_ROWS
    nvalid = jnp.where(r < COL_LR, WPACK_ROWS, jnp.where(r == COL_LR, GLA_LOWRANK, 0))
    row = lax.broadcasted_iota(jnp.int32, w_ref.shape, 0)
    o_ref[...] = jnp.where(row < nvalid, w_ref[...], 0.0).astype(BF16)


def _wpack(w_in_t):
    _, D = w_in_t.shape
    return pl.pallas_call(
        _wpack_kernel,
        grid=(PROJ_W // WPACK_ROWS,),
        in_specs=[pl.BlockSpec((pl.Element(WPACK_ROWS), pl.Element(D)), lambda b: (_wpack_src_row(b), 0))],
        out_specs=pl.BlockSpec((WPACK_ROWS, D), lambda b: (b, 0)),
        out_shape=jax.ShapeDtypeStruct((PROJ_W, D), BF16),
        compiler_params=_params("parallel"),
        name="wpack",
    )(w_in_t)


def _norm_modulate(x, mod_ref, gain):
    return (_rms(x, gain) * (1.0 + mod_ref[1:2, :]) + mod_ref[0:1, :]).astype(BF16)


def _hnorm_kernel(x_ref, mod_ref, g_ref, o_ref):
    o_ref[...] = _norm_modulate(x_ref[...], mod_ref, g_ref[...])


def _hnorm_first(x2, mod3, gain):
    D = x2.shape[1]
    nr = INPROJ_NORM_ROWS
    return pl.pallas_call(
        _hnorm_kernel,
        grid=(INPROJ_TM // nr,),
        in_specs=[pl.BlockSpec((nr, D), lambda r: (r, 0)),
                  pl.BlockSpec((None, 6, D), lambda r: (0, 0, 0)),
                  pl.BlockSpec((1, D), lambda r: (0, 0))],
        out_specs=pl.BlockSpec((nr, D), lambda r: (r, 0)),
        out_shape=jax.ShapeDtypeStruct((INPROJ_TM, D), BF16),
        compiler_params=_params("parallel"),
        name="hnorm_first",
    )(x2, mod3, gain)


def _cast_block(shape, steps):
    rows, cols = shape
    bc = min(cols, 1024)
    assert cols % bc == 0
    for br in range(16, rows + 1, 16):
        if rows % br == 0 and (rows // br) * (cols // bc) <= steps:
            return br, bc
    raise ValueError(f"no cast block for {shape} in {steps} steps")


def _inproj_kernel(xn_ref, modn_ref, g_ref, w_ref, h0_ref, *rest, n_cast):
    cast_in, o_ref, cast_out = rest[:n_cast], rest[n_cast], rest[n_cast + 1:2 * n_cast + 1]
    ha_ref, hb_ref = rest[2 * n_cast + 1:]
    i, j = pl.program_id(0), pl.program_id(1)
    nr = INPROJ_NORM_ROWS

    @pl.when((i == 0) & (j == 0))
    def _():
        ha_ref[...] = h0_ref[...]

    def step(cur_ref, nxt_ref):
        o_ref[...] = _dot_nt(cur_ref[...], w_ref[...]).astype(BF16)
        chunk = jnp.minimum(j, INPROJ_TM // nr - 1)
        rows = pl.ds(pl.multiple_of(chunk * nr, nr), nr)
        nxt_ref[rows, :] = _norm_modulate(xn_ref[...], modn_ref, g_ref[...])
        for src_ref, dst_ref in zip(cast_in, cast_out):
            dst_ref[...] = src_ref[...].astype(BF16)

    @pl.when(i % 2 == 0)
    def _():
        step(ha_ref, hb_ref)

    @pl.when(i % 2 == 1)
    def _():
        step(hb_ref, ha_ref)


def _inproj(x2, mod3, gain, w, cast_weights):
    T, D = x2.shape
    S = T // mod3.shape[0]
    tm, tn, nr = INPROJ_TM, INPROJ_TN, INPROJ_NORM_ROWS
    n_i, n_j, chunks = T // tm, PROJ_W // tn, tm // nr
    assert n_j >= chunks

    def next_tile(i):
        return jnp.minimum(i + 1, n_i - 1)

    def cast_spec(shape):
        br, bc = _cast_block(shape, n_i * n_j)
        ncb = shape[1] // bc
        last = (shape[0] // br) * ncb - 1

        def index(i, j):
            t = jnp.minimum(i * n_j + j, last)
            return t // ncb, t % ncb

        return pl.BlockSpec((br, bc), index)

    cast_specs = [cast_spec(cw.shape) for cw in cast_weights]
    outs = pl.pallas_call(
        functools.partial(_inproj_kernel, n_cast=len(cast_weights)),
        grid=(n_i, n_j),
        in_specs=[pl.BlockSpec((nr, D), lambda i, j: (next_tile(i) * chunks + jnp.minimum(j, chunks - 1), 0)),
                  pl.BlockSpec((None, 6, D), lambda i, j: (next_tile(i) // (S // tm), 0, 0)),
                  pl.BlockSpec((1, D), lambda i, j: (0, 0)),
                  pl.BlockSpec((tn, D), lambda i, j: (j, 0)),
                  pl.BlockSpec((tm, D), lambda i, j: (0, 0), pipeline_mode=pl.Buffered(1))] + cast_specs,
        out_specs=[pl.BlockSpec((tm, tn), lambda i, j: (i, j))] + cast_specs,
        out_shape=[jax.ShapeDtypeStruct((T, PROJ_W), BF16)]
        + [jax.ShapeDtypeStruct(cw.shape, BF16) for cw in cast_weights],
        scratch_shapes=[pltpu.VMEM((tm, D), BF16), pltpu.VMEM((tm, D), BF16)],
        compiler_params=_params("arbitrary", "arbitrary"),
        name="inproj",
    )(x2, mod3, gain, w, _hnorm_first(x2, mod3, gain), *cast_weights)
    return outs[0], outs[1:]


def _attn_kernel(q_ref, kc_ref, kp_ref, vc_ref, vp_ref, b0_ref, b1_ref, qg_ref, kg_ref, sink_ref, seg_ref,
                 o_ref):
    nsub = ATTN_TQ // ATTN_BLOCK
    blk, dh = ATTN_BLOCK, HEAD_DIM
    seg = seg_ref[...]

    def normed(x, gain):
        ssq = _dot((x * x).astype(BF16), seg)
        return x * lax.rsqrt(ssq * (1.0 / dh) + EPS) * gain

    kn = normed(jnp.concatenate([kp_ref[...], kc_ref[...]], axis=0).astype(F32), kg_ref[...])
    vt = jnp.concatenate([vp_ref[...], vc_ref[...]], axis=0).astype(F32).T.astype(BF16)
    lane = lax.broadcasted_iota(jnp.int32, (kn.shape[0], 2 * dh), 1)
    qgain = qg_ref[...] * (dh ** -0.5 * LOG2E)

    kzs, qns = [], []
    for g in range(N_KV_HEADS):
        pair = kn[:, (g // 2) * 2 * dh:(g // 2 + 1) * 2 * dh]
        own = jnp.where(lane >= dh if g % 2 else lane < dh, pair, 0.0)
        swapped = pltpu.roll(own, dh, axis=1)
        kz = [own, swapped] if g % 2 == 0 else [swapped, own]
        kzs.append([z.astype(BF16) for z in kz])
        qns.append(normed(q_ref[:, g * 4 * dh:(g + 1) * 4 * dh].astype(F32), qgain).astype(BF16))

    def scores(g, sb):
        bref = b0_ref if sb == 0 else b1_ref
        band = slice(sb * blk, sb * blk + 2 * blk)
        rows = slice(sb * blk, (sb + 1) * blk)
        kband = jnp.concatenate([kzs[g][0][band], kzs[g][1][band]], axis=0)
        qr = jnp.concatenate([qns[g][rows, 0:2 * dh], qns[g][rows, 2 * dh:4 * dh]], axis=0)
        return _dot_nt(kband, qr) + bref[g]

    def finish(g, sb, s):
        band = slice(sb * blk, sb * blk + 2 * blk)
        rows = slice(sb * blk, (sb + 1) * blk)
        ps, inv = [], []
        for e in range(2):
            se = s[e * 2 * blk:(e + 1) * 2 * blk]
            sink = sink_ref[g, e] * LOG2E
            m = jnp.maximum(jnp.max(se, axis=0, keepdims=True), sink)
            p = jnp.exp2(se - m)
            inv.append(1.0 / (jnp.sum(p, axis=0, keepdims=True) + jnp.exp2(sink - m)))
            ps.append(p.astype(BF16))
        ot = _dot(vt[g * dh:(g + 1) * dh, band], jnp.concatenate(ps, axis=1))
        ot = ot * jnp.concatenate(inv, axis=1)
        o = jnp.concatenate([ot[:, 0:2 * blk], ot[:, 2 * blk:4 * blk]], axis=0).T
        o_ref[rows, (2 * g) * 2 * dh:(2 * g + 1) * 2 * dh] = o[0:blk].astype(BF16)
        o_ref[rows, (2 * g + 1) * 2 * dh:(2 * g + 2) * 2 * dh] = o[blk:2 * blk].astype(BF16)

    units = [(g, sb) for g in range(N_KV_HEADS) for sb in range(nsub)]
    s_next = scores(*units[0])
    for n, unit in enumerate(units):
        s_cur = s_next
        if n + 1 < len(units):
            s_next = scores(*units[n + 1])
        finish(*unit, s_cur)


def _attn(proj3, biasm, q_gain, k_gain, sinks):
    B, S, _ = proj3.shape
    tq, blk = ATTN_TQ, ATTN_BLOCK
    nsub = tq // blk
    kvw = ATTN_KV_W
    seg_id = np.arange(kvw) // HEAD_DIM
    seg = jnp.asarray(seg_id[:, None] == seg_id[None, :], dtype=BF16)
    sink_rows = jnp.repeat(sinks.reshape(N_KV_HEADS, 2, 2).transpose(0, 2, 1), blk, axis=-1)
    sink_rows = sink_rows.reshape(N_KV_HEADS, 2, 1, 2 * blk)

    def prev(i):
        return jnp.maximum(i * nsub - 1, 0)

    bias_block = (None,) + biasm.shape[1:]
    return pl.pallas_call(
        _attn_kernel,
        grid=(B, S // tq),
        in_specs=[pl.BlockSpec((None, tq, ATTN_Q_W), lambda b, i: (b, i, COL_AQ // ATTN_Q_W)),
                  pl.BlockSpec((None, tq, kvw), lambda b, i: (b, i, COL_AK // kvw)),
                  pl.BlockSpec((None, blk, kvw), lambda b, i: (b, prev(i), COL_AK // kvw)),
                  pl.BlockSpec((None, tq, kvw), lambda b, i: (b, i, COL_AV // kvw)),
                  pl.BlockSpec((None, blk, kvw), lambda b, i: (b, prev(i), COL_AV // kvw)),
                  pl.BlockSpec(bias_block, lambda b, i: (jnp.minimum(i, 1), 0, 0, 0)),
                  pl.BlockSpec(bias_block, lambda b, i: (1, 0, 0, 0)),
                  pl.BlockSpec((1, kvw), lambda b, i: (0, 0)),
                  pl.BlockSpec((1, kvw), lambda b, i: (0, 0)),
                  pl.BlockSpec(sink_rows.shape, lambda b, i: (0, 0, 0, 0)),
                  pl.BlockSpec(seg.shape, lambda b, i: (0, 0))],
        out_specs=pl.BlockSpec((None, tq, ATTN_Q_W), lambda b, i: (b, i, 0)),
        out_shape=jax.ShapeDtypeStruct((B, S, ATTN_Q_W), BF16),
        compiler_params=_params("parallel", "parallel"),
        name="attn",
    )(proj3, proj3, proj3, proj3, proj3, biasm, biasm, jnp.tile(q_gain, (1, GQA_GROUP)),
      jnp.tile(k_gain, (1, N_KV_HEADS)), sink_rows, seg)


def _split2(x):
    hi = x.astype(BF16)
    return hi, (x - hi.astype(F32)).astype(BF16)


def _gla_head(q, k, v, r, lr, wgk, bgk, gain, state_ref, masks, store):
    C = GLA_CHUNK
    tril, m_same, m_next, m_far, eye = masks
    z = _dot(lr, wgk) + bgk
    yield
    gl = (jnp.minimum(z, 0.0) - jnp.log(1.0 + jnp.exp(-jnp.abs(z)))) * (1.0 / GLA_NORMALIZER)
    hi, lo = _split2(gl)
    g = _dot(tril, hi) + _dot(tril, lo)
    yield
    t = [g[(c + 1) * C - 1:(c + 1) * C] for c in range(4)]
    t_rows = jnp.concatenate([jnp.broadcast_to(tc, (C, tc.shape[1])) for tc in t], axis=0)

    q_dec = q.astype(F32) * (GLA_DK ** -0.5) * jnp.exp(g)
    kf = k.astype(F32)
    k_inv = (kf * jnp.exp(-g)).astype(BF16)
    k_end = kf * jnp.exp(t_rows - g)
    qd = [q_dec[c * C:(c + 1) * C] for c in range(4)]
    ke = [k_end[c * C:(c + 1) * C] for c in range(4)]

    def rows(parts):
        return jnp.concatenate(parts, axis=0).astype(BF16)

    q_b, k_b = q_dec.astype(BF16), k_end.astype(BF16)
    q_far = rows([qd[0], qd[1], qd[2], qd[3] * jnp.exp(t[2])])
    k_far = rows([ke[0] * jnp.exp(t[1]), ke[1], ke[2], ke[3]])
    q_abs = rows([qd[0], qd[1] * jnp.exp(t[0]), qd[2] * jnp.exp(t[0] + t[1]), qd[3] * jnp.exp(t[0] + t[1] + t[2])])
    k_abs = rows([ke[0] * jnp.exp(t[1] + t[2] + t[3]), ke[1] * jnp.exp(t[2] + t[3]), ke[2] * jnp.exp(t[3]), ke[3]])

    a_same, a_next, a_far = _dot_nt(q_b, k_inv), _dot_nt(q_b, k_b), _dot_nt(q_far, k_far)
    state = state_ref[...]
    o_state = _dot(q_abs, state.astype(BF16))
    update = _dot_tn(k_abs, v)
    yield
    a = jnp.where(m_same, a_same, jnp.where(m_next, a_next, jnp.where(m_far, a_far, 0.0)))
    o = _dot(a.astype(BF16), v) + o_state
    yield

    decay = jnp.exp(t[0] + t[1] + t[2] + t[3])
    decay_col = jnp.sum(jnp.where(eye, jnp.broadcast_to(decay, eye.shape), 0.0), axis=1, keepdims=True)
    state_ref[...] = decay_col * state + update

    rf = r.astype(F32)
    store((_rms(o, gain) * (rf * jax.nn.sigmoid(rf))).astype(BF16))


def _gla_kernel(q_ref, k_ref, v_ref, r_ref, lr_ref, wgk_ref, bgk_ref, gain_ref, o_ref, state_ref):
    R, C, dk, dv = GLA_ROWS, GLA_CHUNK, GLA_DK, GLA_DV

    @pl.when(pl.program_id(2) == 0)
    def _():
        state_ref[...] = jnp.zeros_like(state_ref)

    row = lax.broadcasted_iota(jnp.int32, (R, R), 0)
    col = lax.broadcasted_iota(jnp.int32, (R, R), 1)
    ci, cj = row // C, col // C
    m_same = (row >= col) & (ci == cj)
    m_next = (ci == cj + 1) & (ci != 2)
    m_far = (ci >= 2) & (cj <= 1)
    eye = lax.broadcasted_iota(jnp.int32, (dk, dk), 0) == lax.broadcasted_iota(jnp.int32, (dk, dk), 1)
    masks = (m_same.astype(BF16), m_same, m_next, m_far, eye)

    lr = lr_ref[...]
    heads = []
    for h in range(GLA_HEADS_PER_STEP):
        ks, vs = slice(h * dk, (h + 1) * dk), slice(h * dv, (h + 1) * dv)
        heads.append(_gla_head(q_ref[:, ks], k_ref[:, ks], v_ref[:, vs], r_ref[:, vs], lr, wgk_ref[:, ks],
                               bgk_ref[:, ks], gain_ref[...], state_ref.at[h], masks,
                               functools.partial(o_ref.__setitem__, (slice(None), vs))))
    for _ in itertools.zip_longest(*heads):
        pass


def _gla(proj3, wgk, bgk, gain):
    B, S, _ = proj3.shape
    R, hps = GLA_ROWS, GLA_HEADS_PER_STEP
    dk, dv = hps * GLA_DK, hps * GLA_DV
    return pl.pallas_call(
        _gla_kernel,
        grid=(B, GLA_HEADS // hps, S // R),
        in_specs=[pl.BlockSpec((None, R, dk), lambda b, h, t: (b, t, COL_GQ // dk + h)),
                  pl.BlockSpec((None, R, dk), lambda b, h, t: (b, t, COL_GK // dk + h)),
                  pl.BlockSpec((None, R, dv), lambda b, h, t: (b, t, COL_GV // dv + h)),
                  pl.BlockSpec((None, R, dv), lambda b, h, t: (b, t, COL_GR // dv + h)),
                  pl.BlockSpec((None, R, LR_PAD), lambda b, h, t: (b, t, COL_LR // LR_PAD)),
                  pl.BlockSpec((LR_PAD, dk), lambda b, h, t: (0, h)),
                  pl.BlockSpec((1, dk), lambda b, h, t: (0, h)),
                  pl.BlockSpec((1, GLA_DV), lambda b, h, t: (0, 0))],
        out_specs=pl.BlockSpec((None, R, dv), lambda b, h, t: (b, t, h)),
        out_shape=jax.ShapeDtypeStruct((B, S, GLA_V_W), BF16),
        scratch_shapes=[pltpu.VMEM((hps, GLA_DK, GLA_DV), F32)],
        compiler_params=_params("parallel", "parallel", "arbitrary"),
        name="gla",
    )(proj3, proj3, proj3, proj3, proj3, wgk, bgk, gain)


def _merge_kernel(ya_ref, yg_ref, ga_ref, gb_ref, x_ref, mod_ref, g2_ref, wa_ref, wg_ref, wo_ref,
                  x1_ref, h2_ref):
    def merged_branches(rows):
        ma, mg = _dot(ya_ref[rows, :], wa_ref[...]), _dot(yg_ref[rows, :], wg_ref[...])
        ga = jax.nn.sigmoid(ga_ref[rows, :].astype(F32))
        gb = jax.nn.sigmoid(gb_ref[rows, :].astype(F32))
        return (ga * ma + gb * mg).astype(BF16)

    def project(rows, merged):
        x1 = x_ref[rows, :] + mod_ref[2:3, :] * _dot(merged, wo_ref[...])
        x1_ref[rows, :] = x1
        h2_ref[rows, :] = (_rms(x1, g2_ref[...]) * (1.0 + mod_ref[4:5, :]) + mod_ref[3:4, :]).astype(BF16)

    for r0 in range(0, MERGE_TM, MERGE_SUB):
        rows = slice(r0, r0 + MERGE_SUB)
        project(rows, merged_branches(rows))


def _merge(ya, yg, proj, x2, mod3, gain2, wa, wg, wo):
    T, D = x2.shape
    S = T // mod3.shape[0]
    tm = MERGE_TM
    once = pl.Buffered(1)
    return pl.pallas_call(
        _merge_kernel,
        grid=(T // tm,),
        in_specs=[pl.BlockSpec((tm, ATTN_Q_W), lambda i: (i, 0)),
                  pl.BlockSpec((tm, GLA_V_W), lambda i: (i, 0)),
                  pl.BlockSpec((tm, D), lambda i: (i, COL_GA // D)),
                  pl.BlockSpec((tm, D), lambda i: (i, COL_GB // D)),
                  pl.BlockSpec((tm, D), lambda i: (i, 0)),
                  pl.BlockSpec((None, 6, D), lambda i: (i // (S // tm), 0, 0)),
                  pl.BlockSpec((1, D), lambda i: (0, 0)),
                  pl.BlockSpec((ATTN_Q_W, D), lambda i: (0, 0), pipeline_mode=once),
                  pl.BlockSpec((GLA_V_W, D), lambda i: (0, 0), pipeline_mode=once),
                  pl.BlockSpec((D, D), lambda i: (0, 0), pipeline_mode=once)],
        out_specs=[pl.BlockSpec((tm, D), lambda i: (i, 0)),
                   pl.BlockSpec((tm, D), lambda i: (i, 0))],
        out_shape=[jax.ShapeDtypeStruct((T, D), F32), jax.ShapeDtypeStruct((T, D), BF16)],
        compiler_params=_params("parallel"),
        name="merge",
    )(ya, yg, proj, proj, x2, mod3, gain2, wa, wg, wo)


def _ffn_up_kernel(h_ref, halo_ref, wa_ref, wb_ref, cwa_ref, cwb_ref, cba_ref, cbb_ref, o_ref,
                   hs_ref, ua_ref, ub_ref, *, tiles_per_seq):
    tm, halo = FFN_TM, FFN_HALO

    @pl.when(pl.program_id(1) == 0)
    def _():
        first = (pl.program_id(0) % tiles_per_seq) == 0
        hs_ref[0:halo, :] = jnp.where(first, jnp.zeros_like(halo_ref), halo_ref[...])
        hs_ref[halo:, :] = h_ref[...]

    def conv(u_ref, cw_ref, cb_ref, r0, n):
        u = u_ref[halo + r0 - 8:halo + r0 + n, :]
        y = cb_ref[...] + cw_ref[0:1, :] * pltpu.roll(u, 2, axis=0)[8:]
        y = y + cw_ref[1:2, :] * pltpu.roll(u, 1, axis=0)[8:]
        return y + cw_ref[2:3, :] * u[8:]

    ua_ref[...] = _dot(hs_ref[...], wa_ref[...])
    lo = 0
    for hi in range(halo + FFN_B_SPLIT, tm + halo + 1, FFN_B_SPLIT):
        ub_ref[lo:hi, :] = _dot(hs_ref[lo:hi, :], wb_ref[...])
        lo = hi
    for r0 in range(0, tm, FFN_EPI_ROWS):
        ya = conv(ua_ref, cwa_ref, cba_ref, r0, FFN_EPI_ROWS)
        yb = conv(ub_ref, cwb_ref, cbb_ref, r0, FFN_EPI_ROWS)
        o_ref[r0:r0 + FFN_EPI_ROWS, :] = (ya * jax.nn.sigmoid(ya) * yb).astype(BF16)


def _ffn_up(h2, S, w_up, conv_w, conv_b):
    T, D = h2.shape
    tm, tn, halo = FFN_TM, FFN_TN, FFN_HALO
    nj = D_FF // tn
    return pl.pallas_call(
        functools.partial(_ffn_up_kernel, tiles_per_seq=S // tm),
        grid=(T // tm, D_FF // tn),
        in_specs=[pl.BlockSpec((tm, D), lambda i, j: (i, 0)),
                  pl.BlockSpec((halo, D), lambda i, j: (jnp.maximum(i * (tm // halo) - 1, 0), 0)),
                  pl.BlockSpec((D, tn), lambda i, j: (0, j)),
                  pl.BlockSpec((D, tn), lambda i, j: (0, j + nj)),
                  pl.BlockSpec((3, tn), lambda i, j: (0, j)),
                  pl.BlockSpec((3, tn), lambda i, j: (0, j + nj)),
                  pl.BlockSpec((1, tn), lambda i, j: (0, j)),
                  pl.BlockSpec((1, tn), lambda i, j: (0, j + nj))],
        out_specs=pl.BlockSpec((tm, tn), lambda i, j: (i, j)),
        out_shape=jax.ShapeDtypeStruct((T, D_FF), BF16),
        scratch_shapes=[pltpu.VMEM((tm + halo, D), BF16),
                        pltpu.VMEM((tm + halo, tn), F32),
                        pltpu.VMEM((tm + halo, tn), F32)],
        compiler_params=_params("parallel", "arbitrary"),
        name="ffn_up",
    )(h2, h2, w_up, w_up, conv_w, conv_w, conv_b, conv_b)


def _ffn_down_kernel(a_ref, w_ref, x_ref, mod_ref, o_ref):
    o_ref[...] = x_ref[...] + mod_ref[5:6, :] * _dot(a_ref[...], w_ref[...])


def _ffn_down(act, wd, x1, mod3):
    T, D = x1.shape
    S = T // mod3.shape[0]
    tm = DOWN_TM
    return pl.pallas_call(
        _ffn_down_kernel,
        grid=(T // tm,),
        in_specs=[pl.BlockSpec((tm, D_FF), lambda i: (i, 0)),
                  pl.BlockSpec((D_FF, D), lambda i: (0, 0), pipeline_mode=pl.Buffered(1)),
                  pl.BlockSpec((tm, D), lambda i: (i, 0)),
                  pl.BlockSpec((None, 6, D), lambda i: (i // (S // tm), 0, 0))],
        out_specs=pl.BlockSpec((tm, D), lambda i: (i, 0)),
        out_shape=jax.ShapeDtypeStruct((T, D), F32),
        compiler_params=_params("parallel"),
        name="ffn_down",
    )(act, wd, x1, mod3)


def _layer(x2, B, mod3, biasm, norm1_gain, w_in, q_norm_gain, k_norm_gain, attn_sinks, w_gk_up, b_gk,
           gla_norm_gain, w_branch_attn, w_branch_gla, w_out, norm2_gain, w_ffn_up, ffn_conv_w,
           ffn_conv_b, w_ffn_down):
    T, D = x2.shape
    S = T // B
    w_p = _wpack(w_in.T)
    wgk = jnp.concatenate([w_gk_up, jnp.zeros((LR_PAD - GLA_LOWRANK, GLA_K_W), w_gk_up.dtype)],
                          axis=0).astype(BF16)

    proj, (w_ba, w_bg, w_o, w_up, w_down) = _inproj(
        x2, mod3, norm1_gain.reshape(1, D), w_p, (w_branch_attn, w_branch_gla, w_out, w_ffn_up, w_ffn_down))
    proj3 = proj.reshape(B, S, PROJ_W)
    ya = _attn(proj3, biasm, q_norm_gain.reshape(1, HEAD_DIM), k_norm_gain.reshape(1, HEAD_DIM), attn_sinks)
    yg = _gla(proj3, wgk, b_gk.reshape(1, GLA_K_W), gla_norm_gain.reshape(1, GLA_DV))
    x1, h2 = _merge(ya.reshape(T, ATTN_Q_W), yg.reshape(T, GLA_V_W), proj, x2, mod3,
                    norm2_gain.reshape(1, D), w_ba, w_bg, w_o)
    act = _ffn_up(h2, S, w_up, ffn_conv_w, ffn_conv_b.reshape(1, 2 * D_FF))
    return _ffn_down(act, w_down, x1, mod3)


def kernel(x, c, rel_bias_table, w_ada, b_ada, norm1_gain, w_in, q_norm_gain, k_norm_gain, attn_sinks,
           w_gk_up, b_gk, gla_norm_gain, w_branch_attn, w_branch_gla, w_out, norm2_gain, w_ffn_up,
           ffn_conv_w, ffn_conv_b, w_ffn_down):
    B, S, D = x.shape
    depth = w_in.shape[0]
    biasm = _relbias(rel_bias_table)
    x2 = x.reshape(B * S, D)
    for l in range(depth):
        mod3 = _adaln(c, w_ada[l], b_ada[l]).reshape(B, 6, D)
        x2 = _layer(x2, B, mod3, biasm, norm1_gain[l], w_in[l], q_norm_gain[l], k_norm_gain[l],
                    attn_sinks[l], w_gk_up[l], b_gk[l], gla_norm_gain[l], w_branch_attn[l],
                    w_branch_gla[l], w_out[l], norm2_gain[l], w_ffn_up[l], ffn_conv_w[l], ffn_conv_b[l],
                    w_ffn_down[l])
    return x2.reshape(B, S, D)
```

```python
import functools
import itertools
import math

import numpy as np
import jax
import jax.numpy as jnp
from jax import lax
from jax.experimental import pallas as pl
from jax.experimental.pallas import tpu as pltpu

F32 = jnp.float32
BF16 = jnp.bfloat16

D_MODEL = 2048
N_Q_HEADS = 16
N_KV_HEADS = 4
GQA_GROUP = N_Q_HEADS // N_KV_HEADS
HEAD_DIM = 64
WINDOW = 128
ATTN_BLOCK = 128
N_BUCKETS = 32
MAX_DISTANCE = 128
GLA_HEADS = 4
GLA_DK = 256
GLA_DV = 512
GLA_LOWRANK = 16
GLA_NORMALIZER = 16.0
GLA_CHUNK = 64
D_FF = 5632
EPS = 1e-6
NEG_INF = -1e30
LOG2E = math.log2(math.e)

ATTN_Q_W = N_Q_HEADS * HEAD_DIM
ATTN_KV_W = N_KV_HEADS * HEAD_DIM
GLA_K_W = GLA_HEADS * GLA_DK
GLA_V_W = GLA_HEADS * GLA_DV

COL_GA = 0
COL_GB = COL_GA + D_MODEL
COL_GV = COL_GB + D_MODEL
COL_GR = COL_GV + GLA_V_W
COL_GQ = COL_GR + GLA_V_W
COL_GK = COL_GQ + GLA_K_W
COL_AQ = COL_GK + GLA_K_W
COL_AK = COL_AQ + ATTN_Q_W
COL_AV = COL_AK + ATTN_KV_W
COL_LR = COL_AV + ATTN_KV_W
LR_PAD = 128
PROJ_W = 12288

VMEM_LIMIT = 60 * 1024 * 1024

ADALN_TN = 1024
INPROJ_TM, INPROJ_TN = 1024, 2048
INPROJ_NORM_ROWS = 256
ATTN_TQ = 256
GLA_ROWS = 256
GLA_HEADS_PER_STEP = 4
MERGE_TM, MERGE_SUB = 512, 256
FFN_TM, FFN_TN = 1024, 512
FFN_HALO = 16
FFN_EPI_ROWS = 256
DOWN_TM = 512


def _params(*sem):
    return pltpu.CompilerParams(dimension_semantics=sem, vmem_limit_bytes=VMEM_LIMIT)


def _dot(a, b):
    return jnp.dot(a, b, preferred_element_type=F32)


def _dot_nt(a, b):
    return lax.dot_general(a, b, (((1,), (1,)), ((), ())), preferred_element_type=F32)


def _dot_tn(a, b):
    return lax.dot_general(a, b, (((0,), (0,)), ((), ())), preferred_element_type=F32)


def _rms(x, gain):
    return x * lax.rsqrt(jnp.mean(x * x, axis=-1, keepdims=True) + EPS) * gain


def _adaln_kernel(c_ref, w_ref, b_ref, o_ref):
    c = c_ref[...]
    ca = c * jax.nn.sigmoid(c)
    o_ref[...] = _dot(ca.astype(BF16), w_ref[...].astype(BF16)) + b_ref[...]


def _adaln(c, w_ada, b_ada):
    B, D = c.shape
    N = w_ada.shape[1]
    return pl.pallas_call(
        _adaln_kernel,
        grid=(N // ADALN_TN,),
        in_specs=[pl.BlockSpec((B, D), lambda j: (0, 0)),
                  pl.BlockSpec((D, ADALN_TN), lambda j: (0, j)),
                  pl.BlockSpec((1, ADALN_TN), lambda j: (0, j))],
        out_specs=pl.BlockSpec((B, ADALN_TN), lambda j: (0, j)),
        out_shape=jax.ShapeDtypeStruct((B, N), F32),
        compiler_params=_params("parallel"),
        name="adaln",
    )(c, w_ada, b_ada.reshape(1, N))


def _bucket_table():
    j = np.arange(2 * ATTN_BLOCK)[:, None]
    i = np.arange(ATTN_BLOCK)[None, :]
    dist = i + ATTN_BLOCK - j
    max_exact = N_BUCKETS // 2
    d = np.maximum(dist, 0)
    ratio = np.log(np.maximum(d, 1).astype(np.float32) / np.float32(max_exact)) / np.float32(
        math.log(MAX_DISTANCE / max_exact))
    large = max_exact + (ratio.astype(np.float32) * np.float32(N_BUCKETS - max_exact)).astype(np.int32)
    large = np.minimum(large, N_BUCKETS - 1)
    bucket = np.where(d < max_exact, d, large)
    in_window = (dist >= 0) & (dist < WINDOW)
    return np.where(in_window, bucket, -1).astype(np.int32)


def _relbias_kernel(tab_ref, bkt_ref, o_ref):
    h = pl.program_id(0)
    bkt = bkt_ref[...]
    acc = jnp.zeros(bkt.shape, F32)
    for b in range(N_BUCKETS):
        acc = jnp.where(bkt == b, tab_ref[b, h], acc)
    regular = jnp.where(bkt >= 0, acc * LOG2E, NEG_INF)
    key = lax.broadcasted_iota(jnp.int32, bkt.shape, 0)
    o_ref[0] = jnp.where(key >= ATTN_BLOCK, regular, NEG_INF)
    o_ref[1] = regular


def _relbias(rel_bias_table):
    bkt = jnp.asarray(_bucket_table())
    nk, nq = bkt.shape
    return pl.pallas_call(
        _relbias_kernel,
        grid=(N_Q_HEADS,),
        in_specs=[pl.BlockSpec(memory_space=pltpu.SMEM),
                  pl.BlockSpec(bkt.shape, lambda h: (0, 0))],
        out_specs=pl.BlockSpec((2, None, nk, nq), lambda h: (0, h // GQA_GROUP, h % 2, (h // 2) % 2)),
        out_shape=jax.ShapeDtypeStruct((2, N_KV_HEADS, 2 * nk, 2 * nq), F32),
        compiler_params=_params("parallel"),
        name="relbias",
    )(rel_bias_table, bkt)


SRC_GQ = ATTN_Q_W + 2 * ATTN_KV_W
SRC_GV = SRC_GQ + 2 * GLA_K_W
SRC_LR = SRC_GV + 2 * GLA_V_W
SRC_GA = SRC_LR + GLA_LOWRANK
WPACK_ROWS = 512


def _wpack_src_row(b):
    r = b * WPACK_ROWS
    src = jnp.where(r < COL_GV, SRC_GA + r,
                    jnp.where(r < COL_GQ, SRC_GV + (r - COL_GV),
                              jnp.where(r < COL_AQ, SRC_GQ + (r - COL_GQ),
                                        jnp.where(r < COL_LR, r - COL_AQ, SRC_LR))))
    return pl.multiple_of(src, 8)


def _wpack_kernel(w_ref, o_ref):
    r = pl.program_id(0) * WPACK_ROWS
    nvalid = jnp.where(r < COL_LR, WPACK_ROWS, jnp.where(r == COL_LR, GLA_LOWRANK, 0))
    row = lax.broadcasted_iota(jnp.int32, w_ref.shape, 0)
    o_ref[...] = jnp.where(row < nvalid, w_ref[...], 0.0).astype(BF16)


def _wpack(w_in_t):
    _, D = w_in_t.shape
    return pl.pallas_call(
        _wpack_kernel,
        grid=(PROJ_W // WPACK_ROWS,),
        in_specs=[pl.BlockSpec((pl.Element(WPACK_ROWS), pl.Element(D)), lambda b: (_wpack_src_row(b), 0))],
        out_specs=pl.BlockSpec((WPACK_ROWS, D), lambda b: (b, 0)),
        out_shape=jax.ShapeDtypeStruct((PROJ_W, D), BF16),
        compiler_params=_params("parallel"),
        name="wpack",
    )(w_in_t)


def _norm_modulate(x, mod_ref, gain):
    return (_rms(x, gain) * (1.0 + mod_ref[1:2, :]) + mod_ref[0:1, :]).astype(BF16)


def _hnorm_kernel(x_ref, mod_ref, g_ref, o_ref):
    o_ref[...] = _norm_modulate(x_ref[...], mod_ref, g_ref[...])


def _hnorm_first(x2, mod3, gain):
    D = x2.shape[1]
    nr = INPROJ_NORM_ROWS
    return pl.pallas_call(
        _hnorm_kernel,
        grid=(INPROJ_TM // nr,),
        in_specs=[pl.BlockSpec((nr, D), lambda r: (r, 0)),
                  pl.BlockSpec((None, 6, D), lambda r: (0, 0, 0)),
                  pl.BlockSpec((1, D), lambda r: (0, 0))],
        out_specs=pl.BlockSpec((nr, D), lambda r: (r, 0)),
        out_shape=jax.ShapeDtypeStruct((INPROJ_TM, D), BF16),
        compiler_params=_params("parallel"),
        name="hnorm_first",
    )(x2, mod3, gain)


def _cast_block(shape, steps):
    rows, cols = shape
    bc = min(cols, 1024)
    assert cols % bc == 0
    for br in range(16, rows + 1, 16):
        if rows % br == 0 and (rows // br) * (cols // bc) <= steps:
            return br, bc
    raise ValueError(f"no cast block for {shape} in {steps} steps")


def _cast_specs(weights, n0, n1):
    def spec(shape):
        br, bc = _cast_block(shape, n0 * n1)
        ncb = shape[1] // bc
        last = (shape[0] // br) * ncb - 1

        def index(i, j):
            t = jnp.minimum(i * n1 + j, last)
            return t // ncb, t % ncb

        return pl.BlockSpec((br, bc), index)

    return [spec(w.shape) for w in weights]


def _inproj_kernel(xn_ref, modn_ref, g_ref, w_ref, h0_ref, o_ref, ha_ref, hb_ref):
    i, j = pl.program_id(0), pl.program_id(1)
    nr = INPROJ_NORM_ROWS

    @pl.when((i == 0) & (j == 0))
    def _():
        ha_ref[...] = h0_ref[...]

    def step(cur_ref, nxt_ref):
        o_ref[...] = _dot_nt(cur_ref[...], w_ref[...]).astype(BF16)
        chunk = jnp.minimum(j, INPROJ_TM // nr - 1)
        rows = pl.ds(pl.multiple_of(chunk * nr, nr), nr)
        nxt_ref[rows, :] = _norm_modulate(xn_ref[...], modn_ref, g_ref[...])

    @pl.when(i % 2 == 0)
    def _():
        step(ha_ref, hb_ref)

    @pl.when(i % 2 == 1)
    def _():
        step(hb_ref, ha_ref)


def _inproj(x2, mod3, gain, w):
    T, D = x2.shape
    S = T // mod3.shape[0]
    tm, tn, nr = INPROJ_TM, INPROJ_TN, INPROJ_NORM_ROWS
    n_i, chunks = T // tm, tm // nr
    assert PROJ_W // tn >= chunks

    def next_tile(i):
        return jnp.minimum(i + 1, n_i - 1)

    return pl.pallas_call(
        _inproj_kernel,
        grid=(n_i, PROJ_W // tn),
        in_specs=[pl.BlockSpec((nr, D), lambda i, j: (next_tile(i) * chunks + jnp.minimum(j, chunks - 1), 0)),
                  pl.BlockSpec((None, 6, D), lambda i, j: (next_tile(i) // (S // tm), 0, 0)),
                  pl.BlockSpec((1, D), lambda i, j: (0, 0)),
                  pl.BlockSpec((tn, D), lambda i, j: (j, 0)),
                  pl.BlockSpec((tm, D), lambda i, j: (0, 0), pipeline_mode=pl.Buffered(1))],
        out_specs=pl.BlockSpec((tm, tn), lambda i, j: (i, j)),
        out_shape=jax.ShapeDtypeStruct((T, PROJ_W), BF16),
        scratch_shapes=[pltpu.VMEM((tm, D), BF16), pltpu.VMEM((tm, D), BF16)],
        compiler_params=_params("arbitrary", "arbitrary"),
        name="inproj",
    )(x2, mod3, gain, w, _hnorm_first(x2, mod3, gain))


def _attn_kernel(q_ref, kc_ref, kp_ref, vc_ref, vp_ref, b0_ref, b1_ref, qg_ref, kg_ref, sink_ref, seg_ref,
                 *rest, n_cast):
    cast_in, o_ref, cast_out = rest[:n_cast], rest[n_cast], rest[n_cast + 1:]
    for src_ref, dst_ref in zip(cast_in, cast_out):
        dst_ref[...] = src_ref[...].astype(BF16)

    nsub = ATTN_TQ // ATTN_BLOCK
    blk, dh = ATTN_BLOCK, HEAD_DIM
    seg = seg_ref[...]

    def normed(x, gain):
        ssq = _dot((x * x).astype(BF16), seg)
        return x * lax.rsqrt(ssq * (1.0 / dh) + EPS) * gain

    kn = normed(jnp.concatenate([kp_ref[...], kc_ref[...]], axis=0).astype(F32), kg_ref[...])
    vt = jnp.concatenate([vp_ref[...], vc_ref[...]], axis=0).astype(F32).T.astype(BF16)
    lane = lax.broadcasted_iota(jnp.int32, (kn.shape[0], 2 * dh), 1)
    qgain = qg_ref[...] * (dh ** -0.5 * LOG2E)

    kzs, qns = [], []
    for g in range(N_KV_HEADS):
        pair = kn[:, (g // 2) * 2 * dh:(g // 2 + 1) * 2 * dh]
        own = jnp.where(lane >= dh if g % 2 else lane < dh, pair, 0.0)
        swapped = pltpu.roll(own, dh, axis=1)
        kz = [own, swapped] if g % 2 == 0 else [swapped, own]
        kzs.append([z.astype(BF16) for z in kz])
        qns.append(normed(q_ref[:, g * 4 * dh:(g + 1) * 4 * dh].astype(F32), qgain).astype(BF16))

    def scores(g, sb):
        bref = b0_ref if sb == 0 else b1_ref
        band = slice(sb * blk, sb * blk + 2 * blk)
        rows = slice(sb * blk, (sb + 1) * blk)
        kband = jnp.concatenate([kzs[g][0][band], kzs[g][1][band]], axis=0)
        qr = jnp.concatenate([qns[g][rows, 0:2 * dh], qns[g][rows, 2 * dh:4 * dh]], axis=0)
        return _dot_nt(kband, qr) + bref[g]

    def finish(g, sb, s):
        band = slice(sb * blk, sb * blk + 2 * blk)
        rows = slice(sb * blk, (sb + 1) * blk)
        ps, inv = [], []
        for e in range(2):
            se = s[e * 2 * blk:(e + 1) * 2 * blk]
            sink = sink_ref[g, e] * LOG2E
            m = jnp.maximum(jnp.max(se, axis=0, keepdims=True), sink)
            p = jnp.exp2(se - m)
            inv.append(1.0 / (jnp.sum(p, axis=0, keepdims=True) + jnp.exp2(sink - m)))
            ps.append(p.astype(BF16))
        ot = _dot(vt[g * dh:(g + 1) * dh, band], jnp.concatenate(ps, axis=1))
        ot = ot * jnp.concatenate(inv, axis=1)
        o = jnp.concatenate([ot[:, 0:2 * blk], ot[:, 2 * blk:4 * blk]], axis=0).T
        o_ref[rows, (2 * g) * 2 * dh:(2 * g + 1) * 2 * dh] = o[0:blk].astype(BF16)
        o_ref[rows, (2 * g + 1) * 2 * dh:(2 * g + 2) * 2 * dh] = o[blk:2 * blk].astype(BF16)

    units = [(g, sb) for g in range(N_KV_HEADS) for sb in range(nsub)]
    s_next = scores(*units[0])
    for n, unit in enumerate(units):
        s_cur = s_next
        if n + 1 < len(units):
            s_next = scores(*units[n + 1])
        finish(*unit, s_cur)


def _attn(proj3, biasm, q_gain, k_gain, sinks, cast_weights):
    B, S, _ = proj3.shape
    tq, blk = ATTN_TQ, ATTN_BLOCK
    nsub = tq // blk
    kvw = ATTN_KV_W
    seg_id = np.arange(kvw) // HEAD_DIM
    seg = jnp.asarray(seg_id[:, None] == seg_id[None, :], dtype=BF16)
    sink_rows = jnp.repeat(sinks.reshape(N_KV_HEADS, 2, 2).transpose(0, 2, 1), blk, axis=-1)
    sink_rows = sink_rows.reshape(N_KV_HEADS, 2, 1, 2 * blk)

    def prev(i):
        return jnp.maximum(i * nsub - 1, 0)

    bias_block = (None,) + biasm.shape[1:]
    cast_specs = _cast_specs(cast_weights, B, S // tq)
    outs = pl.pallas_call(
        functools.partial(_attn_kernel, n_cast=len(cast_weights)),
        grid=(B, S // tq),
        in_specs=[pl.BlockSpec((None, tq, ATTN_Q_W), lambda b, i: (b, i, COL_AQ // ATTN_Q_W)),
                  pl.BlockSpec((None, tq, kvw), lambda b, i: (b, i, COL_AK // kvw)),
                  pl.BlockSpec((None, blk, kvw), lambda b, i: (b, prev(i), COL_AK // kvw)),
                  pl.BlockSpec((None, tq, kvw), lambda b, i: (b, i, COL_AV // kvw)),
                  pl.BlockSpec((None, blk, kvw), lambda b, i: (b, prev(i), COL_AV // kvw)),
                  pl.BlockSpec(bias_block, lambda b, i: (jnp.minimum(i, 1), 0, 0, 0)),
                  pl.BlockSpec(bias_block, lambda b, i: (1, 0, 0, 0)),
                  pl.BlockSpec((1, kvw), lambda b, i: (0, 0)),
                  pl.BlockSpec((1, kvw), lambda b, i: (0, 0)),
                  pl.BlockSpec(sink_rows.shape, lambda b, i: (0, 0, 0, 0)),
                  pl.BlockSpec(seg.shape, lambda b, i: (0, 0))] + cast_specs,
        out_specs=[pl.BlockSpec((None, tq, ATTN_Q_W), lambda b, i: (b, i, 0))] + cast_specs,
        out_shape=[jax.ShapeDtypeStruct((B, S, ATTN_Q_W), BF16)]
        + [jax.ShapeDtypeStruct(cw.shape, BF16) for cw in cast_weights],
        compiler_params=_params("arbitrary", "arbitrary"),
        name="attn",
    )(proj3, proj3, proj3, proj3, proj3, biasm, biasm, jnp.tile(q_gain, (1, GQA_GROUP)),
      jnp.tile(k_gain, (1, N_KV_HEADS)), sink_rows, seg, *cast_weights)
    return outs[0], outs[1:]


def _split2(x):
    hi = x.astype(BF16)
    return hi, (x - hi.astype(F32)).astype(BF16)


def _gla_head(q, k, v, r, lr, wgk, bgk, gain, state_ref, masks, store):
    C = GLA_CHUNK
    tril, m_same, m_next, m_far, eye = masks
    z = _dot(lr, wgk) + bgk
    yield
    gl = (jnp.minimum(z, 0.0) - jnp.log(1.0 + jnp.exp(-jnp.abs(z)))) * (1.0 / GLA_NORMALIZER)
    hi, lo = _split2(gl)
    g = _dot(tril, hi) + _dot(tril, lo)
    yield
    t = [g[(c + 1) * C - 1:(c + 1) * C] for c in range(4)]
    t_rows = jnp.concatenate([jnp.broadcast_to(tc, (C, tc.shape[1])) for tc in t], axis=0)

    q_dec = q.astype(F32) * (GLA_DK ** -0.5) * jnp.exp(g)
    kf = k.astype(F32)
    k_inv = (kf * jnp.exp(-g)).astype(BF16)
    k_end = kf * jnp.exp(t_rows - g)
    qd = [q_dec[c * C:(c + 1) * C] for c in range(4)]
    ke = [k_end[c * C:(c + 1) * C] for c in range(4)]

    def rows(parts):
        return jnp.concatenate(parts, axis=0).astype(BF16)

    q_b, k_b = q_dec.astype(BF16), k_end.astype(BF16)
    q_far = rows([qd[0], qd[1], qd[2], qd[3] * jnp.exp(t[2])])
    k_far = rows([ke[0] * jnp.exp(t[1]), ke[1], ke[2], ke[3]])
    q_abs = rows([qd[0], qd[1] * jnp.exp(t[0]), qd[2] * jnp.exp(t[0] + t[1]), qd[3] * jnp.exp(t[0] + t[1] + t[2])])
    k_abs = rows([ke[0] * jnp.exp(t[1] + t[2] + t[3]), ke[1] * jnp.exp(t[2] + t[3]), ke[2] * jnp.exp(t[3]), ke[3]])

    a_same, a_next, a_far = _dot_nt(q_b, k_inv), _dot_nt(q_b, k_b), _dot_nt(q_far, k_far)
    state = state_ref[...]
    o_state = _dot(q_abs, state.astype(BF16))
    update = _dot_tn(k_abs, v)
    yield
    a = jnp.where(m_same, a_same, jnp.where(m_next, a_next, jnp.where(m_far, a_far, 0.0)))
    o = _dot(a.astype(BF16), v) + o_state
    yield

    decay = jnp.exp(t[0] + t[1] + t[2] + t[3])
    decay_col = jnp.sum(jnp.where(eye, jnp.broadcast_to(decay, eye.shape), 0.0), axis=1, keepdims=True)
    state_ref[...] = decay_col * state + update

    rf = r.astype(F32)
    store((_rms(o, gain) * (rf * jax.nn.sigmoid(rf))).astype(BF16))


def _gla_kernel(q_ref, k_ref, v_ref, r_ref, lr_ref, wgk_ref, bgk_ref, gain_ref, o_ref, state_ref):
    R, C, dk, dv = GLA_ROWS, GLA_CHUNK, GLA_DK, GLA_DV

    @pl.when(pl.program_id(2) == 0)
    def _():
        state_ref[...] = jnp.zeros_like(state_ref)

    row = lax.broadcasted_iota(jnp.int32, (R, R), 0)
    col = lax.broadcasted_iota(jnp.int32, (R, R), 1)
    ci, cj = row // C, col // C
    m_same = (row >= col) & (ci == cj)
    m_next = (ci == cj + 1) & (ci != 2)
    m_far = (ci >= 2) & (cj <= 1)
    eye = lax.broadcasted_iota(jnp.int32, (dk, dk), 0) == lax.broadcasted_iota(jnp.int32, (dk, dk), 1)
    masks = (m_same.astype(BF16), m_same, m_next, m_far, eye)

    lr = lr_ref[...]
    heads = []
    for h in range(GLA_HEADS_PER_STEP):
        ks, vs = slice(h * dk, (h + 1) * dk), slice(h * dv, (h + 1) * dv)
        heads.append(_gla_head(q_ref[:, ks], k_ref[:, ks], v_ref[:, vs], r_ref[:, vs], lr, wgk_ref[:, ks],
                               bgk_ref[:, ks], gain_ref[...], state_ref.at[h], masks,
                               functools.partial(o_ref.__setitem__, (slice(None), vs))))
    for _ in itertools.zip_longest(*heads):
        pass


def _gla(proj3, wgk, bgk, gain):
    B, S, _ = proj3.shape
    R, hps = GLA_ROWS, GLA_HEADS_PER_STEP
    dk, dv = hps * GLA_DK, hps * GLA_DV
    return pl.pallas_call(
        _gla_kernel,
        grid=(B, GLA_HEADS // hps, S // R),
        in_specs=[pl.BlockSpec((None, R, dk), lambda b, h, t: (b, t, COL_GQ // dk + h)),
                  pl.BlockSpec((None, R, dk), lambda b, h, t: (b, t, COL_GK // dk + h)),
                  pl.BlockSpec((None, R, dv), lambda b, h, t: (b, t, COL_GV // dv + h)),
                  pl.BlockSpec((None, R, dv), lambda b, h, t: (b, t, COL_GR // dv + h)),
                  pl.BlockSpec((None, R, LR_PAD), lambda b, h, t: (b, t, COL_LR // LR_PAD)),
                  pl.BlockSpec((LR_PAD, dk), lambda b, h, t: (0, h)),
                  pl.BlockSpec((1, dk), lambda b, h, t: (0, h)),
                  pl.BlockSpec((1, GLA_DV), lambda b, h, t: (0, 0))],
        out_specs=pl.BlockSpec((None, R, dv), lambda b, h, t: (b, t, h)),
        out_shape=jax.ShapeDtypeStruct((B, S, GLA_V_W), BF16),
        scratch_shapes=[pltpu.VMEM((hps, GLA_DK, GLA_DV), F32)],
        compiler_params=_params("parallel", "parallel", "arbitrary"),
        name="gla",
    )(proj3, proj3, proj3, proj3, proj3, wgk, bgk, gain)


def _merge_kernel(ya_ref, yg_ref, ga_ref, gb_ref, x_ref, mod_ref, g2_ref, wa_ref, wg_ref, wo_ref,
                  x1_ref, h2_ref):
    def merged_branches(rows):
        ma, mg = _dot(ya_ref[rows, :], wa_ref[...]), _dot(yg_ref[rows, :], wg_ref[...])
        ga = jax.nn.sigmoid(ga_ref[rows, :].astype(F32))
        gb = jax.nn.sigmoid(gb_ref[rows, :].astype(F32))
        return (ga * ma + gb * mg).astype(BF16)

    def project(rows, merged):
        x1 = x_ref[rows, :] + mod_ref[2:3, :] * _dot(merged, wo_ref[...])
        x1_ref[rows, :] = x1
        h2_ref[rows, :] = (_rms(x1, g2_ref[...]) * (1.0 + mod_ref[4:5, :]) + mod_ref[3:4, :]).astype(BF16)

    for r0 in range(0, MERGE_TM, MERGE_SUB):
        rows = slice(r0, r0 + MERGE_SUB)
        project(rows, merged_branches(rows))


def _merge(ya, yg, proj, x2, mod3, gain2, wa, wg, wo):
    T, D = x2.shape
    S = T // mod3.shape[0]
    tm = MERGE_TM
    once = pl.Buffered(1)
    return pl.pallas_call(
        _merge_kernel,
        grid=(T // tm,),
        in_specs=[pl.BlockSpec((tm, ATTN_Q_W), lambda i: (i, 0)),
                  pl.BlockSpec((tm, GLA_V_W), lambda i: (i, 0)),
                  pl.BlockSpec((tm, D), lambda i: (i, COL_GA // D)),
                  pl.BlockSpec((tm, D), lambda i: (i, COL_GB // D)),
                  pl.BlockSpec((tm, D), lambda i: (i, 0)),
                  pl.BlockSpec((None, 6, D), lambda i: (i // (S // tm), 0, 0)),
                  pl.BlockSpec((1, D), lambda i: (0, 0)),
                  pl.BlockSpec((ATTN_Q_W, D), lambda i: (0, 0), pipeline_mode=once),
                  pl.BlockSpec((GLA_V_W, D), lambda i: (0, 0), pipeline_mode=once),
                  pl.BlockSpec((D, D), lambda i: (0, 0), pipeline_mode=once)],
        out_specs=[pl.BlockSpec((tm, D), lambda i: (i, 0)),
                   pl.BlockSpec((tm, D), lambda i: (i, 0))],
        out_shape=[jax.ShapeDtypeStruct((T, D), F32), jax.ShapeDtypeStruct((T, D), BF16)],
        compiler_params=_params("parallel"),
        name="merge",
    )(ya, yg, proj, proj, x2, mod3, gain2, wa, wg, wo)


def _ffn_up_kernel(h_ref, halo_ref, wa_ref, wb_ref, cwa_ref, cwb_ref, cba_ref, cbb_ref, o_ref,
                   hs_ref, ua_ref, ub_ref, *, tiles_per_seq):
    tm, halo = FFN_TM, FFN_HALO

    @pl.when(pl.program_id(1) == 0)
    def _():
        first = (pl.program_id(0) % tiles_per_seq) == 0
        hs_ref[0:halo, :] = jnp.where(first, jnp.zeros_like(halo_ref), halo_ref[...])
        hs_ref[halo:, :] = h_ref[...]

    def conv(u_ref, cw_ref, cb_ref, r0, n):
        u = u_ref[halo + r0 - 8:halo + r0 + n, :]
        y = cb_ref[...] + cw_ref[0:1, :] * pltpu.roll(u, 2, axis=0)[8:]
        y = y + cw_ref[1:2, :] * pltpu.roll(u, 1, axis=0)[8:]
        return y + cw_ref[2:3, :] * u[8:]

    ua_ref[...] = _dot(hs_ref[...], wa_ref[...])
    ub_ref[...] = _dot(hs_ref[...], wb_ref[...])
    for r0 in range(0, tm, FFN_EPI_ROWS):
        ya = conv(ua_ref, cwa_ref, cba_ref, r0, FFN_EPI_ROWS)
        yb = conv(ub_ref, cwb_ref, cbb_ref, r0, FFN_EPI_ROWS)
        o_ref[r0:r0 + FFN_EPI_ROWS, :] = (ya * jax.nn.sigmoid(ya) * yb).astype(BF16)


def _ffn_up(h2, S, w_up, conv_w, conv_b):
    T, D = h2.shape
    tm, tn, halo = FFN_TM, FFN_TN, FFN_HALO
    nj = D_FF // tn
    return pl.pallas_call(
        functools.partial(_ffn_up_kernel, tiles_per_seq=S // tm),
        grid=(T // tm, D_FF // tn),
        in_specs=[pl.BlockSpec((tm, D), lambda i, j: (i, 0)),
                  pl.BlockSpec((halo, D), lambda i, j: (jnp.maximum(i * (tm // halo) - 1, 0), 0)),
                  pl.BlockSpec((D, tn), lambda i, j: (0, j)),
                  pl.BlockSpec((D, tn), lambda i, j: (0, j + nj)),
                  pl.BlockSpec((3, tn), lambda i, j: (0, j)),
                  pl.BlockSpec((3, tn), lambda i, j: (0, j + nj)),
                  pl.BlockSpec((1, tn), lambda i, j: (0, j)),
                  pl.BlockSpec((1, tn), lambda i, j: (0, j + nj))],
        out_specs=pl.BlockSpec((tm, tn), lambda i, j: (i, j)),
        out_shape=jax.ShapeDtypeStruct((T, D_FF), BF16),
        scratch_shapes=[pltpu.VMEM((tm + halo, D), BF16),
                        pltpu.VMEM((tm + halo, tn), F32),
                        pltpu.VMEM((tm + halo, tn), F32)],
        compiler_params=_params("parallel", "arbitrary"),
        name="ffn_up",
    )(h2, h2, w_up, w_up, conv_w, conv_w, conv_b, conv_b)


def _ffn_down_kernel(a_ref, w_ref, x_ref, mod_ref, o_ref):
    o_ref[...] = x_ref[...] + mod_ref[5:6, :] * _dot(a_ref[...], w_ref[...])


def _ffn_down(act, wd, x1, mod3):
    T, D = x1.shape
    S = T // mod3.shape[0]
    tm = DOWN_TM
    return pl.pallas_call(
        _ffn_down_kernel,
        grid=(T // tm,),
        in_specs=[pl.BlockSpec((tm, D_FF), lambda i: (i, 0)),
                  pl.BlockSpec((D_FF, D), lambda i: (0, 0), pipeline_mode=pl.Buffered(1)),
                  pl.BlockSpec((tm, D), lambda i: (i, 0)),
                  pl.BlockSpec((None, 6, D), lambda i: (i // (S // tm), 0, 0))],
        out_specs=pl.BlockSpec((tm, D), lambda i: (i, 0)),
        out_shape=jax.ShapeDtypeStruct((T, D), F32),
        compiler_params=_params("parallel"),
        name="ffn_down",
    )(act, wd, x1, mod3)


def _layer(x2, B, mod3, biasm, norm1_gain, w_in, q_norm_gain, k_norm_gain, attn_sinks, w_gk_up, b_gk,
           gla_norm_gain, w_branch_attn, w_branch_gla, w_out, norm2_gain, w_ffn_up, ffn_conv_w,
           ffn_conv_b, w_ffn_down):
    T, D = x2.shape
    S = T // B
    w_p = _wpack(w_in.T)
    wgk = jnp.concatenate([w_gk_up, jnp.zeros((LR_PAD - GLA_LOWRANK, GLA_K_W), w_gk_up.dtype)],
                          axis=0).astype(BF16)

    proj = _inproj(x2, mod3, norm1_gain.reshape(1, D), w_p)
    proj3 = proj.reshape(B, S, PROJ_W)
    ya, (w_ba, w_bg, w_o, w_up, w_down) = _attn(
        proj3, biasm, q_norm_gain.reshape(1, HEAD_DIM), k_norm_gain.reshape(1, HEAD_DIM), attn_sinks,
        (w_branch_attn, w_branch_gla, w_out, w_ffn_up, w_ffn_down))
    yg = _gla(proj3, wgk, b_gk.reshape(1, GLA_K_W), gla_norm_gain.reshape(1, GLA_DV))
    x1, h2 = _merge(ya.reshape(T, ATTN_Q_W), yg.reshape(T, GLA_V_W), proj, x2, mod3,
                    norm2_gain.reshape(1, D), w_ba, w_bg, w_o)
    act = _ffn_up(h2, S, w_up, ffn_conv_w, ffn_conv_b.reshape(1, 2 * D_FF))
    return _ffn_down(act, w_down, x1, mod3)


def kernel(x, c, rel_bias_table, w_ada, b_ada, norm1_gain, w_in, q_norm_gain, k_norm_gain, attn_sinks,
           w_gk_up, b_gk, gla_norm_gain, w_branch_attn, w_branch_gla, w_out, norm2_gain, w_ffn_up,
           ffn_conv_w, ffn_conv_b, w_ffn_down):
    B, S, D = x.shape
    depth = w_in.shape[0]
    biasm = _relbias(rel_bias_table)
    x2 = x.reshape(B * S, D)
    for l in range(depth):
        mod3 = _adaln(c, w_ada[l], b_ada[l]).reshape(B, 6, D)
        x2 = _layer(x2, B, mod3, biasm, norm1_gain[l], w_in[l], q_norm_gain[l], k_norm_gain[l],
                    attn_sinks[l], w_gk_up[l], b_gk[l], gla_norm_gain[l], w_branch_attn[l],
                    w_branch_gla[l], w_out[l], norm2_gain[l], w_ffn_up[l], ffn_conv_w[l], ffn_conv_b[l],
                    w_ffn_down[l])
    return x2.reshape(B, S, D)
```

```python
import functools
import itertools
import math

import numpy as np
import jax
import jax.numpy as jnp
from jax import lax
from jax.experimental import pallas as pl
from jax.experimental.pallas import tpu as pltpu

F32 = jnp.float32
BF16 = jnp.bfloat16

D_MODEL = 2048
N_Q_HEADS = 16
N_KV_HEADS = 4
GQA_GROUP = N_Q_HEADS // N_KV_HEADS
HEAD_DIM = 64
WINDOW = 128
ATTN_BLOCK = 128
N_BUCKETS = 32
MAX_DISTANCE = 128
GLA_HEADS = 4
GLA_DK = 256
GLA_DV = 512
GLA_LOWRANK = 16
GLA_NORMALIZER = 16.0
GLA_CHUNK = 64
D_FF = 5632
EPS = 1e-6
NEG_INF = -1e30
LOG2E = math.log2(math.e)

ATTN_Q_W = N_Q_HEADS * HEAD_DIM
ATTN_KV_W = N_KV_HEADS * HEAD_DIM
GLA_K_W = GLA_HEADS * GLA_DK
GLA_V_W = GLA_HEADS * GLA_DV

COL_GA = 0
COL_GB = COL_GA + D_MODEL
COL_GV = COL_GB + D_MODEL
COL_GR = COL_GV + GLA_V_W
COL_GQ = COL_GR + GLA_V_W
COL_GK = COL_GQ + GLA_K_W
COL_AQ = COL_GK + GLA_K_W
COL_AK = COL_AQ + ATTN_Q_W
COL_AV = COL_AK + ATTN_KV_W
COL_LR = COL_AV + ATTN_KV_W
LR_PAD = 128
PROJ_W = 12288

VMEM_LIMIT = 60 * 1024 * 1024

ADALN_TN = 1024
INPROJ_TM, INPROJ_TN = 1024, 2048
INPROJ_NORM_ROWS = 256
ATTN_TQ = 512
GLA_ROWS = 256
GLA_HEADS_PER_STEP = 4
MERGE_TM, MERGE_SUB = 512, 256
FFN_TM, FFN_TN = 1024, 512
FFN_HALO = 16
FFN_EPI_ROWS = 256
DOWN_TM = 512


def _params(*sem):
    return pltpu.CompilerParams(dimension_semantics=sem, vmem_limit_bytes=VMEM_LIMIT)


def _dot(a, b):
    return jnp.dot(a, b, preferred_element_type=F32)


def _dot_nt(a, b):
    return lax.dot_general(a, b, (((1,), (1,)), ((), ())), preferred_element_type=F32)


def _dot_tn(a, b):
    return lax.dot_general(a, b, (((0,), (0,)), ((), ())), preferred_element_type=F32)


def _rms(x, gain):
    return x * lax.rsqrt(jnp.mean(x * x, axis=-1, keepdims=True) + EPS) * gain


def _adaln_kernel(c_ref, w_ref, b_ref, o_ref):
    c = c_ref[...]
    ca = c * jax.nn.sigmoid(c)
    o_ref[...] = _dot(ca.astype(BF16), w_ref[...].astype(BF16)) + b_ref[...]


def _adaln(c, w_ada, b_ada):
    B, D = c.shape
    N = w_ada.shape[1]
    return pl.pallas_call(
        _adaln_kernel,
        grid=(N // ADALN_TN,),
        in_specs=[pl.BlockSpec((B, D), lambda j: (0, 0)),
                  pl.BlockSpec((D, ADALN_TN), lambda j: (0, j)),
                  pl.BlockSpec((1, ADALN_TN), lambda j: (0, j))],
        out_specs=pl.BlockSpec((B, ADALN_TN), lambda j: (0, j)),
        out_shape=jax.ShapeDtypeStruct((B, N), F32),
        compiler_params=_params("parallel"),
        name="adaln",
    )(c, w_ada, b_ada.reshape(1, N))


def _bucket_table():
    j = np.arange(2 * ATTN_BLOCK)[:, None]
    i = np.arange(ATTN_BLOCK)[None, :]
    dist = i + ATTN_BLOCK - j
    max_exact = N_BUCKETS // 2
    d = np.maximum(dist, 0)
    ratio = np.log(np.maximum(d, 1).astype(np.float32) / np.float32(max_exact)) / np.float32(
        math.log(MAX_DISTANCE / max_exact))
    large = max_exact + (ratio.astype(np.float32) * np.float32(N_BUCKETS - max_exact)).astype(np.int32)
    large = np.minimum(large, N_BUCKETS - 1)
    bucket = np.where(d < max_exact, d, large)
    in_window = (dist >= 0) & (dist < WINDOW)
    return np.where(in_window, bucket, -1).astype(np.int32)


def _relbias_kernel(tab_ref, bkt_ref, o_ref):
    h = pl.program_id(0)
    bkt = bkt_ref[...]
    acc = jnp.zeros(bkt.shape, F32)
    for b in range(N_BUCKETS):
        acc = jnp.where(bkt == b, tab_ref[b, h], acc)
    regular = jnp.where(bkt >= 0, acc * LOG2E, NEG_INF)
    key = lax.broadcasted_iota(jnp.int32, bkt.shape, 0)
    o_ref[0] = jnp.where(key >= ATTN_BLOCK, regular, NEG_INF)
    o_ref[1] = regular


def _relbias(rel_bias_table):
    bkt = jnp.asarray(_bucket_table())
    nk, nq = bkt.shape
    return pl.pallas_call(
        _relbias_kernel,
        grid=(N_Q_HEADS,),
        in_specs=[pl.BlockSpec(memory_space=pltpu.SMEM),
                  pl.BlockSpec(bkt.shape, lambda h: (0, 0))],
        out_specs=pl.BlockSpec((2, None, nk, nq), lambda h: (0, h // GQA_GROUP, h % 2, (h // 2) % 2)),
        out_shape=jax.ShapeDtypeStruct((2, N_KV_HEADS, 2 * nk, 2 * nq), F32),
        compiler_params=_params("parallel"),
        name="relbias",
    )(rel_bias_table, bkt)


SRC_GQ = ATTN_Q_W + 2 * ATTN_KV_W
SRC_GV = SRC_GQ + 2 * GLA_K_W
SRC_LR = SRC_GV + 2 * GLA_V_W
SRC_GA = SRC_LR + GLA_LOWRANK
WPACK_ROWS = 512


def _wpack_src_row(b):
    r = b * WPACK_ROWS
    src = jnp.where(r < COL_GV, SRC_GA + r,
                    jnp.where(r < COL_GQ, SRC_GV + (r - COL_GV),
                              jnp.where(r < COL_AQ, SRC_GQ + (r - COL_GQ),
                                        jnp.where(r < COL_LR, r - COL_AQ, SRC_LR))))
    return pl.multiple_of(src, 8)


def _wpack_kernel(w_ref, o_ref):
    r = pl.program_id(0) * WPACK_ROWS
    nvalid = jnp.where(r < COL_LR, WPACK_ROWS, jnp.where(r == COL_LR, GLA_LOWRANK, 0))
    row = lax.broadcasted_iota(jnp.int32, w_ref.shape, 0)
    o_ref[...] = jnp.where(row < nvalid, w_ref[...], 0.0).astype(BF16)


def _wpack(w_in_t):
    _, D = w_in_t.shape
    return pl.pallas_call(
        _wpack_kernel,
        grid=(PROJ_W // WPACK_ROWS,),
        in_specs=[pl.BlockSpec((pl.Element(WPACK_ROWS), pl.Element(D)), lambda b: (_wpack_src_row(b), 0))],
        out_specs=pl.BlockSpec((WPACK_ROWS, D), lambda b: (b, 0)),
        out_shape=jax.ShapeDtypeStruct((PROJ_W, D), BF16),
        compiler_params=_params("parallel"),
        name="wpack",
    )(w_in_t)


def _norm_modulate(x, mod_ref, gain):
    return (_rms(x, gain) * (1.0 + mod_ref[1:2, :]) + mod_ref[0:1, :]).astype(BF16)


def _hnorm_kernel(x_ref, mod_ref, g_ref, o_ref):
    o_ref[...] = _norm_modulate(x_ref[...], mod_ref, g_ref[...])


def _hnorm_first(x2, mod3, gain):
    D = x2.shape[1]
    nr = INPROJ_NORM_ROWS
    return pl.pallas_call(
        _hnorm_kernel,
        grid=(INPROJ_TM // nr,),
        in_specs=[pl.BlockSpec((nr, D), lambda r: (r, 0)),
                  pl.BlockSpec((None, 6, D), lambda r: (0, 0, 0)),
                  pl.BlockSpec((1, D), lambda r: (0, 0))],
        out_specs=pl.BlockSpec((nr, D), lambda r: (r, 0)),
        out_shape=jax.ShapeDtypeStruct((INPROJ_TM, D), BF16),
        compiler_params=_params("parallel"),
        name="hnorm_first",
    )(x2, mod3, gain)


def _cast_block(shape, steps):
    rows, cols = shape
    bc = min(cols, 1024)
    assert cols % bc == 0
    for br in range(16, rows + 1, 16):
        if rows % br == 0 and (rows // br) * (cols // bc) <= steps:
            return br, bc
    raise ValueError(f"no cast block for {shape} in {steps} steps")


def _cast_specs(weights, n0, n1):
    def spec(shape):
        br, bc = _cast_block(shape, n0 * n1)
        ncb = shape[1] // bc
        last = (shape[0] // br) * ncb - 1

        def index(i, j):
            t = jnp.minimum(i * n1 + j, last)
            return t // ncb, t % ncb

        return pl.BlockSpec((br, bc), index)

    return [spec(w.shape) for w in weights]


def _inproj_kernel(xn_ref, modn_ref, g_ref, w_ref, h0_ref, o_ref, ha_ref, hb_ref):
    i, j = pl.program_id(0), pl.program_id(1)
    nr = INPROJ_NORM_ROWS

    @pl.when((i == 0) & (j == 0))
    def _():
        ha_ref[...] = h0_ref[...]

    def step(cur_ref, nxt_ref):
        o_ref[...] = _dot_nt(cur_ref[...], w_ref[...]).astype(BF16)
        chunk = jnp.minimum(j, INPROJ_TM // nr - 1)
        rows = pl.ds(pl.multiple_of(chunk * nr, nr), nr)
        nxt_ref[rows, :] = _norm_modulate(xn_ref[...], modn_ref, g_ref[...])

    @pl.when(i % 2 == 0)
    def _():
        step(ha_ref, hb_ref)

    @pl.when(i % 2 == 1)
    def _():
        step(hb_ref, ha_ref)


def _inproj(x2, mod3, gain, w):
    T, D = x2.shape
    S = T // mod3.shape[0]
    tm, tn, nr = INPROJ_TM, INPROJ_TN, INPROJ_NORM_ROWS
    n_i, chunks = T // tm, tm // nr
    assert PROJ_W // tn >= chunks

    def next_tile(i):
        return jnp.minimum(i + 1, n_i - 1)

    return pl.pallas_call(
        _inproj_kernel,
        grid=(n_i, PROJ_W // tn),
        in_specs=[pl.BlockSpec((nr, D), lambda i, j: (next_tile(i) * chunks + jnp.minimum(j, chunks - 1), 0)),
                  pl.BlockSpec((None, 6, D), lambda i, j: (next_tile(i) // (S // tm), 0, 0)),
                  pl.BlockSpec((1, D), lambda i, j: (0, 0)),
                  pl.BlockSpec((tn, D), lambda i, j: (j, 0)),
                  pl.BlockSpec((tm, D), lambda i, j: (0, 0), pipeline_mode=pl.Buffered(1))],
        out_specs=pl.BlockSpec((tm, tn), lambda i, j: (i, j)),
        out_shape=jax.ShapeDtypeStruct((T, PROJ_W), BF16),
        scratch_shapes=[pltpu.VMEM((tm, D), BF16), pltpu.VMEM((tm, D), BF16)],
        compiler_params=_params("arbitrary", "arbitrary"),
        name="inproj",
    )(x2, mod3, gain, w, _hnorm_first(x2, mod3, gain))


def _attn_kernel(q_ref, kc_ref, kp_ref, vc_ref, vp_ref, b0_ref, b1_ref, qg_ref, kg_ref, sink_ref, seg_ref,
                 *rest, n_cast):
    cast_in, o_ref, cast_out = rest[:n_cast], rest[n_cast], rest[n_cast + 1:]
    for src_ref, dst_ref in zip(cast_in, cast_out):
        dst_ref[...] = src_ref[...].astype(BF16)

    nsub = ATTN_TQ // ATTN_BLOCK
    blk, dh = ATTN_BLOCK, HEAD_DIM
    seg = seg_ref[...]

    def normed(x, gain):
        ssq = _dot((x * x).astype(BF16), seg)
        return x * lax.rsqrt(ssq * (1.0 / dh) + EPS) * gain

    kn = normed(jnp.concatenate([kp_ref[...], kc_ref[...]], axis=0).astype(F32), kg_ref[...])
    vt = jnp.concatenate([vp_ref[...], vc_ref[...]], axis=0).astype(F32).T.astype(BF16)
    lane = lax.broadcasted_iota(jnp.int32, (kn.shape[0], 2 * dh), 1)
    qgain = qg_ref[...] * (dh ** -0.5 * LOG2E)

    kzs, qns = [], []
    for g in range(N_KV_HEADS):
        pair = kn[:, (g // 2) * 2 * dh:(g // 2 + 1) * 2 * dh]
        own = jnp.where(lane >= dh if g % 2 else lane < dh, pair, 0.0)
        swapped = pltpu.roll(own, dh, axis=1)
        kz = [own, swapped] if g % 2 == 0 else [swapped, own]
        kzs.append([z.astype(BF16) for z in kz])
        qns.append(normed(q_ref[:, g * 4 * dh:(g + 1) * 4 * dh].astype(F32), qgain).astype(BF16))

    def scores(g, sb):
        bref = b0_ref if sb == 0 else b1_ref
        band = slice(sb * blk, sb * blk + 2 * blk)
        rows = slice(sb * blk, (sb + 1) * blk)
        kband = jnp.concatenate([kzs[g][0][band], kzs[g][1][band]], axis=0)
        qr = jnp.concatenate([qns[g][rows, 0:2 * dh], qns[g][rows, 2 * dh:4 * dh]], axis=0)
        return _dot_nt(kband, qr) + bref[g]

    def finish(g, sb, s):
        band = slice(sb * blk, sb * blk + 2 * blk)
        rows = slice(sb * blk, (sb + 1) * blk)
        ps, inv = [], []
        for e in range(2):
            se = s[e * 2 * blk:(e + 1) * 2 * blk]
            sink = sink_ref[g, e] * LOG2E
            m = jnp.maximum(jnp.max(se, axis=0, keepdims=True), sink)
            p = jnp.exp2(se - m)
            inv.append(1.0 / (jnp.sum(p, axis=0, keepdims=True) + jnp.exp2(sink - m)))
            ps.append(p.astype(BF16))
        ot = _dot(vt[g * dh:(g + 1) * dh, band], jnp.concatenate(ps, axis=1))
        ot = ot * jnp.concatenate(inv, axis=1)
        o = jnp.concatenate([ot[:, 0:2 * blk], ot[:, 2 * blk:4 * blk]], axis=0).T
        o_ref[rows, (2 * g) * 2 * dh:(2 * g + 1) * 2 * dh] = o[0:blk].astype(BF16)
        o_ref[rows, (2 * g + 1) * 2 * dh:(2 * g + 2) * 2 * dh] = o[blk:2 * blk].astype(BF16)

    units = [(g, sb) for g in range(N_KV_HEADS) for sb in range(nsub)]
    s_next = scores(*units[0])
    for n, unit in enumerate(units):
        s_cur = s_next
        if n + 1 < len(units):
            s_next = scores(*units[n + 1])
        finish(*unit, s_cur)


def _attn(proj3, biasm, q_gain, k_gain, sinks, cast_weights):
    B, S, _ = proj3.shape
    tq, blk = ATTN_TQ, ATTN_BLOCK
    nsub = tq // blk
    kvw = ATTN_KV_W
    seg_id = np.arange(kvw) // HEAD_DIM
    seg = jnp.asarray(seg_id[:, None] == seg_id[None, :], dtype=BF16)
    sink_rows = jnp.repeat(sinks.reshape(N_KV_HEADS, 2, 2).transpose(0, 2, 1), blk, axis=-1)
    sink_rows = sink_rows.reshape(N_KV_HEADS, 2, 1, 2 * blk)

    def prev(i):
        return jnp.maximum(i * nsub - 1, 0)

    bias_block = (None,) + biasm.shape[1:]
    cast_specs = _cast_specs(cast_weights, B, S // tq)
    outs = pl.pallas_call(
        functools.partial(_attn_kernel, n_cast=len(cast_weights)),
        grid=(B, S // tq),
        in_specs=[pl.BlockSpec((None, tq, ATTN_Q_W), lambda b, i: (b, i, COL_AQ // ATTN_Q_W)),
                  pl.BlockSpec((None, tq, kvw), lambda b, i: (b, i, COL_AK // kvw)),
                  pl.BlockSpec((None, blk, kvw), lambda b, i: (b, prev(i), COL_AK // kvw)),
                  pl.BlockSpec((None, tq, kvw), lambda b, i: (b, i, COL_AV // kvw)),
                  pl.BlockSpec((None, blk, kvw), lambda b, i: (b, prev(i), COL_AV // kvw)),
                  pl.BlockSpec(bias_block, lambda b, i: (jnp.minimum(i, 1), 0, 0, 0)),
                  pl.BlockSpec(bias_block, lambda b, i: (1, 0, 0, 0)),
                  pl.BlockSpec((1, kvw), lambda b, i: (0, 0)),
                  pl.BlockSpec((1, kvw), lambda b, i: (0, 0)),
                  pl.BlockSpec(sink_rows.shape, lambda b, i: (0, 0, 0, 0)),
                  pl.BlockSpec(seg.shape, lambda b, i: (0, 0))] + cast_specs,
        out_specs=[pl.BlockSpec((None, tq, ATTN_Q_W), lambda b, i: (b, i, 0))] + cast_specs,
        out_shape=[jax.ShapeDtypeStruct((B, S, ATTN_Q_W), BF16)]
        + [jax.ShapeDtypeStruct(cw.shape, BF16) for cw in cast_weights],
        compiler_params=_params("arbitrary", "arbitrary"),
        name="attn",
    )(proj3, proj3, proj3, proj3, proj3, biasm, biasm, jnp.tile(q_gain, (1, GQA_GROUP)),
      jnp.tile(k_gain, (1, N_KV_HEADS)), sink_rows, seg, *cast_weights)
    return outs[0], outs[1:]


def _split2(x):
    hi = x.astype(BF16)
    return hi, (x - hi.astype(F32)).astype(BF16)


def _gla_head(q, k, v, r, lr, wgk, bgk, gain, state_ref, masks, store):
    C = GLA_CHUNK
    tril, m_same, m_next, m_far, eye = masks
    z = _dot(lr, wgk) + bgk
    yield
    gl = (jnp.minimum(z, 0.0) - jnp.log(1.0 + jnp.exp(-jnp.abs(z)))) * (1.0 / GLA_NORMALIZER)
    hi, lo = _split2(gl)
    g = _dot(tril, hi) + _dot(tril, lo)
    yield
    t = [g[(c + 1) * C - 1:(c + 1) * C] for c in range(4)]
    t_rows = jnp.concatenate([jnp.broadcast_to(tc, (C, tc.shape[1])) for tc in t], axis=0)

    q_dec = q.astype(F32) * (GLA_DK ** -0.5) * jnp.exp(g)
    kf = k.astype(F32)
    k_inv = (kf * jnp.exp(-g)).astype(BF16)
    k_end = kf * jnp.exp(t_rows - g)
    qd = [q_dec[c * C:(c + 1) * C] for c in range(4)]
    ke = [k_end[c * C:(c + 1) * C] for c in range(4)]

    def rows(parts):
        return jnp.concatenate(parts, axis=0).astype(BF16)

    q_b, k_b = q_dec.astype(BF16), k_end.astype(BF16)
    q_far = rows([qd[0], qd[1], qd[2], qd[3] * jnp.exp(t[2])])
    k_far = rows([ke[0] * jnp.exp(t[1]), ke[1], ke[2], ke[3]])
    q_abs = rows([qd[0], qd[1] * jnp.exp(t[0]), qd[2] * jnp.exp(t[0] + t[1]), qd[3] * jnp.exp(t[0] + t[1] + t[2])])
    k_abs = rows([ke[0] * jnp.exp(t[1] + t[2] + t[3]), ke[1] * jnp.exp(t[2] + t[3]), ke[2] * jnp.exp(t[3]), ke[3]])

    a_same, a_next, a_far = _dot_nt(q_b, k_inv), _dot_nt(q_b, k_b), _dot_nt(q_far, k_far)
    state = state_ref[...]
    o_state = _dot(q_abs, state.astype(BF16))
    update = _dot_tn(k_abs, v)
    yield
    a = jnp.where(m_same, a_same, jnp.where(m_next, a_next, jnp.where(m_far, a_far, 0.0)))
    o = _dot(a.astype(BF16), v) + o_state
    yield

    decay = jnp.exp(t[0] + t[1] + t[2] + t[3])
    decay_col = jnp.sum(jnp.where(eye, jnp.broadcast_to(decay, eye.shape), 0.0), axis=1, keepdims=True)
    state_ref[...] = decay_col * state + update

    rf = r.astype(F32)
    store((_rms(o, gain) * (rf * jax.nn.sigmoid(rf))).astype(BF16))


def _gla_kernel(q_ref, k_ref, v_ref, r_ref, lr_ref, wgk_ref, bgk_ref, gain_ref, o_ref, state_ref):
    R, C, dk, dv = GLA_ROWS, GLA_CHUNK, GLA_DK, GLA_DV

    @pl.when(pl.program_id(2) == 0)
    def _():
        state_ref[...] = jnp.zeros_like(state_ref)

    row = lax.broadcasted_iota(jnp.int32, (R, R), 0)
    col = lax.broadcasted_iota(jnp.int32, (R, R), 1)
    ci, cj = row // C, col // C
    m_same = (row >= col) & (ci == cj)
    m_next = (ci == cj + 1) & (ci != 2)
    m_far = (ci >= 2) & (cj <= 1)
    eye = lax.broadcasted_iota(jnp.int32, (dk, dk), 0) == lax.broadcasted_iota(jnp.int32, (dk, dk), 1)
    masks = (m_same.astype(BF16), m_same, m_next, m_far, eye)

    lr = lr_ref[...]
    heads = []
    for h in range(GLA_HEADS_PER_STEP):
        ks, vs = slice(h * dk, (h + 1) * dk), slice(h * dv, (h + 1) * dv)
        heads.append(_gla_head(q_ref[:, ks], k_ref[:, ks], v_ref[:, vs], r_ref[:, vs], lr, wgk_ref[:, ks],
                               bgk_ref[:, ks], gain_ref[...], state_ref.at[h], masks,
                               functools.partial(o_ref.__setitem__, (slice(None), vs))))
    for _ in itertools.zip_longest(*heads):
        pass


def _gla(proj3, wgk, bgk, gain):
    B, S, _ = proj3.shape
    R, hps = GLA_ROWS, GLA_HEADS_PER_STEP
    dk, dv = hps * GLA_DK, hps * GLA_DV
    return pl.pallas_call(
        _gla_kernel,
        grid=(B, GLA_HEADS // hps, S // R),
        in_specs=[pl.BlockSpec((None, R, dk), lambda b, h, t: (b, t, COL_GQ // dk + h)),
                  pl.BlockSpec((None, R, dk), lambda b, h, t: (b, t, COL_GK // dk + h)),
                  pl.BlockSpec((None, R, dv), lambda b, h, t: (b, t, COL_GV // dv + h)),
                  pl.BlockSpec((None, R, dv), lambda b, h, t: (b, t, COL_GR // dv + h)),
                  pl.BlockSpec((None, R, LR_PAD), lambda b, h, t: (b, t, COL_LR // LR_PAD)),
                  pl.BlockSpec((LR_PAD, dk), lambda b, h, t: (0, h)),
                  pl.BlockSpec((1, dk), lambda b, h, t: (0, h)),
                  pl.BlockSpec((1, GLA_DV), lambda b, h, t: (0, 0))],
        out_specs=pl.BlockSpec((None, R, dv), lambda b, h, t: (b, t, h)),
        out_shape=jax.ShapeDtypeStruct((B, S, GLA_V_W), BF16),
        scratch_shapes=[pltpu.VMEM((hps, GLA_DK, GLA_DV), F32)],
        compiler_params=_params("parallel", "parallel", "arbitrary"),
        name="gla",
    )(proj3, proj3, proj3, proj3, proj3, wgk, bgk, gain)


def _merge_kernel(ya_ref, yg_ref, ga_ref, gb_ref, x_ref, mod_ref, g2_ref, wa_ref, wg_ref, wo_ref,
                  x1_ref, h2_ref):
    def merged_branches(rows):
        ma, mg = _dot(ya_ref[rows, :], wa_ref[...]), _dot(yg_ref[rows, :], wg_ref[...])
        ga = jax.nn.sigmoid(ga_ref[rows, :].astype(F32))
        gb = jax.nn.sigmoid(gb_ref[rows, :].astype(F32))
        return (ga * ma + gb * mg).astype(BF16)

    def project(rows, merged):
        x1 = x_ref[rows, :] + mod_ref[2:3, :] * _dot(merged, wo_ref[...])
        x1_ref[rows, :] = x1
        h2_ref[rows, :] = (_rms(x1, g2_ref[...]) * (1.0 + mod_ref[4:5, :]) + mod_ref[3:4, :]).astype(BF16)

    for r0 in range(0, MERGE_TM, MERGE_SUB):
        rows = slice(r0, r0 + MERGE_SUB)
        project(rows, merged_branches(rows))


def _merge(ya, yg, proj, x2, mod3, gain2, wa, wg, wo):
    T, D = x2.shape
    S = T // mod3.shape[0]
    tm = MERGE_TM
    once = pl.Buffered(1)
    return pl.pallas_call(
        _merge_kernel,
        grid=(T // tm,),
        in_specs=[pl.BlockSpec((tm, ATTN_Q_W), lambda i: (i, 0)),
                  pl.BlockSpec((tm, GLA_V_W), lambda i: (i, 0)),
                  pl.BlockSpec((tm, D), lambda i: (i, COL_GA // D)),
                  pl.BlockSpec((tm, D), lambda i: (i, COL_GB // D)),
                  pl.BlockSpec((tm, D), lambda i: (i, 0)),
                  pl.BlockSpec((None, 6, D), lambda i: (i // (S // tm), 0, 0)),
                  pl.BlockSpec((1, D), lambda i: (0, 0)),
                  pl.BlockSpec((ATTN_Q_W, D), lambda i: (0, 0), pipeline_mode=once),
                  pl.BlockSpec((GLA_V_W, D), lambda i: (0, 0), pipeline_mode=once),
                  pl.BlockSpec((D, D), lambda i: (0, 0), pipeline_mode=once)],
        out_specs=[pl.BlockSpec((tm, D), lambda i: (i, 0)),
                   pl.BlockSpec((tm, D), lambda i: (i, 0))],
        out_shape=[jax.ShapeDtypeStruct((T, D), F32), jax.ShapeDtypeStruct((T, D), BF16)],
        compiler_params=_params("parallel"),
        name="merge",
    )(ya, yg, proj, proj, x2, mod3, gain2, wa, wg, wo)


def _ffn_up_kernel(h_ref, halo_ref, wa_ref, wb_ref, cwa_ref, cwb_ref, cba_ref, cbb_ref, o_ref,
                   hs_ref, ua_ref, ub_ref, *, tiles_per_seq):
    tm, halo = FFN_TM, FFN_HALO

    @pl.when(pl.program_id(1) == 0)
    def _():
        first = (pl.program_id(0) % tiles_per_seq) == 0
        hs_ref[0:halo, :] = jnp.where(first, jnp.zeros_like(halo_ref), halo_ref[...])
        hs_ref[halo:, :] = h_ref[...]

    def conv(u_ref, cw_ref, cb_ref, r0, n):
        u = u_ref[halo + r0 - 8:halo + r0 + n, :]
        y = cb_ref[...] + cw_ref[0:1, :] * pltpu.roll(u, 2, axis=0)[8:]
        y = y + cw_ref[1:2, :] * pltpu.roll(u, 1, axis=0)[8:]
        return y + cw_ref[2:3, :] * u[8:]

    ua_ref[...] = _dot(hs_ref[...], wa_ref[...])
    ub_ref[...] = _dot(hs_ref[...], wb_ref[...])
    for r0 in range(0, tm, FFN_EPI_ROWS):
        ya = conv(ua_ref, cwa_ref, cba_ref, r0, FFN_EPI_ROWS)
        yb = conv(ub_ref, cwb_ref, cbb_ref, r0, FFN_EPI_ROWS)
        o_ref[r0:r0 + FFN_EPI_ROWS, :] = (ya * jax.nn.sigmoid(ya) * yb).astype(BF16)


def _ffn_up(h2, S, w_up, conv_w, conv_b):
    T, D = h2.shape
    tm, tn, halo = FFN_TM, FFN_TN, FFN_HALO
    nj = D_FF // tn
    return pl.pallas_call(
        functools.partial(_ffn_up_kernel, tiles_per_seq=S // tm),
        grid=(T // tm, D_FF // tn),
        in_specs=[pl.BlockSpec((tm, D), lambda i, j: (i, 0)),
                  pl.BlockSpec((halo, D), lambda i, j: (jnp.maximum(i * (tm // halo) - 1, 0), 0)),
                  pl.BlockSpec((D, tn), lambda i, j: (0, j)),
                  pl.BlockSpec((D, tn), lambda i, j: (0, j + nj)),
                  pl.BlockSpec((3, tn), lambda i, j: (0, j)),
                  pl.BlockSpec((3, tn), lambda i, j: (0, j + nj)),
                  pl.BlockSpec((1, tn), lambda i, j: (0, j)),
                  pl.BlockSpec((1, tn), lambda i, j: (0, j + nj))],
        out_specs=pl.BlockSpec((tm, tn), lambda i, j: (i, j)),
        out_shape=jax.ShapeDtypeStruct((T, D_FF), BF16),
        scratch_shapes=[pltpu.VMEM((tm + halo, D), BF16),
                        pltpu.VMEM((tm + halo, tn), F32),
                        pltpu.VMEM((tm + halo, tn), F32)],
        compiler_params=_params("parallel", "arbitrary"),
        name="ffn_up",
    )(h2, h2, w_up, w_up, conv_w, conv_w, conv_b, conv_b)


def _ffn_down_kernel(a_ref, w_ref, x_ref, mod_ref, o_ref):
    o_ref[...] = x_ref[...] + mod_ref[5:6, :] * _dot(a_ref[...], w_ref[...])


def _ffn_down(act, wd, x1, mod3):
    T, D = x1.shape
    S = T // mod3.shape[0]
    tm = DOWN_TM
    return pl.pallas_call(
        _ffn_down_kernel,
        grid=(T // tm,),
        in_specs=[pl.BlockSpec((tm, D_FF), lambda i: (i, 0)),
                  pl.BlockSpec((D_FF, D), lambda i: (0, 0), pipeline_mode=pl.Buffered(1)),
                  pl.BlockSpec((tm, D), lambda i: (i, 0)),
                  pl.BlockSpec((None, 6, D), lambda i: (i // (S // tm), 0, 0))],
        out_specs=pl.BlockSpec((tm, D), lambda i: (i, 0)),
        out_shape=jax.ShapeDtypeStruct((T, D), F32),
        compiler_params=_params("parallel"),
        name="ffn_down",
    )(act, wd, x1, mod3)


def _layer(x2, B, mod3, biasm, norm1_gain, w_in, q_norm_gain, k_norm_gain, attn_sinks, w_gk_up, b_gk,
           gla_norm_gain, w_branch_attn, w_branch_gla, w_out, norm2_gain, w_ffn_up, ffn_conv_w,
           ffn_conv_b, w_ffn_down):
    T, D = x2.shape
    S = T // B
    w_p = _wpack(w_in.T)
    wgk = jnp.concatenate([w_gk_up, jnp.zeros((LR_PAD - GLA_LOWRANK, GLA_K_W), w_gk_up.dtype)],
                          axis=0).astype(BF16)

    proj = _inproj(x2, mod3, norm1_gain.reshape(1, D), w_p)
    proj3 = proj.reshape(B, S, PROJ_W)
    ya, (w_ba, w_bg, w_o, w_up, w_down) = _attn(
        proj3, biasm, q_norm_gain.reshape(1, HEAD_DIM), k_norm_gain.reshape(1, HEAD_DIM), attn_sinks,
        (w_branch_attn, w_branch_gla, w_out, w_ffn_up, w_ffn_down))
    yg = _gla(proj3, wgk, b_gk.reshape(1, GLA_K_W), gla_norm_gain.reshape(1, GLA_DV))
    x1, h2 = _merge(ya.reshape(T, ATTN_Q_W), yg.reshape(T, GLA_V_W), proj, x2, mod3,
                    norm2_gain.reshape(1, D), w_ba, w_bg, w_o)
    act = _ffn_up(h2, S, w_up, ffn_conv_w, ffn_conv_b.reshape(1, 2 * D_FF))
    return _ffn_down(act, w_down, x1, mod3)


def kernel(x, c, rel_bias_table, w_ada, b_ada, norm1_gain, w_in, q_norm_gain, k_norm_gain, attn_sinks,
           w_gk_up, b_gk, gla_norm_gain, w_branch_attn, w_branch_gla, w_out, norm2_gain, w_ffn_up,
           ffn_conv_w, ffn_conv_b, w_ffn_down):
    B, S, D = x.shape
    depth = w_in.shape[0]
    biasm = _relbias(rel_bias_table)
    x2 = x.reshape(B * S, D)
    for l in range(depth):
        mod3 = _adaln(c, w_ada[l], b_ada[l]).reshape(B, 6, D)
        x2 = _layer(x2, B, mod3, biasm, norm1_gain[l], w_in[l], q_norm_gain[l], k_norm_gain[l],
                    attn_sinks[l], w_gk_up[l], b_gk[l], gla_norm_gain[l], w_branch_attn[l],
                    w_branch_gla[l], w_out[l], norm2_gain[l], w_ffn_up[l], ffn_conv_w[l], ffn_conv_b[l],
                    w_ffn_down[l])
    return x2.reshape(B, S, D)
```

```python
import functools
import itertools
import math

import numpy as np
import jax
import jax.numpy as jnp
from jax import lax
from jax.experimental import pallas as pl
from jax.experimental.pallas import tpu as pltpu

F32 = jnp.float32
BF16 = jnp.bfloat16

D_MODEL = 2048
N_Q_HEADS = 16
N_KV_HEADS = 4
GQA_GROUP = N_Q_HEADS // N_KV_HEADS
HEAD_DIM = 64
WINDOW = 128
ATTN_BLOCK = 128
N_BUCKETS = 32
MAX_DISTANCE = 128
GLA_HEADS = 4
GLA_DK = 256
GLA_DV = 512
GLA_LOWRANK = 16
GLA_NORMALIZER = 16.0
GLA_CHUNK = 64
D_FF = 5632
EPS = 1e-6
NEG_INF = -1e30
LOG2E = math.log2(math.e)

ATTN_Q_W = N_Q_HEADS * HEAD_DIM
ATTN_KV_W = N_KV_HEADS * HEAD_DIM
GLA_K_W = GLA_HEADS * GLA_DK
GLA_V_W = GLA_HEADS * GLA_DV

COL_GA = 0
COL_GB = COL_GA + D_MODEL
COL_GV = COL_GB + D_MODEL
COL_GR = COL_GV + GLA_V_W
COL_GQ = COL_GR + GLA_V_W
COL_GK = COL_GQ + GLA_K_W
COL_AQ = COL_GK + GLA_K_W
COL_AK = COL_AQ + ATTN_Q_W
COL_AV = COL_AK + ATTN_KV_W
COL_LR = COL_AV + ATTN_KV_W
LANE = 128
BF16_ROWS = 16
VMEM_LIMIT = 60 * 1024 * 1024

LR_PAD = LANE
PROJ_W = 12288

ADALN_TN = 1024
INPROJ_TM, INPROJ_TN = 1024, 2048
INPROJ_NORM_ROWS = 256
ATTN_TQ = 512
GLA_ROWS = 256
GLA_HEADS_PER_STEP = 4
MERGE_TM, MERGE_SUB = 512, 256
FFN_TM, FFN_TN = 1024, 512
FFN_HALO = BF16_ROWS
FFN_EPI_ROWS = 256
DOWN_TM = 512


def _params(*sem):
    return pltpu.CompilerParams(dimension_semantics=sem, vmem_limit_bytes=VMEM_LIMIT)


def _dot(a, b):
    return jnp.dot(a, b, preferred_element_type=F32)


def _dot_nt(a, b):
    return lax.dot_general(a, b, (((1,), (1,)), ((), ())), preferred_element_type=F32)


def _dot_tn(a, b):
    return lax.dot_general(a, b, (((0,), (0,)), ((), ())), preferred_element_type=F32)


def _rms(x, gain):
    return x * lax.rsqrt(jnp.mean(x * x, axis=-1, keepdims=True) + EPS) * gain


def _adaln_kernel(c_ref, w_ref, b_ref, o_ref):
    c = c_ref[...]
    ca = c * jax.nn.sigmoid(c)
    o_ref[...] = _dot(ca.astype(BF16), w_ref[...].astype(BF16)) + b_ref[...]


def _adaln(c, w_ada, b_ada):
    B, D = c.shape
    N = w_ada.shape[1]
    return pl.pallas_call(
        _adaln_kernel,
        grid=(N // ADALN_TN,),
        in_specs=[pl.BlockSpec((B, D), lambda j: (0, 0)),
                  pl.BlockSpec((D, ADALN_TN), lambda j: (0, j)),
                  pl.BlockSpec((1, ADALN_TN), lambda j: (0, j))],
        out_specs=pl.BlockSpec((B, ADALN_TN), lambda j: (0, j)),
        out_shape=jax.ShapeDtypeStruct((B, N), F32),
        compiler_params=_params("parallel"),
        name="adaln",
    )(c, w_ada, b_ada.reshape(1, N))


def _bucket_table():
    j = np.arange(2 * ATTN_BLOCK)[:, None]
    i = np.arange(ATTN_BLOCK)[None, :]
    dist = i + ATTN_BLOCK - j
    max_exact = N_BUCKETS // 2
    d = np.maximum(dist, 0)
    ratio = np.log(np.maximum(d, 1).astype(np.float32) / np.float32(max_exact)) / np.float32(
        math.log(MAX_DISTANCE / max_exact))
    large = max_exact + (ratio.astype(np.float32) * np.float32(N_BUCKETS - max_exact)).astype(np.int32)
    large = np.minimum(large, N_BUCKETS - 1)
    bucket = np.where(d < max_exact, d, large)
    in_window = (dist >= 0) & (dist < WINDOW)
    return np.where(in_window, bucket, -1).astype(np.int32)


def _relbias_kernel(tab_ref, bkt_ref, o_ref):
    g = pl.program_id(0)
    bkt = bkt_ref[...]
    nk, nq = bkt.shape
    key = lax.broadcasted_iota(jnp.int32, bkt.shape, 0)
    for e in range(2):
        for p in range(2):
            h = g * GQA_GROUP + 2 * p + e
            acc = jnp.zeros(bkt.shape, F32)
            for b in range(N_BUCKETS):
                acc = jnp.where(bkt == b, tab_ref[b, h], acc)
            regular = jnp.where(bkt >= 0, acc * LOG2E, NEG_INF)
            tile = (slice(e * nk, (e + 1) * nk), slice(p * nq, (p + 1) * nq))
            o_ref[(0,) + tile] = jnp.where(key >= ATTN_BLOCK, regular, NEG_INF)
            o_ref[(1,) + tile] = regular


def _relbias(rel_bias_table):
    bkt = jnp.asarray(_bucket_table())
    nk, nq = bkt.shape
    return pl.pallas_call(
        _relbias_kernel,
        grid=(N_KV_HEADS,),
        in_specs=[pl.BlockSpec(memory_space=pltpu.SMEM),
                  pl.BlockSpec(bkt.shape, lambda g: (0, 0))],
        out_specs=pl.BlockSpec((2, None, 2 * nk, 2 * nq), lambda g: (0, g, 0, 0)),
        out_shape=jax.ShapeDtypeStruct((2, N_KV_HEADS, 2 * nk, 2 * nq), F32),
        compiler_params=_params("parallel"),
        name="relbias",
    )(rel_bias_table, bkt)


SRC_GQ = ATTN_Q_W + 2 * ATTN_KV_W
SRC_GV = SRC_GQ + 2 * GLA_K_W
SRC_LR = SRC_GV + 2 * GLA_V_W
SRC_GA = SRC_LR + GLA_LOWRANK
WPACK_ROWS = 512


def _wpack_src_row(b):
    r = b * WPACK_ROWS
    src = jnp.where(r < COL_GV, SRC_GA + r,
                    jnp.where(r < COL_GQ, SRC_GV + (r - COL_GV),
                              jnp.where(r < COL_AQ, SRC_GQ + (r - COL_GQ),
                                        jnp.where(r < COL_LR, r - COL_AQ, SRC_LR))))
    return pl.multiple_of(src, 8)


def _wpack_kernel(w_ref, o_ref):
    r = pl.program_id(0) * WPACK_ROWS
    nvalid = jnp.where(r < COL_LR, WPACK_ROWS, jnp.where(r == COL_LR, GLA_LOWRANK, 0))
    row = lax.broadcasted_iota(jnp.int32, w_ref.shape, 0)
    o_ref[...] = jnp.where(row < nvalid, w_ref[...], 0.0).astype(BF16)


def _wpack(w_in_t):
    _, D = w_in_t.shape
    return pl.pallas_call(
        _wpack_kernel,
        grid=(PROJ_W // WPACK_ROWS,),
        in_specs=[pl.BlockSpec((pl.Element(WPACK_ROWS), pl.Element(D)), lambda b: (_wpack_src_row(b), 0))],
        out_specs=pl.BlockSpec((WPACK_ROWS, D), lambda b: (b, 0)),
        out_shape=jax.ShapeDtypeStruct((PROJ_W, D), BF16),
        compiler_params=_params("parallel"),
        name="wpack",
    )(w_in_t)


def _norm_modulate(x, mod_ref, gain):
    return (_rms(x, gain) * (1.0 + mod_ref[1:2, :]) + mod_ref[0:1, :]).astype(BF16)


def _hnorm_kernel(x_ref, mod_ref, g_ref, o_ref):
    o_ref[...] = _norm_modulate(x_ref[...], mod_ref, g_ref[...])


def _hnorm_first(x2, mod3, gain):
    D = x2.shape[1]
    nr = INPROJ_NORM_ROWS
    return pl.pallas_call(
        _hnorm_kernel,
        grid=(INPROJ_TM // nr,),
        in_specs=[pl.BlockSpec((nr, D), lambda r: (r, 0)),
                  pl.BlockSpec((None, 6, D), lambda r: (0, 0, 0)),
                  pl.BlockSpec((1, D), lambda r: (0, 0))],
        out_specs=pl.BlockSpec((nr, D), lambda r: (r, 0)),
        out_shape=jax.ShapeDtypeStruct((INPROJ_TM, D), BF16),
        compiler_params=_params("parallel"),
        name="hnorm_first",
    )(x2, mod3, gain)


def _cast_block(shape, steps):
    rows, cols = shape
    bc = min(cols, 1024)
    assert cols % bc == 0
    for br in range(BF16_ROWS, rows + 1, BF16_ROWS):
        if rows % br == 0 and (rows // br) * (cols // bc) <= steps:
            return br, bc
    raise ValueError(f"no cast block for {shape} in {steps} steps")


def _cast_specs(weights, n0, n1):
    def spec(shape):
        br, bc = _cast_block(shape, n0 * n1)
        ncb = shape[1] // bc
        last = (shape[0] // br) * ncb - 1

        def index(i, j):
            t = jnp.minimum(i * n1 + j, last)
            return t // ncb, t % ncb

        return pl.BlockSpec((br, bc), index)

    return [spec(w.shape) for w in weights]


def _inproj_kernel(xn_ref, modn_ref, g_ref, w_ref, h0_ref, o_ref, ha_ref, hb_ref):
    i, j = pl.program_id(0), pl.program_id(1)
    nr = INPROJ_NORM_ROWS

    @pl.when((i == 0) & (j == 0))
    def _():
        ha_ref[...] = h0_ref[...]

    def step(cur_ref, nxt_ref):
        o_ref[...] = _dot_nt(cur_ref[...], w_ref[...]).astype(BF16)
        chunk = jnp.minimum(j, INPROJ_TM // nr - 1)
        rows = pl.ds(pl.multiple_of(chunk * nr, nr), nr)
        nxt_ref[rows, :] = _norm_modulate(xn_ref[...], modn_ref, g_ref[...])

    @pl.when(i % 2 == 0)
    def _():
        step(ha_ref, hb_ref)

    @pl.when(i % 2 == 1)
    def _():
        step(hb_ref, ha_ref)


def _inproj(x2, mod3, gain, w):
    T, D = x2.shape
    S = T // mod3.shape[0]
    tm, tn, nr = INPROJ_TM, INPROJ_TN, INPROJ_NORM_ROWS
    n_i, chunks = T // tm, tm // nr
    assert PROJ_W // tn >= chunks

    def next_tile(i):
        return jnp.minimum(i + 1, n_i - 1)

    return pl.pallas_call(
        _inproj_kernel,
        grid=(n_i, PROJ_W // tn),
        in_specs=[pl.BlockSpec((nr, D), lambda i, j: (next_tile(i) * chunks + jnp.minimum(j, chunks - 1), 0)),
                  pl.BlockSpec((None, 6, D), lambda i, j: (next_tile(i) // (S // tm), 0, 0)),
                  pl.BlockSpec((1, D), lambda i, j: (0, 0)),
                  pl.BlockSpec((tn, D), lambda i, j: (j, 0)),
                  pl.BlockSpec((tm, D), lambda i, j: (0, 0), pipeline_mode=pl.Buffered(1))],
        out_specs=pl.BlockSpec((tm, tn), lambda i, j: (i, j)),
        out_shape=jax.ShapeDtypeStruct((T, PROJ_W), BF16),
        scratch_shapes=[pltpu.VMEM((tm, D), BF16), pltpu.VMEM((tm, D), BF16)],
        compiler_params=_params("arbitrary", "arbitrary"),
        name="inproj",
    )(x2, mod3, gain, w, _hnorm_first(x2, mod3, gain))


def _attn_kernel(q_ref, kc_ref, kp_ref, vc_ref, vp_ref, b0_ref, b1_ref, qg_ref, kg_ref, sink_ref, seg_ref,
                 *rest, n_cast):
    cast_in, o_ref, cast_out = rest[:n_cast], rest[n_cast], rest[n_cast + 1:]
    for src_ref, dst_ref in zip(cast_in, cast_out):
        dst_ref[...] = src_ref[...].astype(BF16)

    nsub = ATTN_TQ // ATTN_BLOCK
    blk, dh = ATTN_BLOCK, HEAD_DIM
    seg = seg_ref[...]

    def normed(x, gain):
        ssq = _dot((x * x).astype(BF16), seg)
        return x * lax.rsqrt(ssq * (1.0 / dh) + EPS) * gain

    kn = normed(jnp.concatenate([kp_ref[...], kc_ref[...]], axis=0).astype(F32), kg_ref[...])
    vt = jnp.concatenate([vp_ref[...], vc_ref[...]], axis=0).astype(F32).T.astype(BF16)
    lane = lax.broadcasted_iota(jnp.int32, (kn.shape[0], 2 * dh), 1)
    qgain = qg_ref[...] * (dh ** -0.5 * LOG2E)

    kzs, qns = [], []
    for g in range(N_KV_HEADS):
        pair = kn[:, (g // 2) * 2 * dh:(g // 2 + 1) * 2 * dh]
        own = jnp.where(lane >= dh if g % 2 else lane < dh, pair, 0.0)
        swapped = pltpu.roll(own, dh, axis=1)
        kz = [own, swapped] if g % 2 == 0 else [swapped, own]
        kzs.append([z.astype(BF16) for z in kz])
        qns.append(normed(q_ref[:, g * 4 * dh:(g + 1) * 4 * dh].astype(F32), qgain).astype(BF16))

    def scores(g, sb):
        bref = b0_ref if sb == 0 else b1_ref
        band = slice(sb * blk, sb * blk + 2 * blk)
        rows = slice(sb * blk, (sb + 1) * blk)
        kband = jnp.concatenate([kzs[g][0][band], kzs[g][1][band]], axis=0)
        qr = jnp.concatenate([qns[g][rows, 0:2 * dh], qns[g][rows, 2 * dh:4 * dh]], axis=0)
        return _dot_nt(kband, qr) + bref[g]

    def finish(g, sb, s):
        band = slice(sb * blk, sb * blk + 2 * blk)
        rows = slice(sb * blk, (sb + 1) * blk)
        ps, inv = [], []
        for e in range(2):
            se = s[e * 2 * blk:(e + 1) * 2 * blk]
            sink = sink_ref[g, e] * LOG2E
            m = jnp.maximum(jnp.max(se, axis=0, keepdims=True), sink)
            p = jnp.exp2(se - m)
            inv.append(1.0 / (jnp.sum(p, axis=0, keepdims=True) + jnp.exp2(sink - m)))
            ps.append(p.astype(BF16))
        ot = _dot(vt[g * dh:(g + 1) * dh, band], jnp.concatenate(ps, axis=1))
        ot = ot * jnp.concatenate(inv, axis=1)
        o = jnp.concatenate([ot[:, 0:2 * blk], ot[:, 2 * blk:4 * blk]], axis=0).T
        o_ref[rows, (2 * g) * 2 * dh:(2 * g + 1) * 2 * dh] = o[0:blk].astype(BF16)
        o_ref[rows, (2 * g + 1) * 2 * dh:(2 * g + 2) * 2 * dh] = o[blk:2 * blk].astype(BF16)

    units = [(g, sb) for g in range(N_KV_HEADS) for sb in range(nsub)]
    s_next = scores(*units[0])
    for n, unit in enumerate(units):
        s_cur = s_next
        if n + 1 < len(units):
            s_next = scores(*units[n + 1])
        finish(*unit, s_cur)


def _attn(proj3, biasm, q_gain, k_gain, sinks, cast_weights):
    B, S, _ = proj3.shape
    tq, blk = ATTN_TQ, ATTN_BLOCK
    nsub = tq // blk
    kvw = ATTN_KV_W
    seg_id = np.arange(kvw) // HEAD_DIM
    seg = jnp.asarray(seg_id[:, None] == seg_id[None, :], dtype=BF16)
    sink_rows = jnp.repeat(sinks.reshape(N_KV_HEADS, 2, 2).transpose(0, 2, 1), blk, axis=-1)
    sink_rows = sink_rows.reshape(N_KV_HEADS, 2, 1, 2 * blk)

    def prev(i):
        return jnp.maximum(i * nsub - 1, 0)

    bias_block = (None,) + biasm.shape[1:]
    cast_specs = _cast_specs(cast_weights, B, S // tq)
    outs = pl.pallas_call(
        functools.partial(_attn_kernel, n_cast=len(cast_weights)),
        grid=(B, S // tq),
        in_specs=[pl.BlockSpec((None, tq, ATTN_Q_W), lambda b, i: (b, i, COL_AQ // ATTN_Q_W)),
                  pl.BlockSpec((None, tq, kvw), lambda b, i: (b, i, COL_AK // kvw)),
                  pl.BlockSpec((None, blk, kvw), lambda b, i: (b, prev(i), COL_AK // kvw)),
                  pl.BlockSpec((None, tq, kvw), lambda b, i: (b, i, COL_AV // kvw)),
                  pl.BlockSpec((None, blk, kvw), lambda b, i: (b, prev(i), COL_AV // kvw)),
                  pl.BlockSpec(bias_block, lambda b, i: (jnp.minimum(i, 1), 0, 0, 0)),
                  pl.BlockSpec(bias_block, lambda b, i: (1, 0, 0, 0)),
                  pl.BlockSpec((1, kvw), lambda b, i: (0, 0)),
                  pl.BlockSpec((1, kvw), lambda b, i: (0, 0)),
                  pl.BlockSpec(sink_rows.shape, lambda b, i: (0, 0, 0, 0)),
                  pl.BlockSpec(seg.shape, lambda b, i: (0, 0))] + cast_specs,
        out_specs=[pl.BlockSpec((None, tq, ATTN_Q_W), lambda b, i: (b, i, 0))] + cast_specs,
        out_shape=[jax.ShapeDtypeStruct((B, S, ATTN_Q_W), BF16)]
        + [jax.ShapeDtypeStruct(cw.shape, BF16) for cw in cast_weights],
        compiler_params=_params("arbitrary", "arbitrary"),
        name="attn",
    )(proj3, proj3, proj3, proj3, proj3, biasm, biasm, jnp.tile(q_gain, (1, GQA_GROUP)),
      jnp.tile(k_gain, (1, N_KV_HEADS)), sink_rows, seg, *cast_weights)
    return outs[0], outs[1:]


def _split2(x):
    hi = x.astype(BF16)
    return hi, (x - hi.astype(F32)).astype(BF16)


def _gla_head(q, k, v, r, lr, wgk, bgk, gain, state_ref, masks, store):
    C = GLA_CHUNK
    tril, m_same, m_next, m_far, eye = masks
    z = _dot(lr, wgk) + bgk
    yield
    gl = (jnp.minimum(z, 0.0) - jnp.log(1.0 + jnp.exp(-jnp.abs(z)))) * (1.0 / GLA_NORMALIZER)
    hi, lo = _split2(gl)
    g = _dot(tril, hi) + _dot(tril, lo)
    yield
    t = [g[(c + 1) * C - 1:(c + 1) * C] for c in range(4)]
    t_rows = jnp.concatenate([jnp.broadcast_to(tc, (C, tc.shape[1])) for tc in t], axis=0)

    q_dec = q.astype(F32) * (GLA_DK ** -0.5) * jnp.exp(g)
    kf = k.astype(F32)
    k_inv = (kf * jnp.exp(-g)).astype(BF16)
    k_end = kf * jnp.exp(t_rows - g)
    qd = [q_dec[c * C:(c + 1) * C] for c in range(4)]
    ke = [k_end[c * C:(c + 1) * C] for c in range(4)]

    def rows(parts):
        return jnp.concatenate(parts, axis=0).astype(BF16)

    q_b, k_b = q_dec.astype(BF16), k_end.astype(BF16)
    q_far = rows([qd[0], qd[1], qd[2], qd[3] * jnp.exp(t[2])])
    k_far = rows([ke[0] * jnp.exp(t[1]), ke[1], ke[2], ke[3]])
    q_abs = rows([qd[0], qd[1] * jnp.exp(t[0]), qd[2] * jnp.exp(t[0] + t[1]), qd[3] * jnp.exp(t[0] + t[1] + t[2])])
    k_abs = rows([ke[0] * jnp.exp(t[1] + t[2] + t[3]), ke[1] * jnp.exp(t[2] + t[3]), ke[2] * jnp.exp(t[3]), ke[3]])

    a_same, a_next, a_far = _dot_nt(q_b, k_inv), _dot_nt(q_b, k_b), _dot_nt(q_far, k_far)
    state = state_ref[...]
    o_state = _dot(q_abs, state.astype(BF16))
    update = _dot_tn(k_abs, v)
    yield
    a = jnp.where(m_same, a_same, jnp.where(m_next, a_next, jnp.where(m_far, a_far, 0.0)))
    o = _dot(a.astype(BF16), v) + o_state
    yield

    decay = jnp.exp(t[0] + t[1] + t[2] + t[3])
    decay_col = jnp.sum(jnp.where(eye, jnp.broadcast_to(decay, eye.shape), 0.0), axis=1, keepdims=True)
    state_ref[...] = decay_col * state + update

    rf = r.astype(F32)
    store((_rms(o, gain) * (rf * jax.nn.sigmoid(rf))).astype(BF16))


def _gla_kernel(q_ref, k_ref, v_ref, r_ref, lr_ref, wgk_ref, bgk_ref, gain_ref, o_ref, state_ref):
    R, C, dk, dv = GLA_ROWS, GLA_CHUNK, GLA_DK, GLA_DV

    @pl.when(pl.program_id(2) == 0)
    def _():
        state_ref[...] = jnp.zeros_like(state_ref)

    row = lax.broadcasted_iota(jnp.int32, (R, R), 0)
    col = lax.broadcasted_iota(jnp.int32, (R, R), 1)
    ci, cj = row // C, col // C
    m_same = (row >= col) & (ci == cj)
    m_next = (ci == cj + 1) & (ci != 2)
    m_far = (ci >= 2) & (cj <= 1)
    eye = lax.broadcasted_iota(jnp.int32, (dk, dk), 0) == lax.broadcasted_iota(jnp.int32, (dk, dk), 1)
    masks = (m_same.astype(BF16), m_same, m_next, m_far, eye)

    lr = lr_ref[...]
    heads = []
    for h in range(GLA_HEADS_PER_STEP):
        ks, vs = slice(h * dk, (h + 1) * dk), slice(h * dv, (h + 1) * dv)
        heads.append(_gla_head(q_ref[:, ks], k_ref[:, ks], v_ref[:, vs], r_ref[:, vs], lr, wgk_ref[:, ks],
                               bgk_ref[:, ks], gain_ref[...], state_ref.at[h], masks,
                               functools.partial(o_ref.__setitem__, (slice(None), vs))))
    for _ in itertools.zip_longest(*heads):
        pass


def _gla(proj3, wgk, bgk, gain):
    B, S, _ = proj3.shape
    R, hps = GLA_ROWS, GLA_HEADS_PER_STEP
    dk, dv = hps * GLA_DK, hps * GLA_DV
    return pl.pallas_call(
        _gla_kernel,
        grid=(B, GLA_HEADS // hps, S // R),
        in_specs=[pl.BlockSpec((None, R, dk), lambda b, h, t: (b, t, COL_GQ // dk + h)),
                  pl.BlockSpec((None, R, dk), lambda b, h, t: (b, t, COL_GK // dk + h)),
                  pl.BlockSpec((None, R, dv), lambda b, h, t: (b, t, COL_GV // dv + h)),
                  pl.BlockSpec((None, R, dv), lambda b, h, t: (b, t, COL_GR // dv + h)),
                  pl.BlockSpec((None, R, LR_PAD), lambda b, h, t: (b, t, COL_LR // LR_PAD)),
                  pl.BlockSpec((LR_PAD, dk), lambda b, h, t: (0, h)),
                  pl.BlockSpec((1, dk), lambda b, h, t: (0, h)),
                  pl.BlockSpec((1, GLA_DV), lambda b, h, t: (0, 0))],
        out_specs=pl.BlockSpec((None, R, dv), lambda b, h, t: (b, t, h)),
        out_shape=jax.ShapeDtypeStruct((B, S, GLA_V_W), BF16),
        scratch_shapes=[pltpu.VMEM((hps, GLA_DK, GLA_DV), F32)],
        compiler_params=_params("parallel", "parallel", "arbitrary"),
        name="gla",
    )(proj3, proj3, proj3, proj3, proj3, wgk, bgk, gain)


def _merge_kernel(ya_ref, yg_ref, ga_ref, gb_ref, x_ref, mod_ref, g2_ref, wa_ref, wg_ref, wo_ref,
                  x1_ref, h2_ref):
    def merged_branches(rows):
        ma, mg = _dot(ya_ref[rows, :], wa_ref[...]), _dot(yg_ref[rows, :], wg_ref[...])
        ga = jax.nn.sigmoid(ga_ref[rows, :].astype(F32))
        gb = jax.nn.sigmoid(gb_ref[rows, :].astype(F32))
        return (ga * ma + gb * mg).astype(BF16)

    def project(rows, merged):
        x1 = x_ref[rows, :] + mod_ref[2:3, :] * _dot(merged, wo_ref[...])
        x1_ref[rows, :] = x1
        h2_ref[rows, :] = (_rms(x1, g2_ref[...]) * (1.0 + mod_ref[4:5, :]) + mod_ref[3:4, :]).astype(BF16)

    for r0 in range(0, MERGE_TM, MERGE_SUB):
        rows = slice(r0, r0 + MERGE_SUB)
        project(rows, merged_branches(rows))


def _merge(ya, yg, proj, x2, mod3, gain2, wa, wg, wo):
    T, D = x2.shape
    S = T // mod3.shape[0]
    tm = MERGE_TM
    once = pl.Buffered(1)
    return pl.pallas_call(
        _merge_kernel,
        grid=(T // tm,),
        in_specs=[pl.BlockSpec((tm, ATTN_Q_W), lambda i: (i, 0)),
                  pl.BlockSpec((tm, GLA_V_W), lambda i: (i, 0)),
                  pl.BlockSpec((tm, D), lambda i: (i, COL_GA // D)),
                  pl.BlockSpec((tm, D), lambda i: (i, COL_GB // D)),
                  pl.BlockSpec((tm, D), lambda i: (i, 0)),
                  pl.BlockSpec((None, 6, D), lambda i: (i // (S // tm), 0, 0)),
                  pl.BlockSpec((1, D), lambda i: (0, 0)),
                  pl.BlockSpec((ATTN_Q_W, D), lambda i: (0, 0), pipeline_mode=once),
                  pl.BlockSpec((GLA_V_W, D), lambda i: (0, 0), pipeline_mode=once),
                  pl.BlockSpec((D, D), lambda i: (0, 0), pipeline_mode=once)],
        out_specs=[pl.BlockSpec((tm, D), lambda i: (i, 0)),
                   pl.BlockSpec((tm, D), lambda i: (i, 0))],
        out_shape=[jax.ShapeDtypeStruct((T, D), F32), jax.ShapeDtypeStruct((T, D), BF16)],
        compiler_params=_params("parallel"),
        name="merge",
    )(ya, yg, proj, proj, x2, mod3, gain2, wa, wg, wo)


def _ffn_up_kernel(h_ref, halo_ref, wa_ref, wb_ref, cwa_ref, cwb_ref, cba_ref, cbb_ref, o_ref,
                   hs_ref, ua_ref, ub_ref, *, tiles_per_seq):
    tm, halo = FFN_TM, FFN_HALO

    @pl.when(pl.program_id(1) == 0)
    def _():
        first = (pl.program_id(0) % tiles_per_seq) == 0
        hs_ref[0:halo, :] = jnp.where(first, jnp.zeros_like(halo_ref), halo_ref[...])
        hs_ref[halo:, :] = h_ref[...]

    def conv(u_ref, cw_ref, cb_ref, r0, n):
        u = u_ref[halo + r0 - 8:halo + r0 + n, :]
        y = cb_ref[...] + cw_ref[0:1, :] * pltpu.roll(u, 2, axis=0)[8:]
        y = y + cw_ref[1:2, :] * pltpu.roll(u, 1, axis=0)[8:]
        return y + cw_ref[2:3, :] * u[8:]

    ua_ref[...] = _dot(hs_ref[...], wa_ref[...])
    ub_ref[...] = _dot(hs_ref[...], wb_ref[...])
    for r0 in range(0, tm, FFN_EPI_ROWS):
        ya = conv(ua_ref, cwa_ref, cba_ref, r0, FFN_EPI_ROWS)
        yb = conv(ub_ref, cwb_ref, cbb_ref, r0, FFN_EPI_ROWS)
        o_ref[r0:r0 + FFN_EPI_ROWS, :] = (ya * jax.nn.sigmoid(ya) * yb).astype(BF16)


def _ffn_up(h2, S, w_up, conv_w, conv_b):
    T, D = h2.shape
    tm, tn, halo = FFN_TM, FFN_TN, FFN_HALO
    nj = D_FF // tn
    return pl.pallas_call(
        functools.partial(_ffn_up_kernel, tiles_per_seq=S // tm),
        grid=(T // tm, D_FF // tn),
        in_specs=[pl.BlockSpec((tm, D), lambda i, j: (i, 0)),
                  pl.BlockSpec((halo, D), lambda i, j: (jnp.maximum(i * (tm // halo) - 1, 0), 0)),
                  pl.BlockSpec((D, tn), lambda i, j: (0, j)),
                  pl.BlockSpec((D, tn), lambda i, j: (0, j + nj)),
                  pl.BlockSpec((3, tn), lambda i, j: (0, j)),
                  pl.BlockSpec((3, tn), lambda i, j: (0, j + nj)),
                  pl.BlockSpec((1, tn), lambda i, j: (0, j)),
                  pl.BlockSpec((1, tn), lambda i, j: (0, j + nj))],
        out_specs=pl.BlockSpec((tm, tn), lambda i, j: (i, j)),
        out_shape=jax.ShapeDtypeStruct((T, D_FF), BF16),
        scratch_shapes=[pltpu.VMEM((tm + halo, D), BF16),
                        pltpu.VMEM((tm + halo, tn), F32),
                        pltpu.VMEM((tm + halo, tn), F32)],
        compiler_params=_params("parallel", "arbitrary"),
        name="ffn_up",
    )(h2, h2, w_up, w_up, conv_w, conv_w, conv_b, conv_b)


def _ffn_down_kernel(a_ref, w_ref, x_ref, mod_ref, o_ref):
    o_ref[...] = x_ref[...] + mod_ref[5:6, :] * _dot(a_ref[...], w_ref[...])


def _ffn_down(act, wd, x1, mod3):
    T, D = x1.shape
    S = T // mod3.shape[0]
    tm = DOWN_TM
    return pl.pallas_call(
        _ffn_down_kernel,
        grid=(T // tm,),
        in_specs=[pl.BlockSpec((tm, D_FF), lambda i: (i, 0)),
                  pl.BlockSpec((D_FF, D), lambda i: (0, 0), pipeline_mode=pl.Buffered(1)),
                  pl.BlockSpec((tm, D), lambda i: (i, 0)),
                  pl.BlockSpec((None, 6, D), lambda i: (i // (S // tm), 0, 0))],
        out_specs=pl.BlockSpec((tm, D), lambda i: (i, 0)),
        out_shape=jax.ShapeDtypeStruct((T, D), F32),
        compiler_params=_params("parallel"),
        name="ffn_down",
    )(act, wd, x1, mod3)


def _layer(x2, B, mod3, biasm, norm1_gain, w_in, q_norm_gain, k_norm_gain, attn_sinks, w_gk_up, b_gk,
           gla_norm_gain, w_branch_attn, w_branch_gla, w_out, norm2_gain, w_ffn_up, ffn_conv_w,
           ffn_conv_b, w_ffn_down):
    T, D = x2.shape
    S = T // B
    w_p = _wpack(w_in.T)
    wgk = jnp.concatenate([w_gk_up, jnp.zeros((LR_PAD - GLA_LOWRANK, GLA_K_W), w_gk_up.dtype)],
                          axis=0).astype(BF16)

    proj = _inproj(x2, mod3, norm1_gain.reshape(1, D), w_p)
    proj3 = proj.reshape(B, S, PROJ_W)
    ya, (w_ba, w_bg, w_o, w_up, w_down) = _attn(
        proj3, biasm, q_norm_gain.reshape(1, HEAD_DIM), k_norm_gain.reshape(1, HEAD_DIM), attn_sinks,
        (w_branch_attn, w_branch_gla, w_out, w_ffn_up, w_ffn_down))
    yg = _gla(proj3, wgk, b_gk.reshape(1, GLA_K_W), gla_norm_gain.reshape(1, GLA_DV))
    x1, h2 = _merge(ya.reshape(T, ATTN_Q_W), yg.reshape(T, GLA_V_W), proj, x2, mod3,
                    norm2_gain.reshape(1, D), w_ba, w_bg, w_o)
    act = _ffn_up(h2, S, w_up, ffn_conv_w, ffn_conv_b.reshape(1, 2 * D_FF))
    return _ffn_down(act, w_down, x1, mod3)


def kernel(x, c, rel_bias_table, w_ada, b_ada, norm1_gain, w_in, q_norm_gain, k_norm_gain, attn_sinks,
           w_gk_up, b_gk, gla_norm_gain, w_branch_attn, w_branch_gla, w_out, norm2_gain, w_ffn_up,
           ffn_conv_w, ffn_conv_b, w_ffn_down):
    B, S, D = x.shape
    depth = w_in.shape[0]
    assert D == D_MODEL and w_in.shape[1:] == (D_MODEL, SRC_GA + 2 * D_MODEL) and w_ffn_up.shape[2] == 2 * D_FF
    assert S % max(FFN_TM, INPROJ_TM, ATTN_TQ, GLA_ROWS, MERGE_TM, DOWN_TM) == 0
    biasm = _relbias(rel_bias_table)
    x2 = x.reshape(B * S, D)
    for l in range(depth):
        mod3 = _adaln(c, w_ada[l], b_ada[l]).reshape(B, 6, D)
        x2 = _layer(x2, B, mod3, biasm, norm1_gain[l], w_in[l], q_norm_gain[l], k_norm_gain[l],
                    attn_sinks[l], w_gk_up[l], b_gk[l], gla_norm_gain[l], w_branch_attn[l],
                    w_branch_gla[l], w_out[l], norm2_gain[l], w_ffn_up[l], ffn_conv_w[l], ffn_conv_b[l],
                    w_ffn_down[l])
    return x2.reshape(B, S, D)
```

```python
import functools
import itertools
import math

import numpy as np
import jax
import jax.numpy as jnp
from jax import lax
from jax.experimental import pallas as pl
from jax.experimental.pallas import tpu as pltpu

F32 = jnp.float32
BF16 = jnp.bfloat16

D_MODEL = 2048
N_Q_HEADS = 16
N_KV_HEADS = 4
GQA_GROUP = N_Q_HEADS // N_KV_HEADS
HEAD_DIM = 64
WINDOW = 128
ATTN_BLOCK = 128
N_BUCKETS = 32
MAX_DISTANCE = 128
GLA_HEADS = 4
GLA_DK = 256
GLA_DV = 512
GLA_LOWRANK = 16
GLA_NORMALIZER = 16.0
GLA_CHUNK = 64
D_FF = 5632
EPS = 1e-6
NEG_INF = -1e30
LOG2E = math.log2(math.e)

ATTN_Q_W = N_Q_HEADS * HEAD_DIM
ATTN_KV_W = N_KV_HEADS * HEAD_DIM
GLA_K_W = GLA_HEADS * GLA_DK
GLA_V_W = GLA_HEADS * GLA_DV

COL_GA = 0
COL_GB = COL_GA + D_MODEL
COL_GV = COL_GB + D_MODEL
COL_GR = COL_GV + GLA_V_W
COL_GQ = COL_GR + GLA_V_W
COL_GK = COL_GQ + GLA_K_W
COL_AQ = COL_GK + GLA_K_W
COL_AK = COL_AQ + ATTN_Q_W
COL_AV = COL_AK + ATTN_KV_W
COL_LR = COL_AV + ATTN_KV_W
LANE = 128
BF16_ROWS = 16
VMEM_LIMIT = 60 * 1024 * 1024

LR_PAD = LANE
PROJ_W = 12288

ADALN_TN = 1024
INPROJ_TM, INPROJ_TN = 1024, 2048
INPROJ_NORM_ROWS = 256
ATTN_TQ = 512
GLA_ROWS = 256
GLA_HEADS_PER_STEP = 4
MERGE_TM, MERGE_SUB = 512, 256
FFN_TM, FFN_TN = 1024, 512
FFN_HALO = BF16_ROWS
FFN_EPI_ROWS = 256
DOWN_TM = 512


def _params(*sem):
    return pltpu.CompilerParams(dimension_semantics=sem, vmem_limit_bytes=VMEM_LIMIT)


def _dot(a, b):
    return jnp.dot(a, b, preferred_element_type=F32)


def _dot_nt(a, b):
    return lax.dot_general(a, b, (((1,), (1,)), ((), ())), preferred_element_type=F32)


def _dot_tn(a, b):
    return lax.dot_general(a, b, (((0,), (0,)), ((), ())), preferred_element_type=F32)


def _rms(x, gain):
    return x * lax.rsqrt(jnp.mean(x * x, axis=-1, keepdims=True) + EPS) * gain


def _adaln_kernel(c_ref, w_ref, b_ref, o_ref):
    c = c_ref[...]
    ca = c * jax.nn.sigmoid(c)
    o_ref[...] = _dot(ca.astype(BF16), w_ref[...].astype(BF16)) + b_ref[...]


def _adaln(c, w_ada, b_ada):
    B, D = c.shape
    N = w_ada.shape[1]
    return pl.pallas_call(
        _adaln_kernel,
        grid=(N // ADALN_TN,),
        in_specs=[pl.BlockSpec((B, D), lambda j: (0, 0)),
                  pl.BlockSpec((D, ADALN_TN), lambda j: (0, j)),
                  pl.BlockSpec((1, ADALN_TN), lambda j: (0, j))],
        out_specs=pl.BlockSpec((B, ADALN_TN), lambda j: (0, j)),
        out_shape=jax.ShapeDtypeStruct((B, N), F32),
        compiler_params=_params("parallel"),
        name="adaln",
    )(c, w_ada, b_ada.reshape(1, N))


def _bucket_table():
    j = np.arange(2 * ATTN_BLOCK)[:, None]
    i = np.arange(ATTN_BLOCK)[None, :]
    dist = i + ATTN_BLOCK - j
    max_exact = N_BUCKETS // 2
    d = np.maximum(dist, 0)
    ratio = np.log(np.maximum(d, 1).astype(np.float32) / np.float32(max_exact)) / np.float32(
        math.log(MAX_DISTANCE / max_exact))
    large = max_exact + (ratio.astype(np.float32) * np.float32(N_BUCKETS - max_exact)).astype(np.int32)
    large = np.minimum(large, N_BUCKETS - 1)
    bucket = np.where(d < max_exact, d, large)
    in_window = (dist >= 0) & (dist < WINDOW)
    return np.where(in_window, bucket, -1).astype(np.int32)


def _relbias_kernel(tab_ref, bkt_ref, o_ref):
    g = pl.program_id(0)
    bkt = bkt_ref[...]
    nk, nq = bkt.shape
    key = lax.broadcasted_iota(jnp.int32, bkt.shape, 0)
    for e in range(2):
        for p in range(2):
            h = g * GQA_GROUP + 2 * p + e
            acc = jnp.zeros(bkt.shape, F32)
            for b in range(N_BUCKETS):
                acc = jnp.where(bkt == b, tab_ref[b, h], acc)
            regular = jnp.where(bkt >= 0, acc * LOG2E, NEG_INF)
            tile = (slice(e * nk, (e + 1) * nk), slice(p * nq, (p + 1) * nq))
            o_ref[(0,) + tile] = jnp.where(key >= ATTN_BLOCK, regular, NEG_INF)
            o_ref[(1,) + tile] = regular


def _relbias(rel_bias_table):
    bkt = jnp.asarray(_bucket_table())
    nk, nq = bkt.shape
    return pl.pallas_call(
        _relbias_kernel,
        grid=(N_KV_HEADS,),
        in_specs=[pl.BlockSpec(memory_space=pltpu.SMEM),
                  pl.BlockSpec(bkt.shape, lambda g: (0, 0))],
        out_specs=pl.BlockSpec((2, None, 2 * nk, 2 * nq), lambda g: (0, g, 0, 0)),
        out_shape=jax.ShapeDtypeStruct((2, N_KV_HEADS, 2 * nk, 2 * nq), F32),
        compiler_params=_params("parallel"),
        name="relbias",
    )(rel_bias_table, bkt)


SRC_GQ = ATTN_Q_W + 2 * ATTN_KV_W
SRC_GV = SRC_GQ + 2 * GLA_K_W
SRC_LR = SRC_GV + 2 * GLA_V_W
SRC_GA = SRC_LR + GLA_LOWRANK
WPACK_ROWS = 512


def _wpack_src_row(b):
    r = b * WPACK_ROWS
    src = jnp.where(r < COL_GV, SRC_GA + r,
                    jnp.where(r < COL_GQ, SRC_GV + (r - COL_GV),
                              jnp.where(r < COL_AQ, SRC_GQ + (r - COL_GQ),
                                        jnp.where(r < COL_LR, r - COL_AQ, SRC_LR))))
    return pl.multiple_of(src, 8)


def _wpack_kernel(w_ref, o_ref):
    r = pl.program_id(0) * WPACK_ROWS
    nvalid = jnp.where(r < COL_LR, WPACK_ROWS, jnp.where(r == COL_LR, GLA_LOWRANK, 0))
    row = lax.broadcasted_iota(jnp.int32, w_ref.shape, 0)
    o_ref[...] = jnp.where(row < nvalid, w_ref[...], 0.0).astype(BF16)


def _wpack(w_in_t):
    _, D = w_in_t.shape
    return pl.pallas_call(
        _wpack_kernel,
        grid=(PROJ_W // WPACK_ROWS,),
        in_specs=[pl.BlockSpec((pl.Element(WPACK_ROWS), pl.Element(D)), lambda b: (_wpack_src_row(b), 0))],
        out_specs=pl.BlockSpec((WPACK_ROWS, D), lambda b: (b, 0)),
        out_shape=jax.ShapeDtypeStruct((PROJ_W, D), BF16),
        compiler_params=_params("parallel"),
        name="wpack",
    )(w_in_t)


def _norm_modulate(x, mod_ref, gain):
    return (_rms(x, gain) * (1.0 + mod_ref[1:2, :]) + mod_ref[0:1, :]).astype(BF16)


def _hnorm_kernel(x_ref, mod_ref, g_ref, o_ref):
    o_ref[...] = _norm_modulate(x_ref[...], mod_ref, g_ref[...])


def _hnorm_first(x2, mod3, gain):
    D = x2.shape[1]
    nr = INPROJ_NORM_ROWS
    return pl.pallas_call(
        _hnorm_kernel,
        grid=(INPROJ_TM // nr,),
        in_specs=[pl.BlockSpec((nr, D), lambda r: (r, 0)),
                  pl.BlockSpec((None, 6, D), lambda r: (0, 0, 0)),
                  pl.BlockSpec((1, D), lambda r: (0, 0))],
        out_specs=pl.BlockSpec((nr, D), lambda r: (r, 0)),
        out_shape=jax.ShapeDtypeStruct((INPROJ_TM, D), BF16),
        compiler_params=_params("parallel"),
        name="hnorm_first",
    )(x2, mod3, gain)


def _cast_block(shape, steps):
    rows, cols = shape
    bc = min(cols, 1024)
    assert cols % bc == 0
    for br in range(BF16_ROWS, rows + 1, BF16_ROWS):
        if rows % br == 0 and (rows // br) * (cols // bc) <= steps:
            return br, bc
    raise ValueError(f"no cast block for {shape} in {steps} steps")


def _cast_specs(weights, n0, n1):
    def spec(shape):
        br, bc = _cast_block(shape, n0 * n1)
        ncb = shape[1] // bc
        last = (shape[0] // br) * ncb - 1

        def index(i, j):
            t = jnp.minimum(i * n1 + j, last)
            return t // ncb, t % ncb

        return pl.BlockSpec((br, bc), index)

    return [spec(w.shape) for w in weights]


def _inproj_kernel(xn_ref, modn_ref, g_ref, w_ref, h0_ref, o_ref, ha_ref, hb_ref):
    i, j = pl.program_id(0), pl.program_id(1)
    nr = INPROJ_NORM_ROWS

    @pl.when((i == 0) & (j == 0))
    def _():
        ha_ref[...] = h0_ref[...]

    def step(cur_ref, nxt_ref):
        o_ref[...] = _dot_nt(cur_ref[...], w_ref[...]).astype(BF16)
        if nxt_ref is not None:
            rows = pl.ds(pl.multiple_of(j * nr, nr), nr)
            nxt_ref[rows, :] = _norm_modulate(xn_ref[...], modn_ref, g_ref[...])

    has_chunk = j < INPROJ_TM // nr
    even = i % 2 == 0
    for buffers, parity in (((ha_ref, hb_ref), even), ((hb_ref, ha_ref), jnp.logical_not(even))):
        pl.when(parity & has_chunk)(functools.partial(step, *buffers))
        pl.when(parity & jnp.logical_not(has_chunk))(functools.partial(step, buffers[0], None))


def _inproj(x2, mod3, gain, w):
    T, D = x2.shape
    S = T // mod3.shape[0]
    tm, tn, nr = INPROJ_TM, INPROJ_TN, INPROJ_NORM_ROWS
    n_i, chunks = T // tm, tm // nr
    assert PROJ_W // tn >= chunks

    def next_tile(i):
        return jnp.minimum(i + 1, n_i - 1)

    return pl.pallas_call(
        _inproj_kernel,
        grid=(n_i, PROJ_W // tn),
        in_specs=[pl.BlockSpec((nr, D), lambda i, j: (next_tile(i) * chunks + jnp.minimum(j, chunks - 1), 0)),
                  pl.BlockSpec((None, 6, D), lambda i, j: (next_tile(i) // (S // tm), 0, 0)),
                  pl.BlockSpec((1, D), lambda i, j: (0, 0)),
                  pl.BlockSpec((tn, D), lambda i, j: (j, 0)),
                  pl.BlockSpec((tm, D), lambda i, j: (0, 0), pipeline_mode=pl.Buffered(1))],
        out_specs=pl.BlockSpec((tm, tn), lambda i, j: (i, j)),
        out_shape=jax.ShapeDtypeStruct((T, PROJ_W), BF16),
        scratch_shapes=[pltpu.VMEM((tm, D), BF16), pltpu.VMEM((tm, D), BF16)],
        compiler_params=_params("arbitrary", "arbitrary"),
        name="inproj",
    )(x2, mod3, gain, w, _hnorm_first(x2, mod3, gain))


def _attn_kernel(q_ref, kc_ref, kp_ref, vc_ref, vp_ref, b0_ref, b1_ref, qg_ref, kg_ref, sink_ref, seg_ref,
                 *rest, n_cast):
    cast_in, o_ref, cast_out = rest[:n_cast], rest[n_cast], rest[n_cast + 1:]
    for src_ref, dst_ref in zip(cast_in, cast_out):
        dst_ref[...] = src_ref[...].astype(BF16)

    nsub = ATTN_TQ // ATTN_BLOCK
    blk, dh = ATTN_BLOCK, HEAD_DIM
    seg = seg_ref[...]

    def normed(x, gain):
        ssq = _dot((x * x).astype(BF16), seg)
        return x * lax.rsqrt(ssq * (1.0 / dh) + EPS) * gain

    kn = normed(jnp.concatenate([kp_ref[...], kc_ref[...]], axis=0).astype(F32), kg_ref[...])
    vt = jnp.concatenate([vp_ref[...], vc_ref[...]], axis=0).astype(F32).T.astype(BF16)
    lane = lax.broadcasted_iota(jnp.int32, (kn.shape[0], 2 * dh), 1)
    qgain = qg_ref[...] * (dh ** -0.5 * LOG2E)

    kzs, qns = [], []
    for g in range(N_KV_HEADS):
        pair = kn[:, (g // 2) * 2 * dh:(g // 2 + 1) * 2 * dh]
        own = jnp.where(lane >= dh if g % 2 else lane < dh, pair, 0.0)
        swapped = pltpu.roll(own, dh, axis=1)
        kz = [own, swapped] if g % 2 == 0 else [swapped, own]
        kzs.append([z.astype(BF16) for z in kz])
        qns.append(normed(q_ref[:, g * 4 * dh:(g + 1) * 4 * dh].astype(F32), qgain).astype(BF16))

    def scores(g, sb):
        bref = b0_ref if sb == 0 else b1_ref
        band = slice(sb * blk, sb * blk + 2 * blk)
        rows = slice(sb * blk, (sb + 1) * blk)
        kband = jnp.concatenate([kzs[g][0][band], kzs[g][1][band]], axis=0)
        qr = jnp.concatenate([qns[g][rows, 0:2 * dh], qns[g][rows, 2 * dh:4 * dh]], axis=0)
        return _dot_nt(kband, qr) + bref[g]

    def finish(g, sb, s):
        band = slice(sb * blk, sb * blk + 2 * blk)
        rows = slice(sb * blk, (sb + 1) * blk)
        ps, inv = [], []
        for e in range(2):
            se = s[e * 2 * blk:(e + 1) * 2 * blk]
            sink = sink_ref[g, e] * LOG2E
            m = jnp.maximum(jnp.max(se, axis=0, keepdims=True), sink)
            p = jnp.exp2(se - m)
            inv.append(1.0 / (jnp.sum(p, axis=0, keepdims=True) + jnp.exp2(sink - m)))
            ps.append(p.astype(BF16))
        ot = _dot(vt[g * dh:(g + 1) * dh, band], jnp.concatenate(ps, axis=1))
        ot = ot * jnp.concatenate(inv, axis=1)
        o = jnp.concatenate([ot[:, 0:2 * blk], ot[:, 2 * blk:4 * blk]], axis=0).T
        o_ref[rows, (2 * g) * 2 * dh:(2 * g + 1) * 2 * dh] = o[0:blk].astype(BF16)
        o_ref[rows, (2 * g + 1) * 2 * dh:(2 * g + 2) * 2 * dh] = o[blk:2 * blk].astype(BF16)

    units = [(g, sb) for g in range(N_KV_HEADS) for sb in range(nsub)]
    s_next = scores(*units[0])
    for n, unit in enumerate(units):
        s_cur = s_next
        if n + 1 < len(units):
            s_next = scores(*units[n + 1])
        finish(*unit, s_cur)


def _attn(proj3, biasm, q_gain, k_gain, sinks, cast_weights):
    B, S, _ = proj3.shape
    tq, blk = ATTN_TQ, ATTN_BLOCK
    nsub = tq // blk
    kvw = ATTN_KV_W
    seg_id = np.arange(kvw) // HEAD_DIM
    seg = jnp.asarray(seg_id[:, None] == seg_id[None, :], dtype=BF16)
    sink_rows = jnp.repeat(sinks.reshape(N_KV_HEADS, 2, 2).transpose(0, 2, 1), blk, axis=-1)
    sink_rows = sink_rows.reshape(N_KV_HEADS, 2, 1, 2 * blk)

    def prev(i):
        return jnp.maximum(i * nsub - 1, 0)

    bias_block = (None,) + biasm.shape[1:]
    cast_specs = _cast_specs(cast_weights, B, S // tq)
    outs = pl.pallas_call(
        functools.partial(_attn_kernel, n_cast=len(cast_weights)),
        grid=(B, S // tq),
        in_specs=[pl.BlockSpec((None, tq, ATTN_Q_W), lambda b, i: (b, i, COL_AQ // ATTN_Q_W)),
                  pl.BlockSpec((None, tq, kvw), lambda b, i: (b, i, COL_AK // kvw)),
                  pl.BlockSpec((None, blk, kvw), lambda b, i: (b, prev(i), COL_AK // kvw)),
                  pl.BlockSpec((None, tq, kvw), lambda b, i: (b, i, COL_AV // kvw)),
                  pl.BlockSpec((None, blk, kvw), lambda b, i: (b, prev(i), COL_AV // kvw)),
                  pl.BlockSpec(bias_block, lambda b, i: (jnp.minimum(i, 1), 0, 0, 0)),
                  pl.BlockSpec(bias_block, lambda b, i: (1, 0, 0, 0)),
                  pl.BlockSpec((1, kvw), lambda b, i: (0, 0)),
                  pl.BlockSpec((1, kvw), lambda b, i: (0, 0)),
                  pl.BlockSpec(sink_rows.shape, lambda b, i: (0, 0, 0, 0)),
                  pl.BlockSpec(seg.shape, lambda b, i: (0, 0))] + cast_specs,
        out_specs=[pl.BlockSpec((None, tq, ATTN_Q_W), lambda b, i: (b, i, 0))] + cast_specs,
        out_shape=[jax.ShapeDtypeStruct((B, S, ATTN_Q_W), BF16)]
        + [jax.ShapeDtypeStruct(cw.shape, BF16) for cw in cast_weights],
        compiler_params=_params("arbitrary", "arbitrary"),
        name="attn",
    )(proj3, proj3, proj3, proj3, proj3, biasm, biasm, jnp.tile(q_gain, (1, GQA_GROUP)),
      jnp.tile(k_gain, (1, N_KV_HEADS)), sink_rows, seg, *cast_weights)
    return outs[0], outs[1:]


def _split2(x):
    hi = x.astype(BF16)
    return hi, (x - hi.astype(F32)).astype(BF16)


def _gla_head(q, k, v, r, lr, wgk, bgk, gain, state_ref, masks, store):
    C = GLA_CHUNK
    tril, m_same, m_next, m_far, eye = masks
    z = _dot(lr, wgk) + bgk
    yield
    gl = (jnp.minimum(z, 0.0) - jnp.log(1.0 + jnp.exp(-jnp.abs(z)))) * (1.0 / GLA_NORMALIZER)
    hi, lo = _split2(gl)
    g = _dot(tril, hi) + _dot(tril, lo)
    yield
    t = [g[(c + 1) * C - 1:(c + 1) * C] for c in range(4)]
    t_rows = jnp.concatenate([jnp.broadcast_to(tc, (C, tc.shape[1])) for tc in t], axis=0)

    q_dec = q.astype(F32) * (GLA_DK ** -0.5) * jnp.exp(g)
    kf = k.astype(F32)
    k_inv = (kf * jnp.exp(-g)).astype(BF16)
    k_end = kf * jnp.exp(t_rows - g)
    qd = [q_dec[c * C:(c + 1) * C] for c in range(4)]
    ke = [k_end[c * C:(c + 1) * C] for c in range(4)]

    def rows(parts):
        return jnp.concatenate(parts, axis=0).astype(BF16)

    q_b, k_b = q_dec.astype(BF16), k_end.astype(BF16)
    q_far = rows([qd[0], qd[1], qd[2], qd[3] * jnp.exp(t[2])])
    k_far = rows([ke[0] * jnp.exp(t[1]), ke[1], ke[2], ke[3]])
    q_abs = rows([qd[0], qd[1] * jnp.exp(t[0]), qd[2] * jnp.exp(t[0] + t[1]), qd[3] * jnp.exp(t[0] + t[1] + t[2])])
    k_abs = rows([ke[0] * jnp.exp(t[1] + t[2] + t[3]), ke[1] * jnp.exp(t[2] + t[3]), ke[2] * jnp.exp(t[3]), ke[3]])

    a_same, a_next, a_far = _dot_nt(q_b, k_inv), _dot_nt(q_b, k_b), _dot_nt(q_far, k_far)
    state = state_ref[...]
    o_state = _dot(q_abs, state.astype(BF16))
    update = _dot_tn(k_abs, v)
    yield
    a = jnp.where(m_same, a_same, jnp.where(m_next, a_next, jnp.where(m_far, a_far, 0.0)))
    o = _dot(a.astype(BF16), v) + o_state
    yield

    decay = jnp.exp(t[0] + t[1] + t[2] + t[3])
    decay_col = jnp.sum(jnp.where(eye, jnp.broadcast_to(decay, eye.shape), 0.0), axis=1, keepdims=True)
    state_ref[...] = decay_col * state + update

    rf = r.astype(F32)
    store((_rms(o, gain) * (rf * jax.nn.sigmoid(rf))).astype(BF16))


def _gla_kernel(q_ref, k_ref, v_ref, r_ref, lr_ref, wgk_ref, bgk_ref, gain_ref, o_ref, state_ref):
    R, C, dk, dv = GLA_ROWS, GLA_CHUNK, GLA_DK, GLA_DV

    @pl.when(pl.program_id(2) == 0)
    def _():
        state_ref[...] = jnp.zeros_like(state_ref)

    row = lax.broadcasted_iota(jnp.int32, (R, R), 0)
    col = lax.broadcasted_iota(jnp.int32, (R, R), 1)
    ci, cj = row // C, col // C
    m_same = (row >= col) & (ci == cj)
    m_next = (ci == cj + 1) & (ci != 2)
    m_far = (ci >= 2) & (cj <= 1)
    eye = lax.broadcasted_iota(jnp.int32, (dk, dk), 0) == lax.broadcasted_iota(jnp.int32, (dk, dk), 1)
    masks = (m_same.astype(BF16), m_same, m_next, m_far, eye)

    lr = lr_ref[...]
    heads = []
    for h in range(GLA_HEADS_PER_STEP):
        ks, vs = slice(h * dk, (h + 1) * dk), slice(h * dv, (h + 1) * dv)
        heads.append(_gla_head(q_ref[:, ks], k_ref[:, ks], v_ref[:, vs], r_ref[:, vs], lr, wgk_ref[:, ks],
                               bgk_ref[:, ks], gain_ref[...], state_ref.at[h], masks,
                               functools.partial(o_ref.__setitem__, (slice(None), vs))))
    for _ in itertools.zip_longest(*heads):
        pass


def _gla(proj3, wgk, bgk, gain):
    B, S, _ = proj3.shape
    R, hps = GLA_ROWS, GLA_HEADS_PER_STEP
    dk, dv = hps * GLA_DK, hps * GLA_DV
    return pl.pallas_call(
        _gla_kernel,
        grid=(B, GLA_HEADS // hps, S // R),
        in_specs=[pl.BlockSpec((None, R, dk), lambda b, h, t: (b, t, COL_GQ // dk + h)),
                  pl.BlockSpec((None, R, dk), lambda b, h, t: (b, t, COL_GK // dk + h)),
                  pl.BlockSpec((None, R, dv), lambda b, h, t: (b, t, COL_GV // dv + h)),
                  pl.BlockSpec((None, R, dv), lambda b, h, t: (b, t, COL_GR // dv + h)),
                  pl.BlockSpec((None, R, LR_PAD), lambda b, h, t: (b, t, COL_LR // LR_PAD)),
                  pl.BlockSpec((LR_PAD, dk), lambda b, h, t: (0, h)),
                  pl.BlockSpec((1, dk), lambda b, h, t: (0, h)),
                  pl.BlockSpec((1, GLA_DV), lambda b, h, t: (0, 0))],
        out_specs=pl.BlockSpec((None, R, dv), lambda b, h, t: (b, t, h)),
        out_shape=jax.ShapeDtypeStruct((B, S, GLA_V_W), BF16),
        scratch_shapes=[pltpu.VMEM((hps, GLA_DK, GLA_DV), F32)],
        compiler_params=_params("parallel", "parallel", "arbitrary"),
        name="gla",
    )(proj3, proj3, proj3, proj3, proj3, wgk, bgk, gain)


def _merge_kernel(ya_ref, yg_ref, ga_ref, gb_ref, x_ref, mod_ref, g2_ref, wa_ref, wg_ref, wo_ref,
                  x1_ref, h2_ref):
    def merged_branches(rows):
        ma, mg = _dot(ya_ref[rows, :], wa_ref[...]), _dot(yg_ref[rows, :], wg_ref[...])
        ga = jax.nn.sigmoid(ga_ref[rows, :].astype(F32))
        gb = jax.nn.sigmoid(gb_ref[rows, :].astype(F32))
        return (ga * ma + gb * mg).astype(BF16)

    def project(rows, merged):
        x1 = x_ref[rows, :] + mod_ref[2:3, :] * _dot(merged, wo_ref[...])
        x1_ref[rows, :] = x1
        h2_ref[rows, :] = (_rms(x1, g2_ref[...]) * (1.0 + mod_ref[4:5, :]) + mod_ref[3:4, :]).astype(BF16)

    for r0 in range(0, MERGE_TM, MERGE_SUB):
        rows = slice(r0, r0 + MERGE_SUB)
        project(rows, merged_branches(rows))


def _merge(ya, yg, proj, x2, mod3, gain2, wa, wg, wo):
    T, D = x2.shape
    S = T // mod3.shape[0]
    tm = MERGE_TM
    once = pl.Buffered(1)
    return pl.pallas_call(
        _merge_kernel,
        grid=(T // tm,),
        in_specs=[pl.BlockSpec((tm, ATTN_Q_W), lambda i: (i, 0)),
                  pl.BlockSpec((tm, GLA_V_W), lambda i: (i, 0)),
                  pl.BlockSpec((tm, D), lambda i: (i, COL_GA // D)),
                  pl.BlockSpec((tm, D), lambda i: (i, COL_GB // D)),
                  pl.BlockSpec((tm, D), lambda i: (i, 0)),
                  pl.BlockSpec((None, 6, D), lambda i: (i // (S // tm), 0, 0)),
                  pl.BlockSpec((1, D), lambda i: (0, 0)),
                  pl.BlockSpec((ATTN_Q_W, D), lambda i: (0, 0), pipeline_mode=once),
                  pl.BlockSpec((GLA_V_W, D), lambda i: (0, 0), pipeline_mode=once),
                  pl.BlockSpec((D, D), lambda i: (0, 0), pipeline_mode=once)],
        out_specs=[pl.BlockSpec((tm, D), lambda i: (i, 0)),
                   pl.BlockSpec((tm, D), lambda i: (i, 0))],
        out_shape=[jax.ShapeDtypeStruct((T, D), F32), jax.ShapeDtypeStruct((T, D), BF16)],
        compiler_params=_params("parallel"),
        name="merge",
    )(ya, yg, proj, proj, x2, mod3, gain2, wa, wg, wo)


def _ffn_up_kernel(h_ref, halo_ref, wa_ref, wb_ref, cwa_ref, cwb_ref, cba_ref, cbb_ref, o_ref,
                   hs_ref, ua_ref, ub_ref, *, tiles_per_seq):
    tm, halo = FFN_TM, FFN_HALO

    @pl.when(pl.program_id(1) == 0)
    def _():
        first = (pl.program_id(0) % tiles_per_seq) == 0
        hs_ref[0:halo, :] = jnp.where(first, jnp.zeros_like(halo_ref), halo_ref[...])
        hs_ref[halo:, :] = h_ref[...]

    def conv(u_ref, cw_ref, cb_ref, r0, n):
        u = u_ref[halo + r0 - 8:halo + r0 + n, :]
        y = cb_ref[...] + cw_ref[0:1, :] * pltpu.roll(u, 2, axis=0)[8:]
        y = y + cw_ref[1:2, :] * pltpu.roll(u, 1, axis=0)[8:]
        return y + cw_ref[2:3, :] * u[8:]

    ua_ref[...] = _dot(hs_ref[...], wa_ref[...])
    ub_ref[...] = _dot(hs_ref[...], wb_ref[...])
    for r0 in range(0, tm, FFN_EPI_ROWS):
        ya = conv(ua_ref, cwa_ref, cba_ref, r0, FFN_EPI_ROWS)
        yb = conv(ub_ref, cwb_ref, cbb_ref, r0, FFN_EPI_ROWS)
        o_ref[r0:r0 + FFN_EPI_ROWS, :] = (ya * jax.nn.sigmoid(ya) * yb).astype(BF16)


def _ffn_up(h2, S, w_up, conv_w, conv_b):
    T, D = h2.shape
    tm, tn, halo = FFN_TM, FFN_TN, FFN_HALO
    nj = D_FF // tn
    return pl.pallas_call(
        functools.partial(_ffn_up_kernel, tiles_per_seq=S // tm),
        grid=(T // tm, D_FF // tn),
        in_specs=[pl.BlockSpec((tm, D), lambda i, j: (i, 0)),
                  pl.BlockSpec((halo, D), lambda i, j: (jnp.maximum(i * (tm // halo) - 1, 0), 0)),
                  pl.BlockSpec((D, tn), lambda i, j: (0, j)),
                  pl.BlockSpec((D, tn), lambda i, j: (0, j + nj)),
                  pl.BlockSpec((3, tn), lambda i, j: (0, j)),
                  pl.BlockSpec((3, tn), lambda i, j: (0, j + nj)),
                  pl.BlockSpec((1, tn), lambda i, j: (0, j)),
                  pl.BlockSpec((1, tn), lambda i, j: (0, j + nj))],
        out_specs=pl.BlockSpec((tm, tn), lambda i, j: (i, j)),
        out_shape=jax.ShapeDtypeStruct((T, D_FF), BF16),
        scratch_shapes=[pltpu.VMEM((tm + halo, D), BF16),
                        pltpu.VMEM((tm + halo, tn), F32),
                        pltpu.VMEM((tm + halo, tn), F32)],
        compiler_params=_params("parallel", "arbitrary"),
        name="ffn_up",
    )(h2, h2, w_up, w_up, conv_w, conv_w, conv_b, conv_b)


def _ffn_down_kernel(a_ref, w_ref, x_ref, mod_ref, o_ref):
    o_ref[...] = x_ref[...] + mod_ref[5:6, :] * _dot(a_ref[...], w_ref[...])


def _ffn_down(act, wd, x1, mod3):
    T, D = x1.shape
    S = T // mod3.shape[0]
    tm = DOWN_TM
    return pl.pallas_call(
        _ffn_down_kernel,
        grid=(T // tm,),
        in_specs=[pl.BlockSpec((tm, D_FF), lambda i: (i, 0)),
                  pl.BlockSpec((D_FF, D), lambda i: (0, 0), pipeline_mode=pl.Buffered(1)),
                  pl.BlockSpec((tm, D), lambda i: (i, 0)),
                  pl.BlockSpec((None, 6, D), lambda i: (i // (S // tm), 0, 0))],
        out_specs=pl.BlockSpec((tm, D), lambda i: (i, 0)),
        out_shape=jax.ShapeDtypeStruct((T, D), F32),
        compiler_params=_params("parallel"),
        name="ffn_down",
    )(act, wd, x1, mod3)


def _layer(x2, B, mod3, biasm, norm1_gain, w_in, q_norm_gain, k_norm_gain, attn_sinks, w_gk_up, b_gk,
           gla_norm_gain, w_branch_attn, w_branch_gla, w_out, norm2_gain, w_ffn_up, ffn_conv_w,
           ffn_conv_b, w_ffn_down):
    T, D = x2.shape
    S = T // B
    w_p = _wpack(w_in.T)
    wgk = jnp.concatenate([w_gk_up, jnp.zeros((LR_PAD - GLA_LOWRANK, GLA_K_W), w_gk_up.dtype)],
                          axis=0).astype(BF16)

    proj = _inproj(x2, mod3, norm1_gain.reshape(1, D), w_p)
    proj3 = proj.reshape(B, S, PROJ_W)
    ya, (w_ba, w_bg, w_o, w_up, w_down) = _attn(
        proj3, biasm, q_norm_gain.reshape(1, HEAD_DIM), k_norm_gain.reshape(1, HEAD_DIM), attn_sinks,
        (w_branch_attn, w_branch_gla, w_out, w_ffn_up, w_ffn_down))
    yg = _gla(proj3, wgk, b_gk.reshape(1, GLA_K_W), gla_norm_gain.reshape(1, GLA_DV))
    x1, h2 = _merge(ya.reshape(T, ATTN_Q_W), yg.reshape(T, GLA_V_W), proj, x2, mod3,
                    norm2_gain.reshape(1, D), w_ba, w_bg, w_o)
    act = _ffn_up(h2, S, w_up, ffn_conv_w, ffn_conv_b.reshape(1, 2 * D_FF))
    return _ffn_down(act, w_down, x1, mod3)


def kernel(x, c, rel_bias_table, w_ada, b_ada, norm1_gain, w_in, q_norm_gain, k_norm_gain, attn_sinks,
           w_gk_up, b_gk, gla_norm_gain, w_branch_attn, w_branch_gla, w_out, norm2_gain, w_ffn_up,
           ffn_conv_w, ffn_conv_b, w_ffn_down):
    B, S, D = x.shape
    depth = w_in.shape[0]
    assert D == D_MODEL and w_in.shape[1:] == (D_MODEL, SRC_GA + 2 * D_MODEL) and w_ffn_up.shape[2] == 2 * D_FF
    assert S % max(FFN_TM, INPROJ_TM, ATTN_TQ, GLA_ROWS, MERGE_TM, DOWN_TM) == 0
    biasm = _relbias(rel_bias_table)
    x2 = x.reshape(B * S, D)
    for l in range(depth):
        mod3 = _adaln(c, w_ada[l], b_ada[l]).reshape(B, 6, D)
        x2 = _layer(x2, B, mod3, biasm, norm1_gain[l], w_in[l], q_norm_gain[l], k_norm_gain[l],
                    attn_sinks[l], w_gk_up[l], b_gk[l], gla_norm_gain[l], w_branch_attn[l],
                    w_branch_gla[l], w_out[l], norm2_gain[l], w_ffn_up[l], ffn_conv_w[l], ffn_conv_b[l],
                    w_ffn_down[l])
    return x2.reshape(B, S, D)
```

```python
import functools
import itertools
import math

import numpy as np
import jax
import jax.numpy as jnp
from jax import lax
from jax.experimental import pallas as pl
from jax.experimental.pallas import tpu as pltpu

F32 = jnp.float32
BF16 = jnp.bfloat16

D_MODEL = 2048
N_Q_HEADS = 16
N_KV_HEADS = 4
GQA_GROUP = N_Q_HEADS // N_KV_HEADS
HEAD_DIM = 64
WINDOW = 128
ATTN_BLOCK = 128
N_BUCKETS = 32
MAX_DISTANCE = 128
GLA_HEADS = 4
GLA_DK = 256
GLA_DV = 512
GLA_LOWRANK = 16
GLA_NORMALIZER = 16.0
GLA_CHUNK = 64
D_FF = 5632
EPS = 1e-6
NEG_INF = -1e30
LOG2E = math.log2(math.e)

ATTN_Q_W = N_Q_HEADS * HEAD_DIM
ATTN_KV_W = N_KV_HEADS * HEAD_DIM
GLA_K_W = GLA_HEADS * GLA_DK
GLA_V_W = GLA_HEADS * GLA_DV

COL_GA = 0
COL_GB = COL_GA + D_MODEL
COL_GV = COL_GB + D_MODEL
COL_GR = COL_GV + GLA_V_W
COL_GQ = COL_GR + GLA_V_W
COL_GK = COL_GQ + GLA_K_W
COL_AQ = COL_GK + GLA_K_W
COL_AK = COL_AQ + ATTN_Q_W
COL_AV = COL_AK + ATTN_KV_W
COL_LR = COL_AV + ATTN_KV_W
LANE = 128
BF16_ROWS = 16
VMEM_LIMIT = 60 * 1024 * 1024

LR_PAD = LANE
PROJ_W = 12288

ADALN_TN = 1024
INPROJ_TM, INPROJ_TN = 1024, 2048
INPROJ_NORM_ROWS = 256
ATTN_TQ = 512
GLA_ROWS = 256
GLA_HEADS_PER_STEP = 4
GLA_BATCH_PER_STEP = 2
MERGE_TM, MERGE_SUB = 512, 256
FFN_TM, FFN_TN = 1024, 512
FFN_HALO = BF16_ROWS
FFN_EPI_ROWS = 256
DOWN_TM = 512


def _params(*sem):
    return pltpu.CompilerParams(dimension_semantics=sem, vmem_limit_bytes=VMEM_LIMIT)


def _dot(a, b):
    return jnp.dot(a, b, preferred_element_type=F32)


def _dot_nt(a, b):
    return lax.dot_general(a, b, (((1,), (1,)), ((), ())), preferred_element_type=F32)


def _dot_tn(a, b):
    return lax.dot_general(a, b, (((0,), (0,)), ((), ())), preferred_element_type=F32)


def _rms(x, gain):
    return x * lax.rsqrt(jnp.mean(x * x, axis=-1, keepdims=True) + EPS) * gain


def _adaln_kernel(c_ref, w_ref, b_ref, o_ref):
    c = c_ref[...]
    ca = c * jax.nn.sigmoid(c)
    o_ref[...] = _dot(ca.astype(BF16), w_ref[...].astype(BF16)) + b_ref[...]


def _adaln(c, w_ada, b_ada):
    B, D = c.shape
    N = w_ada.shape[1]
    return pl.pallas_call(
        _adaln_kernel,
        grid=(N // ADALN_TN,),
        in_specs=[pl.BlockSpec((B, D), lambda j: (0, 0)),
                  pl.BlockSpec((D, ADALN_TN), lambda j: (0, j)),
                  pl.BlockSpec((1, ADALN_TN), lambda j: (0, j))],
        out_specs=pl.BlockSpec((B, ADALN_TN), lambda j: (0, j)),
        out_shape=jax.ShapeDtypeStruct((B, N), F32),
        compiler_params=_params("parallel"),
        name="adaln",
    )(c, w_ada, b_ada.reshape(1, N))


def _bucket_table():
    j = np.arange(2 * ATTN_BLOCK)[:, None]
    i = np.arange(ATTN_BLOCK)[None, :]
    dist = i + ATTN_BLOCK - j
    max_exact = N_BUCKETS // 2
    d = np.maximum(dist, 0)
    ratio = np.log(np.maximum(d, 1).astype(np.float32) / np.float32(max_exact)) / np.float32(
        math.log(MAX_DISTANCE / max_exact))
    large = max_exact + (ratio.astype(np.float32) * np.float32(N_BUCKETS - max_exact)).astype(np.int32)
    large = np.minimum(large, N_BUCKETS - 1)
    bucket = np.where(d < max_exact, d, large)
    in_window = (dist >= 0) & (dist < WINDOW)
    return np.where(in_window, bucket, -1).astype(np.int32)


def _relbias_kernel(tab_ref, bkt_ref, o_ref):
    g = pl.program_id(0)
    bkt = bkt_ref[...]
    nk, nq = bkt.shape
    key = lax.broadcasted_iota(jnp.int32, bkt.shape, 0)
    for e in range(2):
        for p in range(2):
            h = g * GQA_GROUP + 2 * p + e
            acc = jnp.zeros(bkt.shape, F32)
            for b in range(N_BUCKETS):
                acc = jnp.where(bkt == b, tab_ref[b, h], acc)
            regular = jnp.where(bkt >= 0, acc * LOG2E, NEG_INF)
            tile = (slice(e * nk, (e + 1) * nk), slice(p * nq, (p + 1) * nq))
            o_ref[(0,) + tile] = jnp.where(key >= ATTN_BLOCK, regular, NEG_INF)
            o_ref[(1,) + tile] = regular


def _relbias(rel_bias_table):
    bkt = jnp.asarray(_bucket_table())
    nk, nq = bkt.shape
    return pl.pallas_call(
        _relbias_kernel,
        grid=(N_KV_HEADS,),
        in_specs=[pl.BlockSpec(memory_space=pltpu.SMEM),
                  pl.BlockSpec(bkt.shape, lambda g: (0, 0))],
        out_specs=pl.BlockSpec((2, None, 2 * nk, 2 * nq), lambda g: (0, g, 0, 0)),
        out_shape=jax.ShapeDtypeStruct((2, N_KV_HEADS, 2 * nk, 2 * nq), F32),
        compiler_params=_params("parallel"),
        name="relbias",
    )(rel_bias_table, bkt)


SRC_GQ = ATTN_Q_W + 2 * ATTN_KV_W
SRC_GV = SRC_GQ + 2 * GLA_K_W
SRC_LR = SRC_GV + 2 * GLA_V_W
SRC_GA = SRC_LR + GLA_LOWRANK
WPACK_ROWS = 512


def _wpack_src_row(b):
    r = b * WPACK_ROWS
    src = jnp.where(r < COL_GV, SRC_GA + r,
                    jnp.where(r < COL_GQ, SRC_GV + (r - COL_GV),
                              jnp.where(r < COL_AQ, SRC_GQ + (r - COL_GQ),
                                        jnp.where(r < COL_LR, r - COL_AQ, SRC_LR))))
    return pl.multiple_of(src, 8)


def _wpack_kernel(w_ref, o_ref):
    r = pl.program_id(0) * WPACK_ROWS
    nvalid = jnp.where(r < COL_LR, WPACK_ROWS, jnp.where(r == COL_LR, GLA_LOWRANK, 0))
    row = lax.broadcasted_iota(jnp.int32, w_ref.shape, 0)
    o_ref[...] = jnp.where(row < nvalid, w_ref[...], 0.0).astype(BF16)


def _wpack(w_in_t):
    _, D = w_in_t.shape
    return pl.pallas_call(
        _wpack_kernel,
        grid=(PROJ_W // WPACK_ROWS,),
        in_specs=[pl.BlockSpec((pl.Element(WPACK_ROWS), pl.Element(D)), lambda b: (_wpack_src_row(b), 0))],
        out_specs=pl.BlockSpec((WPACK_ROWS, D), lambda b: (b, 0)),
        out_shape=jax.ShapeDtypeStruct((PROJ_W, D), BF16),
        compiler_params=_params("parallel"),
        name="wpack",
    )(w_in_t)


def _norm_modulate(x, mod_ref, gain):
    return (_rms(x, gain) * (1.0 + mod_ref[1:2, :]) + mod_ref[0:1, :]).astype(BF16)


def _hnorm_kernel(x_ref, mod_ref, g_ref, o_ref):
    o_ref[...] = _norm_modulate(x_ref[...], mod_ref, g_ref[...])


def _hnorm_first(x2, mod3, gain):
    D = x2.shape[1]
    nr = INPROJ_NORM_ROWS
    return pl.pallas_call(
        _hnorm_kernel,
        grid=(INPROJ_TM // nr,),
        in_specs=[pl.BlockSpec((nr, D), lambda r: (r, 0)),
                  pl.BlockSpec((None, 6, D), lambda r: (0, 0, 0)),
                  pl.BlockSpec((1, D), lambda r: (0, 0))],
        out_specs=pl.BlockSpec((nr, D), lambda r: (r, 0)),
        out_shape=jax.ShapeDtypeStruct((INPROJ_TM, D), BF16),
        compiler_params=_params("parallel"),
        name="hnorm_first",
    )(x2, mod3, gain)


def _cast_block(shape, steps):
    rows, cols = shape
    bc = min(cols, 1024)
    assert cols % bc == 0
    for br in range(BF16_ROWS, rows + 1, BF16_ROWS):
        if rows % br == 0 and (rows // br) * (cols // bc) <= steps:
            return br, bc
    raise ValueError(f"no cast block for {shape} in {steps} steps")


def _cast_specs(weights, n0, n1):
    def spec(shape):
        br, bc = _cast_block(shape, n0 * n1)
        ncb = shape[1] // bc
        last = (shape[0] // br) * ncb - 1

        def index(i, j):
            t = jnp.minimum(i * n1 + j, last)
            return t // ncb, t % ncb

        return pl.BlockSpec((br, bc), index)

    return [spec(w.shape) for w in weights]


def _inproj_kernel(xn_ref, modn_ref, g_ref, w_ref, h0_ref, o_ref, ha_ref, hb_ref):
    i, j = pl.program_id(0), pl.program_id(1)
    nr = INPROJ_NORM_ROWS

    @pl.when((i == 0) & (j == 0))
    def _():
        ha_ref[...] = h0_ref[...]

    def step(cur_ref, nxt_ref):
        o_ref[...] = _dot_nt(cur_ref[...], w_ref[...]).astype(BF16)
        chunk = jnp.minimum(j, INPROJ_TM // nr - 1)
        rows = pl.ds(pl.multiple_of(chunk * nr, nr), nr)
        nxt_ref[rows, :] = _norm_modulate(xn_ref[...], modn_ref, g_ref[...])

    @pl.when(i % 2 == 0)
    def _():
        step(ha_ref, hb_ref)

    @pl.when(i % 2 == 1)
    def _():
        step(hb_ref, ha_ref)


def _inproj(x2, mod3, gain, w):
    T, D = x2.shape
    S = T // mod3.shape[0]
    tm, tn, nr = INPROJ_TM, INPROJ_TN, INPROJ_NORM_ROWS
    n_i, chunks = T // tm, tm // nr
    assert PROJ_W // tn >= chunks

    def next_tile(i):
        return jnp.minimum(i + 1, n_i - 1)

    return pl.pallas_call(
        _inproj_kernel,
        grid=(n_i, PROJ_W // tn),
        in_specs=[pl.BlockSpec((nr, D), lambda i, j: (next_tile(i) * chunks + jnp.minimum(j, chunks - 1), 0)),
                  pl.BlockSpec((None, 6, D), lambda i, j: (next_tile(i) // (S // tm), 0, 0)),
                  pl.BlockSpec((1, D), lambda i, j: (0, 0)),
                  pl.BlockSpec((tn, D), lambda i, j: (j, 0)),
                  pl.BlockSpec((tm, D), lambda i, j: (0, 0), pipeline_mode=pl.Buffered(1))],
        out_specs=pl.BlockSpec((tm, tn), lambda i, j: (i, j)),
        out_shape=jax.ShapeDtypeStruct((T, PROJ_W), BF16),
        scratch_shapes=[pltpu.VMEM((tm, D), BF16), pltpu.VMEM((tm, D), BF16)],
        compiler_params=_params("arbitrary", "arbitrary"),
        name="inproj",
    )(x2, mod3, gain, w, _hnorm_first(x2, mod3, gain))


def _attn_kernel(q_ref, kc_ref, kp_ref, vc_ref, vp_ref, b0_ref, b1_ref, qg_ref, kg_ref, sink_ref, seg_ref,
                 *rest, n_cast):
    cast_in, o_ref, cast_out = rest[:n_cast], rest[n_cast], rest[n_cast + 1:]
    for src_ref, dst_ref in zip(cast_in, cast_out):
        dst_ref[...] = src_ref[...].astype(BF16)

    nsub = ATTN_TQ // ATTN_BLOCK
    blk, dh = ATTN_BLOCK, HEAD_DIM
    seg = seg_ref[...]

    def normed(x, gain):
        ssq = _dot((x * x).astype(BF16), seg)
        return x * lax.rsqrt(ssq * (1.0 / dh) + EPS) * gain

    kn = normed(jnp.concatenate([kp_ref[...], kc_ref[...]], axis=0).astype(F32), kg_ref[...])
    vt = jnp.concatenate([vp_ref[...], vc_ref[...]], axis=0).astype(F32).T.astype(BF16)
    lane = lax.broadcasted_iota(jnp.int32, (kn.shape[0], 2 * dh), 1)
    qgain = qg_ref[...] * (dh ** -0.5 * LOG2E)

    kzs, qns = [], []
    for g in range(N_KV_HEADS):
        pair = kn[:, (g // 2) * 2 * dh:(g // 2 + 1) * 2 * dh]
        own = jnp.where(lane >= dh if g % 2 else lane < dh, pair, 0.0)
        swapped = pltpu.roll(own, dh, axis=1)
        kz = [own, swapped] if g % 2 == 0 else [swapped, own]
        kzs.append([z.astype(BF16) for z in kz])
        qns.append(normed(q_ref[:, g * 4 * dh:(g + 1) * 4 * dh].astype(F32), qgain).astype(BF16))

    def scores(g, sb):
        bref = b0_ref if sb == 0 else b1_ref
        band = slice(sb * blk, sb * blk + 2 * blk)
        rows = slice(sb * blk, (sb + 1) * blk)
        kband = jnp.concatenate([kzs[g][0][band], kzs[g][1][band]], axis=0)
        qr = jnp.concatenate([qns[g][rows, 0:2 * dh], qns[g][rows, 2 * dh:4 * dh]], axis=0)
        return _dot_nt(kband, qr) + bref[g]

    def finish(g, sb, s):
        band = slice(sb * blk, sb * blk + 2 * blk)
        rows = slice(sb * blk, (sb + 1) * blk)
        ps, inv = [], []
        for e in range(2):
            se = s[e * 2 * blk:(e + 1) * 2 * blk]
            sink = sink_ref[g, e] * LOG2E
            m = jnp.maximum(jnp.max(se, axis=0, keepdims=True), sink)
            p = jnp.exp2(se - m)
            inv.append(1.0 / (jnp.sum(p, axis=0, keepdims=True) + jnp.exp2(sink - m)))
            ps.append(p.astype(BF16))
        ot = _dot(vt[g * dh:(g + 1) * dh, band], jnp.concatenate(ps, axis=1))
        ot = ot * jnp.concatenate(inv, axis=1)
        o = jnp.concatenate([ot[:, 0:2 * blk], ot[:, 2 * blk:4 * blk]], axis=0).T
        o_ref[rows, (2 * g) * 2 * dh:(2 * g + 1) * 2 * dh] = o[0:blk].astype(BF16)
        o_ref[rows, (2 * g + 1) * 2 * dh:(2 * g + 2) * 2 * dh] = o[blk:2 * blk].astype(BF16)

    units = [(g, sb) for g in range(N_KV_HEADS) for sb in range(nsub)]
    s_next = scores(*units[0])
    for n, unit in enumerate(units):
        s_cur = s_next
        if n + 1 < len(units):
            s_next = scores(*units[n + 1])
        finish(*unit, s_cur)


def _attn(proj3, biasm, q_gain, k_gain, sinks, cast_weights):
    B, S, _ = proj3.shape
    tq, blk = ATTN_TQ, ATTN_BLOCK
    nsub = tq // blk
    kvw = ATTN_KV_W
    seg_id = np.arange(kvw) // HEAD_DIM
    seg = jnp.asarray(seg_id[:, None] == seg_id[None, :], dtype=BF16)
    sink_rows = jnp.repeat(sinks.reshape(N_KV_HEADS, 2, 2).transpose(0, 2, 1), blk, axis=-1)
    sink_rows = sink_rows.reshape(N_KV_HEADS, 2, 1, 2 * blk)

    def prev(i):
        return jnp.maximum(i * nsub - 1, 0)

    bias_block = (None,) + biasm.shape[1:]
    cast_specs = _cast_specs(cast_weights, B, S // tq)
    outs = pl.pallas_call(
        functools.partial(_attn_kernel, n_cast=len(cast_weights)),
        grid=(B, S // tq),
        in_specs=[pl.BlockSpec((None, tq, ATTN_Q_W), lambda b, i: (b, i, COL_AQ // ATTN_Q_W)),
                  pl.BlockSpec((None, tq, kvw), lambda b, i: (b, i, COL_AK // kvw)),
                  pl.BlockSpec((None, blk, kvw), lambda b, i: (b, prev(i), COL_AK // kvw)),
                  pl.BlockSpec((None, tq, kvw), lambda b, i: (b, i, COL_AV // kvw)),
                  pl.BlockSpec((None, blk, kvw), lambda b, i: (b, prev(i), COL_AV // kvw)),
                  pl.BlockSpec(bias_block, lambda b, i: (jnp.minimum(i, 1), 0, 0, 0)),
                  pl.BlockSpec(bias_block, lambda b, i: (1, 0, 0, 0)),
                  pl.BlockSpec((1, kvw), lambda b, i: (0, 0)),
                  pl.BlockSpec((1, kvw), lambda b, i: (0, 0)),
                  pl.BlockSpec(sink_rows.shape, lambda b, i: (0, 0, 0, 0)),
                  pl.BlockSpec(seg.shape, lambda b, i: (0, 0))] + cast_specs,
        out_specs=[pl.BlockSpec((None, tq, ATTN_Q_W), lambda b, i: (b, i, 0))] + cast_specs,
        out_shape=[jax.ShapeDtypeStruct((B, S, ATTN_Q_W), BF16)]
        + [jax.ShapeDtypeStruct(cw.shape, BF16) for cw in cast_weights],
        compiler_params=_params("arbitrary", "arbitrary"),
        name="attn",
    )(proj3, proj3, proj3, proj3, proj3, biasm, biasm, jnp.tile(q_gain, (1, GQA_GROUP)),
      jnp.tile(k_gain, (1, N_KV_HEADS)), sink_rows, seg, *cast_weights)
    return outs[0], outs[1:]


def _split2(x):
    hi = x.astype(BF16)
    return hi, (x - hi.astype(F32)).astype(BF16)


def _gla_head(q, k, v, r, lr, wgk, bgk, gain, state_ref, masks, store):
    C = GLA_CHUNK
    tril, m_same, m_next, m_far, eye = masks
    z = _dot(lr, wgk) + bgk
    yield
    gl = (jnp.minimum(z, 0.0) - jnp.log(1.0 + jnp.exp(-jnp.abs(z)))) * (1.0 / GLA_NORMALIZER)
    hi, lo = _split2(gl)
    g = _dot(tril, hi) + _dot(tril, lo)
    yield
    t = [g[(c + 1) * C - 1:(c + 1) * C] for c in range(4)]
    t_rows = jnp.concatenate([jnp.broadcast_to(tc, (C, tc.shape[1])) for tc in t], axis=0)

    q_dec = q.astype(F32) * (GLA_DK ** -0.5) * jnp.exp(g)
    kf = k.astype(F32)
    k_inv = (kf * jnp.exp(-g)).astype(BF16)
    k_end = kf * jnp.exp(t_rows - g)
    qd = [q_dec[c * C:(c + 1) * C] for c in range(4)]
    ke = [k_end[c * C:(c + 1) * C] for c in range(4)]

    def rows(parts):
        return jnp.concatenate(parts, axis=0).astype(BF16)

    q_b, k_b = q_dec.astype(BF16), k_end.astype(BF16)
    q_far = rows([qd[0], qd[1], qd[2], qd[3] * jnp.exp(t[2])])
    k_far = rows([ke[0] * jnp.exp(t[1]), ke[1], ke[2], ke[3]])
    q_abs = rows([qd[0], qd[1] * jnp.exp(t[0]), qd[2] * jnp.exp(t[0] + t[1]), qd[3] * jnp.exp(t[0] + t[1] + t[2])])
    k_abs = rows([ke[0] * jnp.exp(t[1] + t[2] + t[3]), ke[1] * jnp.exp(t[2] + t[3]), ke[2] * jnp.exp(t[3]), ke[3]])

    a_same, a_next, a_far = _dot_nt(q_b, k_inv), _dot_nt(q_b, k_b), _dot_nt(q_far, k_far)
    state = state_ref[...]
    o_state = _dot(q_abs, state.astype(BF16))
    update = _dot_tn(k_abs, v)
    yield
    a = jnp.where(m_same, a_same, jnp.where(m_next, a_next, jnp.where(m_far, a_far, 0.0)))
    o = _dot(a.astype(BF16), v) + o_state
    yield

    decay = jnp.exp(t[0] + t[1] + t[2] + t[3])
    decay_col = jnp.sum(jnp.where(eye, jnp.broadcast_to(decay, eye.shape), 0.0), axis=1, keepdims=True)
    state_ref[...] = decay_col * state + update

    rf = r.astype(F32)
    store((_rms(o, gain) * (rf * jax.nn.sigmoid(rf))).astype(BF16))


def _gla_kernel(q_ref, k_ref, v_ref, r_ref, lr_ref, wgk_ref, bgk_ref, gain_ref, o_ref, state_ref):
    R, C, dk, dv = GLA_ROWS, GLA_CHUNK, GLA_DK, GLA_DV

    @pl.when(pl.program_id(2) == 0)
    def _():
        state_ref[...] = jnp.zeros_like(state_ref)

    row = lax.broadcasted_iota(jnp.int32, (R, R), 0)
    col = lax.broadcasted_iota(jnp.int32, (R, R), 1)
    ci, cj = row // C, col // C
    m_same = (row >= col) & (ci == cj)
    m_next = (ci == cj + 1) & (ci != 2)
    m_far = (ci >= 2) & (cj <= 1)
    eye = lax.broadcasted_iota(jnp.int32, (dk, dk), 0) == lax.broadcasted_iota(jnp.int32, (dk, dk), 1)
    masks = (m_same.astype(BF16), m_same, m_next, m_far, eye)

    heads = []
    for b in range(GLA_BATCH_PER_STEP):
        lr = lr_ref[b]
        for h in range(GLA_HEADS_PER_STEP):
            ks, vs = slice(h * dk, (h + 1) * dk), slice(h * dv, (h + 1) * dv)
            heads.append(_gla_head(q_ref[b, :, ks], k_ref[b, :, ks], v_ref[b, :, vs], r_ref[b, :, vs], lr,
                                   wgk_ref[:, ks], bgk_ref[:, ks], gain_ref[...], state_ref.at[b, h], masks,
                                   functools.partial(o_ref.__setitem__, (b, slice(None), vs))))
    for _ in itertools.zip_longest(*heads):
        pass


def _gla(proj3, wgk, bgk, gain):
    B, S, _ = proj3.shape
    R, hps, bps = GLA_ROWS, GLA_HEADS_PER_STEP, GLA_BATCH_PER_STEP
    dk, dv = hps * GLA_DK, hps * GLA_DV
    return pl.pallas_call(
        _gla_kernel,
        grid=(B // bps, GLA_HEADS // hps, S // R),
        in_specs=[pl.BlockSpec((bps, R, dk), lambda b, h, t: (b, t, COL_GQ // dk + h)),
                  pl.BlockSpec((bps, R, dk), lambda b, h, t: (b, t, COL_GK // dk + h)),
                  pl.BlockSpec((bps, R, dv), lambda b, h, t: (b, t, COL_GV // dv + h)),
                  pl.BlockSpec((bps, R, dv), lambda b, h, t: (b, t, COL_GR // dv + h)),
                  pl.BlockSpec((bps, R, LR_PAD), lambda b, h, t: (b, t, COL_LR // LR_PAD)),
                  pl.BlockSpec((LR_PAD, dk), lambda b, h, t: (0, h)),
                  pl.BlockSpec((1, dk), lambda b, h, t: (0, h)),
                  pl.BlockSpec((1, GLA_DV), lambda b, h, t: (0, 0))],
        out_specs=pl.BlockSpec((bps, R, dv), lambda b, h, t: (b, t, h)),
        out_shape=jax.ShapeDtypeStruct((B, S, GLA_V_W), BF16),
        scratch_shapes=[pltpu.VMEM((bps, hps, GLA_DK, GLA_DV), F32)],
        compiler_params=_params("parallel", "parallel", "arbitrary"),
        name="gla",
    )(proj3, proj3, proj3, proj3, proj3, wgk, bgk, gain)


def _merge_kernel(ya_ref, yg_ref, ga_ref, gb_ref, x_ref, mod_ref, g2_ref, wa_ref, wg_ref, wo_ref,
                  x1_ref, h2_ref):
    def merged_branches(rows):
        ma, mg = _dot(ya_ref[rows, :], wa_ref[...]), _dot(yg_ref[rows, :], wg_ref[...])
        ga = jax.nn.sigmoid(ga_ref[rows, :].astype(F32))
        gb = jax.nn.sigmoid(gb_ref[rows, :].astype(F32))
        return (ga * ma + gb * mg).astype(BF16)

    def project(rows, merged):
        x1 = x_ref[rows, :] + mod_ref[2:3, :] * _dot(merged, wo_ref[...])
        x1_ref[rows, :] = x1
        h2_ref[rows, :] = (_rms(x1, g2_ref[...]) * (1.0 + mod_ref[4:5, :]) + mod_ref[3:4, :]).astype(BF16)

    for r0 in range(0, MERGE_TM, MERGE_SUB):
        rows = slice(r0, r0 + MERGE_SUB)
        project(rows, merged_branches(rows))


def _merge(ya, yg, proj, x2, mod3, gain2, wa, wg, wo):
    T, D = x2.shape
    S = T // mod3.shape[0]
    tm = MERGE_TM
    once = pl.Buffered(1)
    return pl.pallas_call(
        _merge_kernel,
        grid=(T // tm,),
        in_specs=[pl.BlockSpec((tm, ATTN_Q_W), lambda i: (i, 0)),
                  pl.BlockSpec((tm, GLA_V_W), lambda i: (i, 0)),
                  pl.BlockSpec((tm, D), lambda i: (i, COL_GA // D)),
                  pl.BlockSpec((tm, D), lambda i: (i, COL_GB // D)),
                  pl.BlockSpec((tm, D), lambda i: (i, 0)),
                  pl.BlockSpec((None, 6, D), lambda i: (i // (S // tm), 0, 0)),
                  pl.BlockSpec((1, D), lambda i: (0, 0)),
                  pl.BlockSpec((ATTN_Q_W, D), lambda i: (0, 0), pipeline_mode=once),
                  pl.BlockSpec((GLA_V_W, D), lambda i: (0, 0), pipeline_mode=once),
                  pl.BlockSpec((D, D), lambda i: (0, 0), pipeline_mode=once)],
        out_specs=[pl.BlockSpec((tm, D), lambda i: (i, 0)),
                   pl.BlockSpec((tm, D), lambda i: (i, 0))],
        out_shape=[jax.ShapeDtypeStruct((T, D), F32), jax.ShapeDtypeStruct((T, D), BF16)],
        compiler_params=_params("parallel"),
        name="merge",
    )(ya, yg, proj, proj, x2, mod3, gain2, wa, wg, wo)


def _ffn_up_kernel(h_ref, halo_ref, wa_ref, wb_ref, cwa_ref, cwb_ref, cba_ref, cbb_ref, o_ref,
                   hs_ref, ua_ref, ub_ref, *, tiles_per_seq):
    tm, halo = FFN_TM, FFN_HALO

    @pl.when(pl.program_id(1) == 0)
    def _():
        first = (pl.program_id(0) % tiles_per_seq) == 0
        hs_ref[0:halo, :] = jnp.where(first, jnp.zeros_like(halo_ref), halo_ref[...])
        hs_ref[halo:, :] = h_ref[...]

    def conv(u_ref, cw_ref, cb_ref, r0, n):
        u = u_ref[halo + r0 - 8:halo + r0 + n, :]
        y = cb_ref[...] + cw_ref[0:1, :] * pltpu.roll(u, 2, axis=0)[8:]
        y = y + cw_ref[1:2, :] * pltpu.roll(u, 1, axis=0)[8:]
        return y + cw_ref[2:3, :] * u[8:]

    ua_ref[...] = _dot(hs_ref[...], wa_ref[...])
    ub_ref[...] = _dot(hs_ref[...], wb_ref[...])
    for r0 in range(0, tm, FFN_EPI_ROWS):
        ya = conv(ua_ref, cwa_ref, cba_ref, r0, FFN_EPI_ROWS)
        yb = conv(ub_ref, cwb_ref, cbb_ref, r0, FFN_EPI_ROWS)
        o_ref[r0:r0 + FFN_EPI_ROWS, :] = (ya * jax.nn.sigmoid(ya) * yb).astype(BF16)


def _ffn_up(h2, S, w_up, conv_w, conv_b):
    T, D = h2.shape
    tm, tn, halo = FFN_TM, FFN_TN, FFN_HALO
    nj = D_FF // tn
    return pl.pallas_call(
        functools.partial(_ffn_up_kernel, tiles_per_seq=S // tm),
        grid=(T // tm, D_FF // tn),
        in_specs=[pl.BlockSpec((tm, D), lambda i, j: (i, 0)),
                  pl.BlockSpec((halo, D), lambda i, j: (jnp.maximum(i * (tm // halo) - 1, 0), 0)),
                  pl.BlockSpec((D, tn), lambda i, j: (0, j)),
                  pl.BlockSpec((D, tn), lambda i, j: (0, j + nj)),
                  pl.BlockSpec((3, tn), lambda i, j: (0, j)),
                  pl.BlockSpec((3, tn), lambda i, j: (0, j + nj)),
                  pl.BlockSpec((1, tn), lambda i, j: (0, j)),
                  pl.BlockSpec((1, tn), lambda i, j: (0, j + nj))],
        out_specs=pl.BlockSpec((tm, tn), lambda i, j: (i, j)),
        out_shape=jax.ShapeDtypeStruct((T, D_FF), BF16),
        scratch_shapes=[pltpu.VMEM((tm + halo, D), BF16),
                        pltpu.VMEM((tm + halo, tn), F32),
                        pltpu.VMEM((tm + halo, tn), F32)],
        compiler_params=_params("parallel", "arbitrary"),
        name="ffn_up",
    )(h2, h2, w_up, w_up, conv_w, conv_w, conv_b, conv_b)


def _ffn_down_kernel(a_ref, w_ref, x_ref, mod_ref, o_ref):
    o_ref[...] = x_ref[...] + mod_ref[5:6, :] * _dot(a_ref[...], w_ref[...])


def _ffn_down(act, wd, x1, mod3):
    T, D = x1.shape
    S = T // mod3.shape[0]
    tm = DOWN_TM
    return pl.pallas_call(
        _ffn_down_kernel,
        grid=(T // tm,),
        in_specs=[pl.BlockSpec((tm, D_FF), lambda i: (i, 0)),
                  pl.BlockSpec((D_FF, D), lambda i: (0, 0), pipeline_mode=pl.Buffered(1)),
                  pl.BlockSpec((tm, D), lambda i: (i, 0)),
                  pl.BlockSpec((None, 6, D), lambda i: (i // (S // tm), 0, 0))],
        out_specs=pl.BlockSpec((tm, D), lambda i: (i, 0)),
        out_shape=jax.ShapeDtypeStruct((T, D), F32),
        compiler_params=_params("parallel"),
        name="ffn_down",
    )(act, wd, x1, mod3)


def _layer(x2, B, mod3, biasm, norm1_gain, w_in, q_norm_gain, k_norm_gain, attn_sinks, w_gk_up, b_gk,
           gla_norm_gain, w_branch_attn, w_branch_gla, w_out, norm2_gain, w_ffn_up, ffn_conv_w,
           ffn_conv_b, w_ffn_down):
    T, D = x2.shape
    S = T // B
    w_p = _wpack(w_in.T)
    wgk = jnp.concatenate([w_gk_up, jnp.zeros((LR_PAD - GLA_LOWRANK, GLA_K_W), w_gk_up.dtype)],
                          axis=0).astype(BF16)

    proj = _inproj(x2, mod3, norm1_gain.reshape(1, D), w_p)
    proj3 = proj.reshape(B, S, PROJ_W)
    ya, (w_ba, w_bg, w_o, w_up, w_down) = _attn(
        proj3, biasm, q_norm_gain.reshape(1, HEAD_DIM), k_norm_gain.reshape(1, HEAD_DIM), attn_sinks,
        (w_branch_attn, w_branch_gla, w_out, w_ffn_up, w_ffn_down))
    yg = _gla(proj3, wgk, b_gk.reshape(1, GLA_K_W), gla_norm_gain.reshape(1, GLA_DV))
    x1, h2 = _merge(ya.reshape(T, ATTN_Q_W), yg.reshape(T, GLA_V_W), proj, x2, mod3,
                    norm2_gain.reshape(1, D), w_ba, w_bg, w_o)
    act = _ffn_up(h2, S, w_up, ffn_conv_w, ffn_conv_b.reshape(1, 2 * D_FF))
    return _ffn_down(act, w_down, x1, mod3)


def kernel(x, c, rel_bias_table, w_ada, b_ada, norm1_gain, w_in, q_norm_gain, k_norm_gain, attn_sinks,
           w_gk_up, b_gk, gla_norm_gain, w_branch_attn, w_branch_gla, w_out, norm2_gain, w_ffn_up,
           ffn_conv_w, ffn_conv_b, w_ffn_down):
    B, S, D = x.shape
    depth = w_in.shape[0]
    assert D == D_MODEL and w_in.shape[1:] == (D_MODEL, SRC_GA + 2 * D_MODEL) and w_ffn_up.shape[2] == 2 * D_FF
    assert S % max(FFN_TM, INPROJ_TM, ATTN_TQ, GLA_ROWS, MERGE_TM, DOWN_TM) == 0
    biasm = _relbias(rel_bias_table)
    x2 = x.reshape(B * S, D)
    for l in range(depth):
        mod3 = _adaln(c, w_ada[l], b_ada[l]).reshape(B, 6, D)
        x2 = _layer(x2, B, mod3, biasm, norm1_gain[l], w_in[l], q_norm_gain[l], k_norm_gain[l],
                    attn_sinks[l], w_gk_up[l], b_gk[l], gla_norm_gain[l], w_branch_attn[l],
                    w_branch_gla[l], w_out[l], norm2_gain[l], w_ffn_up[l], ffn_conv_w[l], ffn_conv_b[l],
                    w_ffn_down[l])
    return x2.reshape(B, S, D)
```

```python
import functools
import itertools
import math

import numpy as np
import jax
import jax.numpy as jnp
from jax import lax
from jax.experimental import pallas as pl
from jax.experimental.pallas import tpu as pltpu

F32 = jnp.float32
BF16 = jnp.bfloat16

D_MODEL = 2048
N_Q_HEADS = 16
N_KV_HEADS = 4
GQA_GROUP = N_Q_HEADS // N_KV_HEADS
HEAD_DIM = 64
WINDOW = 128
ATTN_BLOCK = 128
N_BUCKETS = 32
MAX_DISTANCE = 128
GLA_HEADS = 4
GLA_DK = 256
GLA_DV = 512
GLA_LOWRANK = 16
GLA_NORMALIZER = 16.0
GLA_CHUNK = 64
D_FF = 5632
EPS = 1e-6
NEG_INF = -1e30
LOG2E = math.log2(math.e)

ATTN_Q_W = N_Q_HEADS * HEAD_DIM
ATTN_KV_W = N_KV_HEADS * HEAD_DIM
GLA_K_W = GLA_HEADS * GLA_DK
GLA_V_W = GLA_HEADS * GLA_DV

COL_GA = 0
COL_GB = COL_GA + D_MODEL
COL_GV = COL_GB + D_MODEL
COL_GR = COL_GV + GLA_V_W
COL_GQ = COL_GR + GLA_V_W
COL_GK = COL_GQ + GLA_K_W
COL_AQ = COL_GK + GLA_K_W
COL_AK = COL_AQ + ATTN_Q_W
COL_AV = COL_AK + ATTN_KV_W
COL_LR = COL_AV + ATTN_KV_W
LANE = 128
BF16_ROWS = 16
VMEM_LIMIT = 60 * 1024 * 1024

LR_PAD = LANE
PROJ_W = 12288

ADALN_TN = 1024
INPROJ_TM, INPROJ_TN = 1024, 3072
INPROJ_NORM_ROWS = 256
ATTN_TQ = 512
GLA_ROWS = 256
GLA_HEADS_PER_STEP = 4
GLA_BATCH_PER_STEP = 2
MERGE_TM, MERGE_SUB = 512, 256
FFN_TM, FFN_TN = 1024, 512
FFN_HALO = BF16_ROWS
FFN_EPI_ROWS = 256
DOWN_TM = 512


def _params(*sem):
    return pltpu.CompilerParams(dimension_semantics=sem, vmem_limit_bytes=VMEM_LIMIT)


def _dot(a, b):
    return jnp.dot(a, b, preferred_element_type=F32)


def _dot_nt(a, b):
    return lax.dot_general(a, b, (((1,), (1,)), ((), ())), preferred_element_type=F32)


def _dot_tn(a, b):
    return lax.dot_general(a, b, (((0,), (0,)), ((), ())), preferred_element_type=F32)


def _rms(x, gain):
    return x * lax.rsqrt(jnp.mean(x * x, axis=-1, keepdims=True) + EPS) * gain


def _adaln_kernel(c_ref, w_ref, b_ref, o_ref):
    c = c_ref[...]
    ca = c * jax.nn.sigmoid(c)
    o_ref[...] = _dot(ca.astype(BF16), w_ref[...].astype(BF16)) + b_ref[...]


def _adaln(c, w_ada, b_ada):
    B, D = c.shape
    N = w_ada.shape[1]
    return pl.pallas_call(
        _adaln_kernel,
        grid=(N // ADALN_TN,),
        in_specs=[pl.BlockSpec((B, D), lambda j: (0, 0)),
                  pl.BlockSpec((D, ADALN_TN), lambda j: (0, j)),
                  pl.BlockSpec((1, ADALN_TN), lambda j: (0, j))],
        out_specs=pl.BlockSpec((B, ADALN_TN), lambda j: (0, j)),
        out_shape=jax.ShapeDtypeStruct((B, N), F32),
        compiler_params=_params("parallel"),
        name="adaln",
    )(c, w_ada, b_ada.reshape(1, N))


def _bucket_table():
    j = np.arange(2 * ATTN_BLOCK)[:, None]
    i = np.arange(ATTN_BLOCK)[None, :]
    dist = i + ATTN_BLOCK - j
    max_exact = N_BUCKETS // 2
    d = np.maximum(dist, 0)
    ratio = np.log(np.maximum(d, 1).astype(np.float32) / np.float32(max_exact)) / np.float32(
        math.log(MAX_DISTANCE / max_exact))
    large = max_exact + (ratio.astype(np.float32) * np.float32(N_BUCKETS - max_exact)).astype(np.int32)
    large = np.minimum(large, N_BUCKETS - 1)
    bucket = np.where(d < max_exact, d, large)
    in_window = (dist >= 0) & (dist < WINDOW)
    return np.where(in_window, bucket, -1).astype(np.int32)


def _relbias_kernel(tab_ref, bkt_ref, o_ref):
    g = pl.program_id(0)
    bkt = bkt_ref[...]
    nk, nq = bkt.shape
    key = lax.broadcasted_iota(jnp.int32, bkt.shape, 0)
    for e in range(2):
        for p in range(2):
            h = g * GQA_GROUP + 2 * p + e
            acc = jnp.zeros(bkt.shape, F32)
            for b in range(N_BUCKETS):
                acc = jnp.where(bkt == b, tab_ref[b, h], acc)
            regular = jnp.where(bkt >= 0, acc * LOG2E, NEG_INF)
            tile = (slice(e * nk, (e + 1) * nk), slice(p * nq, (p + 1) * nq))
            o_ref[(0,) + tile] = jnp.where(key >= ATTN_BLOCK, regular, NEG_INF)
            o_ref[(1,) + tile] = regular


def _relbias(rel_bias_table):
    bkt = jnp.asarray(_bucket_table())
    nk, nq = bkt.shape
    return pl.pallas_call(
        _relbias_kernel,
        grid=(N_KV_HEADS,),
        in_specs=[pl.BlockSpec(memory_space=pltpu.SMEM),
                  pl.BlockSpec(bkt.shape, lambda g: (0, 0))],
        out_specs=pl.BlockSpec((2, None, 2 * nk, 2 * nq), lambda g: (0, g, 0, 0)),
        out_shape=jax.ShapeDtypeStruct((2, N_KV_HEADS, 2 * nk, 2 * nq), F32),
        compiler_params=_params("parallel"),
        name="relbias",
    )(rel_bias_table, bkt)


SRC_GQ = ATTN_Q_W + 2 * ATTN_KV_W
SRC_GV = SRC_GQ + 2 * GLA_K_W
SRC_LR = SRC_GV + 2 * GLA_V_W
SRC_GA = SRC_LR + GLA_LOWRANK
WPACK_ROWS = 512


def _wpack_src_row(b):
    r = b * WPACK_ROWS
    src = jnp.where(r < COL_GV, SRC_GA + r,
                    jnp.where(r < COL_GQ, SRC_GV + (r - COL_GV),
                              jnp.where(r < COL_AQ, SRC_GQ + (r - COL_GQ),
                                        jnp.where(r < COL_LR, r - COL_AQ, SRC_LR))))
    return pl.multiple_of(src, 8)


def _wpack_kernel(w_ref, o_ref):
    r = pl.program_id(0) * WPACK_ROWS
    nvalid = jnp.where(r < COL_LR, WPACK_ROWS, jnp.where(r == COL_LR, GLA_LOWRANK, 0))
    row = lax.broadcasted_iota(jnp.int32, w_ref.shape, 0)
    o_ref[...] = jnp.where(row < nvalid, w_ref[...], 0.0).astype(BF16)


def _wpack(w_in_t):
    _, D = w_in_t.shape
    return pl.pallas_call(
        _wpack_kernel,
        grid=(PROJ_W // WPACK_ROWS,),
        in_specs=[pl.BlockSpec((pl.Element(WPACK_ROWS), pl.Element(D)), lambda b: (_wpack_src_row(b), 0))],
        out_specs=pl.BlockSpec((WPACK_ROWS, D), lambda b: (b, 0)),
        out_shape=jax.ShapeDtypeStruct((PROJ_W, D), BF16),
        compiler_params=_params("parallel"),
        name="wpack",
    )(w_in_t)


def _norm_modulate(x, mod_ref, gain):
    return (_rms(x, gain) * (1.0 + mod_ref[1:2, :]) + mod_ref[0:1, :]).astype(BF16)


def _hnorm_kernel(x_ref, mod_ref, g_ref, o_ref):
    o_ref[...] = _norm_modulate(x_ref[...], mod_ref, g_ref[...])


def _hnorm_first(x2, mod3, gain):
    D = x2.shape[1]
    nr = INPROJ_NORM_ROWS
    return pl.pallas_call(
        _hnorm_kernel,
        grid=(INPROJ_TM // nr,),
        in_specs=[pl.BlockSpec((nr, D), lambda r: (r, 0)),
                  pl.BlockSpec((None, 6, D), lambda r: (0, 0, 0)),
                  pl.BlockSpec((1, D), lambda r: (0, 0))],
        out_specs=pl.BlockSpec((nr, D), lambda r: (r, 0)),
        out_shape=jax.ShapeDtypeStruct((INPROJ_TM, D), BF16),
        compiler_params=_params("parallel"),
        name="hnorm_first",
    )(x2, mod3, gain)


def _cast_block(shape, steps):
    rows, cols = shape
    bc = min(cols, 1024)
    assert cols % bc == 0
    for br in range(BF16_ROWS, rows + 1, BF16_ROWS):
        if rows % br == 0 and (rows // br) * (cols // bc) <= steps:
            return br, bc
    raise ValueError(f"no cast block for {shape} in {steps} steps")


def _cast_specs(weights, n0, n1):
    def spec(shape):
        br, bc = _cast_block(shape, n0 * n1)
        ncb = shape[1] // bc
        last = (shape[0] // br) * ncb - 1

        def index(i, j):
            t = jnp.minimum(i * n1 + j, last)
            return t // ncb, t % ncb

        return pl.BlockSpec((br, bc), index)

    return [spec(w.shape) for w in weights]


def _inproj_kernel(xn_ref, modn_ref, g_ref, w_ref, h0_ref, o_ref, ha_ref, hb_ref):
    i, j = pl.program_id(0), pl.program_id(1)
    nr = INPROJ_NORM_ROWS

    @pl.when((i == 0) & (j == 0))
    def _():
        ha_ref[...] = h0_ref[...]

    def step(cur_ref, nxt_ref):
        o_ref[...] = _dot_nt(cur_ref[...], w_ref[...]).astype(BF16)
        chunk = jnp.minimum(j, INPROJ_TM // nr - 1)
        rows = pl.ds(pl.multiple_of(chunk * nr, nr), nr)
        nxt_ref[rows, :] = _norm_modulate(xn_ref[...], modn_ref, g_ref[...])

    @pl.when(i % 2 == 0)
    def _():
        step(ha_ref, hb_ref)

    @pl.when(i % 2 == 1)
    def _():
        step(hb_ref, ha_ref)


def _inproj(x2, mod3, gain, w):
    T, D = x2.shape
    S = T // mod3.shape[0]
    tm, tn, nr = INPROJ_TM, INPROJ_TN, INPROJ_NORM_ROWS
    n_i, chunks = T // tm, tm // nr
    assert PROJ_W // tn >= chunks

    def next_tile(i):
        return jnp.minimum(i + 1, n_i - 1)

    return pl.pallas_call(
        _inproj_kernel,
        grid=(n_i, PROJ_W // tn),
        in_specs=[pl.BlockSpec((nr, D), lambda i, j: (next_tile(i) * chunks + jnp.minimum(j, chunks - 1), 0)),
                  pl.BlockSpec((None, 6, D), lambda i, j: (next_tile(i) // (S // tm), 0, 0)),
                  pl.BlockSpec((1, D), lambda i, j: (0, 0)),
                  pl.BlockSpec((tn, D), lambda i, j: (j, 0)),
                  pl.BlockSpec((tm, D), lambda i, j: (0, 0), pipeline_mode=pl.Buffered(1))],
        out_specs=pl.BlockSpec((tm, tn), lambda i, j: (i, j)),
        out_shape=jax.ShapeDtypeStruct((T, PROJ_W), BF16),
        scratch_shapes=[pltpu.VMEM((tm, D), BF16), pltpu.VMEM((tm, D), BF16)],
        compiler_params=_params("arbitrary", "arbitrary"),
        name="inproj",
    )(x2, mod3, gain, w, _hnorm_first(x2, mod3, gain))


def _attn_kernel(q_ref, kc_ref, kp_ref, vc_ref, vp_ref, b0_ref, b1_ref, qg_ref, kg_ref, sink_ref, seg_ref,
                 *rest, n_cast):
    cast_in, o_ref, cast_out = rest[:n_cast], rest[n_cast], rest[n_cast + 1:]
    for src_ref, dst_ref in zip(cast_in, cast_out):
        dst_ref[...] = src_ref[...].astype(BF16)

    nsub = ATTN_TQ // ATTN_BLOCK
    blk, dh = ATTN_BLOCK, HEAD_DIM
    seg = seg_ref[...]

    def normed(x, gain):
        ssq = _dot((x * x).astype(BF16), seg)
        return x * lax.rsqrt(ssq * (1.0 / dh) + EPS) * gain

    kn = normed(jnp.concatenate([kp_ref[...], kc_ref[...]], axis=0).astype(F32), kg_ref[...])
    vt = jnp.concatenate([vp_ref[...], vc_ref[...]], axis=0).astype(F32).T.astype(BF16)
    lane = lax.broadcasted_iota(jnp.int32, (kn.shape[0], 2 * dh), 1)
    qgain = qg_ref[...] * (dh ** -0.5 * LOG2E)

    kzs, qns = [], []
    for g in range(N_KV_HEADS):
        pair = kn[:, (g // 2) * 2 * dh:(g // 2 + 1) * 2 * dh]
        own = jnp.where(lane >= dh if g % 2 else lane < dh, pair, 0.0)
        swapped = pltpu.roll(own, dh, axis=1)
        kz = [own, swapped] if g % 2 == 0 else [swapped, own]
        kzs.append([z.astype(BF16) for z in kz])
        qns.append(normed(q_ref[:, g * 4 * dh:(g + 1) * 4 * dh].astype(F32), qgain).astype(BF16))

    def scores(g, sb):
        bref = b0_ref if sb == 0 else b1_ref
        band = slice(sb * blk, sb * blk + 2 * blk)
        rows = slice(sb * blk, (sb + 1) * blk)
        kband = jnp.concatenate([kzs[g][0][band], kzs[g][1][band]], axis=0)
        qr = jnp.concatenate([qns[g][rows, 0:2 * dh], qns[g][rows, 2 * dh:4 * dh]], axis=0)
        return _dot_nt(kband, qr) + bref[g]

    def finish(g, sb, s):
        band = slice(sb * blk, sb * blk + 2 * blk)
        rows = slice(sb * blk, (sb + 1) * blk)
        ps, inv = [], []
        for e in range(2):
            se = s[e * 2 * blk:(e + 1) * 2 * blk]
            sink = sink_ref[g, e] * LOG2E
            m = jnp.maximum(jnp.max(se, axis=0, keepdims=True), sink)
            p = jnp.exp2(se - m)
            inv.append(1.0 / (jnp.sum(p, axis=0, keepdims=True) + jnp.exp2(sink - m)))
            ps.append(p.astype(BF16))
        ot = _dot(vt[g * dh:(g + 1) * dh, band], jnp.concatenate(ps, axis=1))
        ot = ot * jnp.concatenate(inv, axis=1)
        o = jnp.concatenate([ot[:, 0:2 * blk], ot[:, 2 * blk:4 * blk]], axis=0).T
        o_ref[rows, (2 * g) * 2 * dh:(2 * g + 1) * 2 * dh] = o[0:blk].astype(BF16)
        o_ref[rows, (2 * g + 1) * 2 * dh:(2 * g + 2) * 2 * dh] = o[blk:2 * blk].astype(BF16)

    units = [(g, sb) for g in range(N_KV_HEADS) for sb in range(nsub)]
    s_next = scores(*units[0])
    for n, unit in enumerate(units):
        s_cur = s_next
        if n + 1 < len(units):
            s_next = scores(*units[n + 1])
        finish(*unit, s_cur)


def _attn(proj3, biasm, q_gain, k_gain, sinks, cast_weights):
    B, S, _ = proj3.shape
    tq, blk = ATTN_TQ, ATTN_BLOCK
    nsub = tq // blk
    kvw = ATTN_KV_W
    seg_id = np.arange(kvw) // HEAD_DIM
    seg = jnp.asarray(seg_id[:, None] == seg_id[None, :], dtype=BF16)
    sink_rows = jnp.repeat(sinks.reshape(N_KV_HEADS, 2, 2).transpose(0, 2, 1), blk, axis=-1)
    sink_rows = sink_rows.reshape(N_KV_HEADS, 2, 1, 2 * blk)

    def prev(i):
        return jnp.maximum(i * nsub - 1, 0)

    bias_block = (None,) + biasm.shape[1:]
    cast_specs = _cast_specs(cast_weights, B, S // tq)
    outs = pl.pallas_call(
        functools.partial(_attn_kernel, n_cast=len(cast_weights)),
        grid=(B, S // tq),
        in_specs=[pl.BlockSpec((None, tq, ATTN_Q_W), lambda b, i: (b, i, COL_AQ // ATTN_Q_W)),
                  pl.BlockSpec((None, tq, kvw), lambda b, i: (b, i, COL_AK // kvw)),
                  pl.BlockSpec((None, blk, kvw), lambda b, i: (b, prev(i), COL_AK // kvw)),
                  pl.BlockSpec((None, tq, kvw), lambda b, i: (b, i, COL_AV // kvw)),
                  pl.BlockSpec((None, blk, kvw), lambda b, i: (b, prev(i), COL_AV // kvw)),
                  pl.BlockSpec(bias_block, lambda b, i: (jnp.minimum(i, 1), 0, 0, 0)),
                  pl.BlockSpec(bias_block, lambda b, i: (1, 0, 0, 0)),
                  pl.BlockSpec((1, kvw), lambda b, i: (0, 0)),
                  pl.BlockSpec((1, kvw), lambda b, i: (0, 0)),
                  pl.BlockSpec(sink_rows.shape, lambda b, i: (0, 0, 0, 0)),
                  pl.BlockSpec(seg.shape, lambda b, i: (0, 0))] + cast_specs,
        out_specs=[pl.BlockSpec((None, tq, ATTN_Q_W), lambda b, i: (b, i, 0))] + cast_specs,
        out_shape=[jax.ShapeDtypeStruct((B, S, ATTN_Q_W), BF16)]
        + [jax.ShapeDtypeStruct(cw.shape, BF16) for cw in cast_weights],
        compiler_params=_params("arbitrary", "arbitrary"),
        name="attn",
    )(proj3, proj3, proj3, proj3, proj3, biasm, biasm, jnp.tile(q_gain, (1, GQA_GROUP)),
      jnp.tile(k_gain, (1, N_KV_HEADS)), sink_rows, seg, *cast_weights)
    return outs[0], outs[1:]


def _split2(x):
    hi = x.astype(BF16)
    return hi, (x - hi.astype(F32)).astype(BF16)


def _gla_head(q, k, v, r, lr, wgk, bgk, gain, state_ref, masks, store):
    C = GLA_CHUNK
    tril, m_same, m_next, m_far, eye = masks
    z = _dot(lr, wgk) + bgk
    yield
    gl = (jnp.minimum(z, 0.0) - jnp.log(1.0 + jnp.exp(-jnp.abs(z)))) * (1.0 / GLA_NORMALIZER)
    hi, lo = _split2(gl)
    g = _dot(tril, hi) + _dot(tril, lo)
    yield
    t = [g[(c + 1) * C - 1:(c + 1) * C] for c in range(4)]
    t_rows = jnp.concatenate([jnp.broadcast_to(tc, (C, tc.shape[1])) for tc in t], axis=0)

    q_dec = q.astype(F32) * (GLA_DK ** -0.5) * jnp.exp(g)
    kf = k.astype(F32)
    k_inv = (kf * jnp.exp(-g)).astype(BF16)
    k_end = kf * jnp.exp(t_rows - g)
    qd = [q_dec[c * C:(c + 1) * C] for c in range(4)]
    ke = [k_end[c * C:(c + 1) * C] for c in range(4)]

    def rows(parts):
        return jnp.concatenate(parts, axis=0).astype(BF16)

    q_b, k_b = q_dec.astype(BF16), k_end.astype(BF16)
    q_far = rows([qd[0], qd[1], qd[2], qd[3] * jnp.exp(t[2])])
    k_far = rows([ke[0] * jnp.exp(t[1]), ke[1], ke[2], ke[3]])
    q_abs = rows([qd[0], qd[1] * jnp.exp(t[0]), qd[2] * jnp.exp(t[0] + t[1]), qd[3] * jnp.exp(t[0] + t[1] + t[2])])
    k_abs = rows([ke[0] * jnp.exp(t[1] + t[2] + t[3]), ke[1] * jnp.exp(t[2] + t[3]), ke[2] * jnp.exp(t[3]), ke[3]])

    a_same, a_next, a_far = _dot_nt(q_b, k_inv), _dot_nt(q_b, k_b), _dot_nt(q_far, k_far)
    state = state_ref[...]
    o_state = _dot(q_abs, state.astype(BF16))
    update = _dot_tn(k_abs, v)
    yield
    a = jnp.where(m_same, a_same, jnp.where(m_next, a_next, jnp.where(m_far, a_far, 0.0)))
    o = _dot(a.astype(BF16), v) + o_state
    yield

    decay = jnp.exp(t[0] + t[1] + t[2] + t[3])
    decay_col = jnp.sum(jnp.where(eye, jnp.broadcast_to(decay, eye.shape), 0.0), axis=1, keepdims=True)
    state_ref[...] = decay_col * state + update

    rf = r.astype(F32)
    store((_rms(o, gain) * (rf * jax.nn.sigmoid(rf))).astype(BF16))


def _gla_kernel(q_ref, k_ref, v_ref, r_ref, lr_ref, wgk_ref, bgk_ref, gain_ref, o_ref, state_ref):
    R, C, dk, dv = GLA_ROWS, GLA_CHUNK, GLA_DK, GLA_DV

    @pl.when(pl.program_id(2) == 0)
    def _():
        state_ref[...] = jnp.zeros_like(state_ref)

    row = lax.broadcasted_iota(jnp.int32, (R, R), 0)
    col = lax.broadcasted_iota(jnp.int32, (R, R), 1)
    ci, cj = row // C, col // C
    m_same = (row >= col) & (ci == cj)
    m_next = (ci == cj + 1) & (ci != 2)
    m_far = (ci >= 2) & (cj <= 1)
    eye = lax.broadcasted_iota(jnp.int32, (dk, dk), 0) == lax.broadcasted_iota(jnp.int32, (dk, dk), 1)
    masks = (m_same.astype(BF16), m_same, m_next, m_far, eye)

    heads = []
    for b in range(GLA_BATCH_PER_STEP):
        lr = lr_ref[b]
        for h in range(GLA_HEADS_PER_STEP):
            ks, vs = slice(h * dk, (h + 1) * dk), slice(h * dv, (h + 1) * dv)
            heads.append(_gla_head(q_ref[b, :, ks], k_ref[b, :, ks], v_ref[b, :, vs], r_ref[b, :, vs], lr,
                                   wgk_ref[:, ks], bgk_ref[:, ks], gain_ref[...], state_ref.at[b, h], masks,
                                   functools.partial(o_ref.__setitem__, (b, slice(None), vs))))
    for _ in itertools.zip_longest(*heads):
        pass


def _gla(proj3, wgk, bgk, gain):
    B, S, _ = proj3.shape
    R, hps, bps = GLA_ROWS, GLA_HEADS_PER_STEP, GLA_BATCH_PER_STEP
    dk, dv = hps * GLA_DK, hps * GLA_DV
    return pl.pallas_call(
        _gla_kernel,
        grid=(B // bps, GLA_HEADS // hps, S // R),
        in_specs=[pl.BlockSpec((bps, R, dk), lambda b, h, t: (b, t, COL_GQ // dk + h)),
                  pl.BlockSpec((bps, R, dk), lambda b, h, t: (b, t, COL_GK // dk + h)),
                  pl.BlockSpec((bps, R, dv), lambda b, h, t: (b, t, COL_GV // dv + h)),
                  pl.BlockSpec((bps, R, dv), lambda b, h, t: (b, t, COL_GR // dv + h)),
                  pl.BlockSpec((bps, R, LR_PAD), lambda b, h, t: (b, t, COL_LR // LR_PAD)),
                  pl.BlockSpec((LR_PAD, dk), lambda b, h, t: (0, h)),
                  pl.BlockSpec((1, dk), lambda b, h, t: (0, h)),
                  pl.BlockSpec((1, GLA_DV), lambda b, h, t: (0, 0))],
        out_specs=pl.BlockSpec((bps, R, dv), lambda b, h, t: (b, t, h)),
        out_shape=jax.ShapeDtypeStruct((B, S, GLA_V_W), BF16),
        scratch_shapes=[pltpu.VMEM((bps, hps, GLA_DK, GLA_DV), F32)],
        compiler_params=_params("parallel", "parallel", "arbitrary"),
        name="gla",
    )(proj3, proj3, proj3, proj3, proj3, wgk, bgk, gain)


def _merge_kernel(ya_ref, yg_ref, ga_ref, gb_ref, x_ref, mod_ref, g2_ref, wa_ref, wg_ref, wo_ref,
                  x1_ref, h2_ref):
    def merged_branches(rows):
        ma, mg = _dot(ya_ref[rows, :], wa_ref[...]), _dot(yg_ref[rows, :], wg_ref[...])
        ga = jax.nn.sigmoid(ga_ref[rows, :].astype(F32))
        gb = jax.nn.sigmoid(gb_ref[rows, :].astype(F32))
        return (ga * ma + gb * mg).astype(BF16)

    def project(rows, merged):
        x1 = x_ref[rows, :] + mod_ref[2:3, :] * _dot(merged, wo_ref[...])
        x1_ref[rows, :] = x1
        h2_ref[rows, :] = (_rms(x1, g2_ref[...]) * (1.0 + mod_ref[4:5, :]) + mod_ref[3:4, :]).astype(BF16)

    for r0 in range(0, MERGE_TM, MERGE_SUB):
        rows = slice(r0, r0 + MERGE_SUB)
        project(rows, merged_branches(rows))


def _merge(ya, yg, proj, x2, mod3, gain2, wa, wg, wo):
    T, D = x2.shape
    S = T // mod3.shape[0]
    tm = MERGE_TM
    once = pl.Buffered(1)
    return pl.pallas_call(
        _merge_kernel,
        grid=(T // tm,),
        in_specs=[pl.BlockSpec((tm, ATTN_Q_W), lambda i: (i, 0)),
                  pl.BlockSpec((tm, GLA_V_W), lambda i: (i, 0)),
                  pl.BlockSpec((tm, D), lambda i: (i, COL_GA // D)),
                  pl.BlockSpec((tm, D), lambda i: (i, COL_GB // D)),
                  pl.BlockSpec((tm, D), lambda i: (i, 0)),
                  pl.BlockSpec((None, 6, D), lambda i: (i // (S // tm), 0, 0)),
                  pl.BlockSpec((1, D), lambda i: (0, 0)),
                  pl.BlockSpec((ATTN_Q_W, D), lambda i: (0, 0), pipeline_mode=once),
                  pl.BlockSpec((GLA_V_W, D), lambda i: (0, 0), pipeline_mode=once),
                  pl.BlockSpec((D, D), lambda i: (0, 0), pipeline_mode=once)],
        out_specs=[pl.BlockSpec((tm, D), lambda i: (i, 0)),
                   pl.BlockSpec((tm, D), lambda i: (i, 0))],
        out_shape=[jax.ShapeDtypeStruct((T, D), F32), jax.ShapeDtypeStruct((T, D), BF16)],
        compiler_params=_params("parallel"),
        name="merge",
    )(ya, yg, proj, proj, x2, mod3, gain2, wa, wg, wo)


def _ffn_up_kernel(h_ref, halo_ref, wa_ref, wb_ref, cwa_ref, cwb_ref, cba_ref, cbb_ref, o_ref,
                   hs_ref, ua_ref, ub_ref, *, tiles_per_seq):
    tm, halo = FFN_TM, FFN_HALO

    @pl.when(pl.program_id(1) == 0)
    def _():
        first = (pl.program_id(0) % tiles_per_seq) == 0
        hs_ref[0:halo, :] = jnp.where(first, jnp.zeros_like(halo_ref), halo_ref[...])
        hs_ref[halo:, :] = h_ref[...]

    def conv(u_ref, cw_ref, cb_ref, r0, n):
        u = u_ref[halo + r0 - 8:halo + r0 + n, :]
        y = cb_ref[...] + cw_ref[0:1, :] * pltpu.roll(u, 2, axis=0)[8:]
        y = y + cw_ref[1:2, :] * pltpu.roll(u, 1, axis=0)[8:]
        return y + cw_ref[2:3, :] * u[8:]

    ua_ref[...] = _dot(hs_ref[...], wa_ref[...])
    ub_ref[...] = _dot(hs_ref[...], wb_ref[...])
    for r0 in range(0, tm, FFN_EPI_ROWS):
        ya = conv(ua_ref, cwa_ref, cba_ref, r0, FFN_EPI_ROWS)
        yb = conv(ub_ref, cwb_ref, cbb_ref, r0, FFN_EPI_ROWS)
        o_ref[r0:r0 + FFN_EPI_ROWS, :] = (ya * jax.nn.sigmoid(ya) * yb).astype(BF16)


def _ffn_up(h2, S, w_up, conv_w, conv_b):
    T, D = h2.shape
    tm, tn, halo = FFN_TM, FFN_TN, FFN_HALO
    nj = D_FF // tn
    return pl.pallas_call(
        functools.partial(_ffn_up_kernel, tiles_per_seq=S // tm),
        grid=(T // tm, D_FF // tn),
        in_specs=[pl.BlockSpec((tm, D), lambda i, j: (i, 0)),
                  pl.BlockSpec((halo, D), lambda i, j: (jnp.maximum(i * (tm // halo) - 1, 0), 0)),
                  pl.BlockSpec((D, tn), lambda i, j: (0, j)),
                  pl.BlockSpec((D, tn), lambda i, j: (0, j + nj)),
                  pl.BlockSpec((3, tn), lambda i, j: (0, j)),
                  pl.BlockSpec((3, tn), lambda i, j: (0, j + nj)),
                  pl.BlockSpec((1, tn), lambda i, j: (0, j)),
                  pl.BlockSpec((1, tn), lambda i, j: (0, j + nj))],
        out_specs=pl.BlockSpec((tm, tn), lambda i, j: (i, j)),
        out_shape=jax.ShapeDtypeStruct((T, D_FF), BF16),
        scratch_shapes=[pltpu.VMEM((tm + halo, D), BF16),
                        pltpu.VMEM((tm + halo, tn), F32),
                        pltpu.VMEM((tm + halo, tn), F32)],
        compiler_params=_params("parallel", "arbitrary"),
        name="ffn_up",
    )(h2, h2, w_up, w_up, conv_w, conv_w, conv_b, conv_b)


def _ffn_down_kernel(a_ref, w_ref, x_ref, mod_ref, o_ref):
    o_ref[...] = x_ref[...] + mod_ref[5:6, :] * _dot(a_ref[...], w_ref[...])


def _ffn_down(act, wd, x1, mod3):
    T, D = x1.shape
    S = T // mod3.shape[0]
    tm = DOWN_TM
    return pl.pallas_call(
        _ffn_down_kernel,
        grid=(T // tm,),
        in_specs=[pl.BlockSpec((tm, D_FF), lambda i: (i, 0)),
                  pl.BlockSpec((D_FF, D), lambda i: (0, 0), pipeline_mode=pl.Buffered(1)),
                  pl.BlockSpec((tm, D), lambda i: (i, 0)),
                  pl.BlockSpec((None, 6, D), lambda i: (i // (S // tm), 0, 0))],
        out_specs=pl.BlockSpec((tm, D), lambda i: (i, 0)),
        out_shape=jax.ShapeDtypeStruct((T, D), F32),
        compiler_params=_params("parallel"),
        name="ffn_down",
    )(act, wd, x1, mod3)


def _layer(x2, B, mod3, biasm, norm1_gain, w_in, q_norm_gain, k_norm_gain, attn_sinks, w_gk_up, b_gk,
           gla_norm_gain, w_branch_attn, w_branch_gla, w_out, norm2_gain, w_ffn_up, ffn_conv_w,
           ffn_conv_b, w_ffn_down):
    T, D = x2.shape
    S = T // B
    w_p = _wpack(w_in.T)
    wgk = jnp.concatenate([w_gk_up, jnp.zeros((LR_PAD - GLA_LOWRANK, GLA_K_W), w_gk_up.dtype)],
                          axis=0).astype(BF16)

    proj = _inproj(x2, mod3, norm1_gain.reshape(1, D), w_p)
    proj3 = proj.reshape(B, S, PROJ_W)
    ya, (w_ba, w_bg, w_o, w_up, w_down) = _attn(
        proj3, biasm, q_norm_gain.reshape(1, HEAD_DIM), k_norm_gain.reshape(1, HEAD_DIM), attn_sinks,
        (w_branch_attn, w_branch_gla, w_out, w_ffn_up, w_ffn_down))
    yg = _gla(proj3, wgk, b_gk.reshape(1, GLA_K_W), gla_norm_gain.reshape(1, GLA_DV))
    x1, h2 = _merge(ya.reshape(T, ATTN_Q_W), yg.reshape(T, GLA_V_W), proj, x2, mod3,
                    norm2_gain.reshape(1, D), w_ba, w_bg, w_o)
    act = _ffn_up(h2, S, w_up, ffn_conv_w, ffn_conv_b.reshape(1, 2 * D_FF))
    return _ffn_down(act, w_down, x1, mod3)


def kernel(x, c, rel_bias_table, w_ada, b_ada, norm1_gain, w_in, q_norm_gain, k_norm_gain, attn_sinks,
           w_gk_up, b_gk, gla_norm_gain, w_branch_attn, w_branch_gla, w_out, norm2_gain, w_ffn_up,
           ffn_conv_w, ffn_conv_b, w_ffn_down):
    B, S, D = x.shape
    depth = w_in.shape[0]
    assert D == D_MODEL and w_in.shape[1:] == (D_MODEL, SRC_GA + 2 * D_MODEL) and w_ffn_up.shape[2] == 2 * D_FF
    assert S % max(FFN_TM, INPROJ_TM, ATTN_TQ, GLA_ROWS, MERGE_TM, DOWN_TM) == 0
    biasm = _relbias(rel_bias_table)
    x2 = x.reshape(B * S, D)
    for l in range(depth):
        mod3 = _adaln(c, w_ada[l], b_ada[l]).reshape(B, 6, D)
        x2 = _layer(x2, B, mod3, biasm, norm1_gain[l], w_in[l], q_norm_gain[l], k_norm_gain[l],
                    attn_sinks[l], w_gk_up[l], b_gk[l], gla_norm_gain[l], w_branch_attn[l],
                    w_branch_gla[l], w_out[l], norm2_gain[l], w_ffn_up[l], ffn_conv_w[l], ffn_conv_b[l],
                    w_ffn_down[l])
    return x2.reshape(B, S, D)
```

```python
import functools
import itertools
import math

import numpy as np
import jax
import jax.numpy as jnp
from jax import lax
from jax.experimental import pallas as pl
from jax.experimental.pallas import tpu as pltpu

F32 = jnp.float32
BF16 = jnp.bfloat16

D_MODEL = 2048
N_Q_HEADS = 16
N_KV_HEADS = 4
GQA_GROUP = N_Q_HEADS // N_KV_HEADS
HEAD_DIM = 64
WINDOW = 128
ATTN_BLOCK = 128
N_BUCKETS = 32
MAX_DISTANCE = 128
GLA_HEADS = 4
GLA_DK = 256
GLA_DV = 512
GLA_LOWRANK = 16
GLA_NORMALIZER = 16.0
GLA_CHUNK = 64
D_FF = 5632
EPS = 1e-6
NEG_INF = -1e30
LOG2E = math.log2(math.e)

ATTN_Q_W = N_Q_HEADS * HEAD_DIM
ATTN_KV_W = N_KV_HEADS * HEAD_DIM
GLA_K_W = GLA_HEADS * GLA_DK
GLA_V_W = GLA_HEADS * GLA_DV

COL_GA = 0
COL_GB = COL_GA + D_MODEL
COL_GV = COL_GB + D_MODEL
COL_GR = COL_GV + GLA_V_W
COL_GQ = COL_GR + GLA_V_W
COL_GK = COL_GQ + GLA_K_W
COL_AQ = COL_GK + GLA_K_W
COL_AK = COL_AQ + ATTN_Q_W
COL_AV = COL_AK + ATTN_KV_W
COL_LR = COL_AV + ATTN_KV_W
LANE = 128
BF16_ROWS = 16
VMEM_LIMIT = 60 * 1024 * 1024

LR_PAD = LANE
PROJ_W = 12288

ADALN_TN = 1024
INPROJ_TM, INPROJ_TN = 1024, 3072
INPROJ_NORM_ROWS = 256
ATTN_TQ = 512
ATTN_AHEAD = 2
GLA_ROWS = 256
GLA_HEADS_PER_STEP = 4
GLA_BATCH_PER_STEP = 2
MERGE_TM, MERGE_SUB = 512, 256
FFN_TM, FFN_TN = 1024, 512
FFN_HALO = BF16_ROWS
FFN_EPI_ROWS = 256
DOWN_TM = 512


def _params(*sem):
    return pltpu.CompilerParams(dimension_semantics=sem, vmem_limit_bytes=VMEM_LIMIT)


def _dot(a, b):
    return jnp.dot(a, b, preferred_element_type=F32)


def _dot_nt(a, b):
    return lax.dot_general(a, b, (((1,), (1,)), ((), ())), preferred_element_type=F32)


def _dot_tn(a, b):
    return lax.dot_general(a, b, (((0,), (0,)), ((), ())), preferred_element_type=F32)


def _rms(x, gain):
    return x * lax.rsqrt(jnp.mean(x * x, axis=-1, keepdims=True) + EPS) * gain


def _adaln_kernel(c_ref, w_ref, b_ref, o_ref):
    c = c_ref[...]
    ca = c * jax.nn.sigmoid(c)
    o_ref[...] = _dot(ca.astype(BF16), w_ref[...].astype(BF16)) + b_ref[...]


def _adaln(c, w_ada, b_ada):
    B, D = c.shape
    N = w_ada.shape[1]
    return pl.pallas_call(
        _adaln_kernel,
        grid=(N // ADALN_TN,),
        in_specs=[pl.BlockSpec((B, D), lambda j: (0, 0)),
                  pl.BlockSpec((D, ADALN_TN), lambda j: (0, j)),
                  pl.BlockSpec((1, ADALN_TN), lambda j: (0, j))],
        out_specs=pl.BlockSpec((B, ADALN_TN), lambda j: (0, j)),
        out_shape=jax.ShapeDtypeStruct((B, N), F32),
        compiler_params=_params("parallel"),
        name="adaln",
    )(c, w_ada, b_ada.reshape(1, N))


def _bucket_table():
    j = np.arange(2 * ATTN_BLOCK)[:, None]
    i = np.arange(ATTN_BLOCK)[None, :]
    dist = i + ATTN_BLOCK - j
    max_exact = N_BUCKETS // 2
    d = np.maximum(dist, 0)
    ratio = np.log(np.maximum(d, 1).astype(np.float32) / np.float32(max_exact)) / np.float32(
        math.log(MAX_DISTANCE / max_exact))
    large = max_exact + (ratio.astype(np.float32) * np.float32(N_BUCKETS - max_exact)).astype(np.int32)
    large = np.minimum(large, N_BUCKETS - 1)
    bucket = np.where(d < max_exact, d, large)
    in_window = (dist >= 0) & (dist < WINDOW)
    return np.where(in_window, bucket, -1).astype(np.int32)


def _relbias_kernel(tab_ref, bkt_ref, o_ref):
    g = pl.program_id(0)
    bkt = bkt_ref[...]
    nk, nq = bkt.shape
    key = lax.broadcasted_iota(jnp.int32, bkt.shape, 0)
    for e in range(2):
        for p in range(2):
            h = g * GQA_GROUP + 2 * p + e
            acc = jnp.zeros(bkt.shape, F32)
            for b in range(N_BUCKETS):
                acc = jnp.where(bkt == b, tab_ref[b, h], acc)
            regular = jnp.where(bkt >= 0, acc * LOG2E, NEG_INF)
            tile = (slice(e * nk, (e + 1) * nk), slice(p * nq, (p + 1) * nq))
            o_ref[(0,) + tile] = jnp.where(key >= ATTN_BLOCK, regular, NEG_INF)
            o_ref[(1,) + tile] = regular


def _relbias(rel_bias_table):
    bkt = jnp.asarray(_bucket_table())
    nk, nq = bkt.shape
    return pl.pallas_call(
        _relbias_kernel,
        grid=(N_KV_HEADS,),
        in_specs=[pl.BlockSpec(memory_space=pltpu.SMEM),
                  pl.BlockSpec(bkt.shape, lambda g: (0, 0))],
        out_specs=pl.BlockSpec((2, None, 2 * nk, 2 * nq), lambda g: (0, g, 0, 0)),
        out_shape=jax.ShapeDtypeStruct((2, N_KV_HEADS, 2 * nk, 2 * nq), F32),
        compiler_params=_params("parallel"),
        name="relbias",
    )(rel_bias_table, bkt)


SRC_GQ = ATTN_Q_W + 2 * ATTN_KV_W
SRC_GV = SRC_GQ + 2 * GLA_K_W
SRC_LR = SRC_GV + 2 * GLA_V_W
SRC_GA = SRC_LR + GLA_LOWRANK
WPACK_ROWS = 512


def _wpack_src_row(b):
    r = b * WPACK_ROWS
    src = jnp.where(r < COL_GV, SRC_GA + r,
                    jnp.where(r < COL_GQ, SRC_GV + (r - COL_GV),
                              jnp.where(r < COL_AQ, SRC_GQ + (r - COL_GQ),
                                        jnp.where(r < COL_LR, r - COL_AQ, SRC_LR))))
    return pl.multiple_of(src, 8)


def _wpack_kernel(w_ref, o_ref):
    r = pl.program_id(0) * WPACK_ROWS
    nvalid = jnp.where(r < COL_LR, WPACK_ROWS, jnp.where(r == COL_LR, GLA_LOWRANK, 0))
    row = lax.broadcasted_iota(jnp.int32, w_ref.shape, 0)
    o_ref[...] = jnp.where(row < nvalid, w_ref[...], 0.0).astype(BF16)


def _wpack(w_in_t):
    _, D = w_in_t.shape
    return pl.pallas_call(
        _wpack_kernel,
        grid=(PROJ_W // WPACK_ROWS,),
        in_specs=[pl.BlockSpec((pl.Element(WPACK_ROWS), pl.Element(D)), lambda b: (_wpack_src_row(b), 0))],
        out_specs=pl.BlockSpec((WPACK_ROWS, D), lambda b: (b, 0)),
        out_shape=jax.ShapeDtypeStruct((PROJ_W, D), BF16),
        compiler_params=_params("parallel"),
        name="wpack",
    )(w_in_t)


def _norm_modulate(x, mod_ref, gain):
    return (_rms(x, gain) * (1.0 + mod_ref[1:2, :]) + mod_ref[0:1, :]).astype(BF16)


def _hnorm_kernel(x_ref, mod_ref, g_ref, o_ref):
    o_ref[...] = _norm_modulate(x_ref[...], mod_ref, g_ref[...])


def _hnorm_first(x2, mod3, gain):
    D = x2.shape[1]
    nr = INPROJ_NORM_ROWS
    return pl.pallas_call(
        _hnorm_kernel,
        grid=(INPROJ_TM // nr,),
        in_specs=[pl.BlockSpec((nr, D), lambda r: (r, 0)),
                  pl.BlockSpec((None, 6, D), lambda r: (0, 0, 0)),
                  pl.BlockSpec((1, D), lambda r: (0, 0))],
        out_specs=pl.BlockSpec((nr, D), lambda r: (r, 0)),
        out_shape=jax.ShapeDtypeStruct((INPROJ_TM, D), BF16),
        compiler_params=_params("parallel"),
        name="hnorm_first",
    )(x2, mod3, gain)


def _cast_block(shape, steps):
    rows, cols = shape
    bc = min(cols, 1024)
    assert cols % bc == 0
    for br in range(BF16_ROWS, rows + 1, BF16_ROWS):
        if rows % br == 0 and (rows // br) * (cols // bc) <= steps:
            return br, bc
    raise ValueError(f"no cast block for {shape} in {steps} steps")


def _cast_specs(weights, n0, n1):
    def spec(shape):
        br, bc = _cast_block(shape, n0 * n1)
        ncb = shape[1] // bc
        last = (shape[0] // br) * ncb - 1

        def index(i, j):
            t = jnp.minimum(i * n1 + j, last)
            return t // ncb, t % ncb

        return pl.BlockSpec((br, bc), index)

    return [spec(w.shape) for w in weights]


def _inproj_kernel(xn_ref, modn_ref, g_ref, w_ref, h0_ref, o_ref, ha_ref, hb_ref):
    i, j = pl.program_id(0), pl.program_id(1)
    nr = INPROJ_NORM_ROWS

    @pl.when((i == 0) & (j == 0))
    def _():
        ha_ref[...] = h0_ref[...]

    def step(cur_ref, nxt_ref):
        o_ref[...] = _dot_nt(cur_ref[...], w_ref[...]).astype(BF16)
        chunk = jnp.minimum(j, INPROJ_TM // nr - 1)
        rows = pl.ds(pl.multiple_of(chunk * nr, nr), nr)
        nxt_ref[rows, :] = _norm_modulate(xn_ref[...], modn_ref, g_ref[...])

    @pl.when(i % 2 == 0)
    def _():
        step(ha_ref, hb_ref)

    @pl.when(i % 2 == 1)
    def _():
        step(hb_ref, ha_ref)


def _inproj(x2, mod3, gain, w):
    T, D = x2.shape
    S = T // mod3.shape[0]
    tm, tn, nr = INPROJ_TM, INPROJ_TN, INPROJ_NORM_ROWS
    n_i, chunks = T // tm, tm // nr
    assert PROJ_W // tn >= chunks

    def next_tile(i):
        return jnp.minimum(i + 1, n_i - 1)

    return pl.pallas_call(
        _inproj_kernel,
        grid=(n_i, PROJ_W // tn),
        in_specs=[pl.BlockSpec((nr, D), lambda i, j: (next_tile(i) * chunks + jnp.minimum(j, chunks - 1), 0)),
                  pl.BlockSpec((None, 6, D), lambda i, j: (next_tile(i) // (S // tm), 0, 0)),
                  pl.BlockSpec((1, D), lambda i, j: (0, 0)),
                  pl.BlockSpec((tn, D), lambda i, j: (j, 0)),
                  pl.BlockSpec((tm, D), lambda i, j: (0, 0), pipeline_mode=pl.Buffered(1))],
        out_specs=pl.BlockSpec((tm, tn), lambda i, j: (i, j)),
        out_shape=jax.ShapeDtypeStruct((T, PROJ_W), BF16),
        scratch_shapes=[pltpu.VMEM((tm, D), BF16), pltpu.VMEM((tm, D), BF16)],
        compiler_params=_params("arbitrary", "arbitrary"),
        name="inproj",
    )(x2, mod3, gain, w, _hnorm_first(x2, mod3, gain))


def _attn_kernel(q_ref, kc_ref, kp_ref, vc_ref, vp_ref, b0_ref, b1_ref, qg_ref, kg_ref, sink_ref, seg_ref,
                 *rest, n_cast):
    cast_in, o_ref, cast_out = rest[:n_cast], rest[n_cast], rest[n_cast + 1:]
    for src_ref, dst_ref in zip(cast_in, cast_out):
        dst_ref[...] = src_ref[...].astype(BF16)

    nsub = ATTN_TQ // ATTN_BLOCK
    blk, dh = ATTN_BLOCK, HEAD_DIM
    seg = seg_ref[...]

    def normed(x, gain):
        ssq = _dot((x * x).astype(BF16), seg)
        return x * lax.rsqrt(ssq * (1.0 / dh) + EPS) * gain

    kn = normed(jnp.concatenate([kp_ref[...], kc_ref[...]], axis=0).astype(F32), kg_ref[...])
    vt = jnp.concatenate([vp_ref[...], vc_ref[...]], axis=0).astype(F32).T.astype(BF16)
    lane = lax.broadcasted_iota(jnp.int32, (kn.shape[0], 2 * dh), 1)
    qgain = qg_ref[...] * (dh ** -0.5 * LOG2E)

    kzs, qns = [], []
    for g in range(N_KV_HEADS):
        pair = kn[:, (g // 2) * 2 * dh:(g // 2 + 1) * 2 * dh]
        own = jnp.where(lane >= dh if g % 2 else lane < dh, pair, 0.0)
        swapped = pltpu.roll(own, dh, axis=1)
        kz = [own, swapped] if g % 2 == 0 else [swapped, own]
        kzs.append([z.astype(BF16) for z in kz])
        qns.append(normed(q_ref[:, g * 4 * dh:(g + 1) * 4 * dh].astype(F32), qgain).astype(BF16))

    def scores(g, sb):
        bref = b0_ref if sb == 0 else b1_ref
        band = slice(sb * blk, sb * blk + 2 * blk)
        rows = slice(sb * blk, (sb + 1) * blk)
        kband = jnp.concatenate([kzs[g][0][band], kzs[g][1][band]], axis=0)
        qr = jnp.concatenate([qns[g][rows, 0:2 * dh], qns[g][rows, 2 * dh:4 * dh]], axis=0)
        return _dot_nt(kband, qr) + bref[g]

    def finish(g, sb, s):
        band = slice(sb * blk, sb * blk + 2 * blk)
        rows = slice(sb * blk, (sb + 1) * blk)
        ps, inv = [], []
        for e in range(2):
            se = s[e * 2 * blk:(e + 1) * 2 * blk]
            sink = sink_ref[g, e] * LOG2E
            m = jnp.maximum(jnp.max(se, axis=0, keepdims=True), sink)
            p = jnp.exp2(se - m)
            inv.append(1.0 / (jnp.sum(p, axis=0, keepdims=True) + jnp.exp2(sink - m)))
            ps.append(p.astype(BF16))
        ot = _dot(vt[g * dh:(g + 1) * dh, band], jnp.concatenate(ps, axis=1))
        ot = ot * jnp.concatenate(inv, axis=1)
        o = jnp.concatenate([ot[:, 0:2 * blk], ot[:, 2 * blk:4 * blk]], axis=0).T
        o_ref[rows, (2 * g) * 2 * dh:(2 * g + 1) * 2 * dh] = o[0:blk].astype(BF16)
        o_ref[rows, (2 * g + 1) * 2 * dh:(2 * g + 2) * 2 * dh] = o[blk:2 * blk].astype(BF16)

    units = [(g, sb) for g in range(N_KV_HEADS) for sb in range(nsub)]
    pending = [scores(*unit) for unit in units[:ATTN_AHEAD]]
    for n, unit in enumerate(units):
        if n + ATTN_AHEAD < len(units):
            pending.append(scores(*units[n + ATTN_AHEAD]))
        finish(*unit, pending.pop(0))


def _attn(proj3, biasm, q_gain, k_gain, sinks, cast_weights):
    B, S, _ = proj3.shape
    tq, blk = ATTN_TQ, ATTN_BLOCK
    nsub = tq // blk
    kvw = ATTN_KV_W
    seg_id = np.arange(kvw) // HEAD_DIM
    seg = jnp.asarray(seg_id[:, None] == seg_id[None, :], dtype=BF16)
    sink_rows = jnp.repeat(sinks.reshape(N_KV_HEADS, 2, 2).transpose(0, 2, 1), blk, axis=-1)
    sink_rows = sink_rows.reshape(N_KV_HEADS, 2, 1, 2 * blk)

    def prev(i):
        return jnp.maximum(i * nsub - 1, 0)

    bias_block = (None,) + biasm.shape[1:]
    cast_specs = _cast_specs(cast_weights, B, S // tq)
    outs = pl.pallas_call(
        functools.partial(_attn_kernel, n_cast=len(cast_weights)),
        grid=(B, S // tq),
        in_specs=[pl.BlockSpec((None, tq, ATTN_Q_W), lambda b, i: (b, i, COL_AQ // ATTN_Q_W)),
                  pl.BlockSpec((None, tq, kvw), lambda b, i: (b, i, COL_AK // kvw)),
                  pl.BlockSpec((None, blk, kvw), lambda b, i: (b, prev(i), COL_AK // kvw)),
                  pl.BlockSpec((None, tq, kvw), lambda b, i: (b, i, COL_AV // kvw)),
                  pl.BlockSpec((None, blk, kvw), lambda b, i: (b, prev(i), COL_AV // kvw)),
                  pl.BlockSpec(bias_block, lambda b, i: (jnp.minimum(i, 1), 0, 0, 0)),
                  pl.BlockSpec(bias_block, lambda b, i: (1, 0, 0, 0)),
                  pl.BlockSpec((1, kvw), lambda b, i: (0, 0)),
                  pl.BlockSpec((1, kvw), lambda b, i: (0, 0)),
                  pl.BlockSpec(sink_rows.shape, lambda b, i: (0, 0, 0, 0)),
                  pl.BlockSpec(seg.shape, lambda b, i: (0, 0))] + cast_specs,
        out_specs=[pl.BlockSpec((None, tq, ATTN_Q_W), lambda b, i: (b, i, 0))] + cast_specs,
        out_shape=[jax.ShapeDtypeStruct((B, S, ATTN_Q_W), BF16)]
        + [jax.ShapeDtypeStruct(cw.shape, BF16) for cw in cast_weights],
        compiler_params=_params("arbitrary", "arbitrary"),
        name="attn",
    )(proj3, proj3, proj3, proj3, proj3, biasm, biasm, jnp.tile(q_gain, (1, GQA_GROUP)),
      jnp.tile(k_gain, (1, N_KV_HEADS)), sink_rows, seg, *cast_weights)
    return outs[0], outs[1:]


def _split2(x):
    hi = x.astype(BF16)
    return hi, (x - hi.astype(F32)).astype(BF16)


def _gla_head(q, k, v, r, lr, wgk, bgk, gain, state_ref, masks, store):
    C = GLA_CHUNK
    tril, m_same, m_next, m_far, eye = masks
    z = _dot(lr, wgk) + bgk
    yield
    gl = (jnp.minimum(z, 0.0) - jnp.log(1.0 + jnp.exp(-jnp.abs(z)))) * (1.0 / GLA_NORMALIZER)
    hi, lo = _split2(gl)
    g = _dot(tril, hi) + _dot(tril, lo)
    yield
    t = [g[(c + 1) * C - 1:(c + 1) * C] for c in range(4)]
    t_rows = jnp.concatenate([jnp.broadcast_to(tc, (C, tc.shape[1])) for tc in t], axis=0)

    q_dec = q.astype(F32) * (GLA_DK ** -0.5) * jnp.exp(g)
    kf = k.astype(F32)
    k_inv = (kf * jnp.exp(-g)).astype(BF16)
    k_end = kf * jnp.exp(t_rows - g)
    qd = [q_dec[c * C:(c + 1) * C] for c in range(4)]
    ke = [k_end[c * C:(c + 1) * C] for c in range(4)]

    def rows(parts):
        return jnp.concatenate(parts, axis=0).astype(BF16)

    q_b, k_b = q_dec.astype(BF16), k_end.astype(BF16)
    q_far = rows([qd[0], qd[1], qd[2], qd[3] * jnp.exp(t[2])])
    k_far = rows([ke[0] * jnp.exp(t[1]), ke[1], ke[2], ke[3]])
    q_abs = rows([qd[0], qd[1] * jnp.exp(t[0]), qd[2] * jnp.exp(t[0] + t[1]), qd[3] * jnp.exp(t[0] + t[1] + t[2])])
    k_abs = rows([ke[0] * jnp.exp(t[1] + t[2] + t[3]), ke[1] * jnp.exp(t[2] + t[3]), ke[2] * jnp.exp(t[3]), ke[3]])

    a_same, a_next, a_far = _dot_nt(q_b, k_inv), _dot_nt(q_b, k_b), _dot_nt(q_far, k_far)
    state = state_ref[...]
    o_state = _dot(q_abs, state.astype(BF16))
    update = _dot_tn(k_abs, v)
    yield
    a = jnp.where(m_same, a_same, jnp.where(m_next, a_next, jnp.where(m_far, a_far, 0.0)))
    o = _dot(a.astype(BF16), v) + o_state
    yield

    decay = jnp.exp(t[0] + t[1] + t[2] + t[3])
    decay_col = jnp.sum(jnp.where(eye, jnp.broadcast_to(decay, eye.shape), 0.0), axis=1, keepdims=True)
    state_ref[...] = decay_col * state + update

    rf = r.astype(F32)
    store((_rms(o, gain) * (rf * jax.nn.sigmoid(rf))).astype(BF16))


def _gla_kernel(q_ref, k_ref, v_ref, r_ref, lr_ref, wgk_ref, bgk_ref, gain_ref, o_ref, state_ref):
    R, C, dk, dv = GLA_ROWS, GLA_CHUNK, GLA_DK, GLA_DV

    @pl.when(pl.program_id(2) == 0)
    def _():
        state_ref[...] = jnp.zeros_like(state_ref)

    row = lax.broadcasted_iota(jnp.int32, (R, R), 0)
    col = lax.broadcasted_iota(jnp.int32, (R, R), 1)
    ci, cj = row // C, col // C
    m_same = (row >= col) & (ci == cj)
    m_next = (ci == cj + 1) & (ci != 2)
    m_far = (ci >= 2) & (cj <= 1)
    eye = lax.broadcasted_iota(jnp.int32, (dk, dk), 0) == lax.broadcasted_iota(jnp.int32, (dk, dk), 1)
    masks = (m_same.astype(BF16), m_same, m_next, m_far, eye)

    heads = []
    for b in range(GLA_BATCH_PER_STEP):
        lr = lr_ref[b]
        for h in range(GLA_HEADS_PER_STEP):
            ks, vs = slice(h * dk, (h + 1) * dk), slice(h * dv, (h + 1) * dv)
            heads.append(_gla_head(q_ref[b, :, ks], k_ref[b, :, ks], v_ref[b, :, vs], r_ref[b, :, vs], lr,
                                   wgk_ref[:, ks], bgk_ref[:, ks], gain_ref[...], state_ref.at[b, h], masks,
                                   functools.partial(o_ref.__setitem__, (b, slice(None), vs))))
    for _ in itertools.zip_longest(*heads):
        pass


def _gla(proj3, wgk, bgk, gain):
    B, S, _ = proj3.shape
    R, hps, bps = GLA_ROWS, GLA_HEADS_PER_STEP, GLA_BATCH_PER_STEP
    dk, dv = hps * GLA_DK, hps * GLA_DV
    return pl.pallas_call(
        _gla_kernel,
        grid=(B // bps, GLA_HEADS // hps, S // R),
        in_specs=[pl.BlockSpec((bps, R, dk), lambda b, h, t: (b, t, COL_GQ // dk + h)),
                  pl.BlockSpec((bps, R, dk), lambda b, h, t: (b, t, COL_GK // dk + h)),
                  pl.BlockSpec((bps, R, dv), lambda b, h, t: (b, t, COL_GV // dv + h)),
                  pl.BlockSpec((bps, R, dv), lambda b, h, t: (b, t, COL_GR // dv + h)),
                  pl.BlockSpec((bps, R, LR_PAD), lambda b, h, t: (b, t, COL_LR // LR_PAD)),
                  pl.BlockSpec((LR_PAD, dk), lambda b, h, t: (0, h)),
                  pl.BlockSpec((1, dk), lambda b, h, t: (0, h)),
                  pl.BlockSpec((1, GLA_DV), lambda b, h, t: (0, 0))],
        out_specs=pl.BlockSpec((bps, R, dv), lambda b, h, t: (b, t, h)),
        out_shape=jax.ShapeDtypeStruct((B, S, GLA_V_W), BF16),
        scratch_shapes=[pltpu.VMEM((bps, hps, GLA_DK, GLA_DV), F32)],
        compiler_params=_params("parallel", "parallel", "arbitrary"),
        name="gla",
    )(proj3, proj3, proj3, proj3, proj3, wgk, bgk, gain)


def _merge_kernel(ya_ref, yg_ref, ga_ref, gb_ref, x_ref, mod_ref, g2_ref, wa_ref, wg_ref, wo_ref,
                  x1_ref, h2_ref):
    def merged_branches(rows):
        ma, mg = _dot(ya_ref[rows, :], wa_ref[...]), _dot(yg_ref[rows, :], wg_ref[...])
        ga = jax.nn.sigmoid(ga_ref[rows, :].astype(F32))
        gb = jax.nn.sigmoid(gb_ref[rows, :].astype(F32))
        return (ga * ma + gb * mg).astype(BF16)

    def project(rows, merged):
        x1 = x_ref[rows, :] + mod_ref[2:3, :] * _dot(merged, wo_ref[...])
        x1_ref[rows, :] = x1
        h2_ref[rows, :] = (_rms(x1, g2_ref[...]) * (1.0 + mod_ref[4:5, :]) + mod_ref[3:4, :]).astype(BF16)

    for r0 in range(0, MERGE_TM, MERGE_SUB):
        rows = slice(r0, r0 + MERGE_SUB)
        project(rows, merged_branches(rows))


def _merge(ya, yg, proj, x2, mod3, gain2, wa, wg, wo):
    T, D = x2.shape
    S = T // mod3.shape[0]
    tm = MERGE_TM
    once = pl.Buffered(1)
    return pl.pallas_call(
        _merge_kernel,
        grid=(T // tm,),
        in_specs=[pl.BlockSpec((tm, ATTN_Q_W), lambda i: (i, 0)),
                  pl.BlockSpec((tm, GLA_V_W), lambda i: (i, 0)),
                  pl.BlockSpec((tm, D), lambda i: (i, COL_GA // D)),
                  pl.BlockSpec((tm, D), lambda i: (i, COL_GB // D)),
                  pl.BlockSpec((tm, D), lambda i: (i, 0)),
                  pl.BlockSpec((None, 6, D), lambda i: (i // (S // tm), 0, 0)),
                  pl.BlockSpec((1, D), lambda i: (0, 0)),
                  pl.BlockSpec((ATTN_Q_W, D), lambda i: (0, 0), pipeline_mode=once),
                  pl.BlockSpec((GLA_V_W, D), lambda i: (0, 0), pipeline_mode=once),
                  pl.BlockSpec((D, D), lambda i: (0, 0), pipeline_mode=once)],
        out_specs=[pl.BlockSpec((tm, D), lambda i: (i, 0)),
                   pl.BlockSpec((tm, D), lambda i: (i, 0))],
        out_shape=[jax.ShapeDtypeStruct((T, D), F32), jax.ShapeDtypeStruct((T, D), BF16)],
        compiler_params=_params("parallel"),
        name="merge",
    )(ya, yg, proj, proj, x2, mod3, gain2, wa, wg, wo)


def _ffn_up_kernel(h_ref, halo_ref, wa_ref, wb_ref, cwa_ref, cwb_ref, cba_ref, cbb_ref, o_ref,
                   hs_ref, ua_ref, ub_ref, *, tiles_per_seq):
    tm, halo = FFN_TM, FFN_HALO

    @pl.when(pl.program_id(1) == 0)
    def _():
        first = (pl.program_id(0) % tiles_per_seq) == 0
        hs_ref[0:halo, :] = jnp.where(first, jnp.zeros_like(halo_ref), halo_ref[...])
        hs_ref[halo:, :] = h_ref[...]

    def conv(u_ref, cw_ref, cb_ref, r0, n):
        u = u_ref[halo + r0 - 8:halo + r0 + n, :]
        y = cb_ref[...] + cw_ref[0:1, :] * pltpu.roll(u, 2, axis=0)[8:]
        y = y + cw_ref[1:2, :] * pltpu.roll(u, 1, axis=0)[8:]
        return y + cw_ref[2:3, :] * u[8:]

    ua_ref[...] = _dot(hs_ref[...], wa_ref[...])
    ub_ref[...] = _dot(hs_ref[...], wb_ref[...])
    for r0 in range(0, tm, FFN_EPI_ROWS):
        ya = conv(ua_ref, cwa_ref, cba_ref, r0, FFN_EPI_ROWS)
        yb = conv(ub_ref, cwb_ref, cbb_ref, r0, FFN_EPI_ROWS)
        o_ref[r0:r0 + FFN_EPI_ROWS, :] = (ya * jax.nn.sigmoid(ya) * yb).astype(BF16)


def _ffn_up(h2, S, w_up, conv_w, conv_b):
    T, D = h2.shape
    tm, tn, halo = FFN_TM, FFN_TN, FFN_HALO
    nj = D_FF // tn
    return pl.pallas_call(
        functools.partial(_ffn_up_kernel, tiles_per_seq=S // tm),
        grid=(T // tm, D_FF // tn),
        in_specs=[pl.BlockSpec((tm, D), lambda i, j: (i, 0)),
                  pl.BlockSpec((halo, D), lambda i, j: (jnp.maximum(i * (tm // halo) - 1, 0), 0)),
                  pl.BlockSpec((D, tn), lambda i, j: (0, j)),
                  pl.BlockSpec((D, tn), lambda i, j: (0, j + nj)),
                  pl.BlockSpec((3, tn), lambda i, j: (0, j)),
                  pl.BlockSpec((3, tn), lambda i, j: (0, j + nj)),
                  pl.BlockSpec((1, tn), lambda i, j: (0, j)),
                  pl.BlockSpec((1, tn), lambda i, j: (0, j + nj))],
        out_specs=pl.BlockSpec((tm, tn), lambda i, j: (i, j)),
        out_shape=jax.ShapeDtypeStruct((T, D_FF), BF16),
        scratch_shapes=[pltpu.VMEM((tm + halo, D), BF16),
                        pltpu.VMEM((tm + halo, tn), F32),
                        pltpu.VMEM((tm + halo, tn), F32)],
        compiler_params=_params("parallel", "arbitrary"),
        name="ffn_up",
    )(h2, h2, w_up, w_up, conv_w, conv_w, conv_b, conv_b)


def _ffn_down_kernel(a_ref, w_ref, x_ref, mod_ref, o_ref):
    o_ref[...] = x_ref[...] + mod_ref[5:6, :] * _dot(a_ref[...], w_ref[...])


def _ffn_down(act, wd, x1, mod3):
    T, D = x1.shape
    S = T // mod3.shape[0]
    tm = DOWN_TM
    return pl.pallas_call(
        _ffn_down_kernel,
        grid=(T // tm,),
        in_specs=[pl.BlockSpec((tm, D_FF), lambda i: (i, 0)),
                  pl.BlockSpec((D_FF, D), lambda i: (0, 0), pipeline_mode=pl.Buffered(1)),
                  pl.BlockSpec((tm, D), lambda i: (i, 0)),
                  pl.BlockSpec((None, 6, D), lambda i: (i // (S // tm), 0, 0))],
        out_specs=pl.BlockSpec((tm, D), lambda i: (i, 0)),
        out_shape=jax.ShapeDtypeStruct((T, D), F32),
        compiler_params=_params("parallel"),
        name="ffn_down",
    )(act, wd, x1, mod3)


def _layer(x2, B, mod3, biasm, norm1_gain, w_in, q_norm_gain, k_norm_gain, attn_sinks, w_gk_up, b_gk,
           gla_norm_gain, w_branch_attn, w_branch_gla, w_out, norm2_gain, w_ffn_up, ffn_conv_w,
           ffn_conv_b, w_ffn_down):
    T, D = x2.shape
    S = T // B
    w_p = _wpack(w_in.T)
    wgk = jnp.concatenate([w_gk_up, jnp.zeros((LR_PAD - GLA_LOWRANK, GLA_K_W), w_gk_up.dtype)],
                          axis=0).astype(BF16)

    proj = _inproj(x2, mod3, norm1_gain.reshape(1, D), w_p)
    proj3 = proj.reshape(B, S, PROJ_W)
    ya, (w_ba, w_bg, w_o, w_up, w_down) = _attn(
        proj3, biasm, q_norm_gain.reshape(1, HEAD_DIM), k_norm_gain.reshape(1, HEAD_DIM), attn_sinks,
        (w_branch_attn, w_branch_gla, w_out, w_ffn_up, w_ffn_down))
    yg = _gla(proj3, wgk, b_gk.reshape(1, GLA_K_W), gla_norm_gain.reshape(1, GLA_DV))
    x1, h2 = _merge(ya.reshape(T, ATTN_Q_W), yg.reshape(T, GLA_V_W), proj, x2, mod3,
                    norm2_gain.reshape(1, D), w_ba, w_bg, w_o)
    act = _ffn_up(h2, S, w_up, ffn_conv_w, ffn_conv_b.reshape(1, 2 * D_FF))
    return _ffn_down(act, w_down, x1, mod3)


def kernel(x, c, rel_bias_table, w_ada, b_ada, norm1_gain, w_in, q_norm_gain, k_norm_gain, attn_sinks,
           w_gk_up, b_gk, gla_norm_gain, w_branch_attn, w_branch_gla, w_out, norm2_gain, w_ffn_up,
           ffn_conv_w, ffn_conv_b, w_ffn_down):
    B, S, D = x.shape
    depth = w_in.shape[0]
    assert D == D_MODEL and w_in.shape[1:] == (D_MODEL, SRC_GA + 2 * D_MODEL) and w_ffn_up.shape[2] == 2 * D_FF
    assert S % max(FFN_TM, INPROJ_TM, ATTN_TQ, GLA_ROWS, MERGE_TM, DOWN_TM) == 0
    biasm = _relbias(rel_bias_table)
    x2 = x.reshape(B * S, D)
    for l in range(depth):
        mod3 = _adaln(c, w_ada[l], b_ada[l]).reshape(B, 6, D)
        x2 = _layer(x2, B, mod3, biasm, norm1_gain[l], w_in[l], q_norm_gain[l], k_norm_gain[l],
                    attn_sinks[l], w_gk_up[l], b_gk[l], gla_norm_gain[l], w_branch_attn[l],
                    w_branch_gla[l], w_out[l], norm2_gain[l], w_ffn_up[l], ffn_conv_w[l], ffn_conv_b[l],
                    w_ffn_down[l])
    return x2.reshape(B, S, D)
```

```python
import functools
import itertools
import math

import numpy as np
import jax
import jax.numpy as jnp
from jax import lax
from jax.experimental import pallas as pl
from jax.experimental.pallas import tpu as pltpu

F32 = jnp.float32
BF16 = jnp.bfloat16

D_MODEL = 2048
N_Q_HEADS = 16
N_KV_HEADS = 4
GQA_GROUP = N_Q_HEADS // N_KV_HEADS
HEAD_DIM = 64
WINDOW = 128
ATTN_BLOCK = 128
N_BUCKETS = 32
MAX_DISTANCE = 128
GLA_HEADS = 4
GLA_DK = 256
GLA_DV = 512
GLA_LOWRANK = 16
GLA_NORMALIZER = 16.0
GLA_CHUNK = 64
D_FF = 5632
EPS = 1e-6
NEG_INF = -1e30
LOG2E = math.log2(math.e)

ATTN_Q_W = N_Q_HEADS * HEAD_DIM
ATTN_KV_W = N_KV_HEADS * HEAD_DIM
GLA_K_W = GLA_HEADS * GLA_DK
GLA_V_W = GLA_HEADS * GLA_DV

COL_GA = 0
COL_GB = COL_GA + D_MODEL
COL_GV = COL_GB + D_MODEL
COL_GR = COL_GV + GLA_V_W
COL_GQ = COL_GR + GLA_V_W
COL_GK = COL_GQ + GLA_K_W
COL_AQ = COL_GK + GLA_K_W
COL_AK = COL_AQ + ATTN_Q_W
COL_AV = COL_AK + ATTN_KV_W
COL_LR = COL_AV + ATTN_KV_W
LANE = 128
BF16_ROWS = 16
VMEM_LIMIT = 60 * 1024 * 1024

LR_PAD = LANE
PROJ_W = 12288

ADALN_TN = 1024
INPROJ_TM, INPROJ_TN = 1024, 3072
INPROJ_NORM_ROWS = 256
ATTN_TQ = 512
ATTN_AHEAD = 2
GLA_ROWS = 256
GLA_HEADS_PER_STEP = 4
GLA_BATCH_PER_STEP = 2
MERGE_TM, MERGE_SUB = 512, 256
FFN_TM, FFN_TN = 1024, 512
FFN_HALO = BF16_ROWS
FFN_EPI_ROWS = 256


def _params(*sem):
    return pltpu.CompilerParams(dimension_semantics=sem, vmem_limit_bytes=VMEM_LIMIT)


def _dot(a, b):
    return jnp.dot(a, b, preferred_element_type=F32)


def _dot_nt(a, b):
    return lax.dot_general(a, b, (((1,), (1,)), ((), ())), preferred_element_type=F32)


def _dot_tn(a, b):
    return lax.dot_general(a, b, (((0,), (0,)), ((), ())), preferred_element_type=F32)


def _rms(x, gain):
    return x * lax.rsqrt(jnp.mean(x * x, axis=-1, keepdims=True) + EPS) * gain


def _adaln_kernel(c_ref, w_ref, b_ref, o_ref):
    c = c_ref[...]
    ca = c * jax.nn.sigmoid(c)
    o_ref[...] = _dot(ca.astype(BF16), w_ref[...].astype(BF16)) + b_ref[...]


def _adaln(c, w_ada, b_ada):
    B, D = c.shape
    N = w_ada.shape[1]
    return pl.pallas_call(
        _adaln_kernel,
        grid=(N // ADALN_TN,),
        in_specs=[pl.BlockSpec((B, D), lambda j: (0, 0)),
                  pl.BlockSpec((D, ADALN_TN), lambda j: (0, j)),
                  pl.BlockSpec((1, ADALN_TN), lambda j: (0, j))],
        out_specs=pl.BlockSpec((B, ADALN_TN), lambda j: (0, j)),
        out_shape=jax.ShapeDtypeStruct((B, N), F32),
        compiler_params=_params("parallel"),
        name="adaln",
    )(c, w_ada, b_ada.reshape(1, N))


def _bucket_table():
    j = np.arange(2 * ATTN_BLOCK)[:, None]
    i = np.arange(ATTN_BLOCK)[None, :]
    dist = i + ATTN_BLOCK - j
    max_exact = N_BUCKETS // 2
    d = np.maximum(dist, 0)
    ratio = np.log(np.maximum(d, 1).astype(np.float32) / np.float32(max_exact)) / np.float32(
        math.log(MAX_DISTANCE / max_exact))
    large = max_exact + (ratio.astype(np.float32) * np.float32(N_BUCKETS - max_exact)).astype(np.int32)
    large = np.minimum(large, N_BUCKETS - 1)
    bucket = np.where(d < max_exact, d, large)
    in_window = (dist >= 0) & (dist < WINDOW)
    return np.where(in_window, bucket, -1).astype(np.int32)


def _relbias_kernel(tab_ref, bkt_ref, o_ref):
    g = pl.program_id(0)
    bkt = bkt_ref[...]
    nk, nq = bkt.shape
    key = lax.broadcasted_iota(jnp.int32, bkt.shape, 0)
    for e in range(2):
        for p in range(2):
            h = g * GQA_GROUP + 2 * p + e
            acc = jnp.zeros(bkt.shape, F32)
            for b in range(N_BUCKETS):
                acc = jnp.where(bkt == b, tab_ref[b, h], acc)
            regular = jnp.where(bkt >= 0, acc * LOG2E, NEG_INF)
            tile = (slice(e * nk, (e + 1) * nk), slice(p * nq, (p + 1) * nq))
            o_ref[(0,) + tile] = jnp.where(key >= ATTN_BLOCK, regular, NEG_INF)
            o_ref[(1,) + tile] = regular


def _relbias(rel_bias_table):
    bkt = jnp.asarray(_bucket_table())
    nk, nq = bkt.shape
    return pl.pallas_call(
        _relbias_kernel,
        grid=(N_KV_HEADS,),
        in_specs=[pl.BlockSpec(memory_space=pltpu.SMEM),
                  pl.BlockSpec(bkt.shape, lambda g: (0, 0))],
        out_specs=pl.BlockSpec((2, None, 2 * nk, 2 * nq), lambda g: (0, g, 0, 0)),
        out_shape=jax.ShapeDtypeStruct((2, N_KV_HEADS, 2 * nk, 2 * nq), F32),
        compiler_params=_params("parallel"),
        name="relbias",
    )(rel_bias_table, bkt)


SRC_GQ = ATTN_Q_W + 2 * ATTN_KV_W
SRC_GV = SRC_GQ + 2 * GLA_K_W
SRC_LR = SRC_GV + 2 * GLA_V_W
SRC_GA = SRC_LR + GLA_LOWRANK
WPACK_ROWS = 512


def _wpack_src_row(b):
    r = b * WPACK_ROWS
    src = jnp.where(r < COL_GV, SRC_GA + r,
                    jnp.where(r < COL_GQ, SRC_GV + (r - COL_GV),
                              jnp.where(r < COL_AQ, SRC_GQ + (r - COL_GQ),
                                        jnp.where(r < COL_LR, r - COL_AQ, SRC_LR))))
    return pl.multiple_of(src, 8)


def _wpack_kernel(w_ref, o_ref):
    r = pl.program_id(0) * WPACK_ROWS
    nvalid = jnp.where(r < COL_LR, WPACK_ROWS, jnp.where(r == COL_LR, GLA_LOWRANK, 0))
    row = lax.broadcasted_iota(jnp.int32, w_ref.shape, 0)
    o_ref[...] = jnp.where(row < nvalid, w_ref[...], 0.0).astype(BF16)


def _wpack(w_in_t):
    _, D = w_in_t.shape
    return pl.pallas_call(
        _wpack_kernel,
        grid=(PROJ_W // WPACK_ROWS,),
        in_specs=[pl.BlockSpec((pl.Element(WPACK_ROWS), pl.Element(D)), lambda b: (_wpack_src_row(b), 0))],
        out_specs=pl.BlockSpec((WPACK_ROWS, D), lambda b: (b, 0)),
        out_shape=jax.ShapeDtypeStruct((PROJ_W, D), BF16),
        compiler_params=_params("parallel"),
        name="wpack",
    )(w_in_t)


def _norm_modulate(x, mod_ref, gain):
    return (_rms(x, gain) * (1.0 + mod_ref[1:2, :]) + mod_ref[0:1, :]).astype(BF16)


def _hnorm_kernel(x_ref, mod_ref, g_ref, o_ref):
    o_ref[...] = _norm_modulate(x_ref[...], mod_ref, g_ref[...])


def _hnorm_first(x2, mod3, gain):
    D = x2.shape[1]
    nr = INPROJ_NORM_ROWS
    return pl.pallas_call(
        _hnorm_kernel,
        grid=(INPROJ_TM // nr,),
        in_specs=[pl.BlockSpec((nr, D), lambda r: (r, 0)),
                  pl.BlockSpec((None, 6, D), lambda r: (0, 0, 0)),
                  pl.BlockSpec((1, D), lambda r: (0, 0))],
        out_specs=pl.BlockSpec((nr, D), lambda r: (r, 0)),
        out_shape=jax.ShapeDtypeStruct((INPROJ_TM, D), BF16),
        compiler_params=_params("parallel"),
        name="hnorm_first",
    )(x2, mod3, gain)


def _cast_block(shape, steps):
    rows, cols = shape
    bc = min(cols, 1024)
    assert cols % bc == 0
    for br in range(BF16_ROWS, rows + 1, BF16_ROWS):
        if rows % br == 0 and (rows // br) * (cols // bc) <= steps:
            return br, bc
    raise ValueError(f"no cast block for {shape} in {steps} steps")


def _cast_specs(weights, n0, n1):
    def spec(shape):
        br, bc = _cast_block(shape, n0 * n1)
        ncb = shape[1] // bc
        last = (shape[0] // br) * ncb - 1

        def index(i, j):
            t = jnp.minimum(i * n1 + j, last)
            return t // ncb, t % ncb

        return pl.BlockSpec((br, bc), index)

    return [spec(w.shape) for w in weights]


def _inproj_kernel(xn_ref, modn_ref, g_ref, w_ref, h0_ref, o_ref, ha_ref, hb_ref):
    i, j = pl.program_id(0), pl.program_id(1)
    nr = INPROJ_NORM_ROWS

    @pl.when((i == 0) & (j == 0))
    def _():
        ha_ref[...] = h0_ref[...]

    def step(cur_ref, nxt_ref):
        o_ref[...] = _dot_nt(cur_ref[...], w_ref[...]).astype(BF16)
        chunk = jnp.minimum(j, INPROJ_TM // nr - 1)
        rows = pl.ds(pl.multiple_of(chunk * nr, nr), nr)
        nxt_ref[rows, :] = _norm_modulate(xn_ref[...], modn_ref, g_ref[...])

    @pl.when(i % 2 == 0)
    def _():
        step(ha_ref, hb_ref)

    @pl.when(i % 2 == 1)
    def _():
        step(hb_ref, ha_ref)


def _inproj(x2, mod3, gain, w):
    T, D = x2.shape
    S = T // mod3.shape[0]
    tm, tn, nr = INPROJ_TM, INPROJ_TN, INPROJ_NORM_ROWS
    n_i, chunks = T // tm, tm // nr
    assert PROJ_W // tn >= chunks

    def next_tile(i):
        return jnp.minimum(i + 1, n_i - 1)

    return pl.pallas_call(
        _inproj_kernel,
        grid=(n_i, PROJ_W // tn),
        in_specs=[pl.BlockSpec((nr, D), lambda i, j: (next_tile(i) * chunks + jnp.minimum(j, chunks - 1), 0)),
                  pl.BlockSpec((None, 6, D), lambda i, j: (next_tile(i) // (S // tm), 0, 0)),
                  pl.BlockSpec((1, D), lambda i, j: (0, 0)),
                  pl.BlockSpec((tn, D), lambda i, j: (j, 0)),
                  pl.BlockSpec((tm, D), lambda i, j: (0, 0), pipeline_mode=pl.Buffered(1))],
        out_specs=pl.BlockSpec((tm, tn), lambda i, j: (i, j)),
        out_shape=jax.ShapeDtypeStruct((T, PROJ_W), BF16),
        scratch_shapes=[pltpu.VMEM((tm, D), BF16), pltpu.VMEM((tm, D), BF16)],
        compiler_params=_params("arbitrary", "arbitrary"),
        name="inproj",
    )(x2, mod3, gain, w, _hnorm_first(x2, mod3, gain))


def _attn_kernel(q_ref, kc_ref, kp_ref, vc_ref, vp_ref, b0_ref, b1_ref, qg_ref, kg_ref, sink_ref, seg_ref,
                 *rest, n_cast):
    cast_in, o_ref, cast_out = rest[:n_cast], rest[n_cast], rest[n_cast + 1:]
    for src_ref, dst_ref in zip(cast_in, cast_out):
        dst_ref[...] = src_ref[...].astype(BF16)

    nsub = ATTN_TQ // ATTN_BLOCK
    blk, dh = ATTN_BLOCK, HEAD_DIM
    seg = seg_ref[...]

    def normed(x, gain):
        ssq = _dot((x * x).astype(BF16), seg)
        return x * lax.rsqrt(ssq * (1.0 / dh) + EPS) * gain

    kn = normed(jnp.concatenate([kp_ref[...], kc_ref[...]], axis=0).astype(F32), kg_ref[...])
    vt = jnp.concatenate([vp_ref[...], vc_ref[...]], axis=0).astype(F32).T.astype(BF16)
    lane = lax.broadcasted_iota(jnp.int32, (kn.shape[0], 2 * dh), 1)
    qgain = qg_ref[...] * (dh ** -0.5 * LOG2E)

    kzs, qns = [], []
    for g in range(N_KV_HEADS):
        pair = kn[:, (g // 2) * 2 * dh:(g // 2 + 1) * 2 * dh]
        own = jnp.where(lane >= dh if g % 2 else lane < dh, pair, 0.0)
        swapped = pltpu.roll(own, dh, axis=1)
        kz = [own, swapped] if g % 2 == 0 else [swapped, own]
        kzs.append([z.astype(BF16) for z in kz])
        qns.append(normed(q_ref[:, g * 4 * dh:(g + 1) * 4 * dh].astype(F32), qgain).astype(BF16))

    def scores(g, sb):
        bref = b0_ref if sb == 0 else b1_ref
        band = slice(sb * blk, sb * blk + 2 * blk)
        rows = slice(sb * blk, (sb + 1) * blk)
        kband = jnp.concatenate([kzs[g][0][band], kzs[g][1][band]], axis=0)
        qr = jnp.concatenate([qns[g][rows, 0:2 * dh], qns[g][rows, 2 * dh:4 * dh]], axis=0)
        return _dot_nt(kband, qr) + bref[g]

    def finish(g, sb, s):
        band = slice(sb * blk, sb * blk + 2 * blk)
        rows = slice(sb * blk, (sb + 1) * blk)
        ps, inv = [], []
        for e in range(2):
            se = s[e * 2 * blk:(e + 1) * 2 * blk]
            sink = sink_ref[g, e] * LOG2E
            m = jnp.maximum(jnp.max(se, axis=0, keepdims=True), sink)
            p = jnp.exp2(se - m)
            inv.append(1.0 / (jnp.sum(p, axis=0, keepdims=True) + jnp.exp2(sink - m)))
            ps.append(p.astype(BF16))
        ot = _dot(vt[g * dh:(g + 1) * dh, band], jnp.concatenate(ps, axis=1))
        ot = ot * jnp.concatenate(inv, axis=1)
        o = jnp.concatenate([ot[:, 0:2 * blk], ot[:, 2 * blk:4 * blk]], axis=0).T
        o_ref[rows, (2 * g) * 2 * dh:(2 * g + 1) * 2 * dh] = o[0:blk].astype(BF16)
        o_ref[rows, (2 * g + 1) * 2 * dh:(2 * g + 2) * 2 * dh] = o[blk:2 * blk].astype(BF16)

    units = [(g, sb) for g in range(N_KV_HEADS) for sb in range(nsub)]
    pending = [scores(*unit) for unit in units[:ATTN_AHEAD]]
    for n, unit in enumerate(units):
        if n + ATTN_AHEAD < len(units):
            pending.append(scores(*units[n + ATTN_AHEAD]))
        finish(*unit, pending.pop(0))


def _attn(proj3, biasm, q_gain, k_gain, sinks, cast_weights):
    B, S, _ = proj3.shape
    tq, blk = ATTN_TQ, ATTN_BLOCK
    nsub = tq // blk
    kvw = ATTN_KV_W
    seg_id = np.arange(kvw) // HEAD_DIM
    seg = jnp.asarray(seg_id[:, None] == seg_id[None, :], dtype=BF16)
    sink_rows = jnp.repeat(sinks.reshape(N_KV_HEADS, 2, 2).transpose(0, 2, 1), blk, axis=-1)
    sink_rows = sink_rows.reshape(N_KV_HEADS, 2, 1, 2 * blk)

    def prev(i):
        return jnp.maximum(i * nsub - 1, 0)

    bias_block = (None,) + biasm.shape[1:]
    cast_specs = _cast_specs(cast_weights, B, S // tq)
    outs = pl.pallas_call(
        functools.partial(_attn_kernel, n_cast=len(cast_weights)),
        grid=(B, S // tq),
        in_specs=[pl.BlockSpec((None, tq, ATTN_Q_W), lambda b, i: (b, i, COL_AQ // ATTN_Q_W)),
                  pl.BlockSpec((None, tq, kvw), lambda b, i: (b, i, COL_AK // kvw)),
                  pl.BlockSpec((None, blk, kvw), lambda b, i: (b, prev(i), COL_AK // kvw)),
                  pl.BlockSpec((None, tq, kvw), lambda b, i: (b, i, COL_AV // kvw)),
                  pl.BlockSpec((None, blk, kvw), lambda b, i: (b, prev(i), COL_AV // kvw)),
                  pl.BlockSpec(bias_block, lambda b, i: (jnp.minimum(i, 1), 0, 0, 0)),
                  pl.BlockSpec(bias_block, lambda b, i: (1, 0, 0, 0)),
                  pl.BlockSpec((1, kvw), lambda b, i: (0, 0)),
                  pl.BlockSpec((1, kvw), lambda b, i: (0, 0)),
                  pl.BlockSpec(sink_rows.shape, lambda b, i: (0, 0, 0, 0)),
                  pl.BlockSpec(seg.shape, lambda b, i: (0, 0))] + cast_specs,
        out_specs=[pl.BlockSpec((None, tq, ATTN_Q_W), lambda b, i: (b, i, 0))] + cast_specs,
        out_shape=[jax.ShapeDtypeStruct((B, S, ATTN_Q_W), BF16)]
        + [jax.ShapeDtypeStruct(cw.shape, BF16) for cw in cast_weights],
        compiler_params=_params("arbitrary", "arbitrary"),
        name="attn",
    )(proj3, proj3, proj3, proj3, proj3, biasm, biasm, jnp.tile(q_gain, (1, GQA_GROUP)),
      jnp.tile(k_gain, (1, N_KV_HEADS)), sink_rows, seg, *cast_weights)
    return outs[0], outs[1:]


def _split2(x):
    hi = x.astype(BF16)
    return hi, (x - hi.astype(F32)).astype(BF16)


def _gla_head(q, k, v, r, lr, wgk, bgk, gain, state_ref, masks, store):
    C = GLA_CHUNK
    tril, m_same, m_next, m_far, eye = masks
    z = _dot(lr, wgk) + bgk
    yield
    gl = (jnp.minimum(z, 0.0) - jnp.log(1.0 + jnp.exp(-jnp.abs(z)))) * (1.0 / GLA_NORMALIZER)
    hi, lo = _split2(gl)
    g = _dot(tril, hi) + _dot(tril, lo)
    yield
    t = [g[(c + 1) * C - 1:(c + 1) * C] for c in range(4)]
    t_rows = jnp.concatenate([jnp.broadcast_to(tc, (C, tc.shape[1])) for tc in t], axis=0)

    q_dec = q.astype(F32) * (GLA_DK ** -0.5) * jnp.exp(g)
    kf = k.astype(F32)
    k_inv = (kf * jnp.exp(-g)).astype(BF16)
    k_end = kf * jnp.exp(t_rows - g)
    qd = [q_dec[c * C:(c + 1) * C] for c in range(4)]
    ke = [k_end[c * C:(c + 1) * C] for c in range(4)]

    def rows(parts):
        return jnp.concatenate(parts, axis=0).astype(BF16)

    q_b, k_b = q_dec.astype(BF16), k_end.astype(BF16)
    q_far = rows([qd[0], qd[1], qd[2], qd[3] * jnp.exp(t[2])])
    k_far = rows([ke[0] * jnp.exp(t[1]), ke[1], ke[2], ke[3]])
    q_abs = rows([qd[0], qd[1] * jnp.exp(t[0]), qd[2] * jnp.exp(t[0] + t[1]), qd[3] * jnp.exp(t[0] + t[1] + t[2])])
    k_abs = rows([ke[0] * jnp.exp(t[1] + t[2] + t[3]), ke[1] * jnp.exp(t[2] + t[3]), ke[2] * jnp.exp(t[3]), ke[3]])

    a_same, a_next, a_far = _dot_nt(q_b, k_inv), _dot_nt(q_b, k_b), _dot_nt(q_far, k_far)
    state = state_ref[...]
    o_state = _dot(q_abs, state.astype(BF16))
    update = _dot_tn(k_abs, v)
    yield
    a = jnp.where(m_same, a_same, jnp.where(m_next, a_next, jnp.where(m_far, a_far, 0.0)))
    o = _dot(a.astype(BF16), v) + o_state
    yield

    decay = jnp.exp(t[0] + t[1] + t[2] + t[3])
    decay_col = jnp.sum(jnp.where(eye, jnp.broadcast_to(decay, eye.shape), 0.0), axis=1, keepdims=True)
    state_ref[...] = decay_col * state + update

    rf = r.astype(F32)
    store((_rms(o, gain) * (rf * jax.nn.sigmoid(rf))).astype(BF16))


def _gla_kernel(q_ref, k_ref, v_ref, r_ref, lr_ref, wgk_ref, bgk_ref, gain_ref, o_ref, state_ref):
    R, C, dk, dv = GLA_ROWS, GLA_CHUNK, GLA_DK, GLA_DV

    @pl.when(pl.program_id(2) == 0)
    def _():
        state_ref[...] = jnp.zeros_like(state_ref)

    row = lax.broadcasted_iota(jnp.int32, (R, R), 0)
    col = lax.broadcasted_iota(jnp.int32, (R, R), 1)
    ci, cj = row // C, col // C
    m_same = (row >= col) & (ci == cj)
    m_next = (ci == cj + 1) & (ci != 2)
    m_far = (ci >= 2) & (cj <= 1)
    eye = lax.broadcasted_iota(jnp.int32, (dk, dk), 0) == lax.broadcasted_iota(jnp.int32, (dk, dk), 1)
    masks = (m_same.astype(BF16), m_same, m_next, m_far, eye)

    heads = []
    for b in range(GLA_BATCH_PER_STEP):
        lr = lr_ref[b]
        for h in range(GLA_HEADS_PER_STEP):
            ks, vs = slice(h * dk, (h + 1) * dk), slice(h * dv, (h + 1) * dv)
            heads.append(_gla_head(q_ref[b, :, ks], k_ref[b, :, ks], v_ref[b, :, vs], r_ref[b, :, vs], lr,
                                   wgk_ref[:, ks], bgk_ref[:, ks], gain_ref[...], state_ref.at[b, h], masks,
                                   functools.partial(o_ref.__setitem__, (b, slice(None), vs))))
    for _ in itertools.zip_longest(*heads):
        pass


def _gla(proj3, wgk, bgk, gain):
    B, S, _ = proj3.shape
    R, hps, bps = GLA_ROWS, GLA_HEADS_PER_STEP, GLA_BATCH_PER_STEP
    dk, dv = hps * GLA_DK, hps * GLA_DV
    return pl.pallas_call(
        _gla_kernel,
        grid=(B // bps, GLA_HEADS // hps, S // R),
        in_specs=[pl.BlockSpec((bps, R, dk), lambda b, h, t: (b, t, COL_GQ // dk + h)),
                  pl.BlockSpec((bps, R, dk), lambda b, h, t: (b, t, COL_GK // dk + h)),
                  pl.BlockSpec((bps, R, dv), lambda b, h, t: (b, t, COL_GV // dv + h)),
                  pl.BlockSpec((bps, R, dv), lambda b, h, t: (b, t, COL_GR // dv + h)),
                  pl.BlockSpec((bps, R, LR_PAD), lambda b, h, t: (b, t, COL_LR // LR_PAD)),
                  pl.BlockSpec((LR_PAD, dk), lambda b, h, t: (0, h)),
                  pl.BlockSpec((1, dk), lambda b, h, t: (0, h)),
                  pl.BlockSpec((1, GLA_DV), lambda b, h, t: (0, 0))],
        out_specs=pl.BlockSpec((bps, R, dv), lambda b, h, t: (b, t, h)),
        out_shape=jax.ShapeDtypeStruct((B, S, GLA_V_W), BF16),
        scratch_shapes=[pltpu.VMEM((bps, hps, GLA_DK, GLA_DV), F32)],
        compiler_params=_params("parallel", "parallel", "arbitrary"),
        name="gla",
    )(proj3, proj3, proj3, proj3, proj3, wgk, bgk, gain)


def _merge_kernel(ya_ref, yg_ref, ga_ref, gb_ref, x_ref, mod_ref, g2_ref, wa_ref, wg_ref, wo_ref,
                  x1_ref, h2_ref):
    def merged_branches(rows):
        ma, mg = _dot(ya_ref[rows, :], wa_ref[...]), _dot(yg_ref[rows, :], wg_ref[...])
        ga = jax.nn.sigmoid(ga_ref[rows, :].astype(F32))
        gb = jax.nn.sigmoid(gb_ref[rows, :].astype(F32))
        return (ga * ma + gb * mg).astype(BF16)

    def project(rows, merged):
        x1 = x_ref[rows, :] + mod_ref[2:3, :] * _dot(merged, wo_ref[...])
        x1_ref[rows, :] = x1
        h2_ref[rows, :] = (_rms(x1, g2_ref[...]) * (1.0 + mod_ref[4:5, :]) + mod_ref[3:4, :]).astype(BF16)

    for r0 in range(0, MERGE_TM, MERGE_SUB):
        rows = slice(r0, r0 + MERGE_SUB)
        project(rows, merged_branches(rows))


def _merge(ya, yg, proj, x2, mod3, gain2, wa, wg, wo):
    T, D = x2.shape
    S = T // mod3.shape[0]
    tm = MERGE_TM
    once = pl.Buffered(1)
    return pl.pallas_call(
        _merge_kernel,
        grid=(T // tm,),
        in_specs=[pl.BlockSpec((tm, ATTN_Q_W), lambda i: (i, 0)),
                  pl.BlockSpec((tm, GLA_V_W), lambda i: (i, 0)),
                  pl.BlockSpec((tm, D), lambda i: (i, COL_GA // D)),
                  pl.BlockSpec((tm, D), lambda i: (i, COL_GB // D)),
                  pl.BlockSpec((tm, D), lambda i: (i, 0)),
                  pl.BlockSpec((None, 6, D), lambda i: (i // (S // tm), 0, 0)),
                  pl.BlockSpec((1, D), lambda i: (0, 0)),
                  pl.BlockSpec((ATTN_Q_W, D), lambda i: (0, 0), pipeline_mode=once),
                  pl.BlockSpec((GLA_V_W, D), lambda i: (0, 0), pipeline_mode=once),
                  pl.BlockSpec((D, D), lambda i: (0, 0), pipeline_mode=once)],
        out_specs=[pl.BlockSpec((tm, D), lambda i: (i, 0)),
                   pl.BlockSpec((tm, D), lambda i: (i, 0))],
        out_shape=[jax.ShapeDtypeStruct((T, D), F32), jax.ShapeDtypeStruct((T, D), BF16)],
        compiler_params=_params("parallel"),
        name="merge",
    )(ya, yg, proj, proj, x2, mod3, gain2, wa, wg, wo)


def _ffn_kernel(h_ref, halo_ref, wa_ref, wb_ref, cwa_ref, cwb_ref, cba_ref, cbb_ref, wd_ref, x_ref, mod_ref,
                o_ref, hs_ref, ua_ref, ub_ref, act0_ref, act1_ref, *, n_j, tiles_per_seq):
    tm, halo = FFN_TM, FFN_HALO
    j = pl.program_id(1)
    half = o_ref.shape[1] // 2

    def conv(u_ref, cw_ref, cb_ref, r0, n):
        u = u_ref[halo + r0 - 8:halo + r0 + n, :]
        y = cb_ref[...] + cw_ref[0:1, :] * pltpu.roll(u, 2, axis=0)[8:]
        y = y + cw_ref[1:2, :] * pltpu.roll(u, 1, axis=0)[8:]
        return y + cw_ref[2:3, :] * u[8:]

    def down(act_ref):
        for c0 in (0, half):
            o_ref[:, c0:c0 + half] += _dot(act_ref[...], wd_ref[:, c0:c0 + half])

    def up(act_out_ref, act_prev_ref):
        ua_ref[...] = _dot(hs_ref[...], wa_ref[...])
        ub_ref[...] = _dot(hs_ref[...], wb_ref[...])
        if act_prev_ref is not None:
            down(act_prev_ref)
        for r0 in range(0, tm, FFN_EPI_ROWS):
            ya = conv(ua_ref, cwa_ref, cba_ref, r0, FFN_EPI_ROWS)
            yb = conv(ub_ref, cwb_ref, cbb_ref, r0, FFN_EPI_ROWS)
            act_out_ref[r0:r0 + FFN_EPI_ROWS, :] = (ya * jax.nn.sigmoid(ya) * yb).astype(BF16)

    @pl.when(j == 0)
    def _():
        first = (pl.program_id(0) % tiles_per_seq) == 0
        hs_ref[0:halo, :] = jnp.where(first, jnp.zeros_like(halo_ref), halo_ref[...])
        hs_ref[halo:, :] = h_ref[...]
        o_ref[...] = jnp.zeros_like(o_ref)
        up(act0_ref, None)

    @pl.when((j > 0) & (j < n_j) & (j % 2 == 1))
    def _():
        up(act1_ref, act0_ref)

    @pl.when((j > 0) & (j < n_j) & (j % 2 == 0))
    def _():
        up(act0_ref, act1_ref)

    @pl.when(j == n_j)
    def _():
        down((act0_ref, act1_ref)[(n_j - 1) % 2])
        o_ref[...] = x_ref[...] + mod_ref[5:6, :] * o_ref[...]


def _ffn(h2, S, w_up, conv_w, conv_b, w_down, x1, mod3):
    T, D = h2.shape
    tm, tn, halo = FFN_TM, FFN_TN, FFN_HALO
    nj = D_FF // tn

    def col(j):
        return jnp.minimum(j, nj - 1)

    return pl.pallas_call(
        functools.partial(_ffn_kernel, n_j=nj, tiles_per_seq=S // tm),
        grid=(T // tm, nj + 1),
        in_specs=[pl.BlockSpec((tm, D), lambda i, j: (i, 0)),
                  pl.BlockSpec((halo, D), lambda i, j: (jnp.maximum(i * (tm // halo) - 1, 0), 0)),
                  pl.BlockSpec((D, tn), lambda i, j: (0, col(j))),
                  pl.BlockSpec((D, tn), lambda i, j: (0, col(j) + nj)),
                  pl.BlockSpec((3, tn), lambda i, j: (0, col(j))),
                  pl.BlockSpec((3, tn), lambda i, j: (0, col(j) + nj)),
                  pl.BlockSpec((1, tn), lambda i, j: (0, col(j))),
                  pl.BlockSpec((1, tn), lambda i, j: (0, col(j) + nj)),
                  pl.BlockSpec((tn, D), lambda i, j: (jnp.maximum(j - 1, 0), 0)),
                  pl.BlockSpec((tm, D), lambda i, j: (i, 0), pipeline_mode=pl.Buffered(1)),
                  pl.BlockSpec((None, 6, D), lambda i, j: (i // (S // tm), 0, 0))],
        out_specs=pl.BlockSpec((tm, D), lambda i, j: (i, 0)),
        out_shape=jax.ShapeDtypeStruct((T, D), F32),
        scratch_shapes=[pltpu.VMEM((tm + halo, D), BF16),
                        pltpu.VMEM((tm + halo, tn), F32),
                        pltpu.VMEM((tm + halo, tn), F32),
                        pltpu.VMEM((tm, tn), BF16),
                        pltpu.VMEM((tm, tn), BF16)],
        compiler_params=_params("parallel", "arbitrary"),
        name="ffn",
    )(h2, h2, w_up, w_up, conv_w, conv_w, conv_b, conv_b, w_down, x1, mod3)


def _layer(x2, B, mod3, biasm, norm1_gain, w_in, q_norm_gain, k_norm_gain, attn_sinks, w_gk_up, b_gk,
           gla_norm_gain, w_branch_attn, w_branch_gla, w_out, norm2_gain, w_ffn_up, ffn_conv_w,
           ffn_conv_b, w_ffn_down):
    T, D = x2.shape
    S = T // B
    w_p = _wpack(w_in.T)
    wgk = jnp.concatenate([w_gk_up, jnp.zeros((LR_PAD - GLA_LOWRANK, GLA_K_W), w_gk_up.dtype)],
                          axis=0).astype(BF16)

    proj = _inproj(x2, mod3, norm1_gain.reshape(1, D), w_p)
    proj3 = proj.reshape(B, S, PROJ_W)
    ya, (w_ba, w_bg, w_o, w_up, w_down) = _attn(
        proj3, biasm, q_norm_gain.reshape(1, HEAD_DIM), k_norm_gain.reshape(1, HEAD_DIM), attn_sinks,
        (w_branch_attn, w_branch_gla, w_out, w_ffn_up, w_ffn_down))
    yg = _gla(proj3, wgk, b_gk.reshape(1, GLA_K_W), gla_norm_gain.reshape(1, GLA_DV))
    x1, h2 = _merge(ya.reshape(T, ATTN_Q_W), yg.reshape(T, GLA_V_W), proj, x2, mod3,
                    norm2_gain.reshape(1, D), w_ba, w_bg, w_o)
    return _ffn(h2, S, w_up, ffn_conv_w, ffn_conv_b.reshape(1, 2 * D_FF), w_down, x1, mod3)


def kernel(x, c, rel_bias_table, w_ada, b_ada, norm1_gain, w_in, q_norm_gain, k_norm_gain, attn_sinks,
           w_gk_up, b_gk, gla_norm_gain, w_branch_attn, w_branch_gla, w_out, norm2_gain, w_ffn_up,
           ffn_conv_w, ffn_conv_b, w_ffn_down):
    B, S, D = x.shape
    depth = w_in.shape[0]
    assert D == D_MODEL and w_in.shape[1:] == (D_MODEL, SRC_GA + 2 * D_MODEL) and w_ffn_up.shape[2] == 2 * D_FF
    assert S % max(FFN_TM, INPROJ_TM, ATTN_TQ, GLA_ROWS, MERGE_TM) == 0
    biasm = _relbias(rel_bias_table)
    x2 = x.reshape(B * S, D)
    for l in range(depth):
        mod3 = _adaln(c, w_ada[l], b_ada[l]).reshape(B, 6, D)
        x2 = _layer(x2, B, mod3, biasm, norm1_gain[l], w_in[l], q_norm_gain[l], k_norm_gain[l],
                    attn_sinks[l], w_gk_up[l], b_gk[l], gla_norm_gain[l], w_branch_attn[l],
                    w_branch_gla[l], w_out[l], norm2_gain[l], w_ffn_up[l], ffn_conv_w[l], ffn_conv_b[l],
                    w_ffn_down[l])
    return x2.reshape(B, S, D)
```

```python
import functools
import itertools
import math

import numpy as np
import jax
import jax.numpy as jnp
from jax import lax
from jax.experimental import pallas as pl
from jax.experimental.pallas import tpu as pltpu

F32 = jnp.float32
BF16 = jnp.bfloat16

D_MODEL = 2048
N_Q_HEADS = 16
N_KV_HEADS = 4
GQA_GROUP = N_Q_HEADS // N_KV_HEADS
HEAD_DIM = 64
WINDOW = 128
ATTN_BLOCK = 128
N_BUCKETS = 32
MAX_DISTANCE = 128
GLA_HEADS = 4
GLA_DK = 256
GLA_DV = 512
GLA_LOWRANK = 16
GLA_NORMALIZER = 16.0
GLA_CHUNK = 64
D_FF = 5632
EPS = 1e-6
NEG_INF = -1e30
LOG2E = math.log2(math.e)

ATTN_Q_W = N_Q_HEADS * HEAD_DIM
ATTN_KV_W = N_KV_HEADS * HEAD_DIM
GLA_K_W = GLA_HEADS * GLA_DK
GLA_V_W = GLA_HEADS * GLA_DV

COL_GA = 0
COL_GB = COL_GA + D_MODEL
COL_GV = COL_GB + D_MODEL
COL_GR = COL_GV + GLA_V_W
COL_GQ = COL_GR + GLA_V_W
COL_GK = COL_GQ + GLA_K_W
COL_AQ = COL_GK + GLA_K_W
COL_AK = COL_AQ + ATTN_Q_W
COL_AV = COL_AK + ATTN_KV_W
COL_LR = COL_AV + ATTN_KV_W
LANE = 128
BF16_ROWS = 16
VMEM_LIMIT = 60 * 1024 * 1024

LR_PAD = LANE
PROJ_W = 12288

ADALN_TN = 1024
INPROJ_TM, INPROJ_TN = 1024, 3072
INPROJ_NORM_ROWS = 256
ATTN_TQ = 512
ATTN_AHEAD = 2
GLA_ROWS = 256
GLA_HEADS_PER_STEP = 4
GLA_BATCH_PER_STEP = 2
MERGE_TM, MERGE_SUB = 512, 256
FFN_TM, FFN_TN = 1024, 512
FFN_HALO = BF16_ROWS
FFN_EPI_ROWS = 256
DOWN_TM = 512


def _params(*sem):
    return pltpu.CompilerParams(dimension_semantics=sem, vmem_limit_bytes=VMEM_LIMIT)


def _dot(a, b):
    return jnp.dot(a, b, preferred_element_type=F32)


def _dot_nt(a, b):
    return lax.dot_general(a, b, (((1,), (1,)), ((), ())), preferred_element_type=F32)


def _dot_tn(a, b):
    return lax.dot_general(a, b, (((0,), (0,)), ((), ())), preferred_element_type=F32)


def _rms(x, gain):
    return x * lax.rsqrt(jnp.mean(x * x, axis=-1, keepdims=True) + EPS) * gain


def _adaln_kernel(c_ref, w_ref, b_ref, o_ref):
    c = c_ref[...]
    ca = c * jax.nn.sigmoid(c)
    o_ref[...] = _dot(ca.astype(BF16), w_ref[...].astype(BF16)) + b_ref[...]


def _adaln(c, w_ada, b_ada):
    B, D = c.shape
    N = w_ada.shape[1]
    return pl.pallas_call(
        _adaln_kernel,
        grid=(N // ADALN_TN,),
        in_specs=[pl.BlockSpec((B, D), lambda j: (0, 0)),
                  pl.BlockSpec((D, ADALN_TN), lambda j: (0, j)),
                  pl.BlockSpec((1, ADALN_TN), lambda j: (0, j))],
        out_specs=pl.BlockSpec((B, ADALN_TN), lambda j: (0, j)),
        out_shape=jax.ShapeDtypeStruct((B, N), F32),
        compiler_params=_params("parallel"),
        name="adaln",
    )(c, w_ada, b_ada.reshape(1, N))


def _bucket_table():
    j = np.arange(ATTN_BLOCK)[:, None]
    i = np.arange(ATTN_BLOCK)[None, :]
    dist = np.where(j > i, i + ATTN_BLOCK - j, i - j)
    max_exact = N_BUCKETS // 2
    d = np.maximum(dist, 0)
    ratio = np.log(np.maximum(d, 1).astype(np.float32) / np.float32(max_exact)) / np.float32(
        math.log(MAX_DISTANCE / max_exact))
    large = max_exact + (ratio.astype(np.float32) * np.float32(N_BUCKETS - max_exact)).astype(np.int32)
    large = np.minimum(large, N_BUCKETS - 1)
    bucket = np.where(d < max_exact, d, large)
    in_window = (dist >= 0) & (dist < WINDOW)
    return np.where(in_window, bucket, -1).astype(np.int32)


def _relbias_kernel(tab_ref, bkt_ref, o_ref):
    g = pl.program_id(0)
    bkt = bkt_ref[...]
    nk, nq = bkt.shape
    from_prev = lax.broadcasted_iota(jnp.int32, bkt.shape, 0) > lax.broadcasted_iota(jnp.int32, bkt.shape, 1)
    for e in range(2):
        for p in range(2):
            h = g * GQA_GROUP + 2 * p + e
            acc = jnp.zeros(bkt.shape, F32)
            for b in range(N_BUCKETS):
                acc = jnp.where(bkt == b, tab_ref[b, h], acc)
            regular = jnp.where(bkt >= 0, acc * LOG2E, NEG_INF)
            tile = (slice(e * nk, (e + 1) * nk), slice(p * nq, (p + 1) * nq))
            o_ref[(0,) + tile] = jnp.where(from_prev, NEG_INF, regular)
            o_ref[(1,) + tile] = regular


def _relbias(rel_bias_table):
    bkt = jnp.asarray(_bucket_table())
    nk, nq = bkt.shape
    return pl.pallas_call(
        _relbias_kernel,
        grid=(N_KV_HEADS,),
        in_specs=[pl.BlockSpec(memory_space=pltpu.SMEM),
                  pl.BlockSpec(bkt.shape, lambda g: (0, 0))],
        out_specs=pl.BlockSpec((2, None, 2 * nk, 2 * nq), lambda g: (0, g, 0, 0)),
        out_shape=jax.ShapeDtypeStruct((2, N_KV_HEADS, 2 * nk, 2 * nq), F32),
        compiler_params=_params("parallel"),
        name="relbias",
    )(rel_bias_table, bkt)


SRC_GQ = ATTN_Q_W + 2 * ATTN_KV_W
SRC_GV = SRC_GQ + 2 * GLA_K_W
SRC_LR = SRC_GV + 2 * GLA_V_W
SRC_GA = SRC_LR + GLA_LOWRANK
WPACK_ROWS = 512


def _wpack_src_row(b):
    r = b * WPACK_ROWS
    src = jnp.where(r < COL_GV, SRC_GA + r,
                    jnp.where(r < COL_GQ, SRC_GV + (r - COL_GV),
                              jnp.where(r < COL_AQ, SRC_GQ + (r - COL_GQ),
                                        jnp.where(r < COL_LR, r - COL_AQ, SRC_LR))))
    return pl.multiple_of(src, 8)


def _wpack_kernel(w_ref, o_ref):
    r = pl.program_id(0) * WPACK_ROWS
    nvalid = jnp.where(r < COL_LR, WPACK_ROWS, jnp.where(r == COL_LR, GLA_LOWRANK, 0))
    row = lax.broadcasted_iota(jnp.int32, w_ref.shape, 0)
    o_ref[...] = jnp.where(row < nvalid, w_ref[...], 0.0).astype(BF16)


def _wpack(w_in_t):
    _, D = w_in_t.shape
    return pl.pallas_call(
        _wpack_kernel,
        grid=(PROJ_W // WPACK_ROWS,),
        in_specs=[pl.BlockSpec((pl.Element(WPACK_ROWS), pl.Element(D)), lambda b: (_wpack_src_row(b), 0))],
        out_specs=pl.BlockSpec((WPACK_ROWS, D), lambda b: (b, 0)),
        out_shape=jax.ShapeDtypeStruct((PROJ_W, D), BF16),
        compiler_params=_params("parallel"),
        name="wpack",
    )(w_in_t)


def _norm_modulate(x, mod_ref, gain):
    return (_rms(x, gain) * (1.0 + mod_ref[1:2, :]) + mod_ref[0:1, :]).astype(BF16)


def _hnorm_kernel(x_ref, mod_ref, g_ref, o_ref):
    o_ref[...] = _norm_modulate(x_ref[...], mod_ref, g_ref[...])


def _hnorm_first(x2, mod3, gain):
    D = x2.shape[1]
    nr = INPROJ_NORM_ROWS
    return pl.pallas_call(
        _hnorm_kernel,
        grid=(INPROJ_TM // nr,),
        in_specs=[pl.BlockSpec((nr, D), lambda r: (r, 0)),
                  pl.BlockSpec((None, 6, D), lambda r: (0, 0, 0)),
                  pl.BlockSpec((1, D), lambda r: (0, 0))],
        out_specs=pl.BlockSpec((nr, D), lambda r: (r, 0)),
        out_shape=jax.ShapeDtypeStruct((INPROJ_TM, D), BF16),
        compiler_params=_params("parallel"),
        name="hnorm_first",
    )(x2, mod3, gain)


def _cast_block(shape, steps):
    rows, cols = shape
    bc = min(cols, 1024)
    assert cols % bc == 0
    for br in range(BF16_ROWS, rows + 1, BF16_ROWS):
        if rows % br == 0 and (rows // br) * (cols // bc) <= steps:
            return br, bc
    raise ValueError(f"no cast block for {shape} in {steps} steps")


def _cast_specs(weights, n0, n1):
    def spec(shape):
        br, bc = _cast_block(shape, n0 * n1)
        ncb = shape[1] // bc
        last = (shape[0] // br) * ncb - 1

        def index(i, j):
            t = jnp.minimum(i * n1 + j, last)
            return t // ncb, t % ncb

        return pl.BlockSpec((br, bc), index)

    return [spec(w.shape) for w in weights]


def _inproj_kernel(xn_ref, modn_ref, g_ref, w_ref, h0_ref, o_ref, ha_ref, hb_ref):
    i, j = pl.program_id(0), pl.program_id(1)
    nr = INPROJ_NORM_ROWS

    @pl.when((i == 0) & (j == 0))
    def _():
        ha_ref[...] = h0_ref[...]

    def step(cur_ref, nxt_ref):
        o_ref[...] = _dot_nt(cur_ref[...], w_ref[...]).astype(BF16)
        chunk = jnp.minimum(j, INPROJ_TM // nr - 1)
        rows = pl.ds(pl.multiple_of(chunk * nr, nr), nr)
        nxt_ref[rows, :] = _norm_modulate(xn_ref[...], modn_ref, g_ref[...])

    @pl.when(i % 2 == 0)
    def _():
        step(ha_ref, hb_ref)

    @pl.when(i % 2 == 1)
    def _():
        step(hb_ref, ha_ref)


def _inproj(x2, mod3, gain, w):
    T, D = x2.shape
    S = T // mod3.shape[0]
    tm, tn, nr = INPROJ_TM, INPROJ_TN, INPROJ_NORM_ROWS
    n_i, chunks = T // tm, tm // nr
    assert PROJ_W // tn >= chunks

    def next_tile(i):
        return jnp.minimum(i + 1, n_i - 1)

    return pl.pallas_call(
        _inproj_kernel,
        grid=(n_i, PROJ_W // tn),
        in_specs=[pl.BlockSpec((nr, D), lambda i, j: (next_tile(i) * chunks + jnp.minimum(j, chunks - 1), 0)),
                  pl.BlockSpec((None, 6, D), lambda i, j: (next_tile(i) // (S // tm), 0, 0)),
                  pl.BlockSpec((1, D), lambda i, j: (0, 0)),
                  pl.BlockSpec((tn, D), lambda i, j: (j, 0)),
                  pl.BlockSpec((tm, D), lambda i, j: (0, 0), pipeline_mode=pl.Buffered(1))],
        out_specs=pl.BlockSpec((tm, tn), lambda i, j: (i, j)),
        out_shape=jax.ShapeDtypeStruct((T, PROJ_W), BF16),
        scratch_shapes=[pltpu.VMEM((tm, D), BF16), pltpu.VMEM((tm, D), BF16)],
        compiler_params=_params("arbitrary", "arbitrary"),
        name="inproj",
    )(x2, mod3, gain, w, _hnorm_first(x2, mod3, gain))


def _attn_kernel(q_ref, kc_ref, kp_ref, vc_ref, vp_ref, b0_ref, b1_ref, qg_ref, kg_ref, sink_ref, seg_ref,
                 *rest, n_cast):
    cast_in, o_ref, cast_out = rest[:n_cast], rest[n_cast], rest[n_cast + 1:]
    for src_ref, dst_ref in zip(cast_in, cast_out):
        dst_ref[...] = src_ref[...].astype(BF16)

    nsub = ATTN_TQ // ATTN_BLOCK
    blk, dh = ATTN_BLOCK, HEAD_DIM
    seg = seg_ref[...]

    def normed(x, gain):
        ssq = _dot((x * x).astype(BF16), seg)
        return x * lax.rsqrt(ssq * (1.0 / dh) + EPS) * gain

    kn = normed(jnp.concatenate([kp_ref[...], kc_ref[...]], axis=0).astype(F32), kg_ref[...])
    vt = jnp.concatenate([vp_ref[...], vc_ref[...]], axis=0).astype(F32).T.astype(BF16)
    lane = lax.broadcasted_iota(jnp.int32, (kn.shape[0], 2 * dh), 1)
    qgain = qg_ref[...] * (dh ** -0.5 * LOG2E)

    kzs, qns = [], []
    for g in range(N_KV_HEADS):
        pair = kn[:, (g // 2) * 2 * dh:(g // 2 + 1) * 2 * dh]
        own = jnp.where(lane >= dh if g % 2 else lane < dh, pair, 0.0)
        swapped = pltpu.roll(own, dh, axis=1)
        kz = [own, swapped] if g % 2 == 0 else [swapped, own]
        kzs.append([z.astype(BF16) for z in kz])
        qns.append(normed(q_ref[:, g * 4 * dh:(g + 1) * 4 * dh].astype(F32), qgain).astype(BF16))

    def from_prev(n_rows, n_lanes):
        slot = lax.broadcasted_iota(jnp.int32, (n_rows, n_lanes), 0) % blk
        query = lax.broadcasted_iota(jnp.int32, (n_rows, n_lanes), 1) % blk
        return slot > query

    def scores(g, sb):
        bref = b0_ref if sb == 0 else b1_ref
        prv, cur = slice(sb * blk, (sb + 1) * blk), slice((sb + 1) * blk, (sb + 2) * blk)
        rows = slice(sb * blk, (sb + 1) * blk)
        qr = jnp.concatenate([qns[g][rows, 0:2 * dh], qns[g][rows, 2 * dh:4 * dh]], axis=0)
        s_prv = _dot_nt(jnp.concatenate([kzs[g][0][prv], kzs[g][1][prv]], axis=0), qr)
        s_cur = _dot_nt(jnp.concatenate([kzs[g][0][cur], kzs[g][1][cur]], axis=0), qr)
        return jnp.where(from_prev(2 * blk, 2 * blk), s_prv, s_cur) + bref[g]

    def finish(g, sb, s):
        prv, cur = slice(sb * blk, (sb + 1) * blk), slice((sb + 1) * blk, (sb + 2) * blk)
        rows = slice(sb * blk, (sb + 1) * blk)
        ps, inv = [], []
        for e in range(2):
            se = s[e * blk:(e + 1) * blk]
            sink = sink_ref[g, e] * LOG2E
            m = jnp.maximum(jnp.max(se, axis=0, keepdims=True), sink)
            p = jnp.exp2(se - m)
            inv.append(1.0 / (jnp.sum(p, axis=0, keepdims=True) + jnp.exp2(sink - m)))
            ps.append(p)
        pm = jnp.concatenate(ps, axis=1)
        mask = from_prev(blk, 4 * blk)
        vg = vt[g * dh:(g + 1) * dh]
        ot = (_dot(vg[:, prv], jnp.where(mask, pm, 0.0).astype(BF16))
              + _dot(vg[:, cur], jnp.where(mask, 0.0, pm).astype(BF16)))
        ot = ot * jnp.concatenate(inv, axis=1)
        o = jnp.concatenate([ot[:, 0:2 * blk], ot[:, 2 * blk:4 * blk]], axis=0).T
        o_ref[rows, (2 * g) * 2 * dh:(2 * g + 1) * 2 * dh] = o[0:blk].astype(BF16)
        o_ref[rows, (2 * g + 1) * 2 * dh:(2 * g + 2) * 2 * dh] = o[blk:2 * blk].astype(BF16)

    units = [(g, sb) for g in range(N_KV_HEADS) for sb in range(nsub)]
    pending = [scores(*unit) for unit in units[:ATTN_AHEAD]]
    for n, unit in enumerate(units):
        if n + ATTN_AHEAD < len(units):
            pending.append(scores(*units[n + ATTN_AHEAD]))
        finish(*unit, pending.pop(0))


def _attn(proj3, biasm, q_gain, k_gain, sinks, cast_weights):
    B, S, _ = proj3.shape
    tq, blk = ATTN_TQ, ATTN_BLOCK
    nsub = tq // blk
    kvw = ATTN_KV_W
    seg_id = np.arange(kvw) // HEAD_DIM
    seg = jnp.asarray(seg_id[:, None] == seg_id[None, :], dtype=BF16)
    sink_rows = jnp.repeat(sinks.reshape(N_KV_HEADS, 2, 2).transpose(0, 2, 1), blk, axis=-1)
    sink_rows = sink_rows.reshape(N_KV_HEADS, 2, 1, 2 * blk)

    def prev(i):
        return jnp.maximum(i * nsub - 1, 0)

    bias_block = (None,) + biasm.shape[1:]
    cast_specs = _cast_specs(cast_weights, B, S // tq)
    outs = pl.pallas_call(
        functools.partial(_attn_kernel, n_cast=len(cast_weights)),
        grid=(B, S // tq),
        in_specs=[pl.BlockSpec((None, tq, ATTN_Q_W), lambda b, i: (b, i, COL_AQ // ATTN_Q_W)),
                  pl.BlockSpec((None, tq, kvw), lambda b, i: (b, i, COL_AK // kvw)),
                  pl.BlockSpec((None, blk, kvw), lambda b, i: (b, prev(i), COL_AK // kvw)),
                  pl.BlockSpec((None, tq, kvw), lambda b, i: (b, i, COL_AV // kvw)),
                  pl.BlockSpec((None, blk, kvw), lambda b, i: (b, prev(i), COL_AV // kvw)),
                  pl.BlockSpec(bias_block, lambda b, i: (jnp.minimum(i, 1), 0, 0, 0)),
                  pl.BlockSpec(bias_block, lambda b, i: (1, 0, 0, 0)),
                  pl.BlockSpec((1, kvw), lambda b, i: (0, 0)),
                  pl.BlockSpec((1, kvw), lambda b, i: (0, 0)),
                  pl.BlockSpec(sink_rows.shape, lambda b, i: (0, 0, 0, 0)),
                  pl.BlockSpec(seg.shape, lambda b, i: (0, 0))] + cast_specs,
        out_specs=[pl.BlockSpec((None, tq, ATTN_Q_W), lambda b, i: (b, i, 0))] + cast_specs,
        out_shape=[jax.ShapeDtypeStruct((B, S, ATTN_Q_W), BF16)]
        + [jax.ShapeDtypeStruct(cw.shape, BF16) for cw in cast_weights],
        compiler_params=_params("arbitrary", "arbitrary"),
        name="attn",
    )(proj3, proj3, proj3, proj3, proj3, biasm, biasm, jnp.tile(q_gain, (1, GQA_GROUP)),
      jnp.tile(k_gain, (1, N_KV_HEADS)), sink_rows, seg, *cast_weights)
    return outs[0], outs[1:]


def _split2(x):
    hi = x.astype(BF16)
    return hi, (x - hi.astype(F32)).astype(BF16)


def _gla_head(q, k, v, r, lr, wgk, bgk, gain, state_ref, masks, store):
    C = GLA_CHUNK
    tril, m_same, m_next, m_far, eye = masks
    z = _dot(lr, wgk) + bgk
    yield
    gl = (jnp.minimum(z, 0.0) - jnp.log(1.0 + jnp.exp(-jnp.abs(z)))) * (1.0 / GLA_NORMALIZER)
    hi, lo = _split2(gl)
    g = _dot(tril, hi) + _dot(tril, lo)
    yield
    t = [g[(c + 1) * C - 1:(c + 1) * C] for c in range(4)]
    t_rows = jnp.concatenate([jnp.broadcast_to(tc, (C, tc.shape[1])) for tc in t], axis=0)

    q_dec = q.astype(F32) * (GLA_DK ** -0.5) * jnp.exp(g)
    kf = k.astype(F32)
    k_inv = (kf * jnp.exp(-g)).astype(BF16)
    k_end = kf * jnp.exp(t_rows - g)
    qd = [q_dec[c * C:(c + 1) * C] for c in range(4)]
    ke = [k_end[c * C:(c + 1) * C] for c in range(4)]

    def rows(parts):
        return jnp.concatenate(parts, axis=0).astype(BF16)

    q_b, k_b = q_dec.astype(BF16), k_end.astype(BF16)
    q_far = rows([qd[0], qd[1], qd[2], qd[3] * jnp.exp(t[2])])
    k_far = rows([ke[0] * jnp.exp(t[1]), ke[1], ke[2], ke[3]])
    q_abs = rows([qd[0], qd[1] * jnp.exp(t[0]), qd[2] * jnp.exp(t[0] + t[1]), qd[3] * jnp.exp(t[0] + t[1] + t[2])])
    k_abs = rows([ke[0] * jnp.exp(t[1] + t[2] + t[3]), ke[1] * jnp.exp(t[2] + t[3]), ke[2] * jnp.exp(t[3]), ke[3]])

    a_same, a_next, a_far = _dot_nt(q_b, k_inv), _dot_nt(q_b, k_b), _dot_nt(q_far, k_far)
    state = state_ref[...]
    o_state = _dot(q_abs, state.astype(BF16))
    update = _dot_tn(k_abs, v)
    yield
    a = jnp.where(m_same, a_same, jnp.where(m_next, a_next, jnp.where(m_far, a_far, 0.0)))
    o = _dot(a.astype(BF16), v) + o_state
    yield

    decay = jnp.exp(t[0] + t[1] + t[2] + t[3])
    decay_col = jnp.sum(jnp.where(eye, jnp.broadcast_to(decay, eye.shape), 0.0), axis=1, keepdims=True)
    state_ref[...] = decay_col * state + update

    rf = r.astype(F32)
    store((_rms(o, gain) * (rf * jax.nn.sigmoid(rf))).astype(BF16))


def _gla_kernel(q_ref, k_ref, v_ref, r_ref, lr_ref, wgk_ref, bgk_ref, gain_ref, o_ref, state_ref):
    R, C, dk, dv = GLA_ROWS, GLA_CHUNK, GLA_DK, GLA_DV

    @pl.when(pl.program_id(2) == 0)
    def _():
        state_ref[...] = jnp.zeros_like(state_ref)

    row = lax.broadcasted_iota(jnp.int32, (R, R), 0)
    col = lax.broadcasted_iota(jnp.int32, (R, R), 1)
    ci, cj = row // C, col // C
    m_same = (row >= col) & (ci == cj)
    m_next = (ci == cj + 1) & (ci != 2)
    m_far = (ci >= 2) & (cj <= 1)
    eye = lax.broadcasted_iota(jnp.int32, (dk, dk), 0) == lax.broadcasted_iota(jnp.int32, (dk, dk), 1)
    masks = (m_same.astype(BF16), m_same, m_next, m_far, eye)

    heads = []
    for b in range(GLA_BATCH_PER_STEP):
        lr = lr_ref[b]
        for h in range(GLA_HEADS_PER_STEP):
            ks, vs = slice(h * dk, (h + 1) * dk), slice(h * dv, (h + 1) * dv)
            heads.append(_gla_head(q_ref[b, :, ks], k_ref[b, :, ks], v_ref[b, :, vs], r_ref[b, :, vs], lr,
                                   wgk_ref[:, ks], bgk_ref[:, ks], gain_ref[...], state_ref.at[b, h], masks,
                                   functools.partial(o_ref.__setitem__, (b, slice(None), vs))))
    for _ in itertools.zip_longest(*heads):
        pass


def _gla(proj3, wgk, bgk, gain):
    B, S, _ = proj3.shape
    R, hps, bps = GLA_ROWS, GLA_HEADS_PER_STEP, GLA_BATCH_PER_STEP
    dk, dv = hps * GLA_DK, hps * GLA_DV
    return pl.pallas_call(
        _gla_kernel,
        grid=(B // bps, GLA_HEADS // hps, S // R),
        in_specs=[pl.BlockSpec((bps, R, dk), lambda b, h, t: (b, t, COL_GQ // dk + h)),
                  pl.BlockSpec((bps, R, dk), lambda b, h, t: (b, t, COL_GK // dk + h)),
                  pl.BlockSpec((bps, R, dv), lambda b, h, t: (b, t, COL_GV // dv + h)),
                  pl.BlockSpec((bps, R, dv), lambda b, h, t: (b, t, COL_GR // dv + h)),
                  pl.BlockSpec((bps, R, LR_PAD), lambda b, h, t: (b, t, COL_LR // LR_PAD)),
                  pl.BlockSpec((LR_PAD, dk), lambda b, h, t: (0, h)),
                  pl.BlockSpec((1, dk), lambda b, h, t: (0, h)),
                  pl.BlockSpec((1, GLA_DV), lambda b, h, t: (0, 0))],
        out_specs=pl.BlockSpec((bps, R, dv), lambda b, h, t: (b, t, h)),
        out_shape=jax.ShapeDtypeStruct((B, S, GLA_V_W), BF16),
        scratch_shapes=[pltpu.VMEM((bps, hps, GLA_DK, GLA_DV), F32)],
        compiler_params=_params("parallel", "parallel", "arbitrary"),
        name="gla",
    )(proj3, proj3, proj3, proj3, proj3, wgk, bgk, gain)


def _merge_kernel(ya_ref, yg_ref, ga_ref, gb_ref, x_ref, mod_ref, g2_ref, wa_ref, wg_ref, wo_ref,
                  x1_ref, h2_ref):
    def merged_branches(rows):
        ma, mg = _dot(ya_ref[rows, :], wa_ref[...]), _dot(yg_ref[rows, :], wg_ref[...])
        ga = jax.nn.sigmoid(ga_ref[rows, :].astype(F32))
        gb = jax.nn.sigmoid(gb_ref[rows, :].astype(F32))
        return (ga * ma + gb * mg).astype(BF16)

    def project(rows, merged):
        x1 = x_ref[rows, :] + mod_ref[2:3, :] * _dot(merged, wo_ref[...])
        x1_ref[rows, :] = x1
        h2_ref[rows, :] = (_rms(x1, g2_ref[...]) * (1.0 + mod_ref[4:5, :]) + mod_ref[3:4, :]).astype(BF16)

    for r0 in range(0, MERGE_TM, MERGE_SUB):
        rows = slice(r0, r0 + MERGE_SUB)
        project(rows, merged_branches(rows))


def _merge(ya, yg, proj, x2, mod3, gain2, wa, wg, wo):
    T, D = x2.shape
    S = T // mod3.shape[0]
    tm = MERGE_TM
    once = pl.Buffered(1)
    return pl.pallas_call(
        _merge_kernel,
        grid=(T // tm,),
        in_specs=[pl.BlockSpec((tm, ATTN_Q_W), lambda i: (i, 0)),
                  pl.BlockSpec((tm, GLA_V_W), lambda i: (i, 0)),
                  pl.BlockSpec((tm, D), lambda i: (i, COL_GA // D)),
                  pl.BlockSpec((tm, D), lambda i: (i, COL_GB // D)),
                  pl.BlockSpec((tm, D), lambda i: (i, 0)),
                  pl.BlockSpec((None, 6, D), lambda i: (i // (S // tm), 0, 0)),
                  pl.BlockSpec((1, D), lambda i: (0, 0)),
                  pl.BlockSpec((ATTN_Q_W, D), lambda i: (0, 0), pipeline_mode=once),
                  pl.BlockSpec((GLA_V_W, D), lambda i: (0, 0), pipeline_mode=once),
                  pl.BlockSpec((D, D), lambda i: (0, 0), pipeline_mode=once)],
        out_specs=[pl.BlockSpec((tm, D), lambda i: (i, 0)),
                   pl.BlockSpec((tm, D), lambda i: (i, 0))],
        out_shape=[jax.ShapeDtypeStruct((T, D), F32), jax.ShapeDtypeStruct((T, D), BF16)],
        compiler_params=_params("parallel"),
        name="merge",
    )(ya, yg, proj, proj, x2, mod3, gain2, wa, wg, wo)


def _ffn_up_kernel(h_ref, halo_ref, wa_ref, wb_ref, cwa_ref, cwb_ref, cba_ref, cbb_ref, o_ref,
                   hs_ref, ua_ref, ub_ref, *, tiles_per_seq):
    tm, halo = FFN_TM, FFN_HALO

    @pl.when(pl.program_id(1) == 0)
    def _():
        first = (pl.program_id(0) % tiles_per_seq) == 0
        hs_ref[0:halo, :] = jnp.where(first, jnp.zeros_like(halo_ref), halo_ref[...])
        hs_ref[halo:, :] = h_ref[...]

    def conv(u_ref, cw_ref, cb_ref, r0, n):
        u = u_ref[halo + r0 - 8:halo + r0 + n, :]
        y = cb_ref[...] + cw_ref[0:1, :] * pltpu.roll(u, 2, axis=0)[8:]
        y = y + cw_ref[1:2, :] * pltpu.roll(u, 1, axis=0)[8:]
        return y + cw_ref[2:3, :] * u[8:]

    ua_ref[...] = _dot(hs_ref[...], wa_ref[...])
    ub_ref[...] = _dot(hs_ref[...], wb_ref[...])
    for r0 in range(0, tm, FFN_EPI_ROWS):
        ya = conv(ua_ref, cwa_ref, cba_ref, r0, FFN_EPI_ROWS)
        yb = conv(ub_ref, cwb_ref, cbb_ref, r0, FFN_EPI_ROWS)
        o_ref[r0:r0 + FFN_EPI_ROWS, :] = (ya * jax.nn.sigmoid(ya) * yb).astype(BF16)


def _ffn_up(h2, S, w_up, conv_w, conv_b):
    T, D = h2.shape
    tm, tn, halo = FFN_TM, FFN_TN, FFN_HALO
    nj = D_FF // tn
    return pl.pallas_call(
        functools.partial(_ffn_up_kernel, tiles_per_seq=S // tm),
        grid=(T // tm, D_FF // tn),
        in_specs=[pl.BlockSpec((tm, D), lambda i, j: (i, 0)),
                  pl.BlockSpec((halo, D), lambda i, j: (jnp.maximum(i * (tm // halo) - 1, 0), 0)),
                  pl.BlockSpec((D, tn), lambda i, j: (0, j)),
                  pl.BlockSpec((D, tn), lambda i, j: (0, j + nj)),
                  pl.BlockSpec((3, tn), lambda i, j: (0, j)),
                  pl.BlockSpec((3, tn), lambda i, j: (0, j + nj)),
                  pl.BlockSpec((1, tn), lambda i, j: (0, j)),
                  pl.BlockSpec((1, tn), lambda i, j: (0, j + nj))],
        out_specs=pl.BlockSpec((tm, tn), lambda i, j: (i, j)),
        out_shape=jax.ShapeDtypeStruct((T, D_FF), BF16),
        scratch_shapes=[pltpu.VMEM((tm + halo, D), BF16),
                        pltpu.VMEM((tm + halo, tn), F32),
                        pltpu.VMEM((tm + halo, tn), F32)],
        compiler_params=_params("parallel", "arbitrary"),
        name="ffn_up",
    )(h2, h2, w_up, w_up, conv_w, conv_w, conv_b, conv_b)


def _ffn_down_kernel(a_ref, w_ref, x_ref, mod_ref, o_ref):
    o_ref[...] = x_ref[...] + mod_ref[5:6, :] * _dot(a_ref[...], w_ref[...])


def _ffn_down(act, wd, x1, mod3):
    T, D = x1.shape
    S = T // mod3.shape[0]
    tm = DOWN_TM
    return pl.pallas_call(
        _ffn_down_kernel,
        grid=(T // tm,),
        in_specs=[pl.BlockSpec((tm, D_FF), lambda i: (i, 0)),
                  pl.BlockSpec((D_FF, D), lambda i: (0, 0), pipeline_mode=pl.Buffered(1)),
                  pl.BlockSpec((tm, D), lambda i: (i, 0)),
                  pl.BlockSpec((None, 6, D), lambda i: (i // (S // tm), 0, 0))],
        out_specs=pl.BlockSpec((tm, D), lambda i: (i, 0)),
        out_shape=jax.ShapeDtypeStruct((T, D), F32),
        compiler_params=_params("parallel"),
        name="ffn_down",
    )(act, wd, x1, mod3)


def _layer(x2, B, mod3, biasm, norm1_gain, w_in, q_norm_gain, k_norm_gain, attn_sinks, w_gk_up, b_gk,
           gla_norm_gain, w_branch_attn, w_branch_gla, w_out, norm2_gain, w_ffn_up, ffn_conv_w,
           ffn_conv_b, w_ffn_down):
    T, D = x2.shape
    S = T // B
    w_p = _wpack(w_in.T)
    wgk = jnp.concatenate([w_gk_up, jnp.zeros((LR_PAD - GLA_LOWRANK, GLA_K_W), w_gk_up.dtype)],
                          axis=0).astype(BF16)

    proj = _inproj(x2, mod3, norm1_gain.reshape(1, D), w_p)
    proj3 = proj.reshape(B, S, PROJ_W)
    ya, (w_ba, w_bg, w_o, w_up, w_down) = _attn(
        proj3, biasm, q_norm_gain.reshape(1, HEAD_DIM), k_norm_gain.reshape(1, HEAD_DIM), attn_sinks,
        (w_branch_attn, w_branch_gla, w_out, w_ffn_up, w_ffn_down))
    yg = _gla(proj3, wgk, b_gk.reshape(1, GLA_K_W), gla_norm_gain.reshape(1, GLA_DV))
    x1, h2 = _merge(ya.reshape(T, ATTN_Q_W), yg.reshape(T, GLA_V_W), proj, x2, mod3,
                    norm2_gain.reshape(1, D), w_ba, w_bg, w_o)
    act = _ffn_up(h2, S, w_up, ffn_conv_w, ffn_conv_b.reshape(1, 2 * D_FF))
    return _ffn_down(act, w_down, x1, mod3)


def kernel(x, c, rel_bias_table, w_ada, b_ada, norm1_gain, w_in, q_norm_gain, k_norm_gain, attn_sinks,
           w_gk_up, b_gk, gla_norm_gain, w_branch_attn, w_branch_gla, w_out, norm2_gain, w_ffn_up,
           ffn_conv_w, ffn_conv_b, w_ffn_down):
    B, S, D = x.shape
    depth = w_in.shape[0]
    assert D == D_MODEL and w_in.shape[1:] == (D_MODEL, SRC_GA + 2 * D_MODEL) and w_ffn_up.shape[2] == 2 * D_FF
    assert S % max(FFN_TM, INPROJ_TM, ATTN_TQ, GLA_ROWS, MERGE_TM, DOWN_TM) == 0
    biasm = _relbias(rel_bias_table)
    x2 = x.reshape(B * S, D)
    for l in range(depth):
        mod3 = _adaln(c, w_ada[l], b_ada[l]).reshape(B, 6, D)
        x2 = _layer(x2, B, mod3, biasm, norm1_gain[l], w_in[l], q_norm_gain[l], k_norm_gain[l],
                    attn_sinks[l], w_gk_up[l], b_gk[l], gla_norm_gain[l], w_branch_attn[l],
                    w_branch_gla[l], w_out[l], norm2_gain[l], w_ffn_up[l], ffn_conv_w[l], ffn_conv_b[l],
                    w_ffn_down[l])
    return x2.reshape(B, S, D)
```

```python
import functools
import itertools
import math

import numpy as np
import jax
import jax.numpy as jnp
from jax import lax
from jax.experimental import pallas as pl
from jax.experimental.pallas import tpu as pltpu

F32 = jnp.float32
BF16 = jnp.bfloat16

D_MODEL = 2048
N_Q_HEADS = 16
N_KV_HEADS = 4
GQA_GROUP = N_Q_HEADS // N_KV_HEADS
HEAD_DIM = 64
WINDOW = 128
ATTN_BLOCK = 128
N_BUCKETS = 32
MAX_DISTANCE = 128
GLA_HEADS = 4
GLA_DK = 256
GLA_DV = 512
GLA_LOWRANK = 16
GLA_NORMALIZER = 16.0
GLA_CHUNK = 64
D_FF = 5632
EPS = 1e-6
NEG_INF = -1e30
LOG2E = math.log2(math.e)

ATTN_Q_W = N_Q_HEADS * HEAD_DIM
ATTN_KV_W = N_KV_HEADS * HEAD_DIM
GLA_K_W = GLA_HEADS * GLA_DK
GLA_V_W = GLA_HEADS * GLA_DV

COL_GA = 0
COL_GB = COL_GA + D_MODEL
COL_GV = COL_GB + D_MODEL
COL_GR = COL_GV + GLA_V_W
COL_GQ = COL_GR + GLA_V_W
COL_GK = COL_GQ + GLA_K_W
COL_AQ = COL_GK + GLA_K_W
COL_AK = COL_AQ + ATTN_Q_W
COL_AV = COL_AK + ATTN_KV_W
COL_LR = COL_AV + ATTN_KV_W
LANE = 128
BF16_ROWS = 16
VMEM_LIMIT = 60 * 1024 * 1024

LR_PAD = LANE
PROJ_W = 12288

ADALN_TN = 1024
INPROJ_TM, INPROJ_TN = 1024, 3072
INPROJ_NORM_ROWS = 256
ATTN_TQ = 512
ATTN_AHEAD = 2
GLA_ROWS = 256
GLA_HEADS_PER_STEP = 4
GLA_BATCH_PER_STEP = 2
MERGE_TM, MERGE_SUB = 512, 256
FFN_TM, FFN_TN = 1024, 512
FFN_HALO = BF16_ROWS
FFN_EPI_ROWS = 256
DOWN_TM = 512


def _params(*sem):
    return pltpu.CompilerParams(dimension_semantics=sem, vmem_limit_bytes=VMEM_LIMIT)


def _dot(a, b):
    return jnp.dot(a, b, preferred_element_type=F32)


def _dot_nt(a, b):
    return lax.dot_general(a, b, (((1,), (1,)), ((), ())), preferred_element_type=F32)


def _dot_tn(a, b):
    return lax.dot_general(a, b, (((0,), (0,)), ((), ())), preferred_element_type=F32)


def _rms(x, gain):
    return x * lax.rsqrt(jnp.mean(x * x, axis=-1, keepdims=True) + EPS) * gain


def _adaln_kernel(c_ref, w_ref, b_ref, o_ref):
    c = c_ref[...]
    ca = c * jax.nn.sigmoid(c)
    o_ref[...] = _dot(ca.astype(BF16), w_ref[...].astype(BF16)) + b_ref[...]


def _adaln(c, w_ada, b_ada):
    B, D = c.shape
    N = w_ada.shape[1]
    return pl.pallas_call(
        _adaln_kernel,
        grid=(N // ADALN_TN,),
        in_specs=[pl.BlockSpec((B, D), lambda j: (0, 0)),
                  pl.BlockSpec((D, ADALN_TN), lambda j: (0, j)),
                  pl.BlockSpec((1, ADALN_TN), lambda j: (0, j))],
        out_specs=pl.BlockSpec((B, ADALN_TN), lambda j: (0, j)),
        out_shape=jax.ShapeDtypeStruct((B, N), F32),
        compiler_params=_params("parallel"),
        name="adaln",
    )(c, w_ada, b_ada.reshape(1, N))


def _bucket_table():
    j = np.arange(ATTN_BLOCK)[:, None]
    i = np.arange(ATTN_BLOCK)[None, :]
    dist = np.where(j > i, i + ATTN_BLOCK - j, i - j)
    max_exact = N_BUCKETS // 2
    d = np.maximum(dist, 0)
    ratio = np.log(np.maximum(d, 1).astype(np.float32) / np.float32(max_exact)) / np.float32(
        math.log(MAX_DISTANCE / max_exact))
    large = max_exact + (ratio.astype(np.float32) * np.float32(N_BUCKETS - max_exact)).astype(np.int32)
    large = np.minimum(large, N_BUCKETS - 1)
    bucket = np.where(d < max_exact, d, large)
    in_window = (dist >= 0) & (dist < WINDOW)
    return np.where(in_window, bucket, -1).astype(np.int32)


def _relbias_kernel(tab_ref, bkt_ref, o_ref):
    g = pl.program_id(0)
    bkt = bkt_ref[...]
    nk, nq = bkt.shape
    from_prev = lax.broadcasted_iota(jnp.int32, bkt.shape, 0) > lax.broadcasted_iota(jnp.int32, bkt.shape, 1)
    for e in range(2):
        for p in range(2):
            h = g * GQA_GROUP + 2 * p + e
            acc = jnp.zeros(bkt.shape, F32)
            for b in range(N_BUCKETS):
                acc = jnp.where(bkt == b, tab_ref[b, h], acc)
            regular = jnp.where(bkt >= 0, acc * LOG2E, NEG_INF)
            tile = (slice(e * nk, (e + 1) * nk), slice(p * nq, (p + 1) * nq))
            o_ref[(0,) + tile] = jnp.where(from_prev, NEG_INF, regular)
            o_ref[(1,) + tile] = regular


def _relbias(rel_bias_table):
    bkt = jnp.asarray(_bucket_table())
    nk, nq = bkt.shape
    return pl.pallas_call(
        _relbias_kernel,
        grid=(N_KV_HEADS,),
        in_specs=[pl.BlockSpec(memory_space=pltpu.SMEM),
                  pl.BlockSpec(bkt.shape, lambda g: (0, 0))],
        out_specs=pl.BlockSpec((2, None, 2 * nk, 2 * nq), lambda g: (0, g, 0, 0)),
        out_shape=jax.ShapeDtypeStruct((2, N_KV_HEADS, 2 * nk, 2 * nq), F32),
        compiler_params=_params("parallel"),
        name="relbias",
    )(rel_bias_table, bkt)


SRC_GQ = ATTN_Q_W + 2 * ATTN_KV_W
SRC_GV = SRC_GQ + 2 * GLA_K_W
SRC_LR = SRC_GV + 2 * GLA_V_W
SRC_GA = SRC_LR + GLA_LOWRANK
WPACK_ROWS = 512


def _wpack_src_row(b):
    r = b * WPACK_ROWS
    src = jnp.where(r < COL_GV, SRC_GA + r,
                    jnp.where(r < COL_GQ, SRC_GV + (r - COL_GV),
                              jnp.where(r < COL_AQ, SRC_GQ + (r - COL_GQ),
                                        jnp.where(r < COL_LR, r - COL_AQ, SRC_LR))))
    return pl.multiple_of(src, 8)


def _wpack_kernel(w_ref, o_ref):
    r = pl.program_id(0) * WPACK_ROWS
    nvalid = jnp.where(r < COL_LR, WPACK_ROWS, jnp.where(r == COL_LR, GLA_LOWRANK, 0))
    row = lax.broadcasted_iota(jnp.int32, w_ref.shape, 0)
    o_ref[...] = jnp.where(row < nvalid, w_ref[...], 0.0).astype(BF16)


def _wpack(w_in_t):
    _, D = w_in_t.shape
    return pl.pallas_call(
        _wpack_kernel,
        grid=(PROJ_W // WPACK_ROWS,),
        in_specs=[pl.BlockSpec((pl.Element(WPACK_ROWS), pl.Element(D)), lambda b: (_wpack_src_row(b), 0))],
        out_specs=pl.BlockSpec((WPACK_ROWS, D), lambda b: (b, 0)),
        out_shape=jax.ShapeDtypeStruct((PROJ_W, D), BF16),
        compiler_params=_params("parallel"),
        name="wpack",
    )(w_in_t)


def _norm_modulate(x, mod_ref, gain):
    return (_rms(x, gain) * (1.0 + mod_ref[1:2, :]) + mod_ref[0:1, :]).astype(BF16)


def _hnorm_kernel(x_ref, mod_ref, g_ref, o_ref):
    o_ref[...] = _norm_modulate(x_ref[...], mod_ref, g_ref[...])


def _hnorm_first(x2, mod3, gain):
    D = x2.shape[1]
    nr = INPROJ_NORM_ROWS
    return pl.pallas_call(
        _hnorm_kernel,
        grid=(INPROJ_TM // nr,),
        in_specs=[pl.BlockSpec((nr, D), lambda r: (r, 0)),
                  pl.BlockSpec((None, 6, D), lambda r: (0, 0, 0)),
                  pl.BlockSpec((1, D), lambda r: (0, 0))],
        out_specs=pl.BlockSpec((nr, D), lambda r: (r, 0)),
        out_shape=jax.ShapeDtypeStruct((INPROJ_TM, D), BF16),
        compiler_params=_params("parallel"),
        name="hnorm_first",
    )(x2, mod3, gain)


def _cast_block(shape, steps):
    rows, cols = shape
    bc = min(cols, 1024)
    assert cols % bc == 0
    for br in range(BF16_ROWS, rows + 1, BF16_ROWS):
        if rows % br == 0 and (rows // br) * (cols // bc) <= steps:
            return br, bc
    raise ValueError(f"no cast block for {shape} in {steps} steps")


def _cast_specs(weights, n0, n1):
    def spec(shape):
        br, bc = _cast_block(shape, n0 * n1)
        ncb = shape[1] // bc
        last = (shape[0] // br) * ncb - 1

        def index(i, j):
            t = jnp.minimum(i * n1 + j, last)
            return t // ncb, t % ncb

        return pl.BlockSpec((br, bc), index)

    return [spec(w.shape) for w in weights]


def _inproj_kernel(xn_ref, modn_ref, g_ref, w_ref, h0_ref, o_ref, ha_ref, hb_ref):
    i, j = pl.program_id(0), pl.program_id(1)
    nr = INPROJ_NORM_ROWS

    @pl.when((i == 0) & (j == 0))
    def _():
        ha_ref[...] = h0_ref[...]

    def step(cur_ref, nxt_ref):
        o_ref[...] = _dot_nt(cur_ref[...], w_ref[...]).astype(BF16)
        chunk = jnp.minimum(j, INPROJ_TM // nr - 1)
        rows = pl.ds(pl.multiple_of(chunk * nr, nr), nr)
        nxt_ref[rows, :] = _norm_modulate(xn_ref[...], modn_ref, g_ref[...])

    @pl.when(i % 2 == 0)
    def _():
        step(ha_ref, hb_ref)

    @pl.when(i % 2 == 1)
    def _():
        step(hb_ref, ha_ref)


def _inproj(x2, mod3, gain, w):
    T, D = x2.shape
    S = T // mod3.shape[0]
    tm, tn, nr = INPROJ_TM, INPROJ_TN, INPROJ_NORM_ROWS
    n_i, chunks = T // tm, tm // nr
    assert PROJ_W // tn >= chunks

    def next_tile(i):
        return jnp.minimum(i + 1, n_i - 1)

    return pl.pallas_call(
        _inproj_kernel,
        grid=(n_i, PROJ_W // tn),
        in_specs=[pl.BlockSpec((nr, D), lambda i, j: (next_tile(i) * chunks + jnp.minimum(j, chunks - 1), 0)),
                  pl.BlockSpec((None, 6, D), lambda i, j: (next_tile(i) // (S // tm), 0, 0)),
                  pl.BlockSpec((1, D), lambda i, j: (0, 0)),
                  pl.BlockSpec((tn, D), lambda i, j: (j, 0)),
                  pl.BlockSpec((tm, D), lambda i, j: (0, 0), pipeline_mode=pl.Buffered(1))],
        out_specs=pl.BlockSpec((tm, tn), lambda i, j: (i, j)),
        out_shape=jax.ShapeDtypeStruct((T, PROJ_W), BF16),
        scratch_shapes=[pltpu.VMEM((tm, D), BF16), pltpu.VMEM((tm, D), BF16)],
        compiler_params=_params("arbitrary", "arbitrary"),
        name="inproj",
    )(x2, mod3, gain, w, _hnorm_first(x2, mod3, gain))


def _attn_kernel(q_ref, kc_ref, kp_ref, vc_ref, vp_ref, b0_ref, b1_ref, qg_ref, kg_ref, sink_ref, seg_ref,
                 *rest, n_cast):
    cast_in, o_ref, cast_out = rest[:n_cast], rest[n_cast], rest[n_cast + 1:]
    for src_ref, dst_ref in zip(cast_in, cast_out):
        dst_ref[...] = src_ref[...].astype(BF16)

    nsub = ATTN_TQ // ATTN_BLOCK
    blk, dh = ATTN_BLOCK, HEAD_DIM
    seg = seg_ref[...]

    def normed(x, gain):
        ssq = _dot((x * x).astype(BF16), seg)
        return x * lax.rsqrt(ssq * (1.0 / dh) + EPS) * gain

    kn = normed(jnp.concatenate([kp_ref[...], kc_ref[...]], axis=0).astype(F32), kg_ref[...])
    vt = jnp.concatenate([vp_ref[...], vc_ref[...]], axis=0).astype(F32).T.astype(BF16)
    lane = lax.broadcasted_iota(jnp.int32, (kn.shape[0], 2 * dh), 1)
    qgain = qg_ref[...] * (dh ** -0.5 * LOG2E)

    kzs, qns = [], []
    for g in range(N_KV_HEADS):
        pair = kn[:, (g // 2) * 2 * dh:(g // 2 + 1) * 2 * dh]
        own = jnp.where(lane >= dh if g % 2 else lane < dh, pair, 0.0)
        swapped = pltpu.roll(own, dh, axis=1)
        kz = [own, swapped] if g % 2 == 0 else [swapped, own]
        kzs.append([z.astype(BF16) for z in kz])
        qns.append(normed(q_ref[:, g * 4 * dh:(g + 1) * 4 * dh].astype(F32), qgain).astype(BF16))

    def from_prev(n_rows, n_lanes):
        slot = lax.broadcasted_iota(jnp.int32, (n_rows, n_lanes), 0) % blk
        query = lax.broadcasted_iota(jnp.int32, (n_rows, n_lanes), 1) % blk
        return slot > query

    def scores(g, sb):
        bref = b0_ref if sb == 0 else b1_ref
        prv, cur = slice(sb * blk, (sb + 1) * blk), slice((sb + 1) * blk, (sb + 2) * blk)
        rows = slice(sb * blk, (sb + 1) * blk)
        qr = jnp.concatenate([qns[g][rows, 0:2 * dh], qns[g][rows, 2 * dh:4 * dh]], axis=0)
        s_prv = _dot_nt(jnp.concatenate([kzs[g][0][prv], kzs[g][1][prv]], axis=0), qr)
        s_cur = _dot_nt(jnp.concatenate([kzs[g][0][cur], kzs[g][1][cur]], axis=0), qr)
        return jnp.where(from_prev(2 * blk, 2 * blk), s_prv, s_cur) + bref[g]

    def finish(g, sb, s):
        prv, cur = slice(sb * blk, (sb + 1) * blk), slice((sb + 1) * blk, (sb + 2) * blk)
        rows = slice(sb * blk, (sb + 1) * blk)
        ps, sink_terms = [], []
        for e in range(2):
            se = s[e * blk:(e + 1) * blk]
            sink = sink_ref[g, e] * LOG2E
            m = jnp.maximum(jnp.max(se, axis=0, keepdims=True), sink)
            ps.append(jnp.exp2((se - m).astype(BF16)))
            sink_terms.append(jnp.exp2(sink - m))
        pm = jnp.concatenate(ps, axis=1)
        mask = from_prev(blk, 4 * blk)
        zero = jnp.zeros_like(pm)
        ones_rows = jnp.ones((BF16_ROWS, blk), BF16)
        vg = vt[g * dh:(g + 1) * dh]
        ot = (_dot(jnp.concatenate([vg[:, prv], ones_rows], axis=0), jnp.where(mask, pm, zero))
              + _dot(jnp.concatenate([vg[:, cur], ones_rows], axis=0), jnp.where(mask, zero, pm)))
        ot = ot[0:dh] * (1.0 / (ot[dh:dh + 1] + jnp.concatenate(sink_terms, axis=1)))
        o = jnp.concatenate([ot[:, 0:2 * blk], ot[:, 2 * blk:4 * blk]], axis=0).T
        o_ref[rows, (2 * g) * 2 * dh:(2 * g + 1) * 2 * dh] = o[0:blk].astype(BF16)
        o_ref[rows, (2 * g + 1) * 2 * dh:(2 * g + 2) * 2 * dh] = o[blk:2 * blk].astype(BF16)

    units = [(g, sb) for g in range(N_KV_HEADS) for sb in range(nsub)]
    pending = [scores(*unit) for unit in units[:ATTN_AHEAD]]
    for n, unit in enumerate(units):
        if n + ATTN_AHEAD < len(units):
            pending.append(scores(*units[n + ATTN_AHEAD]))
        finish(*unit, pending.pop(0))


def _attn(proj3, biasm, q_gain, k_gain, sinks, cast_weights):
    B, S, _ = proj3.shape
    tq, blk = ATTN_TQ, ATTN_BLOCK
    nsub = tq // blk
    kvw = ATTN_KV_W
    seg_id = np.arange(kvw) // HEAD_DIM
    seg = jnp.asarray(seg_id[:, None] == seg_id[None, :], dtype=BF16)
    sink_rows = jnp.repeat(sinks.reshape(N_KV_HEADS, 2, 2).transpose(0, 2, 1), blk, axis=-1)
    sink_rows = sink_rows.reshape(N_KV_HEADS, 2, 1, 2 * blk)

    def prev(i):
        return jnp.maximum(i * nsub - 1, 0)

    bias_block = (None,) + biasm.shape[1:]
    cast_specs = _cast_specs(cast_weights, B, S // tq)
    outs = pl.pallas_call(
        functools.partial(_attn_kernel, n_cast=len(cast_weights)),
        grid=(B, S // tq),
        in_specs=[pl.BlockSpec((None, tq, ATTN_Q_W), lambda b, i: (b, i, COL_AQ // ATTN_Q_W)),
                  pl.BlockSpec((None, tq, kvw), lambda b, i: (b, i, COL_AK // kvw)),
                  pl.BlockSpec((None, blk, kvw), lambda b, i: (b, prev(i), COL_AK // kvw)),
                  pl.BlockSpec((None, tq, kvw), lambda b, i: (b, i, COL_AV // kvw)),
                  pl.BlockSpec((None, blk, kvw), lambda b, i: (b, prev(i), COL_AV // kvw)),
                  pl.BlockSpec(bias_block, lambda b, i: (jnp.minimum(i, 1), 0, 0, 0)),
                  pl.BlockSpec(bias_block, lambda b, i: (1, 0, 0, 0)),
                  pl.BlockSpec((1, kvw), lambda b, i: (0, 0)),
                  pl.BlockSpec((1, kvw), lambda b, i: (0, 0)),
                  pl.BlockSpec(sink_rows.shape, lambda b, i: (0, 0, 0, 0)),
                  pl.BlockSpec(seg.shape, lambda b, i: (0, 0))] + cast_specs,
        out_specs=[pl.BlockSpec((None, tq, ATTN_Q_W), lambda b, i: (b, i, 0))] + cast_specs,
        out_shape=[jax.ShapeDtypeStruct((B, S, ATTN_Q_W), BF16)]
        + [jax.ShapeDtypeStruct(cw.shape, BF16) for cw in cast_weights],
        compiler_params=_params("arbitrary", "arbitrary"),
        name="attn",
    )(proj3, proj3, proj3, proj3, proj3, biasm, biasm, jnp.tile(q_gain, (1, GQA_GROUP)),
      jnp.tile(k_gain, (1, N_KV_HEADS)), sink_rows, seg, *cast_weights)
    return outs[0], outs[1:]


def _split2(x):
    hi = x.astype(BF16)
    return hi, (x - hi.astype(F32)).astype(BF16)


def _gla_head(q, k, v, r, lr, wgk, bgk, gain, state_ref, masks, store):
    C = GLA_CHUNK
    tril, m_same, m_next, m_far, eye = masks
    z = _dot(lr, wgk) + bgk
    yield
    gl = (jnp.minimum(z, 0.0) - jnp.log(1.0 + jnp.exp(-jnp.abs(z)))) * (1.0 / GLA_NORMALIZER)
    hi, lo = _split2(gl)
    g = _dot(tril, hi) + _dot(tril, lo)
    yield
    t = [g[(c + 1) * C - 1:(c + 1) * C] for c in range(4)]
    t_rows = jnp.concatenate([jnp.broadcast_to(tc, (C, tc.shape[1])) for tc in t], axis=0)

    q_dec = q.astype(F32) * (GLA_DK ** -0.5) * jnp.exp(g)
    kf = k.astype(F32)
    k_inv = (kf * jnp.exp(-g)).astype(BF16)
    k_end = kf * jnp.exp(t_rows - g)
    qd = [q_dec[c * C:(c + 1) * C] for c in range(4)]
    ke = [k_end[c * C:(c + 1) * C] for c in range(4)]

    def rows(parts):
        return jnp.concatenate(parts, axis=0).astype(BF16)

    q_b, k_b = q_dec.astype(BF16), k_end.astype(BF16)
    q_far = rows([qd[0], qd[1], qd[2], qd[3] * jnp.exp(t[2])])
    k_far = rows([ke[0] * jnp.exp(t[1]), ke[1], ke[2], ke[3]])
    q_abs = rows([qd[0], qd[1] * jnp.exp(t[0]), qd[2] * jnp.exp(t[0] + t[1]), qd[3] * jnp.exp(t[0] + t[1] + t[2])])
    k_abs = rows([ke[0] * jnp.exp(t[1] + t[2] + t[3]), ke[1] * jnp.exp(t[2] + t[3]), ke[2] * jnp.exp(t[3]), ke[3]])

    a_same, a_next, a_far = _dot_nt(q_b, k_inv), _dot_nt(q_b, k_b), _dot_nt(q_far, k_far)
    state = state_ref[...]
    o_state = _dot(q_abs, state.astype(BF16))
    update = _dot_tn(k_abs, v)
    yield
    a = jnp.where(m_same, a_same, jnp.where(m_next, a_next, jnp.where(m_far, a_far, 0.0)))
    o = _dot(a.astype(BF16), v) + o_state
    yield

    decay = jnp.exp(t[0] + t[1] + t[2] + t[3])
    decay_col = jnp.sum(jnp.where(eye, jnp.broadcast_to(decay, eye.shape), 0.0), axis=1, keepdims=True)
    state_ref[...] = decay_col * state + update

    rf = r.astype(F32)
    store((_rms(o, gain) * (rf * jax.nn.sigmoid(rf))).astype(BF16))


def _gla_kernel(q_ref, k_ref, v_ref, r_ref, lr_ref, wgk_ref, bgk_ref, gain_ref, o_ref, state_ref):
    R, C, dk, dv = GLA_ROWS, GLA_CHUNK, GLA_DK, GLA_DV

    @pl.when(pl.program_id(2) == 0)
    def _():
        state_ref[...] = jnp.zeros_like(state_ref)

    row = lax.broadcasted_iota(jnp.int32, (R, R), 0)
    col = lax.broadcasted_iota(jnp.int32, (R, R), 1)
    ci, cj = row // C, col // C
    m_same = (row >= col) & (ci == cj)
    m_next = (ci == cj + 1) & (ci != 2)
    m_far = (ci >= 2) & (cj <= 1)
    eye = lax.broadcasted_iota(jnp.int32, (dk, dk), 0) == lax.broadcasted_iota(jnp.int32, (dk, dk), 1)
    masks = (m_same.astype(BF16), m_same, m_next, m_far, eye)

    heads = []
    for b in range(GLA_BATCH_PER_STEP):
        lr = lr_ref[b]
        for h in range(GLA_HEADS_PER_STEP):
            ks, vs = slice(h * dk, (h + 1) * dk), slice(h * dv, (h + 1) * dv)
            heads.append(_gla_head(q_ref[b, :, ks], k_ref[b, :, ks], v_ref[b, :, vs], r_ref[b, :, vs], lr,
                                   wgk_ref[:, ks], bgk_ref[:, ks], gain_ref[...], state_ref.at[b, h], masks,
                                   functools.partial(o_ref.__setitem__, (b, slice(None), vs))))
    for _ in itertools.zip_longest(*heads):
        pass


def _gla(proj3, wgk, bgk, gain):
    B, S, _ = proj3.shape
    R, hps, bps = GLA_ROWS, GLA_HEADS_PER_STEP, GLA_BATCH_PER_STEP
    dk, dv = hps * GLA_DK, hps * GLA_DV
    return pl.pallas_call(
        _gla_kernel,
        grid=(B // bps, GLA_HEADS // hps, S // R),
        in_specs=[pl.BlockSpec((bps, R, dk), lambda b, h, t: (b, t, COL_GQ // dk + h)),
                  pl.BlockSpec((bps, R, dk), lambda b, h, t: (b, t, COL_GK // dk + h)),
                  pl.BlockSpec((bps, R, dv), lambda b, h, t: (b, t, COL_GV // dv + h)),
                  pl.BlockSpec((bps, R, dv), lambda b, h, t: (b, t, COL_GR // dv + h)),
                  pl.BlockSpec((bps, R, LR_PAD), lambda b, h, t: (b, t, COL_LR // LR_PAD)),
                  pl.BlockSpec((LR_PAD, dk), lambda b, h, t: (0, h)),
                  pl.BlockSpec((1, dk), lambda b, h, t: (0, h)),
                  pl.BlockSpec((1, GLA_DV), lambda b, h, t: (0, 0))],
        out_specs=pl.BlockSpec((bps, R, dv), lambda b, h, t: (b, t, h)),
        out_shape=jax.ShapeDtypeStruct((B, S, GLA_V_W), BF16),
        scratch_shapes=[pltpu.VMEM((bps, hps, GLA_DK, GLA_DV), F32)],
        compiler_params=_params("parallel", "parallel", "arbitrary"),
        name="gla",
    )(proj3, proj3, proj3, proj3, proj3, wgk, bgk, gain)


def _merge_kernel(ya_ref, yg_ref, ga_ref, gb_ref, x_ref, mod_ref, g2_ref, wa_ref, wg_ref, wo_ref,
                  x1_ref, h2_ref):
    def merged_branches(rows):
        ma, mg = _dot(ya_ref[rows, :], wa_ref[...]), _dot(yg_ref[rows, :], wg_ref[...])
        ga = jax.nn.sigmoid(ga_ref[rows, :].astype(F32))
        gb = jax.nn.sigmoid(gb_ref[rows, :].astype(F32))
        return (ga * ma + gb * mg).astype(BF16)

    def project(rows, merged):
        x1 = x_ref[rows, :] + mod_ref[2:3, :] * _dot(merged, wo_ref[...])
        x1_ref[rows, :] = x1
        h2_ref[rows, :] = (_rms(x1, g2_ref[...]) * (1.0 + mod_ref[4:5, :]) + mod_ref[3:4, :]).astype(BF16)

    for r0 in range(0, MERGE_TM, MERGE_SUB):
        rows = slice(r0, r0 + MERGE_SUB)
        project(rows, merged_branches(rows))


def _merge(ya, yg, proj, x2, mod3, gain2, wa, wg, wo):
    T, D = x2.shape
    S = T // mod3.shape[0]
    tm = MERGE_TM
    once = pl.Buffered(1)
    return pl.pallas_call(
        _merge_kernel,
        grid=(T // tm,),
        in_specs=[pl.BlockSpec((tm, ATTN_Q_W), lambda i: (i, 0)),
                  pl.BlockSpec((tm, GLA_V_W), lambda i: (i, 0)),
                  pl.BlockSpec((tm, D), lambda i: (i, COL_GA // D)),
                  pl.BlockSpec((tm, D), lambda i: (i, COL_GB // D)),
                  pl.BlockSpec((tm, D), lambda i: (i, 0)),
                  pl.BlockSpec((None, 6, D), lambda i: (i // (S // tm), 0, 0)),
                  pl.BlockSpec((1, D), lambda i: (0, 0)),
                  pl.BlockSpec((ATTN_Q_W, D), lambda i: (0, 0), pipeline_mode=once),
                  pl.BlockSpec((GLA_V_W, D), lambda i: (0, 0), pipeline_mode=once),
                  pl.BlockSpec((D, D), lambda i: (0, 0), pipeline_mode=once)],
        out_specs=[pl.BlockSpec((tm, D), lambda i: (i, 0)),
                   pl.BlockSpec((tm, D), lambda i: (i, 0))],
        out_shape=[jax.ShapeDtypeStruct((T, D), F32), jax.ShapeDtypeStruct((T, D), BF16)],
        compiler_params=_params("parallel"),
        name="merge",
    )(ya, yg, proj, proj, x2, mod3, gain2, wa, wg, wo)


def _ffn_up_kernel(h_ref, halo_ref, wa_ref, wb_ref, cwa_ref, cwb_ref, cba_ref, cbb_ref, o_ref,
                   hs_ref, ua_ref, ub_ref, *, tiles_per_seq):
    tm, halo = FFN_TM, FFN_HALO

    @pl.when(pl.program_id(1) == 0)
    def _():
        first = (pl.program_id(0) % tiles_per_seq) == 0
        hs_ref[0:halo, :] = jnp.where(first, jnp.zeros_like(halo_ref), halo_ref[...])
        hs_ref[halo:, :] = h_ref[...]

    def conv(u_ref, cw_ref, cb_ref, r0, n):
        u = u_ref[halo + r0 - 8:halo + r0 + n, :]
        y = cb_ref[...] + cw_ref[0:1, :] * pltpu.roll(u, 2, axis=0)[8:]
        y = y + cw_ref[1:2, :] * pltpu.roll(u, 1, axis=0)[8:]
        return y + cw_ref[2:3, :] * u[8:]

    ua_ref[...] = _dot(hs_ref[...], wa_ref[...])
    ub_ref[...] = _dot(hs_ref[...], wb_ref[...])
    for r0 in range(0, tm, FFN_EPI_ROWS):
        ya = conv(ua_ref, cwa_ref, cba_ref, r0, FFN_EPI_ROWS)
        yb = conv(ub_ref, cwb_ref, cbb_ref, r0, FFN_EPI_ROWS)
        o_ref[r0:r0 + FFN_EPI_ROWS, :] = (ya * jax.nn.sigmoid(ya) * yb).astype(BF16)


def _ffn_up(h2, S, w_up, conv_w, conv_b):
    T, D = h2.shape
    tm, tn, halo = FFN_TM, FFN_TN, FFN_HALO
    nj = D_FF // tn
    return pl.pallas_call(
        functools.partial(_ffn_up_kernel, tiles_per_seq=S // tm),
        grid=(T // tm, D_FF // tn),
        in_specs=[pl.BlockSpec((tm, D), lambda i, j: (i, 0)),
                  pl.BlockSpec((halo, D), lambda i, j: (jnp.maximum(i * (tm // halo) - 1, 0), 0)),
                  pl.BlockSpec((D, tn), lambda i, j: (0, j)),
                  pl.BlockSpec((D, tn), lambda i, j: (0, j + nj)),
                  pl.BlockSpec((3, tn), lambda i, j: (0, j)),
                  pl.BlockSpec((3, tn), lambda i, j: (0, j + nj)),
                  pl.BlockSpec((1, tn), lambda i, j: (0, j)),
                  pl.BlockSpec((1, tn), lambda i, j: (0, j + nj))],
        out_specs=pl.BlockSpec((tm, tn), lambda i, j: (i, j)),
        out_shape=jax.ShapeDtypeStruct((T, D_FF), BF16),
        scratch_shapes=[pltpu.VMEM((tm + halo, D), BF16),
                        pltpu.VMEM((tm + halo, tn), F32),
                        pltpu.VMEM((tm + halo, tn), F32)],
        compiler_params=_params("parallel", "arbitrary"),
        name="ffn_up",
    )(h2, h2, w_up, w_up, conv_w, conv_w, conv_b, conv_b)


def _ffn_down_kernel(a_ref, w_ref, x_ref, mod_ref, o_ref):
    o_ref[...] = x_ref[...] + mod_ref[5:6, :] * _dot(a_ref[...], w_ref[...])


def _ffn_down(act, wd, x1, mod3):
    T, D = x1.shape
    S = T // mod3.shape[0]
    tm = DOWN_TM
    return pl.pallas_call(
        _ffn_down_kernel,
        grid=(T // tm,),
        in_specs=[pl.BlockSpec((tm, D_FF), lambda i: (i, 0)),
                  pl.BlockSpec((D_FF, D), lambda i: (0, 0), pipeline_mode=pl.Buffered(1)),
                  pl.BlockSpec((tm, D), lambda i: (i, 0)),
                  pl.BlockSpec((None, 6, D), lambda i: (i // (S // tm), 0, 0))],
        out_specs=pl.BlockSpec((tm, D), lambda i: (i, 0)),
        out_shape=jax.ShapeDtypeStruct((T, D), F32),
        compiler_params=_params("parallel"),
        name="ffn_down",
    )(act, wd, x1, mod3)


def _layer(x2, B, mod3, biasm, norm1_gain, w_in, q_norm_gain, k_norm_gain, attn_sinks, w_gk_up, b_gk,
           gla_norm_gain, w_branch_attn, w_branch_gla, w_out, norm2_gain, w_ffn_up, ffn_conv_w,
           ffn_conv_b, w_ffn_down):
    T, D = x2.shape
    S = T // B
    w_p = _wpack(w_in.T)
    wgk = jnp.concatenate([w_gk_up, jnp.zeros((LR_PAD - GLA_LOWRANK, GLA_K_W), w_gk_up.dtype)],
                          axis=0).astype(BF16)

    proj = _inproj(x2, mod3, norm1_gain.reshape(1, D), w_p)
    proj3 = proj.reshape(B, S, PROJ_W)
    ya, (w_ba, w_bg, w_o, w_up, w_down) = _attn(
        proj3, biasm, q_norm_gain.reshape(1, HEAD_DIM), k_norm_gain.reshape(1, HEAD_DIM), attn_sinks,
        (w_branch_attn, w_branch_gla, w_out, w_ffn_up, w_ffn_down))
    yg = _gla(proj3, wgk, b_gk.reshape(1, GLA_K_W), gla_norm_gain.reshape(1, GLA_DV))
    x1, h2 = _merge(ya.reshape(T, ATTN_Q_W), yg.reshape(T, GLA_V_W), proj, x2, mod3,
                    norm2_gain.reshape(1, D), w_ba, w_bg, w_o)
    act = _ffn_up(h2, S, w_up, ffn_conv_w, ffn_conv_b.reshape(1, 2 * D_FF))
    return _ffn_down(act, w_down, x1, mod3)


def kernel(x, c, rel_bias_table, w_ada, b_ada, norm1_gain, w_in, q_norm_gain, k_norm_gain, attn_sinks,
           w_gk_up, b_gk, gla_norm_gain, w_branch_attn, w_branch_gla, w_out, norm2_gain, w_ffn_up,
           ffn_conv_w, ffn_conv_b, w_ffn_down):
    B, S, D = x.shape
    depth = w_in.shape[0]
    assert D == D_MODEL and w_in.shape[1:] == (D_MODEL, SRC_GA + 2 * D_MODEL) and w_ffn_up.shape[2] == 2 * D_FF
    assert S % max(FFN_TM, INPROJ_TM, ATTN_TQ, GLA_ROWS, MERGE_TM, DOWN_TM) == 0
    biasm = _relbias(rel_bias_table)
    x2 = x.reshape(B * S, D)
    for l in range(depth):
        mod3 = _adaln(c, w_ada[l], b_ada[l]).reshape(B, 6, D)
        x2 = _layer(x2, B, mod3, biasm, norm1_gain[l], w_in[l], q_norm_gain[l], k_norm_gain[l],
                    attn_sinks[l], w_gk_up[l], b_gk[l], gla_norm_gain[l], w_branch_attn[l],
                    w_branch_gla[l], w_out[l], norm2_gain[l], w_ffn_up[l], ffn_conv_w[l], ffn_conv_b[l],
                    w_ffn_down[l])
    return x2.reshape(B, S, D)
```

```python
import functools
import itertools
import math

import numpy as np
import jax
import jax.numpy as jnp
from jax import lax
from jax.experimental import pallas as pl
from jax.experimental.pallas import tpu as pltpu

F32 = jnp.float32
BF16 = jnp.bfloat16

D_MODEL = 2048
N_Q_HEADS = 16
N_KV_HEADS = 4
GQA_GROUP = N_Q_HEADS // N_KV_HEADS
HEAD_DIM = 64
WINDOW = 128
ATTN_BLOCK = 128
N_BUCKETS = 32
MAX_DISTANCE = 128
GLA_HEADS = 4
GLA_DK = 256
GLA_DV = 512
GLA_LOWRANK = 16
GLA_NORMALIZER = 16.0
GLA_CHUNK = 64
D_FF = 5632
EPS = 1e-6
NEG_INF = -1e30
LOG2E = math.log2(math.e)

ATTN_Q_W = N_Q_HEADS * HEAD_DIM
ATTN_KV_W = N_KV_HEADS * HEAD_DIM
GLA_K_W = GLA_HEADS * GLA_DK
GLA_V_W = GLA_HEADS * GLA_DV

COL_GA = 0
COL_GB = COL_GA + D_MODEL
COL_GV = COL_GB + D_MODEL
COL_GR = COL_GV + GLA_V_W
COL_GQ = COL_GR + GLA_V_W
COL_GK = COL_GQ + GLA_K_W
COL_AQ = COL_GK + GLA_K_W
COL_AK = COL_AQ + ATTN_Q_W
COL_AV = COL_AK + ATTN_KV_W
COL_LR = COL_AV + ATTN_KV_W
LANE = 128
BF16_ROWS = 16
VMEM_LIMIT = 60 * 1024 * 1024

LR_PAD = LANE
PROJ_W = 12288

ADALN_TN = 1024
INPROJ_TM, INPROJ_TN = 1024, 3072
INPROJ_NORM_ROWS = 256
ATTN_TQ = 512
ATTN_AHEAD = 2
GLA_ROWS = 256
GLA_HEADS_PER_STEP = 4
GLA_BATCH_PER_STEP = 2
MERGE_TM, MERGE_SUB = 512, 256
FFN_TM, FFN_TN = 1024, 512
FFN_HALO = BF16_ROWS
FFN_EPI_ROWS = 256
DOWN_TM = 512


def _params(*sem):
    return pltpu.CompilerParams(dimension_semantics=sem, vmem_limit_bytes=VMEM_LIMIT)


def _dot(a, b):
    return jnp.dot(a, b, preferred_element_type=F32)


def _dot_nt(a, b):
    return lax.dot_general(a, b, (((1,), (1,)), ((), ())), preferred_element_type=F32)


def _dot_tn(a, b):
    return lax.dot_general(a, b, (((0,), (0,)), ((), ())), preferred_element_type=F32)


def _rms(x, gain):
    return x * lax.rsqrt(jnp.mean(x * x, axis=-1, keepdims=True) + EPS) * gain


def _adaln_kernel(c_ref, w_ref, b_ref, o_ref):
    c = c_ref[...]
    ca = c * jax.nn.sigmoid(c)
    o_ref[...] = _dot(ca.astype(BF16), w_ref[...].astype(BF16)) + b_ref[...]


def _adaln(c, w_ada, b_ada):
    B, D = c.shape
    N = w_ada.shape[1]
    return pl.pallas_call(
        _adaln_kernel,
        grid=(N // ADALN_TN,),
        in_specs=[pl.BlockSpec((B, D), lambda j: (0, 0)),
                  pl.BlockSpec((D, ADALN_TN), lambda j: (0, j)),
                  pl.BlockSpec((1, ADALN_TN), lambda j: (0, j))],
        out_specs=pl.BlockSpec((B, ADALN_TN), lambda j: (0, j)),
        out_shape=jax.ShapeDtypeStruct((B, N), F32),
        compiler_params=_params("parallel"),
        name="adaln",
    )(c, w_ada, b_ada.reshape(1, N))


def _bucket_table():
    j = np.arange(ATTN_BLOCK)[:, None]
    i = np.arange(ATTN_BLOCK)[None, :]
    dist = np.where(j > i, i + ATTN_BLOCK - j, i - j)
    max_exact = N_BUCKETS // 2
    d = np.maximum(dist, 0)
    ratio = np.log(np.maximum(d, 1).astype(np.float32) / np.float32(max_exact)) / np.float32(
        math.log(MAX_DISTANCE / max_exact))
    large = max_exact + (ratio.astype(np.float32) * np.float32(N_BUCKETS - max_exact)).astype(np.int32)
    large = np.minimum(large, N_BUCKETS - 1)
    bucket = np.where(d < max_exact, d, large)
    in_window = (dist >= 0) & (dist < WINDOW)
    return np.where(in_window, bucket, -1).astype(np.int32)


def _relbias_kernel(tab_ref, bkt_ref, o_ref):
    g = pl.program_id(0)
    bkt = bkt_ref[...]
    nk, nq = bkt.shape
    from_prev = lax.broadcasted_iota(jnp.int32, bkt.shape, 0) > lax.broadcasted_iota(jnp.int32, bkt.shape, 1)
    for e in range(2):
        for p in range(2):
            h = g * GQA_GROUP + 2 * p + e
            acc = jnp.zeros(bkt.shape, F32)
            for b in range(N_BUCKETS):
                acc = jnp.where(bkt == b, tab_ref[b, h], acc)
            regular = jnp.where(bkt >= 0, acc * LOG2E, NEG_INF)
            tile = (slice(e * nk, (e + 1) * nk), slice(p * nq, (p + 1) * nq))
            o_ref[(0,) + tile] = jnp.where(from_prev, NEG_INF, regular)
            o_ref[(1,) + tile] = regular


def _relbias(rel_bias_table):
    bkt = jnp.asarray(_bucket_table())
    nk, nq = bkt.shape
    return pl.pallas_call(
        _relbias_kernel,
        grid=(N_KV_HEADS,),
        in_specs=[pl.BlockSpec(memory_space=pltpu.SMEM),
                  pl.BlockSpec(bkt.shape, lambda g: (0, 0))],
        out_specs=pl.BlockSpec((2, None, 2 * nk, 2 * nq), lambda g: (0, g, 0, 0)),
        out_shape=jax.ShapeDtypeStruct((2, N_KV_HEADS, 2 * nk, 2 * nq), F32),
        compiler_params=_params("parallel"),
        name="relbias",
    )(rel_bias_table, bkt)


SRC_GQ = ATTN_Q_W + 2 * ATTN_KV_W
SRC_GV = SRC_GQ + 2 * GLA_K_W
SRC_LR = SRC_GV + 2 * GLA_V_W
SRC_GA = SRC_LR + GLA_LOWRANK
WPACK_ROWS = 512


def _wpack_src_row(b):
    r = b * WPACK_ROWS
    src = jnp.where(r < COL_GV, SRC_GA + r,
                    jnp.where(r < COL_GQ, SRC_GV + (r - COL_GV),
                              jnp.where(r < COL_AQ, SRC_GQ + (r - COL_GQ),
                                        jnp.where(r < COL_LR, r - COL_AQ, SRC_LR))))
    return pl.multiple_of(src, 8)


def _wpack_kernel(w_ref, o_ref):
    r = pl.program_id(0) * WPACK_ROWS
    nvalid = jnp.where(r < COL_LR, WPACK_ROWS, jnp.where(r == COL_LR, GLA_LOWRANK, 0))
    row = lax.broadcasted_iota(jnp.int32, w_ref.shape, 0)
    o_ref[...] = jnp.where(row < nvalid, w_ref[...], 0.0).astype(BF16)


def _wpack(w_in_t):
    _, D = w_in_t.shape
    return pl.pallas_call(
        _wpack_kernel,
        grid=(PROJ_W // WPACK_ROWS,),
        in_specs=[pl.BlockSpec((pl.Element(WPACK_ROWS), pl.Element(D)), lambda b: (_wpack_src_row(b), 0))],
        out_specs=pl.BlockSpec((WPACK_ROWS, D), lambda b: (b, 0)),
        out_shape=jax.ShapeDtypeStruct((PROJ_W, D), BF16),
        compiler_params=_params("parallel"),
        name="wpack",
    )(w_in_t)


def _norm_modulate(x, mod_ref, gain):
    return (_rms(x, gain) * (1.0 + mod_ref[1:2, :]) + mod_ref[0:1, :]).astype(BF16)


def _hnorm_kernel(x_ref, mod_ref, g_ref, o_ref):
    o_ref[...] = _norm_modulate(x_ref[...], mod_ref, g_ref[...])


def _hnorm_first(x2, mod3, gain):
    D = x2.shape[1]
    nr = INPROJ_NORM_ROWS
    return pl.pallas_call(
        _hnorm_kernel,
        grid=(INPROJ_TM // nr,),
        in_specs=[pl.BlockSpec((nr, D), lambda r: (r, 0)),
                  pl.BlockSpec((None, 6, D), lambda r: (0, 0, 0)),
                  pl.BlockSpec((1, D), lambda r: (0, 0))],
        out_specs=pl.BlockSpec((nr, D), lambda r: (r, 0)),
        out_shape=jax.ShapeDtypeStruct((INPROJ_TM, D), BF16),
        compiler_params=_params("parallel"),
        name="hnorm_first",
    )(x2, mod3, gain)


def _cast_block(shape, steps):
    rows, cols = shape
    bc = min(cols, 1024)
    assert cols % bc == 0
    for br in range(BF16_ROWS, rows + 1, BF16_ROWS):
        if rows % br == 0 and (rows // br) * (cols // bc) <= steps:
            return br, bc
    raise ValueError(f"no cast block for {shape} in {steps} steps")


def _cast_specs(weights, n0, n1):
    def spec(shape):
        br, bc = _cast_block(shape, n0 * n1)
        ncb = shape[1] // bc
        last = (shape[0] // br) * ncb - 1

        def index(i, j):
            t = jnp.minimum(i * n1 + j, last)
            return t // ncb, t % ncb

        return pl.BlockSpec((br, bc), index)

    return [spec(w.shape) for w in weights]


def _inproj_kernel(xn_ref, modn_ref, g_ref, w_ref, h0_ref, o_ref, ha_ref, hb_ref):
    i, j = pl.program_id(0), pl.program_id(1)
    nr = INPROJ_NORM_ROWS

    @pl.when((i == 0) & (j == 0))
    def _():
        ha_ref[...] = h0_ref[...]

    def step(cur_ref, nxt_ref):
        o_ref[...] = _dot_nt(cur_ref[...], w_ref[...]).astype(BF16)
        chunk = jnp.minimum(j, INPROJ_TM // nr - 1)
        rows = pl.ds(pl.multiple_of(chunk * nr, nr), nr)
        nxt_ref[rows, :] = _norm_modulate(xn_ref[...], modn_ref, g_ref[...])

    @pl.when(i % 2 == 0)
    def _():
        step(ha_ref, hb_ref)

    @pl.when(i % 2 == 1)
    def _():
        step(hb_ref, ha_ref)


def _inproj(x2, mod3, gain, w):
    T, D = x2.shape
    S = T // mod3.shape[0]
    tm, tn, nr = INPROJ_TM, INPROJ_TN, INPROJ_NORM_ROWS
    n_i, chunks = T // tm, tm // nr
    assert PROJ_W // tn >= chunks

    def next_tile(i):
        return jnp.minimum(i + 1, n_i - 1)

    return pl.pallas_call(
        _inproj_kernel,
        grid=(n_i, PROJ_W // tn),
        in_specs=[pl.BlockSpec((nr, D), lambda i, j: (next_tile(i) * chunks + jnp.minimum(j, chunks - 1), 0)),
                  pl.BlockSpec((None, 6, D), lambda i, j: (next_tile(i) // (S // tm), 0, 0)),
                  pl.BlockSpec((1, D), lambda i, j: (0, 0)),
                  pl.BlockSpec((tn, D), lambda i, j: (j, 0)),
                  pl.BlockSpec((tm, D), lambda i, j: (0, 0), pipeline_mode=pl.Buffered(1))],
        out_specs=pl.BlockSpec((tm, tn), lambda i, j: (i, j)),
        out_shape=jax.ShapeDtypeStruct((T, PROJ_W), BF16),
        scratch_shapes=[pltpu.VMEM((tm, D), BF16), pltpu.VMEM((tm, D), BF16)],
        compiler_params=_params("arbitrary", "arbitrary"),
        name="inproj",
    )(x2, mod3, gain, w, _hnorm_first(x2, mod3, gain))


def _attn_kernel(q_ref, kc_ref, kp_ref, vc_ref, vp_ref, b0_ref, b1_ref, qg_ref, kg_ref, sink_ref, seg_ref,
                 *rest, n_cast):
    cast_in, o_ref, cast_out = rest[:n_cast], rest[n_cast], rest[n_cast + 1:]
    for src_ref, dst_ref in zip(cast_in, cast_out):
        dst_ref[...] = src_ref[...].astype(BF16)

    nsub = ATTN_TQ // ATTN_BLOCK
    blk, dh = ATTN_BLOCK, HEAD_DIM
    seg = seg_ref[...]

    def normed(x, gain):
        ssq = _dot((x * x).astype(BF16), seg)
        return x * lax.rsqrt(ssq * (1.0 / dh) + EPS) * gain

    kn = normed(jnp.concatenate([kp_ref[...], kc_ref[...]], axis=0).astype(F32), kg_ref[...])
    vt = jnp.concatenate([vp_ref[...], vc_ref[...]], axis=0).astype(F32).T.astype(BF16)
    lane = lax.broadcasted_iota(jnp.int32, (kn.shape[0], 2 * dh), 1)
    qgain = qg_ref[...] * (dh ** -0.5 * LOG2E)

    kzs, qns = [], []
    for g in range(N_KV_HEADS):
        pair = kn[:, (g // 2) * 2 * dh:(g // 2 + 1) * 2 * dh]
        own = jnp.where(lane >= dh if g % 2 else lane < dh, pair, 0.0)
        swapped = pltpu.roll(own, dh, axis=1)
        kz = [own, swapped] if g % 2 == 0 else [swapped, own]
        kzs.append([z.astype(BF16) for z in kz])
        qns.append(normed(q_ref[:, g * 4 * dh:(g + 1) * 4 * dh].astype(F32), qgain).astype(BF16))

    def from_prev(n_rows, n_lanes):
        slot = lax.broadcasted_iota(jnp.int32, (n_rows, n_lanes), 0) % blk
        query = lax.broadcasted_iota(jnp.int32, (n_rows, n_lanes), 1) % blk
        return slot > query

    def scores(g, sb):
        bref = b0_ref if sb == 0 else b1_ref
        prv, cur = slice(sb * blk, (sb + 1) * blk), slice((sb + 1) * blk, (sb + 2) * blk)
        rows = slice(sb * blk, (sb + 1) * blk)
        qr = jnp.concatenate([qns[g][rows, 0:2 * dh], qns[g][rows, 2 * dh:4 * dh]], axis=0)
        s_prv = _dot_nt(jnp.concatenate([kzs[g][0][prv], kzs[g][1][prv]], axis=0), qr)
        s_cur = _dot_nt(jnp.concatenate([kzs[g][0][cur], kzs[g][1][cur]], axis=0), qr)
        return jnp.where(from_prev(2 * blk, 2 * blk), s_prv, s_cur) + bref[g]

    def finish(g, sb, s):
        prv, cur = slice(sb * blk, (sb + 1) * blk), slice((sb + 1) * blk, (sb + 2) * blk)
        rows = slice(sb * blk, (sb + 1) * blk)
        ps, inv = [], []
        for e in range(2):
            se = s[e * blk:(e + 1) * blk]
            sink = sink_ref[g, e] * LOG2E
            m = jnp.maximum(jnp.max(se, axis=0, keepdims=True), sink)
            p = jnp.exp2(se - m)
            inv.append(1.0 / (jnp.sum(p, axis=0, keepdims=True) + jnp.exp2(sink - m)))
            ps.append(p)
        pm = jnp.concatenate(ps, axis=1)
        mask = from_prev(blk, 4 * blk)
        vg = vt[g * dh:(g + 1) * dh]
        ot = (_dot(vg[:, prv], jnp.where(mask, pm, 0.0).astype(BF16))
              + _dot(vg[:, cur], jnp.where(mask, 0.0, pm).astype(BF16)))
        ot = ot * jnp.concatenate(inv, axis=1)
        o = jnp.concatenate([ot[:, 0:2 * blk], ot[:, 2 * blk:4 * blk]], axis=0).T
        o_ref[rows, (2 * g) * 2 * dh:(2 * g + 1) * 2 * dh] = o[0:blk].astype(BF16)
        o_ref[rows, (2 * g + 1) * 2 * dh:(2 * g + 2) * 2 * dh] = o[blk:2 * blk].astype(BF16)

    units = [(g, sb) for g in range(N_KV_HEADS) for sb in range(nsub)]
    pending = [scores(*unit) for unit in units[:ATTN_AHEAD]]
    for n, unit in enumerate(units):
        if n + ATTN_AHEAD < len(units):
            pending.append(scores(*units[n + ATTN_AHEAD]))
        finish(*unit, pending.pop(0))


def _attn(proj3, biasm, q_gain, k_gain, sinks, cast_weights):
    B, S, _ = proj3.shape
    tq, blk = ATTN_TQ, ATTN_BLOCK
    nsub = tq // blk
    kvw = ATTN_KV_W
    seg_id = np.arange(kvw) // HEAD_DIM
    seg = jnp.asarray(seg_id[:, None] == seg_id[None, :], dtype=BF16)
    sink_rows = jnp.repeat(sinks.reshape(N_KV_HEADS, 2, 2).transpose(0, 2, 1), blk, axis=-1)
    sink_rows = sink_rows.reshape(N_KV_HEADS, 2, 1, 2 * blk)

    def prev(i):
        return jnp.maximum(i * nsub - 1, 0)

    bias_block = (None,) + biasm.shape[1:]
    cast_specs = _cast_specs(cast_weights, B, S // tq)
    outs = pl.pallas_call(
        functools.partial(_attn_kernel, n_cast=len(cast_weights)),
        grid=(B, S // tq),
        in_specs=[pl.BlockSpec((None, tq, ATTN_Q_W), lambda b, i: (b, i, COL_AQ // ATTN_Q_W)),
                  pl.BlockSpec((None, tq, kvw), lambda b, i: (b, i, COL_AK // kvw)),
                  pl.BlockSpec((None, blk, kvw), lambda b, i: (b, prev(i), COL_AK // kvw)),
                  pl.BlockSpec((None, tq, kvw), lambda b, i: (b, i, COL_AV // kvw)),
                  pl.BlockSpec((None, blk, kvw), lambda b, i: (b, prev(i), COL_AV // kvw)),
                  pl.BlockSpec(bias_block, lambda b, i: (jnp.minimum(i, 1), 0, 0, 0)),
                  pl.BlockSpec(bias_block, lambda b, i: (1, 0, 0, 0)),
                  pl.BlockSpec((1, kvw), lambda b, i: (0, 0)),
                  pl.BlockSpec((1, kvw), lambda b, i: (0, 0)),
                  pl.BlockSpec(sink_rows.shape, lambda b, i: (0, 0, 0, 0)),
                  pl.BlockSpec(seg.shape, lambda b, i: (0, 0))] + cast_specs,
        out_specs=[pl.BlockSpec((None, tq, ATTN_Q_W), lambda b, i: (b, i, 0))] + cast_specs,
        out_shape=[jax.ShapeDtypeStruct((B, S, ATTN_Q_W), BF16)]
        + [jax.ShapeDtypeStruct(cw.shape, BF16) for cw in cast_weights],
        compiler_params=_params("arbitrary", "arbitrary"),
        name="attn",
    )(proj3, proj3, proj3, proj3, proj3, biasm, biasm, jnp.tile(q_gain, (1, GQA_GROUP)),
      jnp.tile(k_gain, (1, N_KV_HEADS)), sink_rows, seg, *cast_weights)
    return outs[0], outs[1:]


def _split2(x):
    hi = x.astype(BF16)
    return hi, (x - hi.astype(F32)).astype(BF16)


def _gla_head(q, k, v, r, lr, wgk, bgk, gain, state_ref, masks, store):
    C = GLA_CHUNK
    tril, m_same, m_next, m_far, eye = masks
    z = _dot(lr, wgk) + bgk
    yield
    gl = (jnp.minimum(z, 0.0) - jnp.log(1.0 + jnp.exp(-jnp.abs(z)))) * (1.0 / GLA_NORMALIZER)
    hi, lo = _split2(gl)
    g = _dot(tril, hi) + _dot(tril, lo)
    yield
    t = [g[(c + 1) * C - 1:(c + 1) * C] for c in range(4)]
    t_rows = jnp.concatenate([jnp.broadcast_to(tc, (C, tc.shape[1])) for tc in t], axis=0)

    q_dec = q.astype(F32) * (GLA_DK ** -0.5) * jnp.exp(g)
    kf = k.astype(F32)
    k_inv = (kf * jnp.exp(-g)).astype(BF16)
    k_end = kf * jnp.exp(t_rows - g)
    qd = [q_dec[c * C:(c + 1) * C] for c in range(4)]
    ke = [k_end[c * C:(c + 1) * C] for c in range(4)]

    def rows(parts):
        return jnp.concatenate(parts, axis=0).astype(BF16)

    q_b, k_b = q_dec.astype(BF16), k_end.astype(BF16)
    q_far = rows([qd[2], qd[3] * jnp.exp(t[2])])
    k_far = rows([ke[0] * jnp.exp(t[1]), ke[1]])
    q_abs = rows([qd[0], qd[1] * jnp.exp(t[0]), qd[2] * jnp.exp(t[0] + t[1]), qd[3] * jnp.exp(t[0] + t[1] + t[2])])
    k_abs = rows([ke[0] * jnp.exp(t[1] + t[2] + t[3]), ke[1] * jnp.exp(t[2] + t[3]), ke[2] * jnp.exp(t[3]), ke[3]])

    a_same, a_next, a_far = _dot_nt(q_b, k_inv), _dot_nt(q_b, k_b), _dot_nt(q_far, k_far)
    state = state_ref[...]
    o_state = _dot(q_abs, state.astype(BF16))
    update = _dot_tn(k_abs, v)
    yield
    a = jnp.where(m_same, a_same, jnp.where(m_next, a_next, 0.0))
    hr = a.shape[0] // 2
    a = jnp.concatenate([a[:hr], jnp.concatenate([a_far, a[hr:, hr:]], axis=1)], axis=0)
    o = _dot(a.astype(BF16), v) + o_state
    yield

    decay = jnp.exp(t[0] + t[1] + t[2] + t[3])
    decay_col = jnp.sum(jnp.where(eye, jnp.broadcast_to(decay, eye.shape), 0.0), axis=1, keepdims=True)
    state_ref[...] = decay_col * state + update

    rf = r.astype(F32)
    store((_rms(o, gain) * (rf * jax.nn.sigmoid(rf))).astype(BF16))


def _gla_kernel(q_ref, k_ref, v_ref, r_ref, lr_ref, wgk_ref, bgk_ref, gain_ref, o_ref, state_ref):
    R, C, dk, dv = GLA_ROWS, GLA_CHUNK, GLA_DK, GLA_DV

    @pl.when(pl.program_id(2) == 0)
    def _():
        state_ref[...] = jnp.zeros_like(state_ref)

    row = lax.broadcasted_iota(jnp.int32, (R, R), 0)
    col = lax.broadcasted_iota(jnp.int32, (R, R), 1)
    ci, cj = row // C, col // C
    m_same = (row >= col) & (ci == cj)
    m_next = (ci == cj + 1) & (ci != 2)
    m_far = (ci >= 2) & (cj <= 1)
    eye = lax.broadcasted_iota(jnp.int32, (dk, dk), 0) == lax.broadcasted_iota(jnp.int32, (dk, dk), 1)
    masks = (m_same.astype(BF16), m_same, m_next, m_far, eye)

    heads = []
    for b in range(GLA_BATCH_PER_STEP):
        lr = lr_ref[b]
        for h in range(GLA_HEADS_PER_STEP):
            ks, vs = slice(h * dk, (h + 1) * dk), slice(h * dv, (h + 1) * dv)
            heads.append(_gla_head(q_ref[b, :, ks], k_ref[b, :, ks], v_ref[b, :, vs], r_ref[b, :, vs], lr,
                                   wgk_ref[:, ks], bgk_ref[:, ks], gain_ref[...], state_ref.at[b, h], masks,
                                   functools.partial(o_ref.__setitem__, (b, slice(None), vs))))
    for _ in itertools.zip_longest(*heads):
        pass


def _gla(proj3, wgk, bgk, gain):
    B, S, _ = proj3.shape
    R, hps, bps = GLA_ROWS, GLA_HEADS_PER_STEP, GLA_BATCH_PER_STEP
    dk, dv = hps * GLA_DK, hps * GLA_DV
    return pl.pallas_call(
        _gla_kernel,
        grid=(B // bps, GLA_HEADS // hps, S // R),
        in_specs=[pl.BlockSpec((bps, R, dk), lambda b, h, t: (b, t, COL_GQ // dk + h)),
                  pl.BlockSpec((bps, R, dk), lambda b, h, t: (b, t, COL_GK // dk + h)),
                  pl.BlockSpec((bps, R, dv), lambda b, h, t: (b, t, COL_GV // dv + h)),
                  pl.BlockSpec((bps, R, dv), lambda b, h, t: (b, t, COL_GR // dv + h)),
                  pl.BlockSpec((bps, R, LR_PAD), lambda b, h, t: (b, t, COL_LR // LR_PAD)),
                  pl.BlockSpec((LR_PAD, dk), lambda b, h, t: (0, h)),
                  pl.BlockSpec((1, dk), lambda b, h, t: (0, h)),
                  pl.BlockSpec((1, GLA_DV), lambda b, h, t: (0, 0))],
        out_specs=pl.BlockSpec((bps, R, dv), lambda b, h, t: (b, t, h)),
        out_shape=jax.ShapeDtypeStruct((B, S, GLA_V_W), BF16),
        scratch_shapes=[pltpu.VMEM((bps, hps, GLA_DK, GLA_DV), F32)],
        compiler_params=_params("parallel", "parallel", "arbitrary"),
        name="gla",
    )(proj3, proj3, proj3, proj3, proj3, wgk, bgk, gain)


def _merge_kernel(ya_ref, yg_ref, ga_ref, gb_ref, x_ref, mod_ref, g2_ref, wa_ref, wg_ref, wo_ref,
                  x1_ref, h2_ref):
    def merged_branches(rows):
        ma, mg = _dot(ya_ref[rows, :], wa_ref[...]), _dot(yg_ref[rows, :], wg_ref[...])
        ga = jax.nn.sigmoid(ga_ref[rows, :].astype(F32))
        gb = jax.nn.sigmoid(gb_ref[rows, :].astype(F32))
        return (ga * ma + gb * mg).astype(BF16)

    def project(rows, merged):
        x1 = x_ref[rows, :] + mod_ref[2:3, :] * _dot(merged, wo_ref[...])
        x1_ref[rows, :] = x1
        h2_ref[rows, :] = (_rms(x1, g2_ref[...]) * (1.0 + mod_ref[4:5, :]) + mod_ref[3:4, :]).astype(BF16)

    for r0 in range(0, MERGE_TM, MERGE_SUB):
        rows = slice(r0, r0 + MERGE_SUB)
        project(rows, merged_branches(rows))


def _merge(ya, yg, proj, x2, mod3, gain2, wa, wg, wo):
    T, D = x2.shape
    S = T // mod3.shape[0]
    tm = MERGE_TM
    once = pl.Buffered(1)
    return pl.pallas_call(
        _merge_kernel,
        grid=(T // tm,),
        in_specs=[pl.BlockSpec((tm, ATTN_Q_W), lambda i: (i, 0)),
                  pl.BlockSpec((tm, GLA_V_W), lambda i: (i, 0)),
                  pl.BlockSpec((tm, D), lambda i: (i, COL_GA // D)),
                  pl.BlockSpec((tm, D), lambda i: (i, COL_GB // D)),
                  pl.BlockSpec((tm, D), lambda i: (i, 0)),
                  pl.BlockSpec((None, 6, D), lambda i: (i // (S // tm), 0, 0)),
                  pl.BlockSpec((1, D), lambda i: (0, 0)),
                  pl.BlockSpec((ATTN_Q_W, D), lambda i: (0, 0), pipeline_mode=once),
                  pl.BlockSpec((GLA_V_W, D), lambda i: (0, 0), pipeline_mode=once),
                  pl.BlockSpec((D, D), lambda i: (0, 0), pipeline_mode=once)],
        out_specs=[pl.BlockSpec((tm, D), lambda i: (i, 0)),
                   pl.BlockSpec((tm, D), lambda i: (i, 0))],
        out_shape=[jax.ShapeDtypeStruct((T, D), F32), jax.ShapeDtypeStruct((T, D), BF16)],
        compiler_params=_params("parallel"),
        name="merge",
    )(ya, yg, proj, proj, x2, mod3, gain2, wa, wg, wo)


def _ffn_up_kernel(h_ref, halo_ref, wa_ref, wb_ref, cwa_ref, cwb_ref, cba_ref, cbb_ref, o_ref,
                   hs_ref, ua_ref, ub_ref, *, tiles_per_seq):
    tm, halo = FFN_TM, FFN_HALO

    @pl.when(pl.program_id(1) == 0)
    def _():
        first = (pl.program_id(0) % tiles_per_seq) == 0
        hs_ref[0:halo, :] = jnp.where(first, jnp.zeros_like(halo_ref), halo_ref[...])
        hs_ref[halo:, :] = h_ref[...]

    def conv(u_ref, cw_ref, cb_ref, r0, n):
        u = u_ref[halo + r0 - 8:halo + r0 + n, :]
        y = cb_ref[...] + cw_ref[0:1, :] * pltpu.roll(u, 2, axis=0)[8:]
        y = y + cw_ref[1:2, :] * pltpu.roll(u, 1, axis=0)[8:]
        return y + cw_ref[2:3, :] * u[8:]

    ua_ref[...] = _dot(hs_ref[...], wa_ref[...])
    ub_ref[...] = _dot(hs_ref[...], wb_ref[...])
    for r0 in range(0, tm, FFN_EPI_ROWS):
        ya = conv(ua_ref, cwa_ref, cba_ref, r0, FFN_EPI_ROWS)
        yb = conv(ub_ref, cwb_ref, cbb_ref, r0, FFN_EPI_ROWS)
        o_ref[r0:r0 + FFN_EPI_ROWS, :] = (ya * jax.nn.sigmoid(ya) * yb).astype(BF16)


def _ffn_up(h2, S, w_up, conv_w, conv_b):
    T, D = h2.shape
    tm, tn, halo = FFN_TM, FFN_TN, FFN_HALO
    nj = D_FF // tn
    return pl.pallas_call(
        functools.partial(_ffn_up_kernel, tiles_per_seq=S // tm),
        grid=(T // tm, D_FF // tn),
        in_specs=[pl.BlockSpec((tm, D), lambda i, j: (i, 0)),
                  pl.BlockSpec((halo, D), lambda i, j: (jnp.maximum(i * (tm // halo) - 1, 0), 0)),
                  pl.BlockSpec((D, tn), lambda i, j: (0, j)),
                  pl.BlockSpec((D, tn), lambda i, j: (0, j + nj)),
                  pl.BlockSpec((3, tn), lambda i, j: (0, j)),
                  pl.BlockSpec((3, tn), lambda i, j: (0, j + nj)),
                  pl.BlockSpec((1, tn), lambda i, j: (0, j)),
                  pl.BlockSpec((1, tn), lambda i, j: (0, j + nj))],
        out_specs=pl.BlockSpec((tm, tn), lambda i, j: (i, j)),
        out_shape=jax.ShapeDtypeStruct((T, D_FF), BF16),
        scratch_shapes=[pltpu.VMEM((tm + halo, D), BF16),
                        pltpu.VMEM((tm + halo, tn), F32),
                        pltpu.VMEM((tm + halo, tn), F32)],
        compiler_params=_params("parallel", "arbitrary"),
        name="ffn_up",
    )(h2, h2, w_up, w_up, conv_w, conv_w, conv_b, conv_b)


def _ffn_down_kernel(a_ref, w_ref, x_ref, mod_ref, o_ref):
    o_ref[...] = x_ref[...] + mod_ref[5:6, :] * _dot(a_ref[...], w_ref[...])


def _ffn_down(act, wd, x1, mod3):
    T, D = x1.shape
    S = T // mod3.shape[0]
    tm = DOWN_TM
    return pl.pallas_call(
        _ffn_down_kernel,
        grid=(T // tm,),
        in_specs=[pl.BlockSpec((tm, D_FF), lambda i: (i, 0)),
                  pl.BlockSpec((D_FF, D), lambda i: (0, 0), pipeline_mode=pl.Buffered(1)),
                  pl.BlockSpec((tm, D), lambda i: (i, 0)),
                  pl.BlockSpec((None, 6, D), lambda i: (i // (S // tm), 0, 0))],
        out_specs=pl.BlockSpec((tm, D), lambda i: (i, 0)),
        out_shape=jax.ShapeDtypeStruct((T, D), F32),
        compiler_params=_params("parallel"),
        name="ffn_down",
    )(act, wd, x1, mod3)


def _layer(x2, B, mod3, biasm, norm1_gain, w_in, q_norm_gain, k_norm_gain, attn_sinks, w_gk_up, b_gk,
           gla_norm_gain, w_branch_attn, w_branch_gla, w_out, norm2_gain, w_ffn_up, ffn_conv_w,
           ffn_conv_b, w_ffn_down):
    T, D = x2.shape
    S = T // B
    w_p = _wpack(w_in.T)
    wgk = jnp.concatenate([w_gk_up, jnp.zeros((LR_PAD - GLA_LOWRANK, GLA_K_W), w_gk_up.dtype)],
                          axis=0).astype(BF16)

    proj = _inproj(x2, mod3, norm1_gain.reshape(1, D), w_p)
    proj3 = proj.reshape(B, S, PROJ_W)
    ya, (w_ba, w_bg, w_o, w_up, w_down) = _attn(
        proj3, biasm, q_norm_gain.reshape(1, HEAD_DIM), k_norm_gain.reshape(1, HEAD_DIM), attn_sinks,
        (w_branch_attn, w_branch_gla, w_out, w_ffn_up, w_ffn_down))
    yg = _gla(proj3, wgk, b_gk.reshape(1, GLA_K_W), gla_norm_gain.reshape(1, GLA_DV))
    x1, h2 = _merge(ya.reshape(T, ATTN_Q_W), yg.reshape(T, GLA_V_W), proj, x2, mod3,
                    norm2_gain.reshape(1, D), w_ba, w_bg, w_o)
    act = _ffn_up(h2, S, w_up, ffn_conv_w, ffn_conv_b.reshape(1, 2 * D_FF))
    return _ffn_down(act, w_down, x1, mod3)


def kernel(x, c, rel_bias_table, w_ada, b_ada, norm1_gain, w_in, q_norm_gain, k_norm_gain, attn_sinks,
           w_gk_up, b_gk, gla_norm_gain, w_branch_attn, w_branch_gla, w_out, norm2_gain, w_ffn_up,
           ffn_conv_w, ffn_conv_b, w_ffn_down):
    B, S, D = x.shape
    depth = w_in.shape[0]
    assert D == D_MODEL and w_in.shape[1:] == (D_MODEL, SRC_GA + 2 * D_MODEL) and w_ffn_up.shape[2] == 2 * D_FF
    assert S % max(FFN_TM, INPROJ_TM, ATTN_TQ, GLA_ROWS, MERGE_TM, DOWN_TM) == 0
    biasm = _relbias(rel_bias_table)
    x2 = x.reshape(B * S, D)
    for l in range(depth):
        mod3 = _adaln(c, w_ada[l], b_ada[l]).reshape(B, 6, D)
        x2 = _layer(x2, B, mod3, biasm, norm1_gain[l], w_in[l], q_norm_gain[l], k_norm_gain[l],
                    attn_sinks[l], w_gk_up[l], b_gk[l], gla_norm_gain[l], w_branch_attn[l],
                    w_branch_gla[l], w_out[l], norm2_gain[l], w_ffn_up[l], ffn_conv_w[l], ffn_conv_b[l],
                    w_ffn_down[l])
    return x2.reshape(B, S, D)
```
